```python
import math
import jax
import jax.numpy as jnp
from jax import lax
import numpy as np


D_MODEL = 4096
BATCH = 4
SEQ = 4096
DEPTH = 2

CTX_LEN = 256
GRID_W = 64
GROUP_W = D_MODEL // 4
MIX_WIDTH = 4 * GROUP_W
ROPE_DIM = 64
ROPE_BASE = 10000.0
Q_BLOCK = 128
NORM_EPS = 1e-6
DIFF_HEAD_DIM = 64
DIFF_HEADS = GROUP_W // (2 * DIFF_HEAD_DIM)
DIFF_QK = DIFF_HEADS * 2 * DIFF_HEAD_DIM
S5_CH = 16
S5_GROUPS = GROUP_W // S5_CH
S5_STATE = 64
MLA_NOPE = 128
MLA_ROPE = ROPE_DIM
MLA_V = 128
MLA_HEADS = GROUP_W // MLA_V
MLA_Q_RANK = 3 * D_MODEL // 16
MLA_KV_RANK = D_MODEL // 16
RET_K = ROPE_DIM
RET_V = 128
RET_HEADS = GROUP_W // RET_V
RET_QK = RET_HEADS * RET_K
RET_CHUNK = 128
MOE_GROUPS = 4
MOE_PER_GROUP = 4
MOE_EXPERTS = MOE_GROUPS * MOE_PER_GROUP
MOE_TOPK = 2
MOE_D_FF = D_MODEL // 4
IN_SPLITS = (DIFF_QK, DIFF_QK, GROUP_W, GROUP_W, MLA_Q_RANK, MLA_KV_RANK, MLA_ROPE,
             RET_QK, RET_QK, GROUP_W, GROUP_W)
IN_WIDTH = sum(IN_SPLITS)

kernel_name = "hybrid_headgroup_diffusion_trunk"


def rms_norm(x, g):
    xf = x.astype(jnp.float32)
    y = xf * lax.rsqrt(jnp.mean(xf * xf, axis=-1, keepdims=True) + NORM_EPS)
    return (y * g.astype(jnp.float32)).astype(x.dtype)


def modulate(h, shift, scale):
    return h * (1 + scale) + shift


def axial_rope_tables(n_lat):
    rows = n_lat // GRID_W
    row = jnp.repeat(jnp.arange(rows, dtype=jnp.float32), GRID_W)
    col = jnp.tile(jnp.arange(GRID_W, dtype=jnp.float32), rows)
    quarter = ROPE_DIM // 4
    inv = ROPE_BASE ** (-jnp.arange(quarter, dtype=jnp.float32) / quarter)
    ar = row[:, None] * inv
    ac = col[:, None] * inv
    ang = jnp.concatenate([ar, ar, ac, ac], axis=-1)
    return jnp.cos(ang), jnp.sin(ang)


def apply_rope(x, cos, sin):
    x1, x2, x3, x4 = jnp.split(x, 4, axis=-1)
    rot = jnp.concatenate([-x2, x1, -x4, x3], axis=-1)
    return x * cos[None, :, None, :].astype(x.dtype) + rot * sin[None, :, None, :].astype(x.dtype)


def sweep_query_blocks(fn, q):
    b, n = q.shape[:2]
    qb = jnp.moveaxis(q.reshape((b, n // Q_BLOCK, Q_BLOCK) + q.shape[2:]), 1, 0)
    out = jnp.moveaxis(lax.map(fn, qb), 0, 1)
    return out.reshape((b, n) + out.shape[3:])


def softmax_attend(q, k, v, scale):
    def block(qb):
        s = jnp.einsum('bqhd,bkhd->bhqk', qb, k, preferred_element_type=jnp.float32) * scale
        p = jax.nn.softmax(s, axis=-1)
        return jnp.einsum('bhqk,bkhe->bqhe', p.astype(v.dtype), v)
    return sweep_query_blocks(block, q)


def diff_attend(q, k, v, lam):
    b, _, h2, d = q.shape
    m = k.shape[1]
    scale = d ** -0.5
    def block(qb):
        s = jnp.einsum('bqhd,bkhd->bhqk', qb, k, preferred_element_type=jnp.float32) * scale
        p = jax.nn.softmax(s, axis=-1).reshape(b, h2 // 2, 2, qb.shape[1], m)
        p = p[:, :, 0] - lam * p[:, :, 1]
        return jnp.einsum('bhqk,bkhe->bqhe', p.astype(v.dtype), v)
    return sweep_query_blocks(block, q)


def diff_branch(q, k, v, q_c, k_c, v_c, cos, sin, lam_vec, subln, lam_init, need_ctx):
    hd = lambda t, nh: t.reshape(t.shape[0], t.shape[1], nh, -1)
    lv = lam_vec.astype(jnp.float32)
    lam = jnp.exp(jnp.sum(lv[0] * lv[1])) - jnp.exp(jnp.sum(lv[2] * lv[3])) + lam_init
    q = apply_rope(hd(q, 2 * DIFF_HEADS), cos, sin)
    k = apply_rope(hd(k, 2 * DIFF_HEADS), cos, sin)
    v = hd(v, DIFF_HEADS)
    q_c, k_c, v_c = hd(q_c, 2 * DIFF_HEADS), hd(k_c, 2 * DIFF_HEADS), hd(v_c, DIFF_HEADS)
    out = diff_attend(q, jnp.concatenate([k, k_c], axis=1), jnp.concatenate([v, v_c], axis=1), lam)
    finish = lambda o: (rms_norm(o, subln) * (1.0 - lam_init)).reshape(o.shape[0], o.shape[1], GROUP_W)
    out_c = finish(diff_attend(q_c, k_c, v_c, lam)) if need_ctx else None
    return finish(out), out_c


def _ssm_combine(e1, e2):
    a1, b1 = e1
    a2, b2 = e2
    return a1 * a2, a2 * b1 + b2


def s5_discretize(a_re, a_im, log_dt, b_re, b_im):
    a = lax.complex(a_re.astype(jnp.float32), a_im.astype(jnp.float32))
    dt = jnp.exp(log_dt.astype(jnp.float32))[:, None]
    abar = jnp.exp(a * dt)
    bmat = lax.complex(b_re.astype(jnp.float32), b_im.astype(jnp.float32))
    bbar = ((abar - 1.0) / a)[:, :, None] * bmat
    return abar, bbar


def s5_scan(u, abar, bbar, h0):
    bu = jnp.einsum('gpc,bngc->bngp', bbar, u.astype(jnp.complex64))
    if h0 is not None:
        bu = bu.at[:, 0].add(abar * h0)
    a = jnp.broadcast_to(abar, bu.shape)
    _, xs = lax.associative_scan(_ssm_combine, (a, bu), axis=1)
    return xs


def s5_readout(cmat, xs):
    return jnp.real(jnp.einsum('gcp,bngp->bngc', cmat, xs))


def s5_branch(u, u_c, a_re, a_im, log_dt, b_re, b_im, c_re, c_im, d_skip, glu_w, glu_b, need_ctx):
    b, n, _ = u.shape
    lc = u_c.shape[1]
    ug = u.astype(jnp.float32).reshape(b, n, S5_GROUPS, S5_CH)
    ucg = u_c.astype(jnp.float32).reshape(b, lc, S5_GROUPS, S5_CH)
    dsk = d_skip.astype(jnp.float32)
    y = dsk * ug
    yc = dsk * ucg if need_ctx else None
    for dr in range(2):
        flip = (lambda t: jnp.flip(t, 1)) if dr == 1 else (lambda t: t)
        abar, bbar = s5_discretize(a_re[dr], a_im[dr], log_dt[dr], b_re[dr], b_im[dr])
        cmat = lax.complex(c_re[dr].astype(jnp.float32), c_im[dr].astype(jnp.float32))
        xs_c = s5_scan(flip(ucg), abar, bbar, None)
        xs = s5_scan(flip(ug), abar, bbar, xs_c[:, -1])
        y = y + flip(s5_readout(cmat, xs))
        if need_ctx:
            yc = yc + flip(s5_readout(cmat, xs_c))
    def glu(t):
        g = jax.nn.gelu(t.reshape(t.shape[0], t.shape[1], GROUP_W))
        gate = jax.nn.sigmoid(g @ glu_w.astype(jnp.float32) + glu_b.astype(jnp.float32))
        return (g * gate).astype(u.dtype)
    return glu(y), (glu(yc) if need_ctx else None)


def mla_branch(cq, ckv, kr, cq_c, ckv_c, kr_c, cos, sin, q_norm, kv_norm, w_uq, w_ukv, need_ctx):
    def qkv(cq, ckv, kr, rotate):
        b, n, _ = cq.shape
        q = (rms_norm(cq, q_norm) @ w_uq).reshape(b, n, MLA_HEADS, MLA_NOPE + MLA_ROPE)
        kv = (rms_norm(ckv, kv_norm) @ w_ukv).reshape(b, n, MLA_HEADS, MLA_NOPE + MLA_V)
        q_nope, q_rope = q[..., :MLA_NOPE], q[..., MLA_NOPE:]
        k_nope, v = kv[..., :MLA_NOPE], kv[..., MLA_NOPE:]
        k_rope = kr[:, :, None, :]
        if rotate:
            q_rope = apply_rope(q_rope, cos, sin)
            k_rope = apply_rope(k_rope, cos, sin)
        k = jnp.concatenate([k_nope, jnp.broadcast_to(k_rope, (b, n, MLA_HEADS, MLA_ROPE))], axis=-1)
        return jnp.concatenate([q_nope, q_rope], axis=-1), k, v
    q, k, v = qkv(cq, ckv, kr, True)
    q_c, k_c, v_c = qkv(cq_c, ckv_c, kr_c, False)
    scale = (MLA_NOPE + MLA_ROPE) ** -0.5
    out = softmax_attend(q, jnp.concatenate([k, k_c], axis=1), jnp.concatenate([v, v_c], axis=1), scale)
    out = out.reshape(out.shape[0], out.shape[1], GROUP_W)
    out_c = None
    if need_ctx:
        out_c = softmax_attend(q_c, k_c, v_c, scale)
        out_c = out_c.reshape(out_c.shape[0], out_c.shape[1], GROUP_W)
    return out, out_c


def retention_chunked(q, k, v, log_g, s0, strict):
    b, h, n, dk = q.shape
    dv = v.shape[-1]
    cs = RET_CHUNK
    nc = n // cs
    pos = jnp.arange(cs, dtype=jnp.float32)
    diff = pos[:, None] - pos[None, :]
    keep = diff > 0 if strict else diff >= 0
    dmat = jnp.where(keep[None], jnp.exp(log_g[:, None, None] * jnp.maximum(diff, 0.0)[None]), 0.0)
    qc = q.reshape(b, h, nc, cs, dk)
    kc = k.reshape(b, h, nc, cs, dk)
    vc = v.reshape(b, h, nc, cs, dv)
    scores = jnp.einsum('bhncd,bhnmd->bhncm', qc, kc) * dmat[None, :, None]
    intra = jnp.einsum('bhncm,bhnme->bhnce', scores, vc)
    k_w = kc * jnp.exp(log_g[:, None] * (cs - 1 - pos))[None, :, None, :, None]
    u = jnp.einsum('bhncd,bhnce->nbhde', k_w, vc)
    chunk_decay = jnp.exp(log_g * cs)[None, :, None, None]
    def step(s, u_i):
        return chunk_decay * s + u_i, s
    s_final, s_prev = lax.scan(step, s0, u)
    q_w = qc * jnp.exp(log_g[:, None] * (pos + 1))[None, :, None, :, None]
    cross = jnp.einsum('bhncd,nbhde->bhnce', q_w, s_prev)
    return (intra + cross).reshape(b, h, n, dv), s_final


def retention_branch(q, k, v, g, q_c, k_c, v_c, g_c, cos, sin, decay_raw, norm_g, need_ctx):
    b = q.shape[0]
    log_g = jax.nn.log_sigmoid(decay_raw.astype(jnp.float32))
    bhnd = lambda t: jnp.transpose(t, (0, 2, 1, 3))
    flip = lambda t: jnp.flip(t, 2)
    hd = lambda t, dh: t.reshape(t.shape[0], t.shape[1], RET_HEADS, dh)
    ksc = RET_K ** -0.5
    q = bhnd(apply_rope(hd(q, RET_K), cos, sin))
    k = bhnd(apply_rope(hd(k, RET_K), cos, sin) * ksc)
    v = bhnd(hd(v, RET_V))
    q_c, k_c, v_c = bhnd(hd(q_c, RET_K)), bhnd(hd(k_c, RET_K) * ksc), bhnd(hd(v_c, RET_V))
    s0 = jnp.zeros((b, RET_HEADS, RET_K, RET_V), jnp.float32)
    yc_f, sc_f = retention_chunked(q_c, k_c, v_c, log_g[0], s0, False)
    yc_b, sc_b = retention_chunked(flip(q_c), flip(k_c), flip(v_c), log_g[1], s0, True)
    y_f, _ = retention_chunked(q, k, v, log_g[0], sc_f, False)
    y_b, _ = retention_chunked(flip(q), flip(k), flip(v), log_g[1], sc_b, True)
    def finish(y, gate):
        y = rms_norm(jnp.transpose(y, (0, 2, 1, 3)), norm_g.reshape(RET_HEADS, RET_V))
        return (jax.nn.silu(gate) * y.reshape(y.shape[0], y.shape[1], GROUP_W)).astype(gate.dtype)
    out = finish(y_f + flip(y_b), g)
    out_c = finish(yc_f + flip(yc_b), g_c) if need_ctx else None
    return out, out_c


def token_mixers(h, hc, cos, sin, need_ctx, lam_init, w_in, w_out, diff_lambda, diff_subln,
                 s5_a_re, s5_a_im, s5_log_dt, s5_b_re, s5_b_im, s5_c_re, s5_c_im, s5_d, s5_glu_w, s5_glu_b,
                 mla_q_norm, mla_kv_norm, mla_w_uq, mla_w_ukv, ret_decay, ret_norm):
    pts = np.cumsum(IN_SPLITS)[:-1].tolist()
    dq, dk, dv, su, mcq, mckv, mkr, rq, rk, rv, rg = jnp.split(h @ w_in, pts, axis=-1)
    dq_c, dk_c, dv_c, su_c, mcq_c, mckv_c, mkr_c, rq_c, rk_c, rv_c, rg_c = jnp.split(hc @ w_in, pts, axis=-1)
    a, a_c = diff_branch(dq, dk, dv, dq_c, dk_c, dv_c, cos, sin, diff_lambda, diff_subln, lam_init, need_ctx)
    s, s_c = s5_branch(su, su_c, s5_a_re, s5_a_im, s5_log_dt, s5_b_re, s5_b_im, s5_c_re, s5_c_im,
                       s5_d, s5_glu_w, s5_glu_b, need_ctx)
    m, m_c = mla_branch(mcq, mckv, mkr, mcq_c, mckv_c, mkr_c, cos, sin, mla_q_norm, mla_kv_norm,
                        mla_w_uq, mla_w_ukv, need_ctx)
    r, r_c = retention_branch(rq, rk, rv, rg, rq_c, rk_c, rv_c, rg_c, cos, sin, ret_decay, ret_norm, need_ctx)
    y = jnp.concatenate([a, s, m, r], axis=-1) @ w_out
    y_c = jnp.concatenate([a_c, s_c, m_c, r_c], axis=-1) @ w_out if need_ctx else None
    return y, y_c


def hier_moe(h, wg, bg, we, be, w_gate, w_up, w_down):
    shp = h.shape
    t = h.reshape(-1, shp[-1])
    g_prob = jax.nn.softmax((t @ wg + bg).astype(jnp.float32), axis=-1)
    g_p, g_idx = lax.top_k(g_prob, 1)
    e_logits = (t @ we + be).astype(jnp.float32).reshape(-1, MOE_GROUPS, MOE_PER_GROUP)
    e_logits = e_logits[jnp.arange(t.shape[0]), g_idx[:, 0]]
    e_p, e_idx = lax.top_k(jax.nn.softmax(e_logits, axis=-1), MOE_TOPK)
    w = g_p * e_p / jnp.sum(e_p, axis=-1, keepdims=True)
    ids = g_idx * MOE_PER_GROUP + e_idx
    dense_w = jnp.sum(jax.nn.one_hot(ids, MOE_EXPERTS, dtype=jnp.float32) * w[..., None], axis=1)
    dense_w = dense_w.astype(t.dtype)
    out = jnp.zeros_like(t)
    for e in range(MOE_EXPERTS):
        hid = jax.nn.silu(t @ w_gate[e]) * (t @ w_up[e])
        out = out + dense_w[:, e:e + 1] * (hid @ w_down[e])
    return out.reshape(shp)


def setup_inputs(seed: int = 0) -> dict:
    key = jax.random.key(seed)
    ks = list(jax.random.split(key, 40))
    f32 = jnp.float32
    L, D = DEPTH, D_MODEL
    G, P, CH, H = S5_GROUPS, S5_STATE, S5_CH, RET_HEADS

    def nrm(shape, scale=1.0):
        return scale * jax.random.normal(ks.pop(), shape, f32)

    def gain(shape):
        return 1.0 + nrm(shape, 0.1)

    s5_a_im = jnp.pi * jnp.arange(P, dtype=f32) + nrm((L, 2, G, P), 0.01)
    s5_log_dt = jax.random.uniform(ks.pop(), (L, 2, G), f32, math.log(1e-3), math.log(1e-1))
    ret_decay = jnp.log(2.0 ** (5.0 + jnp.arange(H, dtype=f32)) - 1.0) + nrm((L, 2, H), 0.05)
    return {
        "x": nrm((BATCH, SEQ, D)),
        "c": nrm((BATCH, D)),
        "ctx": nrm((BATCH, CTX_LEN, D)),
        "c_ctx": nrm((D,)),
        "ada_w": nrm((L, D, 6 * D), 0.5 * D ** -0.5),
        "ada_b": nrm((L, 6 * D), 0.01),
        "norm_mix": gain((L, D)),
        "norm_ffn": gain((L, D)),
        "w_in": nrm((L, D, IN_WIDTH), D ** -0.5),
        "w_out": nrm((L, MIX_WIDTH, D), MIX_WIDTH ** -0.5),
        "diff_lambda": nrm((L, 4, DIFF_HEAD_DIM), 0.1),
        "diff_subln": gain((L, 2 * DIFF_HEAD_DIM)),
        "s5_a_re": -0.5 + nrm((L, 2, G, P), 0.01),
        "s5_a_im": s5_a_im,
        "s5_log_dt": s5_log_dt,
        "s5_b_re": nrm((L, 2, G, P, CH), (2 * CH) ** -0.5),
        "s5_b_im": nrm((L, 2, G, P, CH), (2 * CH) ** -0.5),
        "s5_c_re": nrm((L, 2, G, CH, P), P ** -0.5),
        "s5_c_im": nrm((L, 2, G, CH, P), P ** -0.5),
        "s5_d": nrm((L, G, CH)),
        "s5_glu_w": nrm((L, GROUP_W, GROUP_W), GROUP_W ** -0.5),
        "s5_glu_b": nrm((L, GROUP_W), 0.01),
        "mla_q_norm": gain((L, MLA_Q_RANK)),
        "mla_kv_norm": gain((L, MLA_KV_RANK)),
        "mla_w_uq": nrm((L, MLA_Q_RANK, MLA_HEADS * (MLA_NOPE + MLA_ROPE)), MLA_Q_RANK ** -0.5),
        "mla_w_ukv": nrm((L, MLA_KV_RANK, MLA_HEADS * (MLA_NOPE + MLA_V)), MLA_KV_RANK ** -0.5),
        "ret_decay": ret_decay,
        "ret_norm": gain((L, GROUP_W)),
        "moe_wg": nrm((L, D, MOE_GROUPS), D ** -0.5),
        "moe_bg": nrm((L, MOE_GROUPS), 0.01),
        "moe_we": nrm((L, D, MOE_EXPERTS), D ** -0.5),
        "moe_be": nrm((L, MOE_EXPERTS), 0.01),
        "moe_w_gate": nrm((L, MOE_EXPERTS, D, MOE_D_FF), D ** -0.5),
        "moe_w_up": nrm((L, MOE_EXPERTS, D, MOE_D_FF), D ** -0.5),
        "moe_w_down": nrm((L, MOE_EXPERTS, MOE_D_FF, D), MOE_D_FF ** -0.5),
        "final_norm": gain((D,)),
    }


def reference(x, c, ctx, c_ctx, ada_w, ada_b, norm_mix, norm_ffn, w_in, w_out, diff_lambda, diff_subln,
              s5_a_re, s5_a_im, s5_log_dt, s5_b_re, s5_b_im, s5_c_re, s5_c_im, s5_d, s5_glu_w, s5_glu_b,
              mla_q_norm, mla_kv_norm, mla_w_uq, mla_w_ukv, ret_decay, ret_norm,
              moe_wg, moe_bg, moe_we, moe_be, moe_w_gate, moe_w_up, moe_w_down, final_norm):
    cos, sin = axial_rope_tables(x.shape[1])
    xc = ctx
    sc = jax.nn.silu(c)
    scc = jax.nn.silu(c_ctx)
    for l in range(DEPTH):
        need_ctx = l < DEPTH - 1
        lam_init = 0.8 - 0.6 * math.exp(-0.3 * l)
        mod = jnp.split((sc @ ada_w[l] + ada_b[l])[:, None, :], 6, axis=-1)
        mod_c = jnp.split(scc @ ada_w[l] + ada_b[l], 6, axis=-1)
        h = modulate(rms_norm(x, norm_mix[l]), mod[0], mod[1])
        hc = modulate(rms_norm(xc, norm_mix[l]), mod_c[0], mod_c[1])
        y, y_c = token_mixers(h, hc, cos, sin, need_ctx, lam_init, w_in[l], w_out[l], diff_lambda[l],
                              diff_subln[l], s5_a_re[l], s5_a_im[l], s5_log_dt[l], s5_b_re[l], s5_b_im[l],
                              s5_c_re[l], s5_c_im[l], s5_d[l], s5_glu_w[l], s5_glu_b[l],
                              mla_q_norm[l], mla_kv_norm[l], mla_w_uq[l], mla_w_ukv[l],
                              ret_decay[l], ret_norm[l])
        x = x + mod[2] * y
        h = modulate(rms_norm(x, norm_ffn[l]), mod[3], mod[4])
        x = x + mod[5] * hier_moe(h, moe_wg[l], moe_bg[l], moe_we[l], moe_be[l],
                                  moe_w_gate[l], moe_w_up[l], moe_w_down[l])
        if need_ctx:
            xc = xc + mod_c[2] * y_c
            hc = modulate(rms_norm(xc, norm_ffn[l]), mod_c[3], mod_c[4])
            xc = xc + mod_c[5] * hier_moe(hc, moe_wg[l], moe_bg[l], moe_we[l], moe_be[l],
                                          moe_w_gate[l], moe_w_up[l], moe_w_down[l])
    return rms_norm(x, final_norm)
```

```python
import functools
import math

import jax
import jax.numpy as jnp
from jax import lax
from jax.experimental import pallas as pl
from jax.experimental.pallas import tpu as pltpu

BF16 = jnp.bfloat16
F32 = jnp.float32

V7X_VMEM_BYTES = 64 * 2**20
VMEM_LIMIT = V7X_VMEM_BYTES - 12 * 2**20
LANES = 128

GRID_W = 64
ROPE_DIM = 64
ROPE_BASE = 10000.0
NORM_EPS = 1e-6
DIFF_HEAD_DIM = 64
S5_CH = 16
S5_STATE = 64
S5_CHUNK = 16
MLA_NOPE = 128
MLA_V = 128
RET_K = 64
RET_V = 128
RET_CHUNK = 128
MOE_GROUPS = 4
MOE_PER_GROUP = 4
MOE_TOPK = 2
MOE_TILE = 256


def _cp(*sem):
    return pltpu.CompilerParams(dimension_semantics=sem, vmem_limit_bytes=VMEM_LIMIT)


def _pick_tile(n, target, mult=16):
    best = None
    for t in range(mult, min(n, target) + 1, mult):
        if n % t == 0:
            best = t
    assert best is not None, (n, target)
    return best


def _sigmoid(x):
    return 1.0 / (1.0 + jnp.exp(-x))


def _mm_body(*refs, nx, ksizes, has_norm, has_bias, epilogue, tm, tiles_per_batch, ctx_len):
    x_refs = refs[:nx]
    w_ref = refs[nx]
    idx = nx + 1
    g_ref = b_ref = e_ref = res_ref = gate_ref = None
    if has_norm:
        g_ref = refs[idx]; idx += 1
    if has_bias:
        b_ref = refs[idx]; idx += 1
    if epilogue == "glu":
        e_ref = refs[idx]; idx += 1
    if epilogue == "resgate":
        res_ref, gate_ref = refs[idx], refs[idx + 1]; idx += 2
    o_ref = refs[idx]

    acc = None
    off = 0
    for xr, ks in zip(x_refs, ksizes):
        x = xr[...]
        if has_norm:
            xf = x.astype(F32)
            xf = xf * lax.rsqrt(jnp.mean(xf * xf, axis=-1, keepdims=True) + NORM_EPS)
            x = xf * g_ref[...]
        x = x.astype(BF16)
        w = w_ref[off:off + ks, :].astype(BF16)
        part = jnp.dot(x, w, preferred_element_type=F32)
        acc = part if acc is None else acc + part
        off += ks
    if has_bias:
        acc = acc + b_ref[...]
    if epilogue == "glu":
        acc = e_ref[...].astype(F32) * _sigmoid(acc)
    elif epilogue == "resgate":
        i = pl.program_id(0)
        row = (i % tiles_per_batch) * tm + lax.broadcasted_iota(jnp.int32, (tm, 1), 0)
        gate = jnp.where(row < ctx_len, gate_ref[0, 0:1, :], gate_ref[0, 1:2, :])
        acc = res_ref[...] + gate * acc
    o_ref[...] = acc.astype(o_ref.dtype)


def _mm(xs, w, *, name, out_dtype, tm, tn, norm_g=None, bias=None, glu_in=None, res=None, gate=None,
        rows_per_batch=None, ctx_len=0):
    m = xs[0].shape[0]
    ksizes = tuple(x.shape[1] for x in xs)
    k, n = w.shape
    assert sum(ksizes) == k and m % tm == 0 and n % tn == 0
    epilogue = "glu" if glu_in is not None else ("resgate" if res is not None else None)
    tiles_per_batch = (rows_per_batch // tm) if rows_per_batch else 1
    in_specs = [pl.BlockSpec((tm, ks), lambda i, j: (i, 0)) for ks in ksizes]
    in_specs.append(pl.BlockSpec((k, tn), lambda i, j: (0, j)))
    args = list(xs) + [w]
    if norm_g is not None:
        in_specs.append(pl.BlockSpec((1, k), lambda i, j: (0, 0)))
        args.append(norm_g.reshape(1, k).astype(F32))
    if bias is not None:
        in_specs.append(pl.BlockSpec((1, tn), lambda i, j: (0, j)))
        args.append(bias.reshape(1, n).astype(F32))
    if epilogue == "glu":
        in_specs.append(pl.BlockSpec((tm, tn), lambda i, j: (i, j)))
        args.append(glu_in)
    if epilogue == "resgate":
        tpb = tiles_per_batch
        in_specs.append(pl.BlockSpec((tm, tn), lambda i, j: (i, j)))
        in_specs.append(pl.BlockSpec((1, 2, tn), lambda i, j: (i // tpb, 0, j)))
        args += [res, gate]
    body = functools.partial(_mm_body, nx=len(xs), ksizes=ksizes, has_norm=norm_g is not None,
                             has_bias=bias is not None, epilogue=epilogue, tm=tm,
                             tiles_per_batch=tiles_per_batch, ctx_len=ctx_len)
    return pl.pallas_call(
        body,
        grid=(m // tm, n // tn),
        in_specs=in_specs,
        out_specs=pl.BlockSpec((tm, tn), lambda i, j: (i, j)),
        out_shape=jax.ShapeDtypeStruct((m, n), out_dtype),
        compiler_params=_cp("parallel", "arbitrary"),
        name=name,
    )(*args)


def _norm_mod_body(*refs, tm, tiles_per_batch, ctx_len, router):
    if router:
        x_ref, g_ref, sh_ref, sc_ref, whi_ref, wlo_ref, br_ref, h_ref, lg_ref = refs
    else:
        x_ref, g_ref, sh_ref, sc_ref, h_ref = refs
    i = pl.program_id(0)
    x = x_ref[...]
    y = x * lax.rsqrt(jnp.mean(x * x, axis=-1, keepdims=True) + NORM_EPS) * g_ref[...]
    is_ctx = (i % tiles_per_batch) * tm < ctx_len
    sh = jnp.where(is_ctx, sh_ref[0, 0:1, :], sh_ref[0, 1:2, :])
    sc = jnp.where(is_ctx, sc_ref[0, 0:1, :], sc_ref[0, 1:2, :])
    h = y * (1.0 + sc) + sh
    h_ref[...] = h.astype(BF16)
    if router:
        hi = h.astype(BF16)
        lo = (h - hi.astype(F32)).astype(BF16)
        lg = jnp.dot(hi, whi_ref[...], preferred_element_type=F32)
        lg = lg + jnp.dot(hi, wlo_ref[...], preferred_element_type=F32)
        lg = lg + jnp.dot(lo, whi_ref[...], preferred_element_type=F32)
        lg_ref[...] = lg + br_ref[...]


def _norm_mod(x, g, shift, scale, *, rows_per_batch, ctx_len, router=None):
    m, d = x.shape
    tm = _pick_tile(math.gcd(rows_per_batch, ctx_len), 256, 8)
    tpb = rows_per_batch // tm
    in_specs = [
        pl.BlockSpec((tm, d), lambda i: (i, 0)),
        pl.BlockSpec((1, d), lambda i: (0, 0)),
        pl.BlockSpec((1, 2, d), lambda i: (i // tpb, 0, 0)),
        pl.BlockSpec((1, 2, d), lambda i: (i // tpb, 0, 0)),
    ]
    args = [x, g.reshape(1, d), shift, scale]
    out_specs = [pl.BlockSpec((tm, d), lambda i: (i, 0))]
    out_shape = [jax.ShapeDtypeStruct((m, d), BF16)]
    if router is not None:
        whi, wlo, br = router
        in_specs += [pl.BlockSpec((d, LANES), lambda i: (0, 0)),
                     pl.BlockSpec((d, LANES), lambda i: (0, 0)),
                     pl.BlockSpec((1, LANES), lambda i: (0, 0))]
        args += [whi, wlo, br]
        out_specs.append(pl.BlockSpec((tm, LANES), lambda i: (i, 0)))
        out_shape.append(jax.ShapeDtypeStruct((m, LANES), F32))
    body = functools.partial(_norm_mod_body, tm=tm, tiles_per_batch=tpb, ctx_len=ctx_len,
                             router=router is not None)
    outs = pl.pallas_call(body, grid=(m // tm,), in_specs=in_specs, out_specs=out_specs,
                          out_shape=out_shape, compiler_params=_cp("parallel"),
                          name="norm_mod_router" if router is not None else "norm_mod")(*args)
    return outs if router is not None else outs[0]


def _final_norm_body(x_ref, g_ref, o_ref):
    x = x_ref[...]
    o_ref[...] = x * lax.rsqrt(jnp.mean(x * x, axis=-1, keepdims=True) + NORM_EPS) * g_ref[...]


def _final_norm(x, g, *, batch, rows_per_batch, ctx_len):
    d = x.shape[1]
    n_lat = rows_per_batch - ctx_len
    tm = _pick_tile(math.gcd(n_lat, ctx_len), 256, 8)
    tpb, ctx_tiles, lat_tiles = rows_per_batch // tm, ctx_len // tm, n_lat // tm
    out = pl.pallas_call(
        _final_norm_body,
        grid=(batch, lat_tiles),
        in_specs=[pl.BlockSpec((tm, d), lambda b, i: (b * tpb + ctx_tiles + i, 0)),
                  pl.BlockSpec((1, d), lambda b, i: (0, 0))],
        out_specs=pl.BlockSpec((tm, d), lambda b, i: (b * lat_tiles + i, 0)),
        out_shape=jax.ShapeDtypeStruct((batch * n_lat, d), F32),
        compiler_params=_cp("parallel", "parallel"),
        name="final_norm",
    )(x, g.reshape(1, d))
    return out.reshape(batch, n_lat, d)


def _rope_body(x_ref, cos_ref, sa_ref, sb_ref, cs_ref, o_ref, *, reps):
    x = x_ref[...].astype(F32)
    width = x.shape[1]
    x_dn = pltpu.roll(x, width - 16, 1)
    x_up = pltpu.roll(x, 16, 1)
    tile = lambda r: jnp.tile(r[...], (1, reps))
    y = x * tile(cos_ref) + x_dn * tile(sa_ref) + x_up * tile(sb_ref)
    o_ref[...] = (y * cs_ref[...]).astype(o_ref.dtype)


def _rope(x, ncols, tables, col_scale, *, rows_per_batch, col_off=0):
    m = x.shape[0]
    cos, sa, sb = tables
    wb = max(w for w in (512, 256, 128) if ncols % w == 0 and col_off % w == 0)
    tm = _pick_tile(rows_per_batch, 1088)
    tpb = rows_per_batch // tm
    ob = col_off // wb
    body = functools.partial(_rope_body, reps=wb // LANES)
    tspec = pl.BlockSpec((tm, LANES), lambda i, j: (i % tpb, 0))
    return pl.pallas_call(
        body,
        grid=(m // tm, ncols // wb),
        in_specs=[pl.BlockSpec((tm, wb), lambda i, j: (i, j + ob)), tspec, tspec, tspec,
                  pl.BlockSpec((1, wb), lambda i, j: (0, j))],
        out_specs=pl.BlockSpec((tm, wb), lambda i, j: (i, j)),
        out_shape=jax.ShapeDtypeStruct((m, ncols), x.dtype),
        compiler_params=_cp("parallel", "arbitrary"),
        name="rope",
    )(x, cos, sa, sb, col_scale.reshape(1, ncols).astype(F32))


def _rope_tables(n_lat, ctx_len):
    rows = n_lat // GRID_W
    row = jnp.repeat(jnp.arange(rows, dtype=F32), GRID_W)
    col = jnp.tile(jnp.arange(GRID_W, dtype=F32), rows)
    quarter = ROPE_DIM // 4
    inv = ROPE_BASE ** (-jnp.arange(quarter, dtype=F32) / quarter)
    ar = row[:, None] * inv
    ac = col[:, None] * inv
    ang = jnp.concatenate([ar, ar, ac, ac], axis=-1)
    ang = jnp.concatenate([jnp.zeros((ctx_len, ROPE_DIM), F32), ang], axis=0)
    ang = jnp.tile(ang, (1, LANES // ROPE_DIM))
    cos, sin = jnp.cos(ang), jnp.sin(ang)
    lane = jnp.arange(LANES)
    even = ((lane // quarter) % 2 == 0)[None, :]
    sa = jnp.where(even, -sin, 0.0)
    sb = jnp.where(even, 0.0, sin)
    return cos, sa, sb


def _softmax_pv(q, k, v, scale):
    s = lax.dot_general(q, k, (((1,), (1,)), ((), ())), preferred_element_type=F32) * scale
    m = jnp.max(s, axis=-1, keepdims=True)
    e = jnp.exp(s - m)
    l = jnp.sum(e, axis=-1, keepdims=True)
    return jnp.dot(e.astype(BF16), v, preferred_element_type=F32) / l


def _diff_attn_body(lam_ref, q_ref, k_ref, v_ref, sub_ref, o_ref, *, tq, ctx_len, scale, post):
    qi = pl.program_id(2)
    q = q_ref[...]
    lane = lax.broadcasted_iota(jnp.int32, (1, LANES), 1)
    first = lane < DIFF_HEAD_DIM
    zero = jnp.zeros_like(q)
    q0 = jnp.where(first, q, zero)
    q1 = jnp.where(first, zero, q)
    lam = lam_ref[0]

    def attend(k, v):
        o = _softmax_pv(q0, k, v, scale) - lam * _softmax_pv(q1, k, v, scale)
        o = o * lax.rsqrt(jnp.mean(o * o, axis=-1, keepdims=True) + NORM_EPS) * sub_ref[...] * post
        o_ref[...] = o.astype(o_ref.dtype)

    @pl.when(qi * tq < ctx_len)
    def _():
        attend(k_ref[0:ctx_len, :], v_ref[0:ctx_len, :])

    @pl.when(qi * tq >= ctx_len)
    def _():
        attend(k_ref[...], v_ref[...])


def _diff_attn(qk, proj, lam, subln, *, batch, seq, ctx_len, heads, q_blk, k_blk, v_blk, post):
    tq = _pick_tile(math.gcd(seq, ctx_len), 256)
    tpb = seq // tq
    body = functools.partial(_diff_attn_body, tq=tq, ctx_len=ctx_len,
                             scale=DIFF_HEAD_DIM ** -0.5, post=post)
    return pl.pallas_call(
        body,
        grid=(batch, heads, tpb),
        in_specs=[
            pl.BlockSpec(memory_space=pltpu.SMEM),
            pl.BlockSpec((tq, LANES), lambda b, h, i: (b * tpb + i, q_blk + h)),
            pl.BlockSpec((seq, LANES), lambda b, h, i: (b, k_blk + h)),
            pl.BlockSpec((seq, LANES), lambda b, h, i: (b, v_blk + h)),
            pl.BlockSpec((1, LANES), lambda b, h, i: (0, 0)),
        ],
        out_specs=pl.BlockSpec((tq, LANES), lambda b, h, i: (b * tpb + i, h)),
        out_shape=jax.ShapeDtypeStruct((batch * seq, heads * LANES), BF16),
        compiler_params=_cp("parallel", "parallel", "arbitrary"),
        name="diff_attn",
    )(lam.reshape(1).astype(F32), qk, qk, proj, subln.reshape(1, LANES).astype(F32))


def _mla_attn_body(qn_ref, qr_ref, kn_ref, kr_ref, v_ref, o_ref, *, tq, ctx_len, scale):
    h = pl.program_id(1)
    qi = pl.program_id(2)
    qn = qn_ref[...]
    qr = qr_ref[...]
    lane = lax.broadcasted_iota(jnp.int32, (1, LANES), 1)
    mine = (lane < ROPE_DIM) == (h % 2 == 0)
    qr = jnp.where(mine, qr, jnp.zeros_like(qr))

    def attend(kn, kr, v):
        dn = (((1,), (1,)), ((), ()))
        s = lax.dot_general(qn, kn, dn, preferred_element_type=F32)
        s = (s + lax.dot_general(qr, kr, dn, preferred_element_type=F32)) * scale
        m = jnp.max(s, axis=-1, keepdims=True)
        e = jnp.exp(s - m)
        l = jnp.sum(e, axis=-1, keepdims=True)
        o = jnp.dot(e.astype(BF16), v, preferred_element_type=F32) / l
        o_ref[...] = o.astype(o_ref.dtype)

    @pl.when(qi * tq < ctx_len)
    def _():
        attend(kn_ref[0:ctx_len, :], kr_ref[0:ctx_len, :], v_ref[0:ctx_len, :])

    @pl.when(qi * tq >= ctx_len)
    def _():
        attend(kn_ref[...], kr_ref[...], v_ref[...])


def _mla_attn(q_up, q_rope, kv_up, k_rope, *, batch, seq, ctx_len, heads):
    tq = _pick_tile(math.gcd(seq, ctx_len), 256)
    tpb = seq // tq
    body = functools.partial(_mla_attn_body, tq=tq, ctx_len=ctx_len,
                             scale=(MLA_NOPE + ROPE_DIM) ** -0.5)
    return pl.pallas_call(
        body,
        grid=(batch, heads, tpb),
        in_specs=[
            pl.BlockSpec((tq, LANES), lambda b, h, i: (b * tpb + i, h)),
            pl.BlockSpec((tq, LANES), lambda b, h, i: (b * tpb + i, h // 2)),
            pl.BlockSpec((seq, LANES), lambda b, h, i: (b, 2 * h)),
            pl.BlockSpec((seq, LANES), lambda b, h, i: (b, 0)),
            pl.BlockSpec((seq, LANES), lambda b, h, i: (b, 2 * h + 1)),
        ],
        out_specs=pl.BlockSpec((tq, LANES), lambda b, h, i: (b * tpb + i, h)),
        out_shape=jax.ShapeDtypeStruct((batch * seq, heads * LANES), BF16),
        compiler_params=_cp("parallel", "parallel", "arbitrary"),
        name="mla_attn",
    )(q_up, q_rope, kv_up, k_rope, kv_up)


def _retention_body(*refs, backward, cs):
    if backward:
        lg_ref, q_ref, k_ref, v_ref, yf_ref, gate_ref, ng_ref, o_ref, s_ref = refs
    else:
        lg_ref, q_ref, k_ref, v_ref, o_ref, s_ref = refs
    h = pl.program_id(1)
    t = pl.program_id(2)

    @pl.when(t == 0)
    def _():
        s_ref[...] = jnp.zeros_like(s_ref)

    lg = lg_ref[h]
    lane = lax.broadcasted_iota(jnp.int32, (1, LANES), 1)
    mine = (lane < RET_K) == (h % 2 == 0)
    q = q_ref[...]
    q = jnp.where(mine, q, jnp.zeros_like(q))
    k = k_ref[...]
    v = v_ref[...]
    pos_r = lax.broadcasted_iota(jnp.int32, (cs, cs), 0).astype(F32)
    pos_c = lax.broadcasted_iota(jnp.int32, (cs, cs), 1).astype(F32)
    pos = lax.broadcasted_iota(jnp.int32, (cs, 1), 0).astype(F32)
    if backward:
        diff = pos_c - pos_r
        keep = diff > 0
        q_dec = jnp.exp(lg * (cs - pos))
        k_dec = jnp.exp(lg * pos)
    else:
        diff = pos_r - pos_c
        keep = diff >= 0
        q_dec = jnp.exp(lg * (pos + 1.0))
        k_dec = jnp.exp(lg * (cs - 1.0 - pos))
    dmat = jnp.where(keep, jnp.exp(lg * jnp.maximum(diff, 0.0)), 0.0)
    scores = lax.dot_general(q, k, (((1,), (1,)), ((), ())), preferred_element_type=F32) * dmat
    intra = jnp.dot(scores.astype(BF16), v, preferred_element_type=F32)
    state = s_ref[...]
    q_w = (q.astype(F32) * q_dec).astype(BF16)
    cross = jnp.dot(q_w, state.astype(BF16), preferred_element_type=F32)
    k_w = (k.astype(F32) * k_dec).astype(BF16)
    upd = lax.dot_general(k_w, v, (((0,), (0,)), ((), ())), preferred_element_type=F32)
    s_ref[...] = jnp.exp(lg * cs) * state + upd
    y = intra + cross
    if backward:
        y = y + yf_ref[...]
        y = y * lax.rsqrt(jnp.mean(y * y, axis=-1, keepdims=True) + NORM_EPS) * ng_ref[...]
        g = gate_ref[...].astype(F32)
        o_ref[...] = (g * _sigmoid(g) * y).astype(o_ref.dtype)
    else:
        o_ref[...] = y


def _retention(qk, proj, log_g, norm_g, *, batch, seq, ctx_len, heads, q_blk, k_blk, v_blk, g_blk):
    cs = RET_CHUNK
    nc, nc_ctx = seq // cs, ctx_len // cs
    smem = pl.BlockSpec(memory_space=pltpu.SMEM)

    def fwd_chunk(t):
        return t

    def bwd_chunk(t):
        return jnp.where(t < nc_ctx, nc_ctx - 1 - t, nc - 1 - (t - nc_ctx))

    def specs(chunk):
        row = lambda b, t: b * nc + chunk(t)
        return (pl.BlockSpec((cs, LANES), lambda b, h, t: (row(b, t), q_blk + h // 2)),
                pl.BlockSpec((cs, LANES), lambda b, h, t: (row(b, t), k_blk + h // 2)),
                pl.BlockSpec((cs, LANES), lambda b, h, t: (row(b, t), v_blk + h)),
                pl.BlockSpec((cs, LANES), lambda b, h, t: (row(b, t), h)),
                pl.BlockSpec((cs, LANES), lambda b, h, t: (row(b, t), g_blk + h)))

    qs, ks, vs, ys, gs = specs(fwd_chunk)
    y_f = pl.pallas_call(
        functools.partial(_retention_body, backward=False, cs=cs),
        grid=(batch, heads, nc),
        in_specs=[smem, qs, ks, vs],
        out_specs=ys,
        out_shape=jax.ShapeDtypeStruct((batch * seq, heads * LANES), F32),
        scratch_shapes=[pltpu.VMEM((LANES, RET_V), F32)],
        compiler_params=_cp("parallel", "parallel", "arbitrary"),
        name="retention_fwd",
    )(log_g[0].astype(F32), qk, qk, proj)
    qs, ks, vs, ys, gs = specs(bwd_chunk)
    return pl.pallas_call(
        functools.partial(_retention_body, backward=True, cs=cs),
        grid=(batch, heads, nc),
        in_specs=[smem, qs, ks, vs, ys, gs, pl.BlockSpec((1, LANES), lambda b, h, t: (0, h))],
        out_specs=ys,
        out_shape=jax.ShapeDtypeStruct((batch * seq, heads * LANES), BF16),
        scratch_shapes=[pltpu.VMEM((LANES, RET_V), F32)],
        compiler_params=_cp("parallel", "parallel", "arbitrary"),
        name="retention_bwd",
    )(log_g[1].astype(F32), qk, qk, proj, y_f, proj, norm_g.reshape(1, heads * LANES).astype(F32))


def _s5_params(a_re, a_im, log_dt, b_re, b_im, c_re, c_im):
    L, ch = S5_CHUNK, S5_CH
    a_re, a_im = a_re.astype(F32), a_im.astype(F32)
    dt = jnp.exp(log_dt.astype(F32))[..., None]
    e = jnp.arange(L + 1, dtype=F32)[:, None, None, None]
    mag = jnp.exp(a_re * dt * e)
    pw_re, pw_im = mag * jnp.cos(a_im * dt * e), mag * jnp.sin(a_im * dt * e)
    ab_re, ab_im = pw_re[1], pw_im[1]
    den = a_re * a_re + a_im * a_im
    f_re = ((ab_re - 1.0) * a_re + ab_im * a_im) / den
    f_im = (ab_im * a_re - (ab_re - 1.0) * a_im) / den
    bb_re = f_re[..., None] * b_re - f_im[..., None] * b_im
    bb_im = f_re[..., None] * b_im + f_im[..., None] * b_re
    c_re, c_im = c_re.astype(F32), c_im.astype(F32)

    cp_re = c_re[None] * pw_re[:L, :, :, None, :] - c_im[None] * pw_im[:L, :, :, None, :]
    cp_im = c_re[None] * pw_im[:L, :, :, None, :] + c_im[None] * pw_re[:L, :, :, None, :]
    hp = lax.Precision.HIGHEST
    kmat = (jnp.einsum("ldgcp,dgpk->dglck", cp_re, bb_re, precision=hp)
            - jnp.einsum("ldgcp,dgpk->dglck", cp_im, bb_im, precision=hp))
    idx = jnp.arange(L)
    lag_f = idx[None, :] - idx[:, None]
    lag_b = -lag_f

    def toeplitz(k, lag):
        t = k[:, jnp.clip(lag, 0, L - 1)]
        t = jnp.where((lag >= 0)[None, :, :, None, None], t, 0.0)
        return t.transpose(0, 1, 4, 2, 3).reshape(k.shape[0], L * ch, L * ch)

    m_both = jnp.concatenate([toeplitz(kmat[0], lag_f), toeplitz(kmat[1], lag_b)], axis=-1)

    def state_in(d, exps):
        p_re, p_im = pw_re[exps, d], pw_im[exps, d]
        w_re = p_re[..., None] * bb_re[d][None] - p_im[..., None] * bb_im[d][None]
        w_im = p_re[..., None] * bb_im[d][None] + p_im[..., None] * bb_re[d][None]
        w = jnp.concatenate([w_re, w_im], axis=2)
        return w.transpose(1, 0, 3, 2).reshape(w.shape[1], L * ch, 2 * S5_STATE)

    w_both = jnp.concatenate([state_in(0, L - 1 - idx), state_in(1, idx)], axis=-1)

    def state_out(d, exps):
        p_re, p_im = pw_re[exps, d], pw_im[exps, d]
        v_re = c_re[d][None] * p_re[:, :, None, :] - c_im[d][None] * p_im[:, :, None, :]
        v_im = c_re[d][None] * p_im[:, :, None, :] + c_im[d][None] * p_re[:, :, None, :]
        v = jnp.concatenate([v_re, -v_im], axis=-1)
        return v.transpose(1, 3, 0, 2).reshape(v.shape[1], 2 * S5_STATE, L * ch)

    v_mat = jnp.stack([state_out(0, idx + 1), state_out(1, L - idx)])
    a_pow = jnp.concatenate([pw_re[L], pw_im[L]], axis=-1)
    return m_both.astype(BF16), w_both.astype(BF16), v_mat.astype(BF16), a_pow


def _s5_state_in_body(u_ref, w_ref, o_ref):
    s = jnp.dot(u_ref[0], w_ref[0], preferred_element_type=F32)
    o_ref[0] = s[:, :LANES]
    o_ref[1] = s[:, LANES:]


def _s5_scan_body(s_ref, a_ref, o_ref, *, nk, nk_ctx):
    d = pl.program_id(0)
    a = a_ref[0]
    lane = lax.broadcasted_iota(jnp.int32, a.shape, 1)
    first = lane < S5_STATE
    a_sw = pltpu.roll(a, S5_STATE, 1)
    a_same = jnp.where(first, a, a_sw)
    a_cross = jnp.where(first, -a_sw, a)
    a_same, a_cross = a_same[None], a_cross[None]

    def step(k, hstate):
        o_ref[0, k] = hstate
        return a_same * hstate + a_cross * pltpu.roll(hstate, S5_STATE, 2) + s_ref[0, k]

    zero = jnp.zeros(o_ref.shape[2:], F32)

    @pl.when(d == 0)
    def _():
        lax.fori_loop(0, nk, step, zero)

    @pl.when(d == 1)
    def _():
        hc = lax.fori_loop(0, nk_ctx, lambda t, hs: step(nk_ctx - 1 - t, hs), zero)
        lax.fori_loop(0, nk - nk_ctx, lambda t, hs: step(nk - 1 - t, hs), hc)


def _s5_out_body(u_ref, m_ref, h_ref, v_ref, d_ref, o_ref):
    u = u_ref[0]
    y = jnp.dot(u, m_ref[0], preferred_element_type=F32)
    y = y[:, :2 * LANES] + y[:, 2 * LANES:]
    y = y + jnp.dot(h_ref[0].astype(BF16), v_ref[0, 0], preferred_element_type=F32)
    y = y + jnp.dot(h_ref[1].astype(BF16), v_ref[1, 0], preferred_element_type=F32)
    y = y + d_ref[0] * u.astype(F32)
    o_ref[0] = jax.nn.gelu(y).astype(o_ref.dtype)


def _s5_mix(proj, su_off, params, d_skip, *, batch, seq, ctx_len, gw):
    m_both, w_both, v_mat, a_pow = params
    L, ch, st2 = S5_CHUNK, S5_CH, 2 * S5_STATE
    groups = gw // ch
    nk, nk_ctx = seq // L, ctx_len // L
    rows = nk * batch
    cols = L * ch
    u = proj[:, su_off:su_off + gw].reshape(batch, nk, L, groups, ch)
    u = u.transpose(3, 1, 0, 2, 4).reshape(groups, rows, cols)

    s_in = pl.pallas_call(
        _s5_state_in_body,
        grid=(groups,),
        in_specs=[pl.BlockSpec((1, rows, cols), lambda g: (g, 0, 0)),
                  pl.BlockSpec((1, cols, 2 * st2), lambda g: (g, 0, 0))],
        out_specs=pl.BlockSpec((2, rows, st2), lambda g: (0, 0, g)),
        out_shape=jax.ShapeDtypeStruct((2, rows, groups * st2), F32),
        compiler_params=_cp("parallel"),
        name="s5_state_in",
    )(u, w_both)

    gb = 8
    s4 = s_in.reshape(2, nk, batch, groups, st2)
    h_prev = pl.pallas_call(
        functools.partial(_s5_scan_body, nk=nk, nk_ctx=nk_ctx),
        grid=(2, groups // gb),
        in_specs=[pl.BlockSpec((1, nk, batch, gb, st2), lambda d, g: (d, 0, 0, g, 0)),
                  pl.BlockSpec((1, gb, st2), lambda d, g: (d, g, 0))],
        out_specs=pl.BlockSpec((1, nk, batch, gb, st2), lambda d, g: (d, 0, 0, g, 0)),
        out_shape=jax.ShapeDtypeStruct((2, nk, batch, groups, st2), F32),
        compiler_params=_cp("parallel", "parallel"),
        name="s5_scan",
    )(s4, a_pow)
    h2 = h_prev.reshape(2, rows, groups * st2)

    d_rep = jnp.tile(d_skip.astype(F32), (1, L)).reshape(groups, 1, cols)
    y = pl.pallas_call(
        _s5_out_body,
        grid=(groups,),
        in_specs=[pl.BlockSpec((1, rows, cols), lambda g: (g, 0, 0)),
                  pl.BlockSpec((1, cols, 2 * cols), lambda g: (g, 0, 0)),
                  pl.BlockSpec((2, rows, st2), lambda g: (0, 0, g)),
                  pl.BlockSpec((2, 1, st2, cols), lambda g: (0, g, 0, 0)),
                  pl.BlockSpec((1, 1, cols), lambda g: (g, 0, 0))],
        out_specs=pl.BlockSpec((1, rows, cols), lambda g: (g, 0, 0)),
        out_shape=jax.ShapeDtypeStruct((groups, rows, cols), BF16),
        compiler_params=_cp("parallel"),
        name="s5_out",
    )(u, m_both, h2, v_mat, d_rep)
    y = y.reshape(groups, nk, batch, L, ch).transpose(2, 1, 3, 0, 4)
    return y.reshape(batch * seq, gw)


def _moe_up_body(te_ref, x_ref, wg_ref, wu_ref, o_ref):
    x = x_ref[...]
    a = jnp.dot(x, wg_ref[0], preferred_element_type=F32)
    b = jnp.dot(x, wu_ref[0], preferred_element_type=F32)
    o_ref[...] = (a * _sigmoid(a) * b).astype(o_ref.dtype)


def _moe_down_body(te_ref, h_ref, w_ref, rw_ref, o_ref):
    y = jnp.dot(h_ref[...], w_ref[0], preferred_element_type=F32)
    o_ref[...] = rw_ref[...] * y


def _moe_combine_body(x_ref, y0_ref, y1_ref, gate_ref, o_ref, *, tm, tiles_per_batch, ctx_len):
    i = pl.program_id(0)
    is_ctx = (i % tiles_per_batch) * tm < ctx_len
    gate = jnp.where(is_ctx, gate_ref[0, 0:1, :], gate_ref[0, 1:2, :])
    o_ref[...] = x_ref[...] + gate * (y0_ref[...] + y1_ref[...])


def _route(logits):
    g_logit = logits[:, :MOE_GROUPS]
    g_prob = jax.nn.softmax(g_logit, axis=-1)
    g_p, g_idx = lax.top_k(g_prob, 1)
    e_logit = logits[:, MOE_GROUPS:MOE_GROUPS + MOE_GROUPS * MOE_PER_GROUP]
    e_logit = e_logit.reshape(-1, MOE_GROUPS, MOE_PER_GROUP)
    e_logit = jnp.take_along_axis(e_logit, g_idx[:, :, None], axis=1)[:, 0]
    e_p, e_idx = lax.top_k(jax.nn.softmax(e_logit, axis=-1), MOE_TOPK)
    w = g_p * e_p / jnp.sum(e_p, axis=-1, keepdims=True)
    ids = g_idx * MOE_PER_GROUP + e_idx
    return ids, w


def _moe(h, logits, x, gate, w_gate, w_up, w_down, *, rows_per_batch, ctx_len):
    t, d = h.shape
    n_exp, _, dff = w_gate.shape
    tile = MOE_TILE
    ids, wts = _route(logits)
    flat_e = ids.reshape(-1)
    onehot = (flat_e[:, None] == jnp.arange(n_exp)[None, :]).astype(jnp.int32)
    counts = onehot.sum(0)
    rank = jnp.take_along_axis(jnp.cumsum(onehot, axis=0) - onehot, flat_e[:, None], axis=1)[:, 0]
    padded = (counts + tile - 1) // tile * tile
    starts = jnp.cumsum(padded) - padded
    pos = starts[flat_e] + rank
    n_rows = (t * MOE_TOPK // tile + n_exp) * tile
    n_tiles = n_rows // tile
    src = jnp.zeros((n_rows,), jnp.int32).at[pos].set(jnp.arange(t * MOE_TOPK, dtype=jnp.int32) // MOE_TOPK)
    row_w = jnp.zeros((n_rows,), F32).at[pos].set(wts.reshape(-1))
    tile_start = jnp.arange(n_tiles, dtype=jnp.int32) * tile
    ends = starts + padded
    tile_e = jnp.minimum(jnp.sum(tile_start[:, None] >= ends[None, :], axis=1), n_exp - 1).astype(jnp.int32)

    xs = jnp.take(h, src, axis=0)
    hid = pl.pallas_call(
        _moe_up_body,
        grid_spec=pltpu.PrefetchScalarGridSpec(
            num_scalar_prefetch=1,
            grid=(n_tiles,),
            in_specs=[pl.BlockSpec((tile, d), lambda i, te: (i, 0)),
                      pl.BlockSpec((1, d, dff), lambda i, te: (te[i], 0, 0)),
                      pl.BlockSpec((1, d, dff), lambda i, te: (te[i], 0, 0))],
            out_specs=pl.BlockSpec((tile, dff), lambda i, te: (i, 0))),
        out_shape=jax.ShapeDtypeStruct((n_rows, dff), BF16),
        compiler_params=_cp("arbitrary"),
        name="moe_up",
    )(tile_e, xs, w_gate, w_up)
    ys = pl.pallas_call(
        _moe_down_body,
        grid_spec=pltpu.PrefetchScalarGridSpec(
            num_scalar_prefetch=1,
            grid=(n_tiles,),
            in_specs=[pl.BlockSpec((tile, dff), lambda i, te: (i, 0)),
                      pl.BlockSpec((1, dff, d), lambda i, te: (te[i], 0, 0)),
                      pl.BlockSpec((tile, 1), lambda i, te: (i, 0))],
            out_specs=pl.BlockSpec((tile, d), lambda i, te: (i, 0))),
        out_shape=jax.ShapeDtypeStruct((n_rows, d), F32),
        compiler_params=_cp("arbitrary"),
        name="moe_down",
    )(tile_e, hid, w_down, row_w.reshape(n_rows, 1))

    pos2 = pos.reshape(t, MOE_TOPK)
    y0 = jnp.take(ys, pos2[:, 0], axis=0)
    y1 = jnp.take(ys, pos2[:, 1], axis=0)
    tm = _pick_tile(math.gcd(rows_per_batch, ctx_len), 256, 8)
    tpb = rows_per_batch // tm
    row_spec = pl.BlockSpec((tm, d), lambda i: (i, 0))
    return pl.pallas_call(
        functools.partial(_moe_combine_body, tm=tm, tiles_per_batch=tpb, ctx_len=ctx_len),
        grid=(t // tm,),
        in_specs=[row_spec, row_spec, row_spec, pl.BlockSpec((1, 2, d), lambda i: (i // tpb, 0, 0))],
        out_specs=row_spec,
        out_shape=jax.ShapeDtypeStruct((t, d), F32),
        compiler_params=_cp("parallel"),
        name="moe_combine",
    )(x, y0, y1, gate)


def kernel(x, c, ctx, c_ctx, ada_w, ada_b, norm_mix, norm_ffn, w_in, w_out, diff_lambda, diff_subln,
           s5_a_re, s5_a_im, s5_log_dt, s5_b_re, s5_b_im, s5_c_re, s5_c_im, s5_d, s5_glu_w, s5_glu_b,
           mla_q_norm, mla_kv_norm, mla_w_uq, mla_w_ukv, ret_decay, ret_norm,
           moe_wg, moe_bg, moe_we, moe_be, moe_w_gate, moe_w_up, moe_w_down, final_norm):
    batch, n_lat, d = x.shape
    ctx_len = ctx.shape[1]
    depth = ada_w.shape[0]
    seq = ctx_len + n_lat
    rows = batch * seq
    gw = d // 4
    heads = gw // LANES
    q_rank, kv_rank = 3 * d // 16, d // 16
    ret_qk = heads * RET_K
    n_route = MOE_GROUPS + MOE_GROUPS * MOE_PER_GROUP
    assert heads % 2 == 0 and ctx_len % RET_CHUNK == 0 and n_lat % RET_CHUNK == 0

    splits = (gw, gw, gw, gw, q_rank, kv_rank, ROPE_DIM, ret_qk, ret_qk, gw, gw)
    offs = [0]
    for s_ in splits:
        offs.append(offs[-1] + s_)
    names = ("dq", "dk", "dv", "su", "mcq", "mckv", "mkr", "rq", "rk", "rv", "rg")
    src_col = {n_: (offs[i], offs[i + 1]) for i, n_ in enumerate(names)}
    order = ("dq", "dk", "rq", "rk", "dv", "su", "rv", "rg", "mcq", "mckv")
    col = {}
    pos = 0
    for n_ in order:
        col[n_] = pos
        pos += src_col[n_][1] - src_col[n_][0]
    n_main = pos
    n_rope = col["dv"]
    perm = jnp.concatenate([jnp.arange(*src_col[n_]) for n_ in order])
    kr_cols = jnp.concatenate([jnp.arange(*src_col["mkr"])] * (LANES // ROPE_DIM))
    uq_cols = jnp.arange(heads * (MLA_NOPE + ROPE_DIM)).reshape(heads, MLA_NOPE + ROPE_DIM)
    uq_perm = jnp.concatenate([uq_cols[:, :MLA_NOPE].reshape(-1), uq_cols[:, MLA_NOPE:].reshape(-1)])

    tables = _rope_tables(n_lat, ctx_len)
    rope_scale = jnp.ones((n_rope,), F32).at[col["rk"]:col["rk"] + ret_qk].set(RET_K ** -0.5)

    cond = jnp.concatenate([c_ctx[None, :], c], axis=0)
    cond = jnp.pad(cond * _sigmoid(cond), ((0, 8 - (batch + 1) % 8 if (batch + 1) % 8 else 0), (0, 0)))

    xa = jnp.concatenate([ctx, x], axis=1).reshape(rows, d)
    tm_big = _pick_tile(seq, 1088)
    tn = lambda n_: _pick_tile(n_, 512, LANES)

    for l in range(depth):
        lam_init = 0.8 - 0.6 * math.exp(-0.3 * l)
        mod = _mm([cond], ada_w[l], name="ada_mod", out_dtype=F32, tm=cond.shape[0], tn=tn(6 * d), bias=ada_b[l])
        mod = mod.reshape(cond.shape[0], 6, d)
        mods = [jnp.stack([jnp.broadcast_to(mod[0, i], (batch, d)), mod[1:batch + 1, i]], axis=1)
                for i in range(6)]

        w_main = w_in[l][:, perm].astype(BF16)
        w_kr = w_in[l][:, kr_cols].astype(BF16)

        h = _norm_mod(xa, norm_mix[l], mods[0], mods[1], rows_per_batch=seq, ctx_len=ctx_len)
        proj = _mm([h], w_main, name="in_proj", out_dtype=BF16, tm=tm_big, tn=tn(n_main))
        kr = _mm([h], w_kr, name="in_proj_kr", out_dtype=BF16, tm=tm_big, tn=LANES)
        qk = _rope(proj, n_rope, tables, rope_scale, rows_per_batch=seq)
        krr = _rope(kr, LANES, tables, jnp.ones((LANES,), F32), rows_per_batch=seq)

        lv = diff_lambda[l].astype(F32)
        lam = jnp.exp(jnp.sum(lv[0] * lv[1])) - jnp.exp(jnp.sum(lv[2] * lv[3])) + lam_init
        a_out = _diff_attn(qk, proj, lam, diff_subln[l], batch=batch, seq=seq, ctx_len=ctx_len,
                           heads=heads, q_blk=col["dq"] // LANES, k_blk=col["dk"] // LANES,
                           v_blk=col["dv"] // LANES, post=1.0 - lam_init)

        s5p = _s5_params(s5_a_re[l], s5_a_im[l], s5_log_dt[l], s5_b_re[l], s5_b_im[l],
                         s5_c_re[l], s5_c_im[l])
        s_act = _s5_mix(proj, col["su"], s5p, s5_d[l], batch=batch, seq=seq, ctx_len=ctx_len, gw=gw)
        s_out = _mm([s_act], s5_glu_w[l].astype(BF16), name="s5_glu", out_dtype=BF16, tm=tm_big, tn=tn(gw),
                    bias=s5_glu_b[l], glu_in=s_act)

        cq = proj[:, col["mcq"]:col["mcq"] + q_rank]
        ckv = proj[:, col["mckv"]:col["mckv"] + kv_rank]
        w_uq = mla_w_uq[l][:, uq_perm].astype(BF16)
        q_up = _mm([cq], w_uq, name="mla_q_up", out_dtype=BF16, tm=tm_big, tn=tn(w_uq.shape[1]), norm_g=mla_q_norm[l])
        kv_up = _mm([ckv], mla_w_ukv[l].astype(BF16), name="mla_kv_up", out_dtype=BF16, tm=tm_big,
                    tn=tn(mla_w_ukv.shape[2]), norm_g=mla_kv_norm[l])
        q_rope = _rope(q_up, heads * ROPE_DIM, tables, jnp.ones((heads * ROPE_DIM,), F32),
                       rows_per_batch=seq, col_off=heads * MLA_NOPE)
        m_out = _mla_attn(q_up, q_rope, kv_up, krr, batch=batch, seq=seq, ctx_len=ctx_len, heads=heads)

        log_g = jax.nn.log_sigmoid(ret_decay[l].astype(F32))
        r_out = _retention(qk, proj, log_g, ret_norm[l], batch=batch, seq=seq, ctx_len=ctx_len,
                           heads=heads, q_blk=col["rq"] // LANES, k_blk=col["rk"] // LANES,
                           v_blk=col["rv"] // LANES, g_blk=col["rg"] // LANES)

        xa = _mm([a_out, s_out, m_out, r_out], w_out[l].astype(BF16), name="out_proj", out_dtype=F32, tm=tm_big, tn=tn(d),
                 res=xa, gate=mods[2], rows_per_batch=seq, ctx_len=ctx_len)

        w_r = jnp.concatenate([moe_wg[l], moe_we[l]], axis=1).astype(F32)
        w_r = jnp.pad(w_r, ((0, 0), (0, LANES - n_route)))
        w_r_hi = w_r.astype(BF16)
        w_r_lo = (w_r - w_r_hi.astype(F32)).astype(BF16)
        b_r = jnp.pad(jnp.concatenate([moe_bg[l], moe_be[l]]).astype(F32), (0, LANES - n_route))
        h, logits = _norm_mod(xa, norm_ffn[l], mods[3], mods[4], rows_per_batch=seq, ctx_len=ctx_len,
                              router=(w_r_hi, w_r_lo, b_r.reshape(1, LANES)))
        xa = _moe(h, logits, xa, mods[5], moe_w_gate[l].astype(BF16), moe_w_up[l].astype(BF16),
                  moe_w_down[l].astype(BF16), rows_per_batch=seq, ctx_len=ctx_len)

    return _final_norm(xa, final_norm, batch=batch, rows_per_batch=seq, ctx_len=ctx_len)
```

```python
import functools
import math

import jax
import jax.numpy as jnp
from jax import lax
from jax.experimental import pallas as pl
from jax.experimental.pallas import tpu as pltpu

BF16 = jnp.bfloat16
F32 = jnp.float32

V7X_VMEM_BYTES = 64 * 2**20
VMEM_LIMIT = V7X_VMEM_BYTES - 12 * 2**20
LANES = 128

GRID_W = 64
ROPE_DIM = 64
ROPE_BASE = 10000.0
NORM_EPS = 1e-6
DIFF_HEAD_DIM = 64
S5_CH = 16
S5_STATE = 64
S5_CHUNK = 16
MLA_NOPE = 128
MLA_V = 128
RET_K = 64
RET_V = 128
RET_CHUNK = 128
MOE_GROUPS = 4
MOE_PER_GROUP = 4
MOE_TOPK = 2
MOE_TILE = 256


def _cp(*sem):
    return pltpu.CompilerParams(dimension_semantics=sem, vmem_limit_bytes=VMEM_LIMIT)


def _pick_tile(n, target, mult=16):
    best = None
    for t in range(mult, min(n, target) + 1, mult):
        if n % t == 0:
            best = t
    assert best is not None, (n, target)
    return best


def _sigmoid(x):
    return 1.0 / (1.0 + jnp.exp(-x))


def _mm_body(*refs, nx, ksizes, has_norm, has_bias, epilogue, tm, tiles_per_batch, ctx_len):
    x_refs = refs[:nx]
    w_ref = refs[nx]
    idx = nx + 1
    g_ref = b_ref = e_ref = res_ref = gate_ref = None
    if has_norm:
        g_ref = refs[idx]; idx += 1
    if has_bias:
        b_ref = refs[idx]; idx += 1
    if epilogue == "glu":
        e_ref = refs[idx]; idx += 1
    if epilogue == "resgate":
        res_ref, gate_ref = refs[idx], refs[idx + 1]; idx += 2
    o_ref = refs[idx]

    acc = None
    off = 0
    for xr, ks in zip(x_refs, ksizes):
        x = xr[...]
        if has_norm:
            xf = x.astype(F32)
            xf = xf * lax.rsqrt(jnp.mean(xf * xf, axis=-1, keepdims=True) + NORM_EPS)
            x = xf * g_ref[...]
        x = x.astype(BF16)
        w = w_ref[off:off + ks, :].astype(BF16)
        part = jnp.dot(x, w, preferred_element_type=F32)
        acc = part if acc is None else acc + part
        off += ks
    if has_bias:
        acc = acc + b_ref[...]
    if epilogue == "glu":
        acc = e_ref[...].astype(F32) * _sigmoid(acc)
    elif epilogue == "resgate":
        i = pl.program_id(0)
        row = (i % tiles_per_batch) * tm + lax.broadcasted_iota(jnp.int32, (tm, 1), 0)
        gate = jnp.where(row < ctx_len, gate_ref[0, 0:1, :], gate_ref[0, 1:2, :])
        acc = res_ref[...] + gate * acc
    o_ref[...] = acc.astype(o_ref.dtype)


def _mm(xs, w, *, name, out_dtype, tm, tn, norm_g=None, bias=None, glu_in=None, res=None, gate=None,
        rows_per_batch=None, ctx_len=0):
    m = xs[0].shape[0]
    ksizes = tuple(x.shape[1] for x in xs)
    k, n = w.shape
    assert sum(ksizes) == k and m % tm == 0 and n % tn == 0
    epilogue = "glu" if glu_in is not None else ("resgate" if res is not None else None)
    tiles_per_batch = (rows_per_batch // tm) if rows_per_batch else 1
    in_specs = [pl.BlockSpec((tm, ks), lambda i, j: (i, 0)) for ks in ksizes]
    in_specs.append(pl.BlockSpec((k, tn), lambda i, j: (0, j)))
    args = list(xs) + [w]
    if norm_g is not None:
        in_specs.append(pl.BlockSpec((1, k), lambda i, j: (0, 0)))
        args.append(norm_g.reshape(1, k).astype(F32))
    if bias is not None:
        in_specs.append(pl.BlockSpec((1, tn), lambda i, j: (0, j)))
        args.append(bias.reshape(1, n).astype(F32))
    if epilogue == "glu":
        in_specs.append(pl.BlockSpec((tm, tn), lambda i, j: (i, j)))
        args.append(glu_in)
    if epilogue == "resgate":
        tpb = tiles_per_batch
        in_specs.append(pl.BlockSpec((tm, tn), lambda i, j: (i, j)))
        in_specs.append(pl.BlockSpec((1, 2, tn), lambda i, j: (i // tpb, 0, j)))
        args += [res, gate]
    body = functools.partial(_mm_body, nx=len(xs), ksizes=ksizes, has_norm=norm_g is not None,
                             has_bias=bias is not None, epilogue=epilogue, tm=tm,
                             tiles_per_batch=tiles_per_batch, ctx_len=ctx_len)
    return pl.pallas_call(
        body,
        grid=(m // tm, n // tn),
        in_specs=in_specs,
        out_specs=pl.BlockSpec((tm, tn), lambda i, j: (i, j)),
        out_shape=jax.ShapeDtypeStruct((m, n), out_dtype),
        compiler_params=_cp("parallel", "arbitrary"),
        name=name,
    )(*args)


def _ada_body(c_ref, w_ref, b_ref, o_ref):
    acc = jnp.dot(c_ref[...].astype(BF16), w_ref[0].astype(BF16), preferred_element_type=F32)
    o_ref[0] = acc + b_ref[0]


def _ada_mod(cond, ada_w, ada_b):
    depth, d, n6 = ada_w.shape
    rows = cond.shape[0]
    tn = _pick_tile(n6, 512, LANES)
    return pl.pallas_call(
        _ada_body,
        grid=(depth, n6 // tn),
        in_specs=[pl.BlockSpec((rows, d), lambda l, j: (0, 0)),
                  pl.BlockSpec((1, d, tn), lambda l, j: (l, 0, j)),
                  pl.BlockSpec((1, 1, tn), lambda l, j: (l, 0, j))],
        out_specs=pl.BlockSpec((1, rows, tn), lambda l, j: (l, 0, j)),
        out_shape=jax.ShapeDtypeStruct((depth, rows, n6), F32),
        compiler_params=_cp("parallel", "arbitrary"),
        name="ada_mod",
    )(cond, ada_w, ada_b.reshape(depth, 1, n6).astype(F32))


def _norm_mod_body(*refs, tm, tiles_per_batch, ctx_len, router):
    if router:
        x_ref, g_ref, sh_ref, sc_ref, whi_ref, wlo_ref, br_ref, h_ref, lg_ref = refs
    else:
        x_ref, g_ref, sh_ref, sc_ref, h_ref = refs
    i = pl.program_id(0)
    x = x_ref[...]
    y = x * lax.rsqrt(jnp.mean(x * x, axis=-1, keepdims=True) + NORM_EPS) * g_ref[...]
    is_ctx = (i % tiles_per_batch) * tm < ctx_len
    sh = jnp.where(is_ctx, sh_ref[0, 0:1, :], sh_ref[0, 1:2, :])
    sc = jnp.where(is_ctx, sc_ref[0, 0:1, :], sc_ref[0, 1:2, :])
    h = y * (1.0 + sc) + sh
    h_ref[...] = h.astype(BF16)
    if router:
        hi = h.astype(BF16)
        lo = (h - hi.astype(F32)).astype(BF16)
        lg = jnp.dot(hi, whi_ref[...], preferred_element_type=F32)
        lg = lg + jnp.dot(hi, wlo_ref[...], preferred_element_type=F32)
        lg = lg + jnp.dot(lo, whi_ref[...], preferred_element_type=F32)
        lg_ref[...] = lg + br_ref[...]


def _norm_mod(x, g, shift, scale, *, rows_per_batch, ctx_len, router=None):
    m, d = x.shape
    tm = _pick_tile(math.gcd(rows_per_batch, ctx_len), 256, 8)
    tpb = rows_per_batch // tm
    in_specs = [
        pl.BlockSpec((tm, d), lambda i: (i, 0)),
        pl.BlockSpec((1, d), lambda i: (0, 0)),
        pl.BlockSpec((1, 2, d), lambda i: (i // tpb, 0, 0)),
        pl.BlockSpec((1, 2, d), lambda i: (i // tpb, 0, 0)),
    ]
    args = [x, g.reshape(1, d), shift, scale]
    out_specs = [pl.BlockSpec((tm, d), lambda i: (i, 0))]
    out_shape = [jax.ShapeDtypeStruct((m, d), BF16)]
    if router is not None:
        whi, wlo, br = router
        in_specs += [pl.BlockSpec((d, LANES), lambda i: (0, 0)),
                     pl.BlockSpec((d, LANES), lambda i: (0, 0)),
                     pl.BlockSpec((1, LANES), lambda i: (0, 0))]
        args += [whi, wlo, br]
        out_specs.append(pl.BlockSpec((tm, LANES), lambda i: (i, 0)))
        out_shape.append(jax.ShapeDtypeStruct((m, LANES), F32))
    body = functools.partial(_norm_mod_body, tm=tm, tiles_per_batch=tpb, ctx_len=ctx_len,
                             router=router is not None)
    outs = pl.pallas_call(body, grid=(m // tm,), in_specs=in_specs, out_specs=out_specs,
                          out_shape=out_shape, compiler_params=_cp("parallel"),
                          name="norm_mod_router" if router is not None else "norm_mod")(*args)
    return outs if router is not None else outs[0]


def _final_norm_body(x_ref, g_ref, o_ref):
    x = x_ref[...]
    o_ref[...] = x * lax.rsqrt(jnp.mean(x * x, axis=-1, keepdims=True) + NORM_EPS) * g_ref[...]


def _final_norm(x, g, *, batch, rows_per_batch, ctx_len):
    d = x.shape[1]
    n_lat = rows_per_batch - ctx_len
    tm = _pick_tile(math.gcd(n_lat, ctx_len), 256, 8)
    tpb, ctx_tiles, lat_tiles = rows_per_batch // tm, ctx_len // tm, n_lat // tm
    out = pl.pallas_call(
        _final_norm_body,
        grid=(batch, lat_tiles),
        in_specs=[pl.BlockSpec((tm, d), lambda b, i: (b * tpb + ctx_tiles + i, 0)),
                  pl.BlockSpec((1, d), lambda b, i: (0, 0))],
        out_specs=pl.BlockSpec((tm, d), lambda b, i: (b * lat_tiles + i, 0)),
        out_shape=jax.ShapeDtypeStruct((batch * n_lat, d), F32),
        compiler_params=_cp("parallel", "parallel"),
        name="final_norm",
    )(x, g.reshape(1, d))
    return out.reshape(batch, n_lat, d)


def _rope_body(x_ref, cos_ref, sa_ref, sb_ref, cs_ref, o_ref, *, reps):
    x = x_ref[...].astype(F32)
    width = x.shape[1]
    x_dn = pltpu.roll(x, width - 16, 1)
    x_up = pltpu.roll(x, 16, 1)
    tile = lambda r: jnp.tile(r[...], (1, reps))
    y = x * tile(cos_ref) + x_dn * tile(sa_ref) + x_up * tile(sb_ref)
    o_ref[...] = (y * cs_ref[...]).astype(o_ref.dtype)


def _rope(x, ncols, tables, col_scale, *, rows_per_batch, col_off=0):
    m = x.shape[0]
    cos, sa, sb = tables
    wb = max(w for w in (512, 256, 128) if ncols % w == 0 and col_off % w == 0)
    tm = _pick_tile(rows_per_batch, 1088)
    tpb = rows_per_batch // tm
    ob = col_off // wb
    body = functools.partial(_rope_body, reps=wb // LANES)
    tspec = pl.BlockSpec((tm, LANES), lambda i, j: (i % tpb, 0))
    return pl.pallas_call(
        body,
        grid=(m // tm, ncols // wb),
        in_specs=[pl.BlockSpec((tm, wb), lambda i, j: (i, j + ob)), tspec, tspec, tspec,
                  pl.BlockSpec((1, wb), lambda i, j: (0, j))],
        out_specs=pl.BlockSpec((tm, wb), lambda i, j: (i, j)),
        out_shape=jax.ShapeDtypeStruct((m, ncols), x.dtype),
        compiler_params=_cp("parallel", "arbitrary"),
        name="rope",
    )(x, cos, sa, sb, col_scale.reshape(1, ncols).astype(F32))


def _rope_tables(n_lat, ctx_len):
    rows = n_lat // GRID_W
    row = jnp.repeat(jnp.arange(rows, dtype=F32), GRID_W)
    col = jnp.tile(jnp.arange(GRID_W, dtype=F32), rows)
    quarter = ROPE_DIM // 4
    inv = ROPE_BASE ** (-jnp.arange(quarter, dtype=F32) / quarter)
    ar = row[:, None] * inv
    ac = col[:, None] * inv
    ang = jnp.concatenate([ar, ar, ac, ac], axis=-1)
    ang = jnp.concatenate([jnp.zeros((ctx_len, ROPE_DIM), F32), ang], axis=0)
    ang = jnp.tile(ang, (1, LANES // ROPE_DIM))
    cos, sin = jnp.cos(ang), jnp.sin(ang)
    lane = jnp.arange(LANES)
    even = ((lane // quarter) % 2 == 0)[None, :]
    sa = jnp.where(even, -sin, 0.0)
    sb = jnp.where(even, 0.0, sin)
    return cos, sa, sb


def _softmax_pv(q, k, v_aug):
    s = lax.dot_general(q, k, (((1,), (1,)), ((), ())), preferred_element_type=F32)
    m = jnp.max(s, axis=-1, keepdims=True)
    e = jnp.exp2(s - m).astype(BF16)
    ov = jnp.dot(e, v_aug, preferred_element_type=F32)
    return ov[:, :LANES] / ov[:, LANES:]


def _diff_attn_body(lam_ref, q_ref, k_ref, v_ref, sub_ref, o_ref, va_ref, *, tq, ctx_len, post):
    qi = pl.program_id(2)

    @pl.when(qi == 0)
    def _():
        va_ref[:, :LANES] = v_ref[...]
        va_ref[:, LANES:] = jnp.ones(v_ref.shape, BF16)

    q = q_ref[...]
    lane = lax.broadcasted_iota(jnp.int32, (1, LANES), 1)
    first = lane < DIFF_HEAD_DIM
    zero = jnp.zeros_like(q)
    q0 = jnp.where(first, q, zero)
    q1 = jnp.where(first, zero, q)
    lam = lam_ref[0]

    def attend(k, va):
        o = _softmax_pv(q0, k, va) - lam * _softmax_pv(q1, k, va)
        o = o * lax.rsqrt(jnp.mean(o * o, axis=-1, keepdims=True) + NORM_EPS) * sub_ref[...] * post
        o_ref[...] = o.astype(o_ref.dtype)

    @pl.when(qi * tq < ctx_len)
    def _():
        attend(k_ref[0:ctx_len, :], va_ref[0:ctx_len, :])

    @pl.when(qi * tq >= ctx_len)
    def _():
        attend(k_ref[...], va_ref[...])


def _diff_attn(qk, proj, lam, subln, *, batch, seq, ctx_len, heads, q_blk, k_blk, v_blk, post):
    tq = _pick_tile(math.gcd(seq, ctx_len), 256)
    tpb = seq // tq
    body = functools.partial(_diff_attn_body, tq=tq, ctx_len=ctx_len, post=post)
    return pl.pallas_call(
        body,
        grid=(batch, heads, tpb),
        in_specs=[
            pl.BlockSpec(memory_space=pltpu.SMEM),
            pl.BlockSpec((tq, LANES), lambda b, h, i: (b * tpb + i, q_blk + h)),
            pl.BlockSpec((seq, LANES), lambda b, h, i: (b, k_blk + h)),
            pl.BlockSpec((seq, LANES), lambda b, h, i: (b, v_blk + h)),
            pl.BlockSpec((1, LANES), lambda b, h, i: (0, 0)),
        ],
        out_specs=pl.BlockSpec((tq, LANES), lambda b, h, i: (b * tpb + i, h)),
        out_shape=jax.ShapeDtypeStruct((batch * seq, heads * LANES), BF16),
        scratch_shapes=[pltpu.VMEM((seq, 2 * LANES), BF16)],
        compiler_params=_cp("parallel", "parallel", "arbitrary"),
        name="diff_attn",
    )(lam.reshape(1).astype(F32), qk, qk, proj, subln.reshape(1, LANES).astype(F32))


def _mla_attn_body(qn_ref, qr_ref, kn_ref, kr_ref, v_ref, o_ref, ka_ref, va_ref, *, tq, ctx_len):
    h = pl.program_id(1)
    qi = pl.program_id(2)

    @pl.when(qi == 0)
    def _():
        ka_ref[:, :LANES] = kn_ref[...]
        ka_ref[:, LANES:] = kr_ref[...]
        va_ref[:, :LANES] = v_ref[...]
        va_ref[:, LANES:] = jnp.ones(v_ref.shape, BF16)

    qr = qr_ref[...]
    lane = lax.broadcasted_iota(jnp.int32, (1, LANES), 1)
    mine = (lane < ROPE_DIM) == (h % 2 == 0)
    qr = jnp.where(mine, qr, jnp.zeros_like(qr))
    q = jnp.concatenate([qn_ref[...], qr], axis=1)

    @pl.when(qi * tq < ctx_len)
    def _():
        o_ref[...] = _softmax_pv(q, ka_ref[0:ctx_len, :], va_ref[0:ctx_len, :]).astype(o_ref.dtype)

    @pl.when(qi * tq >= ctx_len)
    def _():
        o_ref[...] = _softmax_pv(q, ka_ref[...], va_ref[...]).astype(o_ref.dtype)


def _mla_attn(q_up, q_rope, kv_up, k_rope, *, batch, seq, ctx_len, heads):
    tq = _pick_tile(math.gcd(seq, ctx_len), 256)
    tpb = seq // tq
    body = functools.partial(_mla_attn_body, tq=tq, ctx_len=ctx_len)
    return pl.pallas_call(
        body,
        grid=(batch, heads, tpb),
        in_specs=[
            pl.BlockSpec((tq, LANES), lambda b, h, i: (b * tpb + i, h)),
            pl.BlockSpec((tq, LANES), lambda b, h, i: (b * tpb + i, h // 2)),
            pl.BlockSpec((seq, LANES), lambda b, h, i: (b, 2 * h)),
            pl.BlockSpec((seq, LANES), lambda b, h, i: (b, 0)),
            pl.BlockSpec((seq, LANES), lambda b, h, i: (b, 2 * h + 1)),
        ],
        out_specs=pl.BlockSpec((tq, LANES), lambda b, h, i: (b * tpb + i, h)),
        out_shape=jax.ShapeDtypeStruct((batch * seq, heads * LANES), BF16),
        scratch_shapes=[pltpu.VMEM((seq, 2 * LANES), BF16), pltpu.VMEM((seq, 2 * LANES), BF16)],
        compiler_params=_cp("parallel", "parallel", "arbitrary"),
        name="mla_attn",
    )(q_up, q_rope, kv_up, k_rope, kv_up)


def _retention_body(*refs, backward, cs, heads):
    if backward:
        lg_ref, q_ref, k_ref, v_ref, yf_ref, gate_ref, ng_ref, o_ref, s_ref, d_ref, qd_ref, kd_ref = refs
    else:
        lg_ref, q_ref, k_ref, v_ref, o_ref, s_ref, d_ref, qd_ref, kd_ref = refs
    t = pl.program_id(1)

    @pl.when(t == 0)
    def _():
        s_ref[...] = jnp.zeros_like(s_ref)
        pos_r = lax.broadcasted_iota(jnp.int32, (cs, cs), 0).astype(F32)
        pos_c = lax.broadcasted_iota(jnp.int32, (cs, cs), 1).astype(F32)
        pos = lax.broadcasted_iota(jnp.int32, (cs, 1), 0).astype(F32)
        for h in range(heads):
            lg = lg_ref[h]
            if backward:
                diff = pos_c - pos_r
                keep = diff > 0
                qd_ref[h] = jnp.exp(lg * (cs - pos))
                kd_ref[h] = jnp.exp(lg * pos)
            else:
                diff = pos_r - pos_c
                keep = diff >= 0
                qd_ref[h] = jnp.exp(lg * (pos + 1.0))
                kd_ref[h] = jnp.exp(lg * (cs - 1.0 - pos))
            d_ref[h] = jnp.where(keep, jnp.exp(lg * jnp.maximum(diff, 0.0)), 0.0)

    lane = lax.broadcasted_iota(jnp.int32, (1, LANES), 1)
    nt = (((1,), (1,)), ((), ()))
    tn = (((0,), (0,)), ((), ()))
    for h in range(heads):
        blk = slice((h // 2) * LANES, (h // 2 + 1) * LANES)
        col = slice(h * RET_V, (h + 1) * RET_V)
        mine = (lane < RET_K) == (h % 2 == 0)
        q = q_ref[:, blk]
        q = jnp.where(mine, q, jnp.zeros_like(q))
        k = k_ref[:, blk]
        v = v_ref[:, col]
        scores = lax.dot_general(q, k, nt, preferred_element_type=F32) * d_ref[h]
        intra = jnp.dot(scores.astype(BF16), v, preferred_element_type=F32)
        state = s_ref[h]
        q_w = (q.astype(F32) * qd_ref[h]).astype(BF16)
        cross = jnp.dot(q_w, state.astype(BF16), preferred_element_type=F32)
        k_w = (k.astype(F32) * kd_ref[h]).astype(BF16)
        upd = lax.dot_general(k_w, v, tn, preferred_element_type=F32)
        s_ref[h] = jnp.exp(lg_ref[h] * cs) * state + upd
        y = intra + cross
        if backward:
            y = y + yf_ref[:, col]
            y = y * lax.rsqrt(jnp.mean(y * y, axis=-1, keepdims=True) + NORM_EPS) * ng_ref[:, col]
            g = gate_ref[:, col].astype(F32)
            o_ref[:, col] = (g * _sigmoid(g) * y).astype(o_ref.dtype)
        else:
            o_ref[:, col] = y


def _retention(qk, proj, log_g, norm_g, *, batch, seq, ctx_len, heads, q_off, k_off, v_off, g_off):
    cs = RET_CHUNK
    nc, nc_ctx = seq // cs, ctx_len // cs
    wqk, wv = heads * RET_K, heads * RET_V
    assert q_off % wqk == 0 and k_off % wqk == 0 and v_off % wv == 0 and g_off % wv == 0
    smem = pl.BlockSpec(memory_space=pltpu.SMEM)

    def fwd_chunk(t):
        return t

    def bwd_chunk(t):
        return jnp.where(t < nc_ctx, nc_ctx - 1 - t, nc - 1 - (t - nc_ctx))

    def specs(chunk):
        row = lambda b, t: b * nc + chunk(t)
        return (pl.BlockSpec((cs, wqk), lambda b, t: (row(b, t), q_off // wqk)),
                pl.BlockSpec((cs, wqk), lambda b, t: (row(b, t), k_off // wqk)),
                pl.BlockSpec((cs, wv), lambda b, t: (row(b, t), v_off // wv)),
                pl.BlockSpec((cs, wv), lambda b, t: (row(b, t), 0)),
                pl.BlockSpec((cs, wv), lambda b, t: (row(b, t), g_off // wv)))

    scratch = [pltpu.VMEM((heads, LANES, RET_V), F32), pltpu.VMEM((heads, cs, cs), F32),
               pltpu.VMEM((heads, cs, 1), F32), pltpu.VMEM((heads, cs, 1), F32)]
    qs, ks, vs, ys, gs = specs(fwd_chunk)
    y_f = pl.pallas_call(
        functools.partial(_retention_body, backward=False, cs=cs, heads=heads),
        grid=(batch, nc),
        in_specs=[smem, qs, ks, vs],
        out_specs=ys,
        out_shape=jax.ShapeDtypeStruct((batch * seq, wv), F32),
        scratch_shapes=scratch,
        compiler_params=_cp("parallel", "arbitrary"),
        name="retention_fwd",
    )(log_g[0].astype(F32), qk, qk, proj)
    qs, ks, vs, ys, gs = specs(bwd_chunk)
    return pl.pallas_call(
        functools.partial(_retention_body, backward=True, cs=cs, heads=heads),
        grid=(batch, nc),
        in_specs=[smem, qs, ks, vs, ys, gs, pl.BlockSpec((1, wv), lambda b, t: (0, 0))],
        out_specs=ys,
        out_shape=jax.ShapeDtypeStruct((batch * seq, wv), BF16),
        scratch_shapes=scratch,
        compiler_params=_cp("parallel", "arbitrary"),
        name="retention_bwd",
    )(log_g[1].astype(F32), qk, qk, proj, y_f, proj, norm_g.reshape(1, wv).astype(F32))


def _s5_params(a_re, a_im, log_dt, b_re, b_im, c_re, c_im):
    L, ch = S5_CHUNK, S5_CH
    a_re, a_im = a_re.astype(F32), a_im.astype(F32)
    dt = jnp.exp(log_dt.astype(F32))[..., None]
    e = jnp.arange(L + 1, dtype=F32)[:, None, None, None]
    mag = jnp.exp(a_re * dt * e)
    pw_re, pw_im = mag * jnp.cos(a_im * dt * e), mag * jnp.sin(a_im * dt * e)
    ab_re, ab_im = pw_re[1], pw_im[1]
    den = a_re * a_re + a_im * a_im
    f_re = ((ab_re - 1.0) * a_re + ab_im * a_im) / den
    f_im = (ab_im * a_re - (ab_re - 1.0) * a_im) / den
    bb_re = f_re[..., None] * b_re - f_im[..., None] * b_im
    bb_im = f_re[..., None] * b_im + f_im[..., None] * b_re
    c_re, c_im = c_re.astype(F32), c_im.astype(F32)

    cp_re = c_re[None] * pw_re[:L, :, :, None, :] - c_im[None] * pw_im[:L, :, :, None, :]
    cp_im = c_re[None] * pw_im[:L, :, :, None, :] + c_im[None] * pw_re[:L, :, :, None, :]
    hp = lax.Precision.HIGHEST
    kmat = (jnp.einsum("ldgcp,dgpk->dglck", cp_re, bb_re, precision=hp)
            - jnp.einsum("ldgcp,dgpk->dglck", cp_im, bb_im, precision=hp))
    idx = jnp.arange(L)
    lag_f = idx[None, :] - idx[:, None]
    lag_b = -lag_f

    def toeplitz(k, lag):
        t = k[:, jnp.clip(lag, 0, L - 1)]
        t = jnp.where((lag >= 0)[None, :, :, None, None], t, 0.0)
        return t.transpose(0, 1, 4, 2, 3).reshape(k.shape[0], L * ch, L * ch)

    m_both = jnp.concatenate([toeplitz(kmat[0], lag_f), toeplitz(kmat[1], lag_b)], axis=-1)

    def state_in(d, exps):
        p_re, p_im = pw_re[exps, d], pw_im[exps, d]
        w_re = p_re[..., None] * bb_re[d][None] - p_im[..., None] * bb_im[d][None]
        w_im = p_re[..., None] * bb_im[d][None] + p_im[..., None] * bb_re[d][None]
        w = jnp.concatenate([w_re, w_im], axis=2)
        return w.transpose(1, 0, 3, 2).reshape(w.shape[1], L * ch, 2 * S5_STATE)

    w_both = jnp.concatenate([state_in(0, L - 1 - idx), state_in(1, idx)], axis=-1)

    def state_out(d, exps):
        p_re, p_im = pw_re[exps, d], pw_im[exps, d]
        v_re = c_re[d][None] * p_re[:, :, None, :] - c_im[d][None] * p_im[:, :, None, :]
        v_im = c_re[d][None] * p_im[:, :, None, :] + c_im[d][None] * p_re[:, :, None, :]
        v = jnp.concatenate([v_re, -v_im], axis=-1)
        return v.transpose(1, 3, 0, 2).reshape(v.shape[1], 2 * S5_STATE, L * ch)

    v_mat = jnp.stack([state_out(0, idx + 1), state_out(1, L - idx)])
    a_pow = jnp.concatenate([pw_re[L], pw_im[L]], axis=-1)
    return m_both.astype(BF16), w_both.astype(BF16), v_mat.astype(BF16), a_pow


def _s5_state_in_body(u_ref, w_ref, o_ref):
    s = jnp.dot(u_ref[0], w_ref[0], preferred_element_type=F32)
    o_ref[0] = s[:, :LANES]
    o_ref[1] = s[:, LANES:]


def _s5_scan_body(s_ref, a_ref, o_ref, *, nk, nk_ctx):
    d = pl.program_id(0)
    a = a_ref[0]
    lane = lax.broadcasted_iota(jnp.int32, a.shape, 1)
    first = lane < S5_STATE
    a_sw = pltpu.roll(a, S5_STATE, 1)
    a_same = jnp.where(first, a, a_sw)
    a_cross = jnp.where(first, -a_sw, a)
    a_same, a_cross = a_same[None], a_cross[None]

    def step(k, carry):
        hstate, hswap = carry
        o_ref[0, k] = hstate
        s = s_ref[0, k]
        return (a_same * hstate + a_cross * hswap + s,
                a_same * hswap - a_cross * hstate + pltpu.roll(s, S5_STATE, 2))

    zero = jnp.zeros(o_ref.shape[2:], F32)
    unroll = 8 if (nk % 8 == 0 and nk_ctx % 8 == 0) else 1

    @pl.when(d == 0)
    def _():
        lax.fori_loop(0, nk, step, (zero, zero), unroll=unroll)

    @pl.when(d == 1)
    def _():
        hc = lax.fori_loop(0, nk_ctx, lambda t, c: step(nk_ctx - 1 - t, c), (zero, zero), unroll=unroll)
        lax.fori_loop(0, nk - nk_ctx, lambda t, c: step(nk - 1 - t, c), hc, unroll=unroll)


def _s5_out_body(u_ref, m_ref, h_ref, v_ref, d_ref, o_ref):
    u = u_ref[0]
    y = jnp.dot(u, m_ref[0], preferred_element_type=F32)
    y = y[:, :2 * LANES] + y[:, 2 * LANES:]
    y = y + jnp.dot(h_ref[0].astype(BF16), v_ref[0, 0], preferred_element_type=F32)
    y = y + jnp.dot(h_ref[1].astype(BF16), v_ref[1, 0], preferred_element_type=F32)
    y = y + d_ref[0] * u.astype(F32)
    o_ref[0] = jax.nn.gelu(y).astype(o_ref.dtype)


def _s5_mix(proj, su_off, params, d_skip, *, batch, seq, ctx_len, gw):
    m_both, w_both, v_mat, a_pow = params
    L, ch, st2 = S5_CHUNK, S5_CH, 2 * S5_STATE
    groups = gw // ch
    nk, nk_ctx = seq // L, ctx_len // L
    rows = nk * batch
    cols = L * ch
    u = proj[:, su_off:su_off + gw].reshape(batch, nk, L, groups, ch)
    u = u.transpose(3, 1, 0, 2, 4).reshape(groups, rows, cols)

    s_in = pl.pallas_call(
        _s5_state_in_body,
        grid=(groups,),
        in_specs=[pl.BlockSpec((1, rows, cols), lambda g: (g, 0, 0)),
                  pl.BlockSpec((1, cols, 2 * st2), lambda g: (g, 0, 0))],
        out_specs=pl.BlockSpec((2, rows, st2), lambda g: (0, 0, g)),
        out_shape=jax.ShapeDtypeStruct((2, rows, groups * st2), F32),
        compiler_params=_cp("parallel"),
        name="s5_state_in",
    )(u, w_both)

    gb = 8
    s4 = s_in.reshape(2, nk, batch, groups, st2)
    h_prev = pl.pallas_call(
        functools.partial(_s5_scan_body, nk=nk, nk_ctx=nk_ctx),
        grid=(2, groups // gb),
        in_specs=[pl.BlockSpec((1, nk, batch, gb, st2), lambda d, g: (d, 0, 0, g, 0)),
                  pl.BlockSpec((1, gb, st2), lambda d, g: (d, g, 0))],
        out_specs=pl.BlockSpec((1, nk, batch, gb, st2), lambda d, g: (d, 0, 0, g, 0)),
        out_shape=jax.ShapeDtypeStruct((2, nk, batch, groups, st2), F32),
        compiler_params=_cp("parallel", "parallel"),
        name="s5_scan",
    )(s4, a_pow)
    h2 = h_prev.reshape(2, rows, groups * st2)

    d_rep = jnp.tile(d_skip.astype(F32), (1, L)).reshape(groups, 1, cols)
    y = pl.pallas_call(
        _s5_out_body,
        grid=(groups,),
        in_specs=[pl.BlockSpec((1, rows, cols), lambda g: (g, 0, 0)),
                  pl.BlockSpec((1, cols, 2 * cols), lambda g: (g, 0, 0)),
                  pl.BlockSpec((2, rows, st2), lambda g: (0, 0, g)),
                  pl.BlockSpec((2, 1, st2, cols), lambda g: (0, g, 0, 0)),
                  pl.BlockSpec((1, 1, cols), lambda g: (g, 0, 0))],
        out_specs=pl.BlockSpec((1, rows, cols), lambda g: (g, 0, 0)),
        out_shape=jax.ShapeDtypeStruct((groups, rows, cols), BF16),
        compiler_params=_cp("parallel"),
        name="s5_out",
    )(u, m_both, h2, v_mat, d_rep)
    y = y.reshape(groups, nk, batch, L, ch).transpose(2, 1, 3, 0, 4)
    return y.reshape(batch * seq, gw)


def _moe_up_body(te_ref, x_ref, wg_ref, wu_ref, o_ref):
    x = x_ref[...]
    a = jnp.dot(x, wg_ref[0], preferred_element_type=F32)
    b = jnp.dot(x, wu_ref[0], preferred_element_type=F32)
    o_ref[...] = (a * _sigmoid(a) * b).astype(o_ref.dtype)


def _moe_down_body(te_ref, h_ref, w_ref, rw_ref, o_ref):
    y = jnp.dot(h_ref[...], w_ref[0], preferred_element_type=F32)
    o_ref[...] = rw_ref[...] * y


def _moe_combine_body(x_ref, y0_ref, y1_ref, gate_ref, o_ref, *, tm, tiles_per_batch, ctx_len):
    i = pl.program_id(0)
    is_ctx = (i % tiles_per_batch) * tm < ctx_len
    gate = jnp.where(is_ctx, gate_ref[0, 0:1, :], gate_ref[0, 1:2, :])
    o_ref[...] = x_ref[...] + gate * (y0_ref[...] + y1_ref[...])


def _route(logits):
    assert MOE_TOPK == 2
    g_logit = logits[:, :MOE_GROUPS]
    g_prob = jax.nn.softmax(g_logit, axis=-1)
    g_idx = jnp.argmax(g_prob, axis=-1)
    g_p = jnp.max(g_prob, axis=-1)
    e_logit = logits[:, MOE_GROUPS:MOE_GROUPS + MOE_GROUPS * MOE_PER_GROUP]
    e_logit = e_logit.reshape(-1, MOE_GROUPS, MOE_PER_GROUP)
    sel = (jnp.arange(MOE_GROUPS)[None, :] == g_idx[:, None])[:, :, None]
    e_logit = jnp.sum(jnp.where(sel, e_logit, 0.0), axis=1)
    e_prob = jax.nn.softmax(e_logit, axis=-1)
    i0 = jnp.argmax(e_prob, axis=-1)
    p0 = jnp.max(e_prob, axis=-1)
    rest = jnp.where(jnp.arange(MOE_PER_GROUP)[None, :] == i0[:, None], -1.0, e_prob)
    i1 = jnp.argmax(rest, axis=-1)
    p1 = jnp.max(rest, axis=-1)
    e_p = jnp.stack([p0, p1], axis=-1)
    w = g_p[:, None] * e_p / jnp.sum(e_p, axis=-1, keepdims=True)
    ids = g_idx[:, None] * MOE_PER_GROUP + jnp.stack([i0, i1], axis=-1)
    return ids.astype(jnp.int32), w


def _moe(h, logits, x, gate, w_gate, w_up, w_down, *, rows_per_batch, ctx_len):
    t, d = h.shape
    n_exp, _, dff = w_gate.shape
    tile = MOE_TILE
    ids, wts = _route(logits)
    flat_e = ids.reshape(-1)
    onehot = (flat_e[:, None] == jnp.arange(n_exp)[None, :]).astype(jnp.int32)
    counts = onehot.sum(0)
    rank = jnp.take_along_axis(jnp.cumsum(onehot, axis=0) - onehot, flat_e[:, None], axis=1)[:, 0]
    padded = (counts + tile - 1) // tile * tile
    starts = jnp.cumsum(padded) - padded
    pos = starts[flat_e] + rank
    n_rows = (t * MOE_TOPK // tile + n_exp) * tile
    n_tiles = n_rows // tile
    src = jnp.zeros((n_rows,), jnp.int32).at[pos].set(jnp.arange(t * MOE_TOPK, dtype=jnp.int32) // MOE_TOPK)
    row_w = jnp.zeros((n_rows,), F32).at[pos].set(wts.reshape(-1))
    tile_start = jnp.arange(n_tiles, dtype=jnp.int32) * tile
    ends = starts + padded
    tile_e = jnp.minimum(jnp.sum(tile_start[:, None] >= ends[None, :], axis=1), n_exp - 1).astype(jnp.int32)

    xs = h.at[src].get(mode="promise_in_bounds")
    hid = pl.pallas_call(
        _moe_up_body,
        grid_spec=pltpu.PrefetchScalarGridSpec(
            num_scalar_prefetch=1,
            grid=(n_tiles,),
            in_specs=[pl.BlockSpec((tile, d), lambda i, te: (i, 0)),
                      pl.BlockSpec((1, d, dff), lambda i, te: (te[i], 0, 0)),
                      pl.BlockSpec((1, d, dff), lambda i, te: (te[i], 0, 0))],
            out_specs=pl.BlockSpec((tile, dff), lambda i, te: (i, 0))),
        out_shape=jax.ShapeDtypeStruct((n_rows, dff), BF16),
        compiler_params=_cp("arbitrary"),
        name="moe_up",
    )(tile_e, xs, w_gate, w_up)
    ys = pl.pallas_call(
        _moe_down_body,
        grid_spec=pltpu.PrefetchScalarGridSpec(
            num_scalar_prefetch=1,
            grid=(n_tiles,),
            in_specs=[pl.BlockSpec((tile, dff), lambda i, te: (i, 0)),
                      pl.BlockSpec((1, dff, d), lambda i, te: (te[i], 0, 0)),
                      pl.BlockSpec((tile, 1), lambda i, te: (i, 0))],
            out_specs=pl.BlockSpec((tile, d), lambda i, te: (i, 0))),
        out_shape=jax.ShapeDtypeStruct((n_rows, d), F32),
        compiler_params=_cp("arbitrary"),
        name="moe_down",
    )(tile_e, hid, w_down, row_w.reshape(n_rows, 1))

    pos2 = pos.reshape(t, MOE_TOPK)
    y0 = ys.at[pos2[:, 0]].get(mode="promise_in_bounds")
    y1 = ys.at[pos2[:, 1]].get(mode="promise_in_bounds")
    tm = _pick_tile(math.gcd(rows_per_batch, ctx_len), 256, 8)
    tpb = rows_per_batch // tm
    row_spec = pl.BlockSpec((tm, d), lambda i: (i, 0))
    return pl.pallas_call(
        functools.partial(_moe_combine_body, tm=tm, tiles_per_batch=tpb, ctx_len=ctx_len),
        grid=(t // tm,),
        in_specs=[row_spec, row_spec, row_spec, pl.BlockSpec((1, 2, d), lambda i: (i // tpb, 0, 0))],
        out_specs=row_spec,
        out_shape=jax.ShapeDtypeStruct((t, d), F32),
        compiler_params=_cp("parallel"),
        name="moe_combine",
    )(x, y0, y1, gate)


def kernel(x, c, ctx, c_ctx, ada_w, ada_b, norm_mix, norm_ffn, w_in, w_out, diff_lambda, diff_subln,
           s5_a_re, s5_a_im, s5_log_dt, s5_b_re, s5_b_im, s5_c_re, s5_c_im, s5_d, s5_glu_w, s5_glu_b,
           mla_q_norm, mla_kv_norm, mla_w_uq, mla_w_ukv, ret_decay, ret_norm,
           moe_wg, moe_bg, moe_we, moe_be, moe_w_gate, moe_w_up, moe_w_down, final_norm):
    batch, n_lat, d = x.shape
    ctx_len = ctx.shape[1]
    depth = ada_w.shape[0]
    seq = ctx_len + n_lat
    rows = batch * seq
    gw = d // 4
    heads = gw // LANES
    q_rank, kv_rank = 3 * d // 16, d // 16
    ret_qk = heads * RET_K
    n_route = MOE_GROUPS + MOE_GROUPS * MOE_PER_GROUP
    assert heads % 2 == 0 and ctx_len % RET_CHUNK == 0 and n_lat % RET_CHUNK == 0

    splits = (gw, gw, gw, gw, q_rank, kv_rank, ROPE_DIM, ret_qk, ret_qk, gw, gw)
    offs = [0]
    for s_ in splits:
        offs.append(offs[-1] + s_)
    names = ("dq", "dk", "dv", "su", "mcq", "mckv", "mkr", "rq", "rk", "rv", "rg")
    src_col = {n_: (offs[i], offs[i + 1]) for i, n_ in enumerate(names)}
    order = ("dq", "dk", "rq", "rk", "dv", "su", "rv", "rg", "mcq", "mckv")
    col = {}
    pos = 0
    for n_ in order:
        col[n_] = pos
        pos += src_col[n_][1] - src_col[n_][0]
    n_main = pos
    n_rope = col["dv"]
    uq_cols = jnp.arange(heads * (MLA_NOPE + ROPE_DIM)).reshape(heads, MLA_NOPE + ROPE_DIM)
    uq_perm = jnp.concatenate([uq_cols[:, :MLA_NOPE].reshape(-1), uq_cols[:, MLA_NOPE:].reshape(-1)])

    tables = _rope_tables(n_lat, ctx_len)
    log2e = math.log2(math.e)
    rope_scale = jnp.ones((n_rope,), F32).at[col["rk"]:col["rk"] + ret_qk].set(RET_K ** -0.5)
    rope_scale = rope_scale.at[col["dq"]:col["dq"] + gw].set(DIFF_HEAD_DIM ** -0.5 * log2e)
    mla_q_scale = (MLA_NOPE + ROPE_DIM) ** -0.5 * log2e

    cond = jnp.concatenate([c_ctx[None, :], c], axis=0)
    cond = jnp.pad(cond * _sigmoid(cond), ((0, 8 - (batch + 1) % 8 if (batch + 1) % 8 else 0), (0, 0)))

    mod_all = _ada_mod(cond, ada_w, ada_b)

    xa = jnp.concatenate([ctx, x], axis=1).reshape(rows, d)
    tm_big = _pick_tile(seq, 1088)
    tn = lambda n_: _pick_tile(n_, 512, LANES)

    for l in range(depth):
        lam_init = 0.8 - 0.6 * math.exp(-0.3 * l)
        mod = mod_all[l].reshape(cond.shape[0], 6, d)
        mods = [jnp.stack([jnp.broadcast_to(mod[0, i], (batch, d)), mod[1:batch + 1, i]], axis=1)
                for i in range(6)]

        w_main = jnp.concatenate([w_in[l, :, src_col[n_][0]:src_col[n_][1]] for n_ in order],
                                 axis=1).astype(BF16)
        w_kr = jnp.concatenate([w_in[l, :, src_col["mkr"][0]:src_col["mkr"][1]]] * (LANES // ROPE_DIM),
                               axis=1).astype(BF16)

        h = _norm_mod(xa, norm_mix[l], mods[0], mods[1], rows_per_batch=seq, ctx_len=ctx_len)
        proj = _mm([h], w_main, name="in_proj", out_dtype=BF16, tm=tm_big, tn=tn(n_main))
        kr = _mm([h], w_kr, name="in_proj_kr", out_dtype=BF16, tm=tm_big, tn=LANES)
        qk = _rope(proj, n_rope, tables, rope_scale, rows_per_batch=seq)
        krr = _rope(kr, LANES, tables, jnp.ones((LANES,), F32), rows_per_batch=seq)

        lv = diff_lambda[l].astype(F32)
        lam = jnp.exp(jnp.sum(lv[0] * lv[1])) - jnp.exp(jnp.sum(lv[2] * lv[3])) + lam_init
        a_out = _diff_attn(qk, proj, lam, diff_subln[l], batch=batch, seq=seq, ctx_len=ctx_len,
                           heads=heads, q_blk=col["dq"] // LANES, k_blk=col["dk"] // LANES,
                           v_blk=col["dv"] // LANES, post=1.0 - lam_init)

        s5p = _s5_params(s5_a_re[l], s5_a_im[l], s5_log_dt[l], s5_b_re[l], s5_b_im[l],
                         s5_c_re[l], s5_c_im[l])
        s_act = _s5_mix(proj, col["su"], s5p, s5_d[l], batch=batch, seq=seq, ctx_len=ctx_len, gw=gw)
        s_out = _mm([s_act], s5_glu_w[l].astype(BF16), name="s5_glu", out_dtype=BF16, tm=tm_big, tn=tn(gw),
                    bias=s5_glu_b[l], glu_in=s_act)

        cq = proj[:, col["mcq"]:col["mcq"] + q_rank]
        ckv = proj[:, col["mckv"]:col["mckv"] + kv_rank]
        w_uq = (mla_w_uq[l][:, uq_perm] * mla_q_scale).astype(BF16)
        q_up = _mm([cq], w_uq, name="mla_q_up", out_dtype=BF16, tm=tm_big, tn=tn(w_uq.shape[1]), norm_g=mla_q_norm[l])
        kv_up = _mm([ckv], mla_w_ukv[l].astype(BF16), name="mla_kv_up", out_dtype=BF16, tm=tm_big,
                    tn=tn(mla_w_ukv.shape[2]), norm_g=mla_kv_norm[l])
        q_rope = _rope(q_up, heads * ROPE_DIM, tables, jnp.ones((heads * ROPE_DIM,), F32),
                       rows_per_batch=seq, col_off=heads * MLA_NOPE)
        m_out = _mla_attn(q_up, q_rope, kv_up, krr, batch=batch, seq=seq, ctx_len=ctx_len, heads=heads)

        log_g = jax.nn.log_sigmoid(ret_decay[l].astype(F32))
        r_out = _retention(qk, proj, log_g, ret_norm[l], batch=batch, seq=seq, ctx_len=ctx_len,
                           heads=heads, q_off=col["rq"], k_off=col["rk"], v_off=col["rv"], g_off=col["rg"])

        xa = _mm([a_out, s_out, m_out, r_out], w_out[l].astype(BF16), name="out_proj", out_dtype=F32, tm=tm_big, tn=tn(d),
                 res=xa, gate=mods[2], rows_per_batch=seq, ctx_len=ctx_len)

        w_r = jnp.concatenate([moe_wg[l], moe_we[l]], axis=1).astype(F32)
        w_r = jnp.pad(w_r, ((0, 0), (0, LANES - n_route)))
        w_r_hi = w_r.astype(BF16)
        w_r_lo = (w_r - w_r_hi.astype(F32)).astype(BF16)
        b_r = jnp.pad(jnp.concatenate([moe_bg[l], moe_be[l]]).astype(F32), (0, LANES - n_route))
        h, logits = _norm_mod(xa, norm_ffn[l], mods[3], mods[4], rows_per_batch=seq, ctx_len=ctx_len,
                              router=(w_r_hi, w_r_lo, b_r.reshape(1, LANES)))
        xa = _moe(h, logits, xa, mods[5], moe_w_gate[l].astype(BF16), moe_w_up[l].astype(BF16),
                  moe_w_down[l].astype(BF16), rows_per_batch=seq, ctx_len=ctx_len)

    return _final_norm(xa, final_norm, batch=batch, rows_per_batch=seq, ctx_len=ctx_len)
```

```python
import functools
import math

import jax
import jax.numpy as jnp
from jax import lax
from jax.experimental import pallas as pl
from jax.experimental.pallas import tpu as pltpu

BF16 = jnp.bfloat16
F32 = jnp.float32

V7X_VMEM_BYTES = 64 * 2**20
VMEM_LIMIT = V7X_VMEM_BYTES - 12 * 2**20
LANES = 128

GRID_W = 64
ROPE_DIM = 64
ROPE_BASE = 10000.0
NORM_EPS = 1e-6
DIFF_HEAD_DIM = 64
S5_CH = 16
S5_STATE = 64
S5_CHUNK = 16
MLA_NOPE = 128
MLA_V = 128
RET_K = 64
RET_V = 128
RET_CHUNK = 128
MOE_GROUPS = 4
MOE_PER_GROUP = 4
MOE_TOPK = 2
MOE_TILE = 256


def _cp(*sem):
    return pltpu.CompilerParams(dimension_semantics=sem, vmem_limit_bytes=VMEM_LIMIT)


def _pick_tile(n, target, mult=16):
    best = None
    for t in range(mult, min(n, target) + 1, mult):
        if n % t == 0:
            best = t
    assert best is not None, (n, target)
    return best


def _sigmoid(x):
    return 1.0 / (1.0 + jnp.exp(-x))


def _mm_body(*refs, nx, ksizes, has_norm, has_bias, epilogue, tm, tiles_per_batch, ctx_len):
    x_refs = refs[:nx]
    w_ref = refs[nx]
    idx = nx + 1
    g_ref = b_ref = e_ref = res_ref = gate_ref = None
    if has_norm:
        g_ref = refs[idx]; idx += 1
    if has_bias:
        b_ref = refs[idx]; idx += 1
    if epilogue == "glu":
        e_ref = refs[idx]; idx += 1
    if epilogue == "resgate":
        res_ref, gate_ref = refs[idx], refs[idx + 1]; idx += 2
    o_ref = refs[idx]

    acc = None
    off = 0
    for xr, ks in zip(x_refs, ksizes):
        x = xr[...]
        if has_norm:
            xf = x.astype(F32)
            xf = xf * lax.rsqrt(jnp.mean(xf * xf, axis=-1, keepdims=True) + NORM_EPS)
            x = xf * g_ref[...]
        x = x.astype(BF16)
        w = w_ref[off:off + ks, :].astype(BF16)
        part = jnp.dot(x, w, preferred_element_type=F32)
        acc = part if acc is None else acc + part
        off += ks
    if has_bias:
        acc = acc + b_ref[...]
    if epilogue == "glu":
        acc = e_ref[...].astype(F32) * _sigmoid(acc)
    elif epilogue == "resgate":
        i = pl.program_id(0)
        row = (i % tiles_per_batch) * tm + lax.broadcasted_iota(jnp.int32, (tm, 1), 0)
        gate = jnp.where(row < ctx_len, gate_ref[0, 0:1, :], gate_ref[0, 1:2, :])
        acc = res_ref[...] + gate * acc
    o_ref[...] = acc.astype(o_ref.dtype)


def _mm(xs, w, *, name, out_dtype, tm, tn, norm_g=None, bias=None, glu_in=None, res=None, gate=None,
        rows_per_batch=None, ctx_len=0):
    m = xs[0].shape[0]
    ksizes = tuple(x.shape[1] for x in xs)
    k, n = w.shape
    assert sum(ksizes) == k and m % tm == 0 and n % tn == 0
    epilogue = "glu" if glu_in is not None else ("resgate" if res is not None else None)
    tiles_per_batch = (rows_per_batch // tm) if rows_per_batch else 1
    in_specs = [pl.BlockSpec((tm, ks), lambda i, j: (i, 0)) for ks in ksizes]
    in_specs.append(pl.BlockSpec((k, tn), lambda i, j: (0, j)))
    args = list(xs) + [w]
    if norm_g is not None:
        in_specs.append(pl.BlockSpec((1, k), lambda i, j: (0, 0)))
        args.append(norm_g.reshape(1, k).astype(F32))
    if bias is not None:
        in_specs.append(pl.BlockSpec((1, tn), lambda i, j: (0, j)))
        args.append(bias.reshape(1, n).astype(F32))
    if epilogue == "glu":
        in_specs.append(pl.BlockSpec((tm, tn), lambda i, j: (i, j)))
        args.append(glu_in)
    if epilogue == "resgate":
        tpb = tiles_per_batch
        in_specs.append(pl.BlockSpec((tm, tn), lambda i, j: (i, j)))
        in_specs.append(pl.BlockSpec((1, 2, tn), lambda i, j: (i // tpb, 0, j)))
        args += [res, gate]
    body = functools.partial(_mm_body, nx=len(xs), ksizes=ksizes, has_norm=norm_g is not None,
                             has_bias=bias is not None, epilogue=epilogue, tm=tm,
                             tiles_per_batch=tiles_per_batch, ctx_len=ctx_len)
    return pl.pallas_call(
        body,
        grid=(m // tm, n // tn),
        in_specs=in_specs,
        out_specs=pl.BlockSpec((tm, tn), lambda i, j: (i, j)),
        out_shape=jax.ShapeDtypeStruct((m, n), out_dtype),
        compiler_params=_cp("parallel", "arbitrary"),
        name=name,
    )(*args)


def _ada_body(c_ref, w_ref, b_ref, o_ref):
    acc = jnp.dot(c_ref[...].astype(BF16), w_ref[0].astype(BF16), preferred_element_type=F32)
    o_ref[0] = acc + b_ref[0]


def _ada_mod(cond, ada_w, ada_b):
    depth, d, n6 = ada_w.shape
    rows = cond.shape[0]
    tn = _pick_tile(n6, 512, LANES)
    return pl.pallas_call(
        _ada_body,
        grid=(depth, n6 // tn),
        in_specs=[pl.BlockSpec((rows, d), lambda l, j: (0, 0)),
                  pl.BlockSpec((1, d, tn), lambda l, j: (l, 0, j)),
                  pl.BlockSpec((1, 1, tn), lambda l, j: (l, 0, j))],
        out_specs=pl.BlockSpec((1, rows, tn), lambda l, j: (l, 0, j)),
        out_shape=jax.ShapeDtypeStruct((depth, rows, n6), F32),
        compiler_params=_cp("parallel", "arbitrary"),
        name="ada_mod",
    )(cond, ada_w, ada_b.reshape(depth, 1, n6).astype(F32))


def _norm_mod_body(*refs, tm, tiles_per_batch, ctx_len, router):
    if router:
        x_ref, g_ref, sh_ref, sc_ref, whi_ref, wlo_ref, br_ref, h_ref, lg_ref = refs
    else:
        x_ref, g_ref, sh_ref, sc_ref, h_ref = refs
    i = pl.program_id(0)
    x = x_ref[...]
    y = x * lax.rsqrt(jnp.mean(x * x, axis=-1, keepdims=True) + NORM_EPS) * g_ref[...]
    is_ctx = (i % tiles_per_batch) * tm < ctx_len
    sh = jnp.where(is_ctx, sh_ref[0, 0:1, :], sh_ref[0, 1:2, :])
    sc = jnp.where(is_ctx, sc_ref[0, 0:1, :], sc_ref[0, 1:2, :])
    h = y * (1.0 + sc) + sh
    h_ref[...] = h.astype(BF16)
    if router:
        hi = h.astype(BF16)
        lo = (h - hi.astype(F32)).astype(BF16)
        lg = jnp.dot(hi, whi_ref[...], preferred_element_type=F32)
        lg = lg + jnp.dot(hi, wlo_ref[...], preferred_element_type=F32)
        lg = lg + jnp.dot(lo, whi_ref[...], preferred_element_type=F32)
        lg_ref[...] = lg + br_ref[...]


def _norm_mod(x, g, shift, scale, *, rows_per_batch, ctx_len, router=None):
    m, d = x.shape
    tm = _pick_tile(math.gcd(rows_per_batch, ctx_len), 256, 8)
    tpb = rows_per_batch // tm
    in_specs = [
        pl.BlockSpec((tm, d), lambda i: (i, 0)),
        pl.BlockSpec((1, d), lambda i: (0, 0)),
        pl.BlockSpec((1, 2, d), lambda i: (i // tpb, 0, 0)),
        pl.BlockSpec((1, 2, d), lambda i: (i // tpb, 0, 0)),
    ]
    args = [x, g.reshape(1, d), shift, scale]
    out_specs = [pl.BlockSpec((tm, d), lambda i: (i, 0))]
    out_shape = [jax.ShapeDtypeStruct((m, d), BF16)]
    if router is not None:
        whi, wlo, br = router
        in_specs += [pl.BlockSpec((d, LANES), lambda i: (0, 0)),
                     pl.BlockSpec((d, LANES), lambda i: (0, 0)),
                     pl.BlockSpec((1, LANES), lambda i: (0, 0))]
        args += [whi, wlo, br]
        out_specs.append(pl.BlockSpec((tm, LANES), lambda i: (i, 0)))
        out_shape.append(jax.ShapeDtypeStruct((m, LANES), F32))
    body = functools.partial(_norm_mod_body, tm=tm, tiles_per_batch=tpb, ctx_len=ctx_len,
                             router=router is not None)
    outs = pl.pallas_call(body, grid=(m // tm,), in_specs=in_specs, out_specs=out_specs,
                          out_shape=out_shape, compiler_params=_cp("parallel"),
                          name="norm_mod_router" if router is not None else "norm_mod")(*args)
    return outs if router is not None else outs[0]


def _final_norm_body(x_ref, g_ref, o_ref):
    x = x_ref[...]
    o_ref[...] = x * lax.rsqrt(jnp.mean(x * x, axis=-1, keepdims=True) + NORM_EPS) * g_ref[...]


def _final_norm(x, g, *, batch, rows_per_batch, ctx_len):
    d = x.shape[1]
    n_lat = rows_per_batch - ctx_len
    tm = _pick_tile(math.gcd(n_lat, ctx_len), 256, 8)
    tpb, ctx_tiles, lat_tiles = rows_per_batch // tm, ctx_len // tm, n_lat // tm
    out = pl.pallas_call(
        _final_norm_body,
        grid=(batch, lat_tiles),
        in_specs=[pl.BlockSpec((tm, d), lambda b, i: (b * tpb + ctx_tiles + i, 0)),
                  pl.BlockSpec((1, d), lambda b, i: (0, 0))],
        out_specs=pl.BlockSpec((tm, d), lambda b, i: (b * lat_tiles + i, 0)),
        out_shape=jax.ShapeDtypeStruct((batch * n_lat, d), F32),
        compiler_params=_cp("parallel", "parallel"),
        name="final_norm",
    )(x, g.reshape(1, d))
    return out.reshape(batch, n_lat, d)


def _rope_body(x_ref, cos_ref, sa_ref, sb_ref, cs_ref, o_ref, *, reps):
    x = x_ref[...].astype(F32)
    width = x.shape[1]
    x_dn = pltpu.roll(x, width - 16, 1)
    x_up = pltpu.roll(x, 16, 1)
    tile = lambda r: jnp.tile(r[...], (1, reps))
    y = x * tile(cos_ref) + x_dn * tile(sa_ref) + x_up * tile(sb_ref)
    o_ref[...] = (y * cs_ref[...]).astype(o_ref.dtype)


def _rope(x, ncols, tables, col_scale, *, rows_per_batch, col_off=0):
    m = x.shape[0]
    cos, sa, sb = tables
    wb = max(w for w in (512, 256, 128) if ncols % w == 0 and col_off % w == 0)
    tm = _pick_tile(rows_per_batch, 1088)
    tpb = rows_per_batch // tm
    ob = col_off // wb
    body = functools.partial(_rope_body, reps=wb // LANES)
    tspec = pl.BlockSpec((tm, LANES), lambda i, j: (i % tpb, 0))
    return pl.pallas_call(
        body,
        grid=(m // tm, ncols // wb),
        in_specs=[pl.BlockSpec((tm, wb), lambda i, j: (i, j + ob)), tspec, tspec, tspec,
                  pl.BlockSpec((1, wb), lambda i, j: (0, j))],
        out_specs=pl.BlockSpec((tm, wb), lambda i, j: (i, j)),
        out_shape=jax.ShapeDtypeStruct((m, ncols), x.dtype),
        compiler_params=_cp("parallel", "arbitrary"),
        name="rope",
    )(x, cos, sa, sb, col_scale.reshape(1, ncols).astype(F32))


def _rope_tables(n_lat, ctx_len):
    rows = n_lat // GRID_W
    row = jnp.repeat(jnp.arange(rows, dtype=F32), GRID_W)
    col = jnp.tile(jnp.arange(GRID_W, dtype=F32), rows)
    quarter = ROPE_DIM // 4
    inv = ROPE_BASE ** (-jnp.arange(quarter, dtype=F32) / quarter)
    ar = row[:, None] * inv
    ac = col[:, None] * inv
    ang = jnp.concatenate([ar, ar, ac, ac], axis=-1)
    ang = jnp.concatenate([jnp.zeros((ctx_len, ROPE_DIM), F32), ang], axis=0)
    ang = jnp.tile(ang, (1, LANES // ROPE_DIM))
    cos, sin = jnp.cos(ang), jnp.sin(ang)
    lane = jnp.arange(LANES)
    even = ((lane // quarter) % 2 == 0)[None, :]
    sa = jnp.where(even, -sin, 0.0)
    sb = jnp.where(even, 0.0, sin)
    return cos, sa, sb


ATTN_ALIGN = 256


def _softmax_pv(q, k_ref, va_ref, bounds):
    ms, ovs = [], []
    for lo, hi in bounds:
        s = lax.dot_general(q, k_ref[lo:hi, :], (((1,), (1,)), ((), ())), preferred_element_type=F32)
        m = jnp.max(s, axis=-1, keepdims=True)
        e = jnp.exp2(s - m).astype(BF16)
        ovs.append(jnp.dot(e, va_ref[lo:hi, :], preferred_element_type=F32))
        ms.append(m)
    m_all = functools.reduce(jnp.maximum, ms)
    acc = sum(ov * jnp.exp2(m - m_all) for m, ov in zip(ms, ovs))
    return acc[:, :LANES] / acc[:, LANES:]


def _key_chunks(n_keys):
    if n_keys < 2 * ATTN_ALIGN:
        return ((0, n_keys),)
    half = (n_keys // ATTN_ALIGN + 1) // 2 * ATTN_ALIGN
    return ((0, half), (half, n_keys))


def _lat_tile(seq, ctx_len):
    n_lat = seq - ctx_len
    assert ctx_len % ATTN_ALIGN == 0 and n_lat % ATTN_ALIGN == 0
    return _pick_tile(n_lat, 512, ATTN_ALIGN)


def _lat_rows(seq, ctx_len, tq):
    return lambda b, i: pl.multiple_of(b * seq + ctx_len + i * tq, ATTN_ALIGN)


def _diff_attn_body(lam_ref, q_ref, k_ref, v_ref, sub_ref, *rest, post, bounds):
    o_ref, va_ref = rest[-2:]

    def fill():
        va_ref[:, :LANES] = v_ref[...]
        va_ref[:, LANES:] = jnp.ones(v_ref.shape, BF16)

    if len(rest) == 2:
        pl.when(pl.program_id(2) == 0)(fill)
    else:
        fill()

    q = q_ref[...]
    lane = lax.broadcasted_iota(jnp.int32, (1, LANES), 1)
    first = lane < DIFF_HEAD_DIM
    zero = jnp.zeros_like(q)
    q0 = jnp.where(first, q, zero)
    q1 = jnp.where(first, zero, q)
    o = _softmax_pv(q0, k_ref, va_ref, bounds) - lam_ref[0] * _softmax_pv(q1, k_ref, va_ref, bounds)
    o = o * lax.rsqrt(jnp.mean(o * o, axis=-1, keepdims=True) + NORM_EPS) * sub_ref[...] * post
    o_ref[...] = o.astype(o_ref.dtype)


def _diff_attn(qk, proj, lam, subln, *, batch, seq, ctx_len, heads, q_blk, k_blk, v_blk, post):
    tq = _lat_tile(seq, ctx_len)
    rows = _lat_rows(seq, ctx_len, tq)
    smem = pl.BlockSpec(memory_space=pltpu.SMEM)
    args = (lam.reshape(1).astype(F32), qk, qk, proj, subln.reshape(1, LANES).astype(F32))
    out_shape = jax.ShapeDtypeStruct((batch * seq, heads * LANES), BF16)
    elem = (pl.Element(tq), pl.Element(LANES))
    lat = pl.pallas_call(
        functools.partial(_diff_attn_body, post=post, bounds=_key_chunks(seq)),
        grid=(batch, heads, (seq - ctx_len) // tq),
        in_specs=[
            smem,
            pl.BlockSpec(elem, lambda b, h, i: (rows(b, i), pl.multiple_of((q_blk + h) * LANES, LANES))),
            pl.BlockSpec((seq, LANES), lambda b, h, i: (b, k_blk + h)),
            pl.BlockSpec((seq, LANES), lambda b, h, i: (b, v_blk + h)),
            pl.BlockSpec((1, LANES), lambda b, h, i: (0, 0)),
        ],
        out_specs=pl.BlockSpec(elem, lambda b, h, i: (rows(b, i), pl.multiple_of(h * LANES, LANES))),
        out_shape=out_shape,
        scratch_shapes=[pltpu.VMEM((seq, 2 * LANES), BF16)],
        compiler_params=_cp("parallel", "parallel", "arbitrary"),
        name="diff_attn",
    )(*args)
    cpb = seq // ctx_len
    ctx_spec = lambda blk: pl.BlockSpec((ctx_len, LANES), lambda b, h: (b * cpb, blk + h))
    return pl.pallas_call(
        functools.partial(_diff_attn_body, post=post, bounds=_key_chunks(ctx_len)),
        grid=(batch, heads),
        in_specs=[smem, ctx_spec(q_blk), ctx_spec(k_blk), ctx_spec(v_blk),
                  pl.BlockSpec((1, LANES), lambda b, h: (0, 0)), pl.BlockSpec(memory_space=pl.ANY)],
        out_specs=ctx_spec(0),
        out_shape=out_shape,
        scratch_shapes=[pltpu.VMEM((ctx_len, 2 * LANES), BF16)],
        input_output_aliases={5: 0},
        compiler_params=_cp("parallel", "parallel"),
        name="diff_attn_ctx",
    )(*args, lat)


def _mla_attn_body(qn_ref, qr_ref, kn_ref, kr_ref, v_ref, *rest, bounds):
    o_ref, ka_ref, va_ref = rest[-3:]
    h = pl.program_id(1)

    def fill():
        ka_ref[:, :LANES] = kn_ref[...]
        ka_ref[:, LANES:] = kr_ref[...]
        va_ref[:, :LANES] = v_ref[...]
        va_ref[:, LANES:] = jnp.ones(v_ref.shape, BF16)

    if len(rest) == 3:
        pl.when(pl.program_id(2) == 0)(fill)
    else:
        fill()

    qr = qr_ref[...]
    lane = lax.broadcasted_iota(jnp.int32, (1, LANES), 1)
    mine = (lane < ROPE_DIM) == (h % 2 == 0)
    qr = jnp.where(mine, qr, jnp.zeros_like(qr))
    q = jnp.concatenate([qn_ref[...], qr], axis=1)
    o_ref[...] = _softmax_pv(q, ka_ref, va_ref, bounds).astype(o_ref.dtype)


def _mla_attn(q_up, q_rope, kv_up, k_rope, *, batch, seq, ctx_len, heads):
    tq = _lat_tile(seq, ctx_len)
    rows = _lat_rows(seq, ctx_len, tq)
    args = (q_up, q_rope, kv_up, k_rope, kv_up)
    out_shape = jax.ShapeDtypeStruct((batch * seq, heads * LANES), BF16)
    elem = (pl.Element(tq), pl.Element(LANES))
    lat = pl.pallas_call(
        functools.partial(_mla_attn_body, bounds=_key_chunks(seq)),
        grid=(batch, heads, (seq - ctx_len) // tq),
        in_specs=[
            pl.BlockSpec(elem, lambda b, h, i: (rows(b, i), pl.multiple_of(h * LANES, LANES))),
            pl.BlockSpec(elem, lambda b, h, i: (rows(b, i), pl.multiple_of((h // 2) * LANES, LANES))),
            pl.BlockSpec((seq, LANES), lambda b, h, i: (b, 2 * h)),
            pl.BlockSpec((seq, LANES), lambda b, h, i: (b, 0)),
            pl.BlockSpec((seq, LANES), lambda b, h, i: (b, 2 * h + 1)),
        ],
        out_specs=pl.BlockSpec(elem, lambda b, h, i: (rows(b, i), pl.multiple_of(h * LANES, LANES))),
        out_shape=out_shape,
        scratch_shapes=[pltpu.VMEM((seq, 2 * LANES), BF16), pltpu.VMEM((seq, 2 * LANES), BF16)],
        compiler_params=_cp("parallel", "parallel", "arbitrary"),
        name="mla_attn",
    )(*args)
    cpb = seq // ctx_len
    ctx_spec = lambda col: pl.BlockSpec((ctx_len, LANES), lambda b, h: (b * cpb, col(h)))
    return pl.pallas_call(
        functools.partial(_mla_attn_body, bounds=_key_chunks(ctx_len)),
        grid=(batch, heads),
        in_specs=[ctx_spec(lambda h: h), ctx_spec(lambda h: h // 2), ctx_spec(lambda h: 2 * h),
                  ctx_spec(lambda h: 0), ctx_spec(lambda h: 2 * h + 1), pl.BlockSpec(memory_space=pl.ANY)],
        out_specs=ctx_spec(lambda h: h),
        out_shape=out_shape,
        scratch_shapes=[pltpu.VMEM((ctx_len, 2 * LANES), BF16), pltpu.VMEM((ctx_len, 2 * LANES), BF16)],
        input_output_aliases={5: 0},
        compiler_params=_cp("parallel", "parallel"),
        name="mla_attn_ctx",
    )(*args, lat)


def _retention_body(*refs, backward, cs, heads):
    if backward:
        lg_ref, q_ref, k_ref, v_ref, yf_ref, gate_ref, ng_ref, o_ref, s_ref, d_ref, qd_ref, kd_ref = refs
    else:
        lg_ref, q_ref, k_ref, v_ref, o_ref, s_ref, d_ref, qd_ref, kd_ref = refs
    t = pl.program_id(1)

    @pl.when(t == 0)
    def _():
        s_ref[...] = jnp.zeros_like(s_ref)
        pos_r = lax.broadcasted_iota(jnp.int32, (cs, cs), 0).astype(F32)
        pos_c = lax.broadcasted_iota(jnp.int32, (cs, cs), 1).astype(F32)
        pos = lax.broadcasted_iota(jnp.int32, (cs, 1), 0).astype(F32)
        for h in range(heads):
            lg = lg_ref[h]
            if backward:
                diff = pos_c - pos_r
                keep = diff > 0
                qd_ref[h] = jnp.exp(lg * (cs - pos))
                kd_ref[h] = jnp.exp(lg * pos)
            else:
                diff = pos_r - pos_c
                keep = diff >= 0
                qd_ref[h] = jnp.exp(lg * (pos + 1.0))
                kd_ref[h] = jnp.exp(lg * (cs - 1.0 - pos))
            d_ref[h] = jnp.where(keep, jnp.exp(lg * jnp.maximum(diff, 0.0)), 0.0)

    lane = lax.broadcasted_iota(jnp.int32, (1, LANES), 1)
    nt = (((1,), (1,)), ((), ()))
    tn = (((0,), (0,)), ((), ()))
    for h in range(heads):
        blk = slice((h // 2) * LANES, (h // 2 + 1) * LANES)
        col = slice(h * RET_V, (h + 1) * RET_V)
        mine = (lane < RET_K) == (h % 2 == 0)
        q = q_ref[:, blk]
        q = jnp.where(mine, q, jnp.zeros_like(q))
        k = k_ref[:, blk]
        v = v_ref[:, col]
        scores = lax.dot_general(q, k, nt, preferred_element_type=F32) * d_ref[h]
        intra = jnp.dot(scores.astype(BF16), v, preferred_element_type=F32)
        state = s_ref[h]
        q_w = (q.astype(F32) * qd_ref[h]).astype(BF16)
        cross = jnp.dot(q_w, state.astype(BF16), preferred_element_type=F32)
        k_w = (k.astype(F32) * kd_ref[h]).astype(BF16)
        upd = lax.dot_general(k_w, v, tn, preferred_element_type=F32)
        s_ref[h] = jnp.exp(lg_ref[h] * cs) * state + upd
        y = intra + cross
        if backward:
            y = y + yf_ref[:, col]
            y = y * lax.rsqrt(jnp.mean(y * y, axis=-1, keepdims=True) + NORM_EPS) * ng_ref[:, col]
            g = gate_ref[:, col].astype(F32)
            o_ref[:, col] = (g * _sigmoid(g) * y).astype(o_ref.dtype)
        else:
            o_ref[:, col] = y


def _retention(qk, proj, log_g, norm_g, *, batch, seq, ctx_len, heads, q_off, k_off, v_off, g_off):
    cs = RET_CHUNK
    nc, nc_ctx = seq // cs, ctx_len // cs
    wqk, wv = heads * RET_K, heads * RET_V
    assert q_off % wqk == 0 and k_off % wqk == 0 and v_off % wv == 0 and g_off % wv == 0
    smem = pl.BlockSpec(memory_space=pltpu.SMEM)

    def fwd_chunk(t):
        return t

    def bwd_chunk(t):
        return jnp.where(t < nc_ctx, nc_ctx - 1 - t, nc - 1 - (t - nc_ctx))

    def specs(chunk):
        row = lambda b, t: b * nc + chunk(t)
        return (pl.BlockSpec((cs, wqk), lambda b, t: (row(b, t), q_off // wqk)),
                pl.BlockSpec((cs, wqk), lambda b, t: (row(b, t), k_off // wqk)),
                pl.BlockSpec((cs, wv), lambda b, t: (row(b, t), v_off // wv)),
                pl.BlockSpec((cs, wv), lambda b, t: (row(b, t), 0)),
                pl.BlockSpec((cs, wv), lambda b, t: (row(b, t), g_off // wv)))

    scratch = [pltpu.VMEM((heads, LANES, RET_V), F32), pltpu.VMEM((heads, cs, cs), F32),
               pltpu.VMEM((heads, cs, 1), F32), pltpu.VMEM((heads, cs, 1), F32)]
    qs, ks, vs, ys, gs = specs(fwd_chunk)
    y_f = pl.pallas_call(
        functools.partial(_retention_body, backward=False, cs=cs, heads=heads),
        grid=(batch, nc),
        in_specs=[smem, qs, ks, vs],
        out_specs=ys,
        out_shape=jax.ShapeDtypeStruct((batch * seq, wv), F32),
        scratch_shapes=scratch,
        compiler_params=_cp("parallel", "arbitrary"),
        name="retention_fwd",
    )(log_g[0].astype(F32), qk, qk, proj)
    qs, ks, vs, ys, gs = specs(bwd_chunk)
    return pl.pallas_call(
        functools.partial(_retention_body, backward=True, cs=cs, heads=heads),
        grid=(batch, nc),
        in_specs=[smem, qs, ks, vs, ys, gs, pl.BlockSpec((1, wv), lambda b, t: (0, 0))],
        out_specs=ys,
        out_shape=jax.ShapeDtypeStruct((batch * seq, wv), BF16),
        scratch_shapes=scratch,
        compiler_params=_cp("parallel", "arbitrary"),
        name="retention_bwd",
    )(log_g[1].astype(F32), qk, qk, proj, y_f, proj, norm_g.reshape(1, wv).astype(F32))


def _s5_params(a_re, a_im, log_dt, b_re, b_im, c_re, c_im):
    L, ch = S5_CHUNK, S5_CH
    a_re, a_im = a_re.astype(F32), a_im.astype(F32)
    dt = jnp.exp(log_dt.astype(F32))[..., None]
    e = jnp.arange(L + 1, dtype=F32)[:, None, None, None]
    mag = jnp.exp(a_re * dt * e)
    pw_re, pw_im = mag * jnp.cos(a_im * dt * e), mag * jnp.sin(a_im * dt * e)
    ab_re, ab_im = pw_re[1], pw_im[1]
    den = a_re * a_re + a_im * a_im
    f_re = ((ab_re - 1.0) * a_re + ab_im * a_im) / den
    f_im = (ab_im * a_re - (ab_re - 1.0) * a_im) / den
    bb_re = f_re[..., None] * b_re - f_im[..., None] * b_im
    bb_im = f_re[..., None] * b_im + f_im[..., None] * b_re
    c_re, c_im = c_re.astype(F32), c_im.astype(F32)

    cp_re = c_re[None] * pw_re[:L, :, :, None, :] - c_im[None] * pw_im[:L, :, :, None, :]
    cp_im = c_re[None] * pw_im[:L, :, :, None, :] + c_im[None] * pw_re[:L, :, :, None, :]
    hp = lax.Precision.HIGHEST
    kmat = (jnp.einsum("ldgcp,dgpk->dglck", cp_re, bb_re, precision=hp)
            - jnp.einsum("ldgcp,dgpk->dglck", cp_im, bb_im, precision=hp))
    idx = jnp.arange(L)
    lag_f = idx[None, :] - idx[:, None]
    lag_b = -lag_f

    def toeplitz(k, lag):
        t = k[:, jnp.clip(lag, 0, L - 1)]
        t = jnp.where((lag >= 0)[None, :, :, None, None], t, 0.0)
        return t.transpose(0, 1, 4, 2, 3).reshape(k.shape[0], L * ch, L * ch)

    m_both = jnp.concatenate([toeplitz(kmat[0], lag_f), toeplitz(kmat[1], lag_b)], axis=-1)

    def state_in(d, exps):
        p_re, p_im = pw_re[exps, d], pw_im[exps, d]
        w_re = p_re[..., None] * bb_re[d][None] - p_im[..., None] * bb_im[d][None]
        w_im = p_re[..., None] * bb_im[d][None] + p_im[..., None] * bb_re[d][None]
        w = jnp.concatenate([w_re, w_im], axis=2)
        return w.transpose(1, 0, 3, 2).reshape(w.shape[1], L * ch, 2 * S5_STATE)

    w_both = jnp.concatenate([state_in(0, L - 1 - idx), state_in(1, idx)], axis=-1)

    def state_out(d, exps):
        p_re, p_im = pw_re[exps, d], pw_im[exps, d]
        v_re = c_re[d][None] * p_re[:, :, None, :] - c_im[d][None] * p_im[:, :, None, :]
        v_im = c_re[d][None] * p_im[:, :, None, :] + c_im[d][None] * p_re[:, :, None, :]
        v = jnp.concatenate([v_re, -v_im], axis=-1)
        return v.transpose(1, 3, 0, 2).reshape(v.shape[1], 2 * S5_STATE, L * ch)

    v_mat = jnp.stack([state_out(0, idx + 1), state_out(1, L - idx)])
    a_pow = jnp.concatenate([pw_re[L], pw_im[L]], axis=-1)
    return m_both.astype(BF16), w_both.astype(BF16), v_mat.astype(BF16), a_pow


def _s5_state_in_body(u_ref, w_ref, o_ref):
    s = jnp.dot(u_ref[0], w_ref[0], preferred_element_type=F32)
    o_ref[0] = s[:, :LANES]
    o_ref[1] = s[:, LANES:]


def _s5_scan_body(s_ref, a_ref, o_ref, *, nk, nk_ctx):
    d = pl.program_id(0)
    a = a_ref[0]
    lane = lax.broadcasted_iota(jnp.int32, a.shape, 1)
    first = lane < S5_STATE
    a_sw = pltpu.roll(a, S5_STATE, 1)
    a_same = jnp.where(first, a, a_sw)
    a_cross = jnp.where(first, -a_sw, a)
    a_same, a_cross = a_same[None], a_cross[None]

    def step(k, carry):
        hstate, hswap = carry
        o_ref[0, k] = hstate
        s = s_ref[0, k]
        return (a_same * hstate + a_cross * hswap + s,
                a_same * hswap - a_cross * hstate + pltpu.roll(s, S5_STATE, 2))

    zero = jnp.zeros(o_ref.shape[2:], F32)
    unroll = 8 if (nk % 8 == 0 and nk_ctx % 8 == 0) else 1

    @pl.when(d == 0)
    def _():
        lax.fori_loop(0, nk, step, (zero, zero), unroll=unroll)

    @pl.when(d == 1)
    def _():
        hc = lax.fori_loop(0, nk_ctx, lambda t, c: step(nk_ctx - 1 - t, c), (zero, zero), unroll=unroll)
        lax.fori_loop(0, nk - nk_ctx, lambda t, c: step(nk - 1 - t, c), hc, unroll=unroll)


def _s5_out_body(u_ref, m_ref, h_ref, v_ref, d_ref, o_ref):
    u = u_ref[0]
    y = jnp.dot(u, m_ref[0], preferred_element_type=F32)
    y = y[:, :2 * LANES] + y[:, 2 * LANES:]
    y = y + jnp.dot(h_ref[0].astype(BF16), v_ref[0, 0], preferred_element_type=F32)
    y = y + jnp.dot(h_ref[1].astype(BF16), v_ref[1, 0], preferred_element_type=F32)
    y = y + d_ref[0] * u.astype(F32)
    o_ref[0] = jax.nn.gelu(y).astype(o_ref.dtype)


def _s5_mix(proj, su_off, params, d_skip, *, batch, seq, ctx_len, gw):
    m_both, w_both, v_mat, a_pow = params
    L, ch, st2 = S5_CHUNK, S5_CH, 2 * S5_STATE
    groups = gw // ch
    nk, nk_ctx = seq // L, ctx_len // L
    rows = nk * batch
    cols = L * ch
    u = proj[:, su_off:su_off + gw].reshape(batch, nk, L, groups, ch)
    u = u.transpose(3, 1, 0, 2, 4).reshape(groups, rows, cols)

    s_in = pl.pallas_call(
        _s5_state_in_body,
        grid=(groups,),
        in_specs=[pl.BlockSpec((1, rows, cols), lambda g: (g, 0, 0)),
                  pl.BlockSpec((1, cols, 2 * st2), lambda g: (g, 0, 0))],
        out_specs=pl.BlockSpec((2, rows, st2), lambda g: (0, 0, g)),
        out_shape=jax.ShapeDtypeStruct((2, rows, groups * st2), F32),
        compiler_params=_cp("parallel"),
        name="s5_state_in",
    )(u, w_both)

    gb = 8
    s4 = s_in.reshape(2, nk, batch, groups, st2)
    h_prev = pl.pallas_call(
        functools.partial(_s5_scan_body, nk=nk, nk_ctx=nk_ctx),
        grid=(2, groups // gb),
        in_specs=[pl.BlockSpec((1, nk, batch, gb, st2), lambda d, g: (d, 0, 0, g, 0)),
                  pl.BlockSpec((1, gb, st2), lambda d, g: (d, g, 0))],
        out_specs=pl.BlockSpec((1, nk, batch, gb, st2), lambda d, g: (d, 0, 0, g, 0)),
        out_shape=jax.ShapeDtypeStruct((2, nk, batch, groups, st2), F32),
        compiler_params=_cp("parallel", "parallel"),
        name="s5_scan",
    )(s4, a_pow)
    h2 = h_prev.reshape(2, rows, groups * st2)

    d_rep = jnp.tile(d_skip.astype(F32), (1, L)).reshape(groups, 1, cols)
    y = pl.pallas_call(
        _s5_out_body,
        grid=(groups,),
        in_specs=[pl.BlockSpec((1, rows, cols), lambda g: (g, 0, 0)),
                  pl.BlockSpec((1, cols, 2 * cols), lambda g: (g, 0, 0)),
                  pl.BlockSpec((2, rows, st2), lambda g: (0, 0, g)),
                  pl.BlockSpec((2, 1, st2, cols), lambda g: (0, g, 0, 0)),
                  pl.BlockSpec((1, 1, cols), lambda g: (g, 0, 0))],
        out_specs=pl.BlockSpec((1, rows, cols), lambda g: (g, 0, 0)),
        out_shape=jax.ShapeDtypeStruct((groups, rows, cols), BF16),
        compiler_params=_cp("parallel"),
        name="s5_out",
    )(u, m_both, h2, v_mat, d_rep)
    y = y.reshape(groups, nk, batch, L, ch).transpose(2, 1, 3, 0, 4)
    return y.reshape(batch * seq, gw)


def _moe_up_body(te_ref, x_ref, wg_ref, wu_ref, o_ref):
    x = x_ref[...]
    a = jnp.dot(x, wg_ref[0], preferred_element_type=F32)
    b = jnp.dot(x, wu_ref[0], preferred_element_type=F32)
    o_ref[...] = (a * _sigmoid(a) * b).astype(o_ref.dtype)


def _moe_down_body(te_ref, h_ref, w_ref, rw_ref, o_ref):
    y = jnp.dot(h_ref[...], w_ref[0], preferred_element_type=F32)
    o_ref[...] = (rw_ref[...] * y).astype(o_ref.dtype)


def _moe_combine_body(x_ref, y0_ref, y1_ref, gate_ref, o_ref, *, tm, tiles_per_batch, ctx_len):
    i = pl.program_id(0)
    is_ctx = (i % tiles_per_batch) * tm < ctx_len
    gate = jnp.where(is_ctx, gate_ref[0, 0:1, :], gate_ref[0, 1:2, :])
    o_ref[...] = x_ref[...] + gate * (y0_ref[...].astype(F32) + y1_ref[...].astype(F32))


def _route(logits):
    assert MOE_TOPK == 2
    g_logit = logits[:, :MOE_GROUPS]
    g_prob = jax.nn.softmax(g_logit, axis=-1)
    g_idx = jnp.argmax(g_prob, axis=-1)
    g_p = jnp.max(g_prob, axis=-1)
    e_logit = logits[:, MOE_GROUPS:MOE_GROUPS + MOE_GROUPS * MOE_PER_GROUP]
    e_logit = e_logit.reshape(-1, MOE_GROUPS, MOE_PER_GROUP)
    sel = (jnp.arange(MOE_GROUPS)[None, :] == g_idx[:, None])[:, :, None]
    e_logit = jnp.sum(jnp.where(sel, e_logit, 0.0), axis=1)
    e_prob = jax.nn.softmax(e_logit, axis=-1)
    i0 = jnp.argmax(e_prob, axis=-1)
    p0 = jnp.max(e_prob, axis=-1)
    rest = jnp.where(jnp.arange(MOE_PER_GROUP)[None, :] == i0[:, None], -1.0, e_prob)
    i1 = jnp.argmax(rest, axis=-1)
    p1 = jnp.max(rest, axis=-1)
    e_p = jnp.stack([p0, p1], axis=-1)
    w = g_p[:, None] * e_p / jnp.sum(e_p, axis=-1, keepdims=True)
    ids = g_idx[:, None] * MOE_PER_GROUP + jnp.stack([i0, i1], axis=-1)
    return ids.astype(jnp.int32), w


def _moe(h, logits, x, gate, w_gate, w_up, w_down, *, rows_per_batch, ctx_len):
    t, d = h.shape
    n_exp, _, dff = w_gate.shape
    tile = MOE_TILE
    ids, wts = _route(logits)
    flat_e = ids.reshape(-1)
    onehot = (flat_e[:, None] == jnp.arange(n_exp)[None, :]).astype(jnp.int32)
    counts = onehot.sum(0)
    rank = jnp.take_along_axis(jnp.cumsum(onehot, axis=0) - onehot, flat_e[:, None], axis=1)[:, 0]
    padded = (counts + tile - 1) // tile * tile
    starts = jnp.cumsum(padded) - padded
    pos = starts[flat_e] + rank
    n_rows = (t * MOE_TOPK // tile + n_exp) * tile
    n_tiles = n_rows // tile
    src = jnp.zeros((n_rows,), jnp.int32).at[pos].set(jnp.arange(t * MOE_TOPK, dtype=jnp.int32) // MOE_TOPK)
    row_w = jnp.zeros((n_rows,), F32).at[pos].set(wts.reshape(-1))
    tile_start = jnp.arange(n_tiles, dtype=jnp.int32) * tile
    ends = starts + padded
    tile_e = jnp.minimum(jnp.sum(tile_start[:, None] >= ends[None, :], axis=1), n_exp - 1).astype(jnp.int32)

    xs = h.at[src].get(mode="promise_in_bounds")
    hid = pl.pallas_call(
        _moe_up_body,
        grid_spec=pltpu.PrefetchScalarGridSpec(
            num_scalar_prefetch=1,
            grid=(n_tiles,),
            in_specs=[pl.BlockSpec((tile, d), lambda i, te: (i, 0)),
                      pl.BlockSpec((1, d, dff), lambda i, te: (te[i], 0, 0)),
                      pl.BlockSpec((1, d, dff), lambda i, te: (te[i], 0, 0))],
            out_specs=pl.BlockSpec((tile, dff), lambda i, te: (i, 0))),
        out_shape=jax.ShapeDtypeStruct((n_rows, dff), BF16),
        compiler_params=_cp("arbitrary"),
        name="moe_up",
    )(tile_e, xs, w_gate, w_up)
    ys = pl.pallas_call(
        _moe_down_body,
        grid_spec=pltpu.PrefetchScalarGridSpec(
            num_scalar_prefetch=1,
            grid=(n_tiles,),
            in_specs=[pl.BlockSpec((tile, dff), lambda i, te: (i, 0)),
                      pl.BlockSpec((1, dff, d), lambda i, te: (te[i], 0, 0)),
                      pl.BlockSpec((tile, 1), lambda i, te: (i, 0))],
            out_specs=pl.BlockSpec((tile, d), lambda i, te: (i, 0))),
        out_shape=jax.ShapeDtypeStruct((n_rows, d), BF16),
        compiler_params=_cp("arbitrary"),
        name="moe_down",
    )(tile_e, hid, w_down, row_w.reshape(n_rows, 1))

    pos2 = pos.reshape(t, MOE_TOPK)
    y0 = ys.at[pos2[:, 0]].get(mode="promise_in_bounds")
    y1 = ys.at[pos2[:, 1]].get(mode="promise_in_bounds")
    tm = _pick_tile(math.gcd(rows_per_batch, ctx_len), 256, 8)
    tpb = rows_per_batch // tm
    row_spec = pl.BlockSpec((tm, d), lambda i: (i, 0))
    return pl.pallas_call(
        functools.partial(_moe_combine_body, tm=tm, tiles_per_batch=tpb, ctx_len=ctx_len),
        grid=(t // tm,),
        in_specs=[row_spec, row_spec, row_spec, pl.BlockSpec((1, 2, d), lambda i: (i // tpb, 0, 0))],
        out_specs=row_spec,
        out_shape=jax.ShapeDtypeStruct((t, d), F32),
        compiler_params=_cp("parallel"),
        name="moe_combine",
    )(x, y0, y1, gate)


def kernel(x, c, ctx, c_ctx, ada_w, ada_b, norm_mix, norm_ffn, w_in, w_out, diff_lambda, diff_subln,
           s5_a_re, s5_a_im, s5_log_dt, s5_b_re, s5_b_im, s5_c_re, s5_c_im, s5_d, s5_glu_w, s5_glu_b,
           mla_q_norm, mla_kv_norm, mla_w_uq, mla_w_ukv, ret_decay, ret_norm,
           moe_wg, moe_bg, moe_we, moe_be, moe_w_gate, moe_w_up, moe_w_down, final_norm):
    batch, n_lat, d = x.shape
    ctx_len = ctx.shape[1]
    depth = ada_w.shape[0]
    seq = ctx_len + n_lat
    rows = batch * seq
    gw = d // 4
    heads = gw // LANES
    q_rank, kv_rank = 3 * d // 16, d // 16
    ret_qk = heads * RET_K
    n_route = MOE_GROUPS + MOE_GROUPS * MOE_PER_GROUP
    assert heads % 2 == 0 and ctx_len % RET_CHUNK == 0 and n_lat % RET_CHUNK == 0

    splits = (gw, gw, gw, gw, q_rank, kv_rank, ROPE_DIM, ret_qk, ret_qk, gw, gw)
    offs = [0]
    for s_ in splits:
        offs.append(offs[-1] + s_)
    names = ("dq", "dk", "dv", "su", "mcq", "mckv", "mkr", "rq", "rk", "rv", "rg")
    src_col = {n_: (offs[i], offs[i + 1]) for i, n_ in enumerate(names)}
    order = ("dq", "dk", "rq", "rk", "dv", "su", "rv", "rg", "mcq", "mckv")
    col = {}
    pos = 0
    for n_ in order:
        col[n_] = pos
        pos += src_col[n_][1] - src_col[n_][0]
    n_main = pos
    n_rope = col["dv"]
    uq_cols = jnp.arange(heads * (MLA_NOPE + ROPE_DIM)).reshape(heads, MLA_NOPE + ROPE_DIM)
    uq_perm = jnp.concatenate([uq_cols[:, :MLA_NOPE].reshape(-1), uq_cols[:, MLA_NOPE:].reshape(-1)])

    tables = _rope_tables(n_lat, ctx_len)
    log2e = math.log2(math.e)
    rope_scale = jnp.ones((n_rope,), F32).at[col["rk"]:col["rk"] + ret_qk].set(RET_K ** -0.5)
    rope_scale = rope_scale.at[col["dq"]:col["dq"] + gw].set(DIFF_HEAD_DIM ** -0.5 * log2e)
    mla_q_scale = (MLA_NOPE + ROPE_DIM) ** -0.5 * log2e

    cond = jnp.concatenate([c_ctx[None, :], c], axis=0)
    cond = jnp.pad(cond * _sigmoid(cond), ((0, 8 - (batch + 1) % 8 if (batch + 1) % 8 else 0), (0, 0)))

    mod_all = _ada_mod(cond, ada_w, ada_b)

    xa = jnp.concatenate([ctx, x], axis=1).reshape(rows, d)
    tm_big = _pick_tile(seq, 1088)
    tn = lambda n_: _pick_tile(n_, 512, LANES)

    for l in range(depth):
        lam_init = 0.8 - 0.6 * math.exp(-0.3 * l)
        mod = mod_all[l].reshape(cond.shape[0], 6, d)
        mods = [jnp.stack([jnp.broadcast_to(mod[0, i], (batch, d)), mod[1:batch + 1, i]], axis=1)
                for i in range(6)]

        w_main = jnp.concatenate([w_in[l, :, src_col[n_][0]:src_col[n_][1]] for n_ in order],
                                 axis=1).astype(BF16)
        w_kr = jnp.concatenate([w_in[l, :, src_col["mkr"][0]:src_col["mkr"][1]]] * (LANES // ROPE_DIM),
                               axis=1).astype(BF16)

        h = _norm_mod(xa, norm_mix[l], mods[0], mods[1], rows_per_batch=seq, ctx_len=ctx_len)
        proj = _mm([h], w_main, name="in_proj", out_dtype=BF16, tm=tm_big, tn=tn(n_main))
        kr = _mm([h], w_kr, name="in_proj_kr", out_dtype=BF16, tm=tm_big, tn=LANES)
        qk = _rope(proj, n_rope, tables, rope_scale, rows_per_batch=seq)
        krr = _rope(kr, LANES, tables, jnp.ones((LANES,), F32), rows_per_batch=seq)

        lv = diff_lambda[l].astype(F32)
        lam = jnp.exp(jnp.sum(lv[0] * lv[1])) - jnp.exp(jnp.sum(lv[2] * lv[3])) + lam_init
        a_out = _diff_attn(qk, proj, lam, diff_subln[l], batch=batch, seq=seq, ctx_len=ctx_len,
                           heads=heads, q_blk=col["dq"] // LANES, k_blk=col["dk"] // LANES,
                           v_blk=col["dv"] // LANES, post=1.0 - lam_init)

        s5p = _s5_params(s5_a_re[l], s5_a_im[l], s5_log_dt[l], s5_b_re[l], s5_b_im[l],
                         s5_c_re[l], s5_c_im[l])
        s_act = _s5_mix(proj, col["su"], s5p, s5_d[l], batch=batch, seq=seq, ctx_len=ctx_len, gw=gw)
        s_out = _mm([s_act], s5_glu_w[l].astype(BF16), name="s5_glu", out_dtype=BF16, tm=tm_big, tn=tn(gw),
                    bias=s5_glu_b[l], glu_in=s_act)

        cq = proj[:, col["mcq"]:col["mcq"] + q_rank]
        ckv = proj[:, col["mckv"]:col["mckv"] + kv_rank]
        w_uq = (mla_w_uq[l][:, uq_perm] * mla_q_scale).astype(BF16)
        q_up = _mm([cq], w_uq, name="mla_q_up", out_dtype=BF16, tm=tm_big, tn=tn(w_uq.shape[1]), norm_g=mla_q_norm[l])
        kv_up = _mm([ckv], mla_w_ukv[l].astype(BF16), name="mla_kv_up", out_dtype=BF16, tm=tm_big,
                    tn=tn(mla_w_ukv.shape[2]), norm_g=mla_kv_norm[l])
        q_rope = _rope(q_up, heads * ROPE_DIM, tables, jnp.ones((heads * ROPE_DIM,), F32),
                       rows_per_batch=seq, col_off=heads * MLA_NOPE)
        m_out = _mla_attn(q_up, q_rope, kv_up, krr, batch=batch, seq=seq, ctx_len=ctx_len, heads=heads)

        log_g = jax.nn.log_sigmoid(ret_decay[l].astype(F32))
        r_out = _retention(qk, proj, log_g, ret_norm[l], batch=batch, seq=seq, ctx_len=ctx_len,
                           heads=heads, q_off=col["rq"], k_off=col["rk"], v_off=col["rv"], g_off=col["rg"])

        xa = _mm([a_out, s_out, m_out, r_out], w_out[l].astype(BF16), name="out_proj", out_dtype=F32, tm=tm_big, tn=tn(d),
                 res=xa, gate=mods[2], rows_per_batch=seq, ctx_len=ctx_len)

        w_r = jnp.concatenate([moe_wg[l], moe_we[l]], axis=1).astype(F32)
        w_r = jnp.pad(w_r, ((0, 0), (0, LANES - n_route)))
        w_r_hi = w_r.astype(BF16)
        w_r_lo = (w_r - w_r_hi.astype(F32)).astype(BF16)
        b_r = jnp.pad(jnp.concatenate([moe_bg[l], moe_be[l]]).astype(F32), (0, LANES - n_route))
        h, logits = _norm_mod(xa, norm_ffn[l], mods[3], mods[4], rows_per_batch=seq, ctx_len=ctx_len,
                              router=(w_r_hi, w_r_lo, b_r.reshape(1, LANES)))
        xa = _moe(h, logits, xa, mods[5], moe_w_gate[l].astype(BF16), moe_w_up[l].astype(BF16),
                  moe_w_down[l].astype(BF16), rows_per_batch=seq, ctx_len=ctx_len)

    return _final_norm(xa, final_norm, batch=batch, rows_per_batch=seq, ctx_len=ctx_len)
```

```python
import functools
import math

import jax
import jax.numpy as jnp
from jax import lax
from jax.experimental import pallas as pl
from jax.experimental.pallas import tpu as pltpu

BF16 = jnp.bfloat16
F32 = jnp.float32

V7X_VMEM_BYTES = 64 * 2**20
VMEM_LIMIT = V7X_VMEM_BYTES - 12 * 2**20
LANES = 128

GRID_W = 64
ROPE_DIM = 64
ROPE_BASE = 10000.0
NORM_EPS = 1e-6
DIFF_HEAD_DIM = 64
S5_CH = 16
S5_STATE = 64
S5_CHUNK = 16
MLA_NOPE = 128
MLA_V = 128
RET_K = 64
RET_V = 128
RET_CHUNK = 128
MOE_GROUPS = 4
MOE_PER_GROUP = 4
MOE_TOPK = 2
MOE_TILE = 256


def _cp(*sem):
    return pltpu.CompilerParams(dimension_semantics=sem, vmem_limit_bytes=VMEM_LIMIT)


def _pick_tile(n, target, mult=16):
    best = None
    for t in range(mult, min(n, target) + 1, mult):
        if n % t == 0:
            best = t
    assert best is not None, (n, target)
    return best


def _sigmoid(x):
    return 1.0 / (1.0 + jnp.exp(-x))


def _mm_body(*refs, nx, ksizes, has_norm, has_bias, epilogue, tm, tiles_per_batch, ctx_len):
    x_refs = refs[:nx]
    w_ref = refs[nx]
    idx = nx + 1
    g_ref = b_ref = e_ref = res_ref = gate_ref = None
    if has_norm:
        g_ref = refs[idx]; idx += 1
    if has_bias:
        b_ref = refs[idx]; idx += 1
    if epilogue == "glu":
        e_ref = refs[idx]; idx += 1
    if epilogue == "resgate":
        res_ref, gate_ref = refs[idx], refs[idx + 1]; idx += 2
    o_ref = refs[idx]

    acc = None
    off = 0
    for xr, ks in zip(x_refs, ksizes):
        x = xr[...]
        if has_norm:
            xf = x.astype(F32)
            xf = xf * lax.rsqrt(jnp.mean(xf * xf, axis=-1, keepdims=True) + NORM_EPS)
            x = xf * g_ref[...]
        x = x.astype(BF16)
        w = w_ref[off:off + ks, :].astype(BF16)
        part = jnp.dot(x, w, preferred_element_type=F32)
        acc = part if acc is None else acc + part
        off += ks
    if has_bias:
        acc = acc + b_ref[...]
    if epilogue == "glu":
        acc = e_ref[...].astype(F32) * _sigmoid(acc)
    elif epilogue == "resgate":
        i = pl.program_id(0)
        row = (i % tiles_per_batch) * tm + lax.broadcasted_iota(jnp.int32, (tm, 1), 0)
        gate = jnp.where(row < ctx_len, gate_ref[0, 0:1, :], gate_ref[0, 1:2, :])
        acc = res_ref[...] + gate * acc
    o_ref[...] = acc.astype(o_ref.dtype)


def _mm(xs, w, *, name, out_dtype, tm, tn, norm_g=None, bias=None, glu_in=None, res=None, gate=None,
        rows_per_batch=None, ctx_len=0):
    m = xs[0].shape[0]
    ksizes = tuple(x.shape[1] for x in xs)
    k, n = w.shape
    assert sum(ksizes) == k and m % tm == 0 and n % tn == 0
    epilogue = "glu" if glu_in is not None else ("resgate" if res is not None else None)
    tiles_per_batch = (rows_per_batch // tm) if rows_per_batch else 1
    in_specs = [pl.BlockSpec((tm, ks), lambda i, j: (i, 0)) for ks in ksizes]
    in_specs.append(pl.BlockSpec((k, tn), lambda i, j: (0, j)))
    args = list(xs) + [w]
    if norm_g is not None:
        in_specs.append(pl.BlockSpec((1, k), lambda i, j: (0, 0)))
        args.append(norm_g.reshape(1, k).astype(F32))
    if bias is not None:
        in_specs.append(pl.BlockSpec((1, tn), lambda i, j: (0, j)))
        args.append(bias.reshape(1, n).astype(F32))
    if epilogue == "glu":
        in_specs.append(pl.BlockSpec((tm, tn), lambda i, j: (i, j)))
        args.append(glu_in)
    if epilogue == "resgate":
        tpb = tiles_per_batch
        in_specs.append(pl.BlockSpec((tm, tn), lambda i, j: (i, j)))
        in_specs.append(pl.BlockSpec((1, 2, tn), lambda i, j: (i // tpb, 0, j)))
        args += [res, gate]
    body = functools.partial(_mm_body, nx=len(xs), ksizes=ksizes, has_norm=norm_g is not None,
                             has_bias=bias is not None, epilogue=epilogue, tm=tm,
                             tiles_per_batch=tiles_per_batch, ctx_len=ctx_len)
    return pl.pallas_call(
        body,
        grid=(m // tm, n // tn),
        in_specs=in_specs,
        out_specs=pl.BlockSpec((tm, tn), lambda i, j: (i, j)),
        out_shape=jax.ShapeDtypeStruct((m, n), out_dtype),
        compiler_params=_cp("parallel", "arbitrary"),
        name=name,
    )(*args)


def _ada_body(c_ref, w_ref, b_ref, o_ref):
    acc = jnp.dot(c_ref[...].astype(BF16), w_ref[0].astype(BF16), preferred_element_type=F32)
    o_ref[0] = acc + b_ref[0]


def _ada_mod(cond, ada_w, ada_b):
    depth, d, n6 = ada_w.shape
    rows = cond.shape[0]
    tn = _pick_tile(n6, 512, LANES)
    return pl.pallas_call(
        _ada_body,
        grid=(depth, n6 // tn),
        in_specs=[pl.BlockSpec((rows, d), lambda l, j: (0, 0)),
                  pl.BlockSpec((1, d, tn), lambda l, j: (l, 0, j)),
                  pl.BlockSpec((1, 1, tn), lambda l, j: (l, 0, j))],
        out_specs=pl.BlockSpec((1, rows, tn), lambda l, j: (l, 0, j)),
        out_shape=jax.ShapeDtypeStruct((depth, rows, n6), F32),
        compiler_params=_cp("parallel", "arbitrary"),
        name="ada_mod",
    )(cond, ada_w, ada_b.reshape(depth, 1, n6).astype(F32))


def _norm_mod_body(*refs, tm, tiles_per_batch, ctx_len, router):
    if router:
        x_ref, g_ref, sh_ref, sc_ref, whi_ref, wlo_ref, br_ref, h_ref, lg_ref = refs
    else:
        x_ref, g_ref, sh_ref, sc_ref, h_ref = refs
    i = pl.program_id(0)
    x = x_ref[...]
    y = x * lax.rsqrt(jnp.mean(x * x, axis=-1, keepdims=True) + NORM_EPS) * g_ref[...]
    is_ctx = (i % tiles_per_batch) * tm < ctx_len
    sh = jnp.where(is_ctx, sh_ref[0, 0:1, :], sh_ref[0, 1:2, :])
    sc = jnp.where(is_ctx, sc_ref[0, 0:1, :], sc_ref[0, 1:2, :])
    h = y * (1.0 + sc) + sh
    h_ref[...] = h.astype(BF16)
    if router:
        hi = h.astype(BF16)
        lo = (h - hi.astype(F32)).astype(BF16)
        lg = jnp.dot(hi, whi_ref[...], preferred_element_type=F32)
        lg = lg + jnp.dot(hi, wlo_ref[...], preferred_element_type=F32)
        lg = lg + jnp.dot(lo, whi_ref[...], preferred_element_type=F32)
        lg_ref[...] = lg + br_ref[...]


def _norm_mod(x, g, shift, scale, *, rows_per_batch, ctx_len, router=None):
    m, d = x.shape
    tm = _pick_tile(math.gcd(rows_per_batch, ctx_len), 256, 8)
    tpb = rows_per_batch // tm
    in_specs = [
        pl.BlockSpec((tm, d), lambda i: (i, 0)),
        pl.BlockSpec((1, d), lambda i: (0, 0)),
        pl.BlockSpec((1, 2, d), lambda i: (i // tpb, 0, 0)),
        pl.BlockSpec((1, 2, d), lambda i: (i // tpb, 0, 0)),
    ]
    args = [x, g.reshape(1, d), shift, scale]
    out_specs = [pl.BlockSpec((tm, d), lambda i: (i, 0))]
    out_shape = [jax.ShapeDtypeStruct((m, d), BF16)]
    if router is not None:
        whi, wlo, br = router
        in_specs += [pl.BlockSpec((d, LANES), lambda i: (0, 0)),
                     pl.BlockSpec((d, LANES), lambda i: (0, 0)),
                     pl.BlockSpec((1, LANES), lambda i: (0, 0))]
        args += [whi, wlo, br]
        out_specs.append(pl.BlockSpec((tm, LANES), lambda i: (i, 0)))
        out_shape.append(jax.ShapeDtypeStruct((m, LANES), F32))
    body = functools.partial(_norm_mod_body, tm=tm, tiles_per_batch=tpb, ctx_len=ctx_len,
                             router=router is not None)
    outs = pl.pallas_call(body, grid=(m // tm,), in_specs=in_specs, out_specs=out_specs,
                          out_shape=out_shape, compiler_params=_cp("parallel"),
                          name="norm_mod_router" if router is not None else "norm_mod")(*args)
    return outs if router is not None else outs[0]


def _final_norm_body(x_ref, g_ref, o_ref):
    x = x_ref[...]
    o_ref[...] = x * lax.rsqrt(jnp.mean(x * x, axis=-1, keepdims=True) + NORM_EPS) * g_ref[...]


def _final_norm(x, g, *, batch, rows_per_batch, ctx_len):
    d = x.shape[1]
    n_lat = rows_per_batch - ctx_len
    tm = _pick_tile(math.gcd(n_lat, ctx_len), 256, 8)
    tpb, ctx_tiles, lat_tiles = rows_per_batch // tm, ctx_len // tm, n_lat // tm
    out = pl.pallas_call(
        _final_norm_body,
        grid=(batch, lat_tiles),
        in_specs=[pl.BlockSpec((tm, d), lambda b, i: (b * tpb + ctx_tiles + i, 0)),
                  pl.BlockSpec((1, d), lambda b, i: (0, 0))],
        out_specs=pl.BlockSpec((tm, d), lambda b, i: (b * lat_tiles + i, 0)),
        out_shape=jax.ShapeDtypeStruct((batch * n_lat, d), F32),
        compiler_params=_cp("parallel", "parallel"),
        name="final_norm",
    )(x, g.reshape(1, d))
    return out.reshape(batch, n_lat, d)


def _rope_body(x_ref, cos_ref, sa_ref, sb_ref, cs_ref, o_ref, *, reps):
    x = x_ref[...].astype(F32)
    width = x.shape[1]
    x_dn = pltpu.roll(x, width - 16, 1)
    x_up = pltpu.roll(x, 16, 1)
    tile = lambda r: jnp.tile(r[...], (1, reps))
    y = x * tile(cos_ref) + x_dn * tile(sa_ref) + x_up * tile(sb_ref)
    o_ref[...] = (y * cs_ref[...]).astype(o_ref.dtype)


def _rope(x, ncols, tables, col_scale, *, rows_per_batch, col_off=0):
    m = x.shape[0]
    cos, sa, sb = tables
    wb = max(w for w in (512, 256, 128) if ncols % w == 0 and col_off % w == 0)
    tm = _pick_tile(rows_per_batch, 1088)
    tpb = rows_per_batch // tm
    ob = col_off // wb
    body = functools.partial(_rope_body, reps=wb // LANES)
    tspec = pl.BlockSpec((tm, LANES), lambda i, j: (i % tpb, 0))
    return pl.pallas_call(
        body,
        grid=(m // tm, ncols // wb),
        in_specs=[pl.BlockSpec((tm, wb), lambda i, j: (i, j + ob)), tspec, tspec, tspec,
                  pl.BlockSpec((1, wb), lambda i, j: (0, j))],
        out_specs=pl.BlockSpec((tm, wb), lambda i, j: (i, j)),
        out_shape=jax.ShapeDtypeStruct((m, ncols), x.dtype),
        compiler_params=_cp("parallel", "arbitrary"),
        name="rope",
    )(x, cos, sa, sb, col_scale.reshape(1, ncols).astype(F32))


def _rope_tables(n_lat, ctx_len):
    rows = n_lat // GRID_W
    row = jnp.repeat(jnp.arange(rows, dtype=F32), GRID_W)
    col = jnp.tile(jnp.arange(GRID_W, dtype=F32), rows)
    quarter = ROPE_DIM // 4
    inv = ROPE_BASE ** (-jnp.arange(quarter, dtype=F32) / quarter)
    ar = row[:, None] * inv
    ac = col[:, None] * inv
    ang = jnp.concatenate([ar, ar, ac, ac], axis=-1)
    ang = jnp.concatenate([jnp.zeros((ctx_len, ROPE_DIM), F32), ang], axis=0)
    ang = jnp.tile(ang, (1, LANES // ROPE_DIM))
    cos, sin = jnp.cos(ang), jnp.sin(ang)
    lane = jnp.arange(LANES)
    even = ((lane // quarter) % 2 == 0)[None, :]
    sa = jnp.where(even, -sin, 0.0)
    sb = jnp.where(even, 0.0, sin)
    return cos, sa, sb


ATTN_ALIGN = 256


def _softmax_pv(q, k_ref, va_ref, bounds):
    ms, ovs = [], []
    for lo, hi in bounds:
        s = lax.dot_general(q, k_ref[lo:hi, :], (((1,), (1,)), ((), ())), preferred_element_type=F32)
        m = jnp.max(s, axis=-1, keepdims=True)
        e = jnp.exp2(s - m).astype(BF16)
        ovs.append(jnp.dot(e, va_ref[lo:hi, :], preferred_element_type=F32))
        ms.append(m)
    m_all = functools.reduce(jnp.maximum, ms)
    acc = sum(ov * jnp.exp2(m - m_all) for m, ov in zip(ms, ovs))
    return acc[:, :LANES] / acc[:, LANES:]


def _key_chunks(n_keys):
    if n_keys < 2 * ATTN_ALIGN:
        return ((0, n_keys),)
    half = (n_keys // ATTN_ALIGN + 1) // 2 * ATTN_ALIGN
    return ((0, half), (half, n_keys))


def _lat_tile(seq, ctx_len):
    n_lat = seq - ctx_len
    assert ctx_len % ATTN_ALIGN == 0 and n_lat % ATTN_ALIGN == 0
    return _pick_tile(n_lat, 512, ATTN_ALIGN)


def _lat_rows(seq, ctx_len, tq):
    return lambda b, i: pl.multiple_of(b * seq + ctx_len + i * tq, ATTN_ALIGN)


def _diff_attn_body(lam_ref, q_ref, k_ref, v_ref, sub_ref, *rest, post, bounds):
    o_ref, va_ref = rest[-2:]

    def fill():
        va_ref[:, :LANES] = v_ref[...]
        va_ref[:, LANES:] = jnp.ones(v_ref.shape, BF16)

    if len(rest) == 2:
        pl.when(pl.program_id(2) == 0)(fill)
    else:
        fill()

    q = q_ref[...]
    lane = lax.broadcasted_iota(jnp.int32, (1, LANES), 1)
    first = lane < DIFF_HEAD_DIM
    zero = jnp.zeros_like(q)
    q0 = jnp.where(first, q, zero)
    q1 = jnp.where(first, zero, q)
    o = _softmax_pv(q0, k_ref, va_ref, bounds) - lam_ref[0] * _softmax_pv(q1, k_ref, va_ref, bounds)
    o = o * lax.rsqrt(jnp.mean(o * o, axis=-1, keepdims=True) + NORM_EPS) * sub_ref[...] * post
    o_ref[...] = o.astype(o_ref.dtype)


def _diff_attn(qk, proj, lam, subln, *, batch, seq, ctx_len, heads, q_blk, k_blk, v_blk, post):
    tq = _lat_tile(seq, ctx_len)
    rows = _lat_rows(seq, ctx_len, tq)
    smem = pl.BlockSpec(memory_space=pltpu.SMEM)
    args = (lam.reshape(1).astype(F32), qk, qk, proj, subln.reshape(1, LANES).astype(F32))
    out_shape = jax.ShapeDtypeStruct((batch * seq, heads * LANES), BF16)
    elem = (pl.Element(tq), pl.Element(LANES))
    lat = pl.pallas_call(
        functools.partial(_diff_attn_body, post=post, bounds=_key_chunks(seq)),
        grid=(batch, heads, (seq - ctx_len) // tq),
        in_specs=[
            smem,
            pl.BlockSpec(elem, lambda b, h, i: (rows(b, i), pl.multiple_of((q_blk + h) * LANES, LANES))),
            pl.BlockSpec((seq, LANES), lambda b, h, i: (b, k_blk + h)),
            pl.BlockSpec((seq, LANES), lambda b, h, i: (b, v_blk + h)),
            pl.BlockSpec((1, LANES), lambda b, h, i: (0, 0)),
        ],
        out_specs=pl.BlockSpec(elem, lambda b, h, i: (rows(b, i), pl.multiple_of(h * LANES, LANES))),
        out_shape=out_shape,
        scratch_shapes=[pltpu.VMEM((seq, 2 * LANES), BF16)],
        compiler_params=_cp("parallel", "parallel", "arbitrary"),
        name="diff_attn",
    )(*args)
    cpb = seq // ctx_len
    ctx_spec = lambda blk: pl.BlockSpec((ctx_len, LANES), lambda b, h: (b * cpb, blk + h))
    return pl.pallas_call(
        functools.partial(_diff_attn_body, post=post, bounds=_key_chunks(ctx_len)),
        grid=(batch, heads),
        in_specs=[smem, ctx_spec(q_blk), ctx_spec(k_blk), ctx_spec(v_blk),
                  pl.BlockSpec((1, LANES), lambda b, h: (0, 0)), pl.BlockSpec(memory_space=pl.ANY)],
        out_specs=ctx_spec(0),
        out_shape=out_shape,
        scratch_shapes=[pltpu.VMEM((ctx_len, 2 * LANES), BF16)],
        input_output_aliases={5: 0},
        compiler_params=_cp("parallel", "parallel"),
        name="diff_attn_ctx",
    )(*args, lat)


def _mla_attn_body(qn_ref, qr_ref, kn_ref, kr_ref, v_ref, *rest, bounds):
    o_ref, ka_ref, va_ref = rest[-3:]
    h = pl.program_id(1)

    def fill():
        ka_ref[:, :LANES] = kn_ref[...]
        ka_ref[:, LANES:] = kr_ref[...]
        va_ref[:, :LANES] = v_ref[...]
        va_ref[:, LANES:] = jnp.ones(v_ref.shape, BF16)

    if len(rest) == 3:
        pl.when(pl.program_id(2) == 0)(fill)
    else:
        fill()

    qr = qr_ref[...]
    lane = lax.broadcasted_iota(jnp.int32, (1, LANES), 1)
    mine = (lane < ROPE_DIM) == (h % 2 == 0)
    qr = jnp.where(mine, qr, jnp.zeros_like(qr))
    q = jnp.concatenate([qn_ref[...], qr], axis=1)
    o_ref[...] = _softmax_pv(q, ka_ref, va_ref, bounds).astype(o_ref.dtype)


def _mla_attn(q_up, q_rope, kv_up, k_rope, *, batch, seq, ctx_len, heads):
    tq = _lat_tile(seq, ctx_len)
    rows = _lat_rows(seq, ctx_len, tq)
    args = (q_up, q_rope, kv_up, k_rope, kv_up)
    out_shape = jax.ShapeDtypeStruct((batch * seq, heads * LANES), BF16)
    elem = (pl.Element(tq), pl.Element(LANES))
    lat = pl.pallas_call(
        functools.partial(_mla_attn_body, bounds=_key_chunks(seq)),
        grid=(batch, heads, (seq - ctx_len) // tq),
        in_specs=[
            pl.BlockSpec(elem, lambda b, h, i: (rows(b, i), pl.multiple_of(h * LANES, LANES))),
            pl.BlockSpec(elem, lambda b, h, i: (rows(b, i), pl.multiple_of((h // 2) * LANES, LANES))),
            pl.BlockSpec((seq, LANES), lambda b, h, i: (b, 2 * h)),
            pl.BlockSpec((seq, LANES), lambda b, h, i: (b, 0)),
            pl.BlockSpec((seq, LANES), lambda b, h, i: (b, 2 * h + 1)),
        ],
        out_specs=pl.BlockSpec(elem, lambda b, h, i: (rows(b, i), pl.multiple_of(h * LANES, LANES))),
        out_shape=out_shape,
        scratch_shapes=[pltpu.VMEM((seq, 2 * LANES), BF16), pltpu.VMEM((seq, 2 * LANES), BF16)],
        compiler_params=_cp("parallel", "parallel", "arbitrary"),
        name="mla_attn",
    )(*args)
    cpb = seq // ctx_len
    ctx_spec = lambda col: pl.BlockSpec((ctx_len, LANES), lambda b, h: (b * cpb, col(h)))
    return pl.pallas_call(
        functools.partial(_mla_attn_body, bounds=_key_chunks(ctx_len)),
        grid=(batch, heads),
        in_specs=[ctx_spec(lambda h: h), ctx_spec(lambda h: h // 2), ctx_spec(lambda h: 2 * h),
                  ctx_spec(lambda h: 0), ctx_spec(lambda h: 2 * h + 1), pl.BlockSpec(memory_space=pl.ANY)],
        out_specs=ctx_spec(lambda h: h),
        out_shape=out_shape,
        scratch_shapes=[pltpu.VMEM((ctx_len, 2 * LANES), BF16), pltpu.VMEM((ctx_len, 2 * LANES), BF16)],
        input_output_aliases={5: 0},
        compiler_params=_cp("parallel", "parallel"),
        name="mla_attn_ctx",
    )(*args, lat)


def _retention_body(*refs, backward, cs, heads):
    if backward:
        lg_ref, q_ref, k_ref, v_ref, yf_ref, gate_ref, ng_ref, o_ref, s_ref, d_ref, qd_ref, kd_ref = refs
    else:
        lg_ref, q_ref, k_ref, v_ref, o_ref, s_ref, d_ref, qd_ref, kd_ref = refs
    t = pl.program_id(1)

    @pl.when(t == 0)
    def _():
        s_ref[...] = jnp.zeros_like(s_ref)
        pos_r = lax.broadcasted_iota(jnp.int32, (cs, cs), 0).astype(F32)
        pos_c = lax.broadcasted_iota(jnp.int32, (cs, cs), 1).astype(F32)
        pos = lax.broadcasted_iota(jnp.int32, (cs, 1), 0).astype(F32)
        for h in range(heads):
            lg = lg_ref[h]
            if backward:
                diff = pos_c - pos_r
                keep = diff > 0
                qd_ref[h] = jnp.exp(lg * (cs - pos))
                kd_ref[h] = jnp.exp(lg * pos)
            else:
                diff = pos_r - pos_c
                keep = diff >= 0
                qd_ref[h] = jnp.exp(lg * (pos + 1.0))
                kd_ref[h] = jnp.exp(lg * (cs - 1.0 - pos))
            d_ref[h] = jnp.where(keep, jnp.exp(lg * jnp.maximum(diff, 0.0)), 0.0)

    lane = lax.broadcasted_iota(jnp.int32, (1, LANES), 1)
    nt = (((1,), (1,)), ((), ()))
    tn = (((0,), (0,)), ((), ()))
    for h in range(heads):
        blk = slice((h // 2) * LANES, (h // 2 + 1) * LANES)
        col = slice(h * RET_V, (h + 1) * RET_V)
        mine = (lane < RET_K) == (h % 2 == 0)
        q = q_ref[:, blk]
        q = jnp.where(mine, q, jnp.zeros_like(q))
        k = k_ref[:, blk]
        v = v_ref[:, col]
        scores = lax.dot_general(q, k, nt, preferred_element_type=F32) * d_ref[h]
        intra = jnp.dot(scores.astype(BF16), v, preferred_element_type=F32)
        state = s_ref[h]
        q_w = (q.astype(F32) * qd_ref[h]).astype(BF16)
        cross = jnp.dot(q_w, state.astype(BF16), preferred_element_type=F32)
        k_w = (k.astype(F32) * kd_ref[h]).astype(BF16)
        upd = lax.dot_general(k_w, v, tn, preferred_element_type=F32)
        s_ref[h] = jnp.exp(lg_ref[h] * cs) * state + upd
        y = intra + cross
        if backward:
            y = y + yf_ref[:, col]
            y = y * lax.rsqrt(jnp.mean(y * y, axis=-1, keepdims=True) + NORM_EPS) * ng_ref[:, col]
            g = gate_ref[:, col].astype(F32)
            o_ref[:, col] = (g * _sigmoid(g) * y).astype(o_ref.dtype)
        else:
            o_ref[:, col] = y


def _retention(qk, proj, log_g, norm_g, *, batch, seq, ctx_len, heads, q_off, k_off, v_off, g_off):
    cs = RET_CHUNK
    nc, nc_ctx = seq // cs, ctx_len // cs
    wqk, wv = heads * RET_K, heads * RET_V
    assert q_off % wqk == 0 and k_off % wqk == 0 and v_off % wv == 0 and g_off % wv == 0
    smem = pl.BlockSpec(memory_space=pltpu.SMEM)

    def fwd_chunk(t):
        return t

    def bwd_chunk(t):
        return jnp.where(t < nc_ctx, nc_ctx - 1 - t, nc - 1 - (t - nc_ctx))

    def specs(chunk):
        row = lambda b, t: b * nc + chunk(t)
        return (pl.BlockSpec((cs, wqk), lambda b, t: (row(b, t), q_off // wqk)),
                pl.BlockSpec((cs, wqk), lambda b, t: (row(b, t), k_off // wqk)),
                pl.BlockSpec((cs, wv), lambda b, t: (row(b, t), v_off // wv)),
                pl.BlockSpec((cs, wv), lambda b, t: (row(b, t), 0)),
                pl.BlockSpec((cs, wv), lambda b, t: (row(b, t), g_off // wv)))

    scratch = [pltpu.VMEM((heads, LANES, RET_V), F32), pltpu.VMEM((heads, cs, cs), F32),
               pltpu.VMEM((heads, cs, 1), F32), pltpu.VMEM((heads, cs, 1), F32)]
    qs, ks, vs, ys, gs = specs(fwd_chunk)
    y_f = pl.pallas_call(
        functools.partial(_retention_body, backward=False, cs=cs, heads=heads),
        grid=(batch, nc),
        in_specs=[smem, qs, ks, vs],
        out_specs=ys,
        out_shape=jax.ShapeDtypeStruct((batch * seq, wv), F32),
        scratch_shapes=scratch,
        compiler_params=_cp("parallel", "arbitrary"),
        name="retention_fwd",
    )(log_g[0].astype(F32), qk, qk, proj)
    qs, ks, vs, ys, gs = specs(bwd_chunk)
    return pl.pallas_call(
        functools.partial(_retention_body, backward=True, cs=cs, heads=heads),
        grid=(batch, nc),
        in_specs=[smem, qs, ks, vs, ys, gs, pl.BlockSpec((1, wv), lambda b, t: (0, 0))],
        out_specs=ys,
        out_shape=jax.ShapeDtypeStruct((batch * seq, wv), BF16),
        scratch_shapes=scratch,
        compiler_params=_cp("parallel", "arbitrary"),
        name="retention_bwd",
    )(log_g[1].astype(F32), qk, qk, proj, y_f, proj, norm_g.reshape(1, wv).astype(F32))


def _s5_params(a_re, a_im, log_dt, b_re, b_im, c_re, c_im):
    L, ch = S5_CHUNK, S5_CH
    a_re, a_im = a_re.astype(F32), a_im.astype(F32)
    dt = jnp.exp(log_dt.astype(F32))[..., None]
    e = jnp.arange(L + 1, dtype=F32)[:, None, None, None]
    mag = jnp.exp(a_re * dt * e)
    pw_re, pw_im = mag * jnp.cos(a_im * dt * e), mag * jnp.sin(a_im * dt * e)
    ab_re, ab_im = pw_re[1], pw_im[1]
    den = a_re * a_re + a_im * a_im
    f_re = ((ab_re - 1.0) * a_re + ab_im * a_im) / den
    f_im = (ab_im * a_re - (ab_re - 1.0) * a_im) / den
    bb_re = f_re[..., None] * b_re - f_im[..., None] * b_im
    bb_im = f_re[..., None] * b_im + f_im[..., None] * b_re
    c_re, c_im = c_re.astype(F32), c_im.astype(F32)

    cp_re = c_re[None] * pw_re[:L, :, :, None, :] - c_im[None] * pw_im[:L, :, :, None, :]
    cp_im = c_re[None] * pw_im[:L, :, :, None, :] + c_im[None] * pw_re[:L, :, :, None, :]
    hp = lax.Precision.HIGHEST
    kmat = (jnp.einsum("ldgcp,dgpk->dglck", cp_re, bb_re, precision=hp)
            - jnp.einsum("ldgcp,dgpk->dglck", cp_im, bb_im, precision=hp))
    idx = jnp.arange(L)
    lag_f = idx[None, :] - idx[:, None]
    lag_b = -lag_f

    def toeplitz(k, lag):
        t = k[:, jnp.clip(lag, 0, L - 1)]
        t = jnp.where((lag >= 0)[None, :, :, None, None], t, 0.0)
        return t.transpose(0, 1, 4, 2, 3).reshape(k.shape[0], L * ch, L * ch)

    m_both = jnp.concatenate([toeplitz(kmat[0], lag_f), toeplitz(kmat[1], lag_b)], axis=-1)

    def state_in(d, exps):
        p_re, p_im = pw_re[exps, d], pw_im[exps, d]
        w_re = p_re[..., None] * bb_re[d][None] - p_im[..., None] * bb_im[d][None]
        w_im = p_re[..., None] * bb_im[d][None] + p_im[..., None] * bb_re[d][None]
        w = jnp.concatenate([w_re, w_im], axis=2)
        return w.transpose(1, 0, 3, 2).reshape(w.shape[1], L * ch, 2 * S5_STATE)

    w_both = jnp.concatenate([state_in(0, L - 1 - idx), state_in(1, idx)], axis=-1)

    def state_out(d, exps):
        p_re, p_im = pw_re[exps, d], pw_im[exps, d]
        v_re = c_re[d][None] * p_re[:, :, None, :] - c_im[d][None] * p_im[:, :, None, :]
        v_im = c_re[d][None] * p_im[:, :, None, :] + c_im[d][None] * p_re[:, :, None, :]
        v = jnp.concatenate([v_re, -v_im], axis=-1)
        return v.transpose(1, 3, 0, 2).reshape(v.shape[1], 2 * S5_STATE, L * ch)

    v_mat = jnp.stack([state_out(0, idx + 1), state_out(1, L - idx)])
    a_pow = jnp.concatenate([pw_re[L], pw_im[L]], axis=-1)
    return m_both.astype(BF16), w_both.astype(BF16), v_mat.astype(BF16), a_pow


def _s5_state_in_body(u_ref, w_ref, o_ref):
    s = jnp.dot(u_ref[0], w_ref[0], preferred_element_type=F32)
    o_ref[0] = s[:, :LANES]
    o_ref[1] = s[:, LANES:]


def _s5_scan_body(s_ref, a_ref, o_ref, *, nk, nk_ctx):
    d = pl.program_id(0)
    a = a_ref[0]
    lane = lax.broadcasted_iota(jnp.int32, a.shape, 1)
    first = lane < S5_STATE
    a_sw = pltpu.roll(a, S5_STATE, 1)
    a_same = jnp.where(first, a, a_sw)
    a_cross = jnp.where(first, -a_sw, a)
    a_same, a_cross = a_same[None], a_cross[None]

    def step(k, carry):
        hstate, hswap = carry
        o_ref[0, k] = hstate
        s = s_ref[0, k]
        return (a_same * hstate + a_cross * hswap + s,
                a_same * hswap - a_cross * hstate + pltpu.roll(s, S5_STATE, 2))

    zero = jnp.zeros(o_ref.shape[2:], F32)
    unroll = 8 if (nk % 8 == 0 and nk_ctx % 8 == 0) else 1

    @pl.when(d == 0)
    def _():
        lax.fori_loop(0, nk, step, (zero, zero), unroll=unroll)

    @pl.when(d == 1)
    def _():
        hc = lax.fori_loop(0, nk_ctx, lambda t, c: step(nk_ctx - 1 - t, c), (zero, zero), unroll=unroll)
        lax.fori_loop(0, nk - nk_ctx, lambda t, c: step(nk - 1 - t, c), hc, unroll=unroll)


def _s5_out_body(u_ref, m_ref, h_ref, v_ref, d_ref, o_ref):
    u = u_ref[0]
    y = jnp.dot(u, m_ref[0], preferred_element_type=F32)
    y = y[:, :2 * LANES] + y[:, 2 * LANES:]
    y = y + jnp.dot(h_ref[0].astype(BF16), v_ref[0, 0], preferred_element_type=F32)
    y = y + jnp.dot(h_ref[1].astype(BF16), v_ref[1, 0], preferred_element_type=F32)
    y = y + d_ref[0] * u.astype(F32)
    o_ref[0] = jax.nn.gelu(y).astype(o_ref.dtype)


def _s5_mix(proj, su_off, params, d_skip, *, batch, seq, ctx_len, gw):
    m_both, w_both, v_mat, a_pow = params
    L, ch, st2 = S5_CHUNK, S5_CH, 2 * S5_STATE
    groups = gw // ch
    nk, nk_ctx = seq // L, ctx_len // L
    rows = nk * batch
    cols = L * ch
    u = proj[:, su_off:su_off + gw].reshape(batch, nk, L, groups, ch)
    u = u.transpose(3, 1, 0, 2, 4).reshape(groups, rows, cols)

    s_in = pl.pallas_call(
        _s5_state_in_body,
        grid=(groups,),
        in_specs=[pl.BlockSpec((1, rows, cols), lambda g: (g, 0, 0)),
                  pl.BlockSpec((1, cols, 2 * st2), lambda g: (g, 0, 0))],
        out_specs=pl.BlockSpec((2, rows, st2), lambda g: (0, 0, g)),
        out_shape=jax.ShapeDtypeStruct((2, rows, groups * st2), F32),
        compiler_params=_cp("parallel"),
        name="s5_state_in",
    )(u, w_both)

    gb = 8
    s4 = s_in.reshape(2, nk, batch, groups, st2)
    h_prev = pl.pallas_call(
        functools.partial(_s5_scan_body, nk=nk, nk_ctx=nk_ctx),
        grid=(2, groups // gb),
        in_specs=[pl.BlockSpec((1, nk, batch, gb, st2), lambda d, g: (d, 0, 0, g, 0)),
                  pl.BlockSpec((1, gb, st2), lambda d, g: (d, g, 0))],
        out_specs=pl.BlockSpec((1, nk, batch, gb, st2), lambda d, g: (d, 0, 0, g, 0)),
        out_shape=jax.ShapeDtypeStruct((2, nk, batch, groups, st2), F32),
        compiler_params=_cp("parallel", "parallel"),
        name="s5_scan",
    )(s4, a_pow)
    h2 = h_prev.reshape(2, rows, groups * st2)

    d_rep = jnp.tile(d_skip.astype(F32), (1, L)).reshape(groups, 1, cols)
    y = pl.pallas_call(
        _s5_out_body,
        grid=(groups,),
        in_specs=[pl.BlockSpec((1, rows, cols), lambda g: (g, 0, 0)),
                  pl.BlockSpec((1, cols, 2 * cols), lambda g: (g, 0, 0)),
                  pl.BlockSpec((2, rows, st2), lambda g: (0, 0, g)),
                  pl.BlockSpec((2, 1, st2, cols), lambda g: (0, g, 0, 0)),
                  pl.BlockSpec((1, 1, cols), lambda g: (g, 0, 0))],
        out_specs=pl.BlockSpec((1, rows, cols), lambda g: (g, 0, 0)),
        out_shape=jax.ShapeDtypeStruct((groups, rows, cols), BF16),
        compiler_params=_cp("parallel"),
        name="s5_out",
    )(u, m_both, h2, v_mat, d_rep)
    y = y.reshape(groups, nk, batch, L, ch).transpose(2, 1, 3, 0, 4)
    return y.reshape(batch * seq, gw)


S5N_CHUNK = 8
S5N_GROUPS = LANES // S5_CH
S5N_HALF = S5N_GROUPS * S5_STATE


def _s5n_params(a_re, a_im, log_dt, b_re, b_im, c_re, c_im, d_skip):
    L, ch, gl = S5N_CHUNK, S5_CH, S5N_GROUPS
    a_re, a_im = a_re.astype(F32), a_im.astype(F32)
    groups = a_re.shape[1]
    nb = groups // gl
    dt = jnp.exp(log_dt.astype(F32))[..., None]
    e = jnp.arange(L + 1, dtype=F32)[:, None, None, None]
    mag = jnp.exp(a_re * dt * e)
    pw_re, pw_im = mag * jnp.cos(a_im * dt * e), mag * jnp.sin(a_im * dt * e)
    ab_re, ab_im = pw_re[1], pw_im[1]
    den = a_re * a_re + a_im * a_im
    f_re = ((ab_re - 1.0) * a_re + ab_im * a_im) / den
    f_im = (ab_im * a_re - (ab_re - 1.0) * a_im) / den
    bb_re = f_re[..., None] * b_re - f_im[..., None] * b_im
    bb_im = f_re[..., None] * b_im + f_im[..., None] * b_re
    c_re, c_im = c_re.astype(F32), c_im.astype(F32)
    hp = lax.Precision.HIGHEST
    idx = jnp.arange(L)
    eye = jnp.eye(gl, dtype=F32)

    cp_re = c_re[None] * pw_re[:L, :, :, None, :] - c_im[None] * pw_im[:L, :, :, None, :]
    cp_im = c_re[None] * pw_im[:L, :, :, None, :] + c_im[None] * pw_re[:L, :, :, None, :]
    kmat = (jnp.einsum("ldgcp,dgpk->dglck", cp_re, bb_re, precision=hp)
            - jnp.einsum("ldgcp,dgpk->dglck", cp_im, bb_im, precision=hp))
    lag = idx[None, :] - idx[:, None]

    def toeplitz(k, lg):
        return jnp.where((lg >= 0)[None, :, :, None, None], k[:, jnp.clip(lg, 0, L - 1)], 0.0)

    t = toeplitz(kmat[0], lag) + toeplitz(kmat[1], -lag)
    skip = (idx[:, None] == idx[None, :])[None, :, :, None, None] * (
        jnp.eye(ch, dtype=F32)[None, None, None] * d_skip.astype(F32)[:, None, None, :, None])
    t = (t + skip).reshape(nb, gl, L, L, ch, ch)
    m_mat = t.transpose(0, 2, 1, 5, 3, 4).reshape(nb, L, LANES, L * ch)

    def state_in(d, exps):
        p_re, p_im = pw_re[exps, d], pw_im[exps, d]
        w_re = p_re[..., None] * bb_re[d][None] - p_im[..., None] * bb_im[d][None]
        w_im = p_re[..., None] * bb_im[d][None] + p_im[..., None] * bb_re[d][None]
        return jnp.stack([w_re, w_im], axis=0).transpose(2, 1, 4, 0, 3)

    w_full = jnp.stack([state_in(0, L - 1 - idx), state_in(1, idx)], axis=3)
    w_full = w_full.reshape(nb, gl, L, ch, 2, 2, S5_STATE)
    w_mat = w_full.transpose(0, 2, 1, 3, 4, 5, 6).reshape(nb, L, LANES, 4 * S5_STATE)

    def state_out(d, exps):
        p_re, p_im = pw_re[exps, d], pw_im[exps, d]
        v_re = c_re[d][None] * p_re[:, :, None, :] - c_im[d][None] * p_im[:, :, None, :]
        v_im = c_re[d][None] * p_im[:, :, None, :] + c_im[d][None] * p_re[:, :, None, :]
        return jnp.stack([v_re, -v_im], axis=0).transpose(2, 0, 4, 1, 3)

    v_full = jnp.stack([state_out(0, idx + 1), state_out(1, L - idx)], axis=1)
    v_full = v_full.reshape(nb, gl, 2, 2, S5_STATE, L, ch)
    v_mat = v_full.transpose(0, 2, 3, 1, 4, 5, 6).reshape(nb, 4 * S5N_HALF, L * ch)

    a_pow = jnp.stack([pw_re[L], pw_im[L]], axis=1).reshape(2, 2, nb * S5N_HALF)
    return m_mat.astype(BF16), w_mat.astype(BF16), v_mat.astype(BF16), a_pow


def _s5n_spreaders():
    L, ch, gl, st = S5N_CHUNK, S5_CH, S5N_GROUPS, S5_STATE
    e_out = jnp.einsum("ij,cd->icjd", jnp.eye(L), jnp.eye(ch))
    e_out = jnp.broadcast_to(e_out[:, :, :, None, :], (L, ch, L, gl, ch)).reshape(L * ch, L * LANES)
    e_st = jnp.broadcast_to(jnp.eye(4 * st).reshape(4 * st, 4, 1, st), (4 * st, 4, gl, st))
    return e_out.astype(BF16), e_st.reshape(4 * st, 4 * S5N_HALF).astype(BF16)


def _s5n_expand(compact, spread, row_group, col_group):
    full = jnp.dot(compact, spread, preferred_element_type=F32)
    rows = lax.broadcasted_iota(jnp.int32, full.shape, 0)
    cols = lax.broadcasted_iota(jnp.int32, full.shape, 1)
    return jnp.where(row_group(rows) == col_group(cols), full, 0.0).astype(BF16)


_S5N_IN_ROW_GROUP = lambda r: r // S5_CH
_S5N_STATE_GROUP = lambda c: (c % S5N_HALF) // S5_STATE
_S5N_OUT_COL_GROUP = lambda c: (c % LANES) // S5_CH


def _s5n_state_in_body(u_ref, w_ref, e_ref, o_ref, w_exp):
    @pl.when(pl.program_id(1) == 0)
    def _():
        for j in range(S5N_CHUNK):
            w_exp[j] = _s5n_expand(w_ref[0, j], e_ref[...], _S5N_IN_ROW_GROUP, _S5N_STATE_GROUP)

    acc = None
    for j in range(S5N_CHUNK):
        part = jnp.dot(u_ref[:, j, :], w_exp[j], preferred_element_type=F32)
        acc = part if acc is None else acc + part
    for d in range(2):
        for r in range(2):
            lo = (2 * d + r) * S5N_HALF
            o_ref[d, r] = acc[:, lo:lo + S5N_HALF]


def _s5n_scan_body(sr_ref, si_ref, a_ref, o_ref, *, nk, nk_ctx):
    d = pl.program_id(0)
    ar, ai = a_ref[0, 0], a_ref[0, 1]

    def step(k, carry):
        hr, hi = carry
        o_ref[0, 0, pl.ds(k, 1), :] = hr
        o_ref[0, 1, pl.ds(k, 1), :] = hi
        sr = sr_ref[0, 0, pl.ds(k, 1), :]
        si = si_ref[0, 0, pl.ds(k, 1), :]
        return ar * hr - ai * hi + sr, ar * hi + ai * hr + si

    zero = jnp.zeros_like(ar)
    unroll = 8 if (nk % 8 == 0 and nk_ctx % 8 == 0) else 1

    @pl.when(d == 0)
    def _():
        lax.fori_loop(0, nk, step, (zero, zero), unroll=unroll)

    @pl.when(d == 1)
    def _():
        hc = lax.fori_loop(0, nk_ctx, lambda t, c: step(nk_ctx - 1 - t, c), (zero, zero), unroll=unroll)
        lax.fori_loop(0, nk - nk_ctx, lambda t, c: step(nk - 1 - t, c), hc, unroll=unroll)


def _s5n_out_body(u_ref, m_ref, h_ref, v_ref, e_ref, o_ref, m_exp, v_exp):
    @pl.when(pl.program_id(1) == 0)
    def _():
        for j in range(S5N_CHUNK):
            m_exp[j] = _s5n_expand(m_ref[0, j], e_ref[...], _S5N_IN_ROW_GROUP, _S5N_OUT_COL_GROUP)
        v_exp[...] = _s5n_expand(v_ref[0], e_ref[...], _S5N_STATE_GROUP, _S5N_OUT_COL_GROUP)

    acc = None
    for j in range(S5N_CHUNK):
        part = jnp.dot(u_ref[:, j, :], m_exp[j], preferred_element_type=F32)
        acc = part if acc is None else acc + part
    for d in range(2):
        for r in range(2):
            lo = (2 * d + r) * S5N_HALF
            acc = acc + jnp.dot(h_ref[d, r].astype(BF16), v_exp[lo:lo + S5N_HALF, :],
                                preferred_element_type=F32)
    y = jax.nn.gelu(acc).astype(o_ref.dtype)
    for i in range(S5N_CHUNK):
        o_ref[:, i, :] = y[:, i * LANES:(i + 1) * LANES]


def _s5n_mix(proj, su_off, params, *, batch, seq, ctx_len, gw):
    m_mat, w_mat, v_mat, a_pow = params
    L, half = S5N_CHUNK, S5N_HALF
    nb = gw // LANES
    nk, nk_ctx = seq // L, ctx_len // L
    assert su_off % LANES == 0 and seq % L == 0 and ctx_len % L == 0
    ub = su_off // LANES
    u3 = proj.reshape(batch * nk, L, proj.shape[1])
    lanes = batch * nb * half
    u_spec = pl.BlockSpec((nk, L, LANES), lambda n, b: (b, 0, ub + n))
    h_spec = pl.BlockSpec((2, 2, nk, half), lambda n, b: (0, 0, 0, b * nb + n))

    e_out, e_st = _s5n_spreaders()
    whole = lambda a: pl.BlockSpec(a.shape, lambda n, b: (0,) * a.ndim)
    s_in = pl.pallas_call(
        _s5n_state_in_body,
        grid=(nb, batch),
        in_specs=[u_spec, pl.BlockSpec((1,) + w_mat.shape[1:], lambda n, b: (n, 0, 0, 0)), whole(e_st)],
        out_specs=h_spec,
        out_shape=jax.ShapeDtypeStruct((2, 2, nk, lanes), F32),
        scratch_shapes=[pltpu.VMEM((L, LANES, 4 * half), BF16)],
        compiler_params=_cp("parallel", "arbitrary"),
        name="s5_state_in",
    )(u3, w_mat, e_st)

    wl = _pick_tile(lanes, 2048, LANES)
    part_spec = lambda r: pl.BlockSpec((1, 1, nk, wl), lambda d, w: (d, r, 0, w))
    a_lanes = jnp.tile(a_pow.reshape(2, 2, 1, nb * half), (1, 1, 1, batch))
    h_prev = pl.pallas_call(
        functools.partial(_s5n_scan_body, nk=nk, nk_ctx=nk_ctx),
        grid=(2, lanes // wl),
        in_specs=[part_spec(0), part_spec(1), pl.BlockSpec((1, 2, 1, wl), lambda d, w: (d, 0, 0, w))],
        out_specs=pl.BlockSpec((1, 2, nk, wl), lambda d, w: (d, 0, 0, w)),
        out_shape=jax.ShapeDtypeStruct((2, 2, nk, lanes), F32),
        compiler_params=_cp("parallel", "parallel"),
        name="s5_scan",
    )(s_in, s_in, a_lanes)

    y3 = pl.pallas_call(
        _s5n_out_body,
        grid=(nb, batch),
        in_specs=[u_spec,
                  pl.BlockSpec((1,) + m_mat.shape[1:], lambda n, b: (n, 0, 0, 0)),
                  h_spec,
                  pl.BlockSpec((1,) + v_mat.shape[1:], lambda n, b: (n, 0, 0)),
                  whole(e_out)],
        out_specs=pl.BlockSpec((nk, L, LANES), lambda n, b: (b, 0, n)),
        out_shape=jax.ShapeDtypeStruct((batch * nk, L, gw), BF16),
        scratch_shapes=[pltpu.VMEM((L, LANES, L * LANES), BF16), pltpu.VMEM((4 * half, L * LANES), BF16)],
        compiler_params=_cp("parallel", "arbitrary"),
        name="s5_out",
    )(u3, m_mat, h_prev, v_mat, e_out)
    return y3.reshape(batch * seq, gw)


def _expert_changed(te_ref):
    i = pl.program_id(1)
    return (i == 0) | (te_ref[i] != te_ref[jnp.maximum(i - 1, 0)])


def _moe_up_body(te_ref, x_ref, wg_ref, wu_ref, o_ref, wg_bf, wu_bf):
    @pl.when(_expert_changed(te_ref))
    def _():
        wg_bf[...] = wg_ref[0, 0].astype(BF16)
        wu_bf[...] = wu_ref[0, 0].astype(BF16)

    x = x_ref[...]
    a = jnp.dot(x, wg_bf[...], preferred_element_type=F32)
    b = jnp.dot(x, wu_bf[...], preferred_element_type=F32)
    o_ref[...] = (a * _sigmoid(a) * b).astype(o_ref.dtype)


def _moe_down_body(te_ref, h_ref, w_ref, rw_ref, o_ref, w_bf):
    @pl.when(_expert_changed(te_ref))
    def _():
        w_bf[...] = w_ref[0, 0].astype(BF16)

    y = jnp.dot(h_ref[...], w_bf[...], preferred_element_type=F32)
    o_ref[...] = (rw_ref[...] * y).astype(o_ref.dtype)


def _moe_combine_body(x_ref, y0_ref, y1_ref, gate_ref, o_ref, *, tm, tiles_per_batch, ctx_len):
    i = pl.program_id(0)
    is_ctx = (i % tiles_per_batch) * tm < ctx_len
    gate = jnp.where(is_ctx, gate_ref[0, 0:1, :], gate_ref[0, 1:2, :])
    o_ref[...] = x_ref[...] + gate * (y0_ref[...].astype(F32) + y1_ref[...].astype(F32))


def _moe_final_body(x_ref, y0_ref, y1_ref, gate_ref, g_ref, o_ref):
    x = x_ref[...] + gate_ref[0, 1:2, :] * (y0_ref[...].astype(F32) + y1_ref[...].astype(F32))
    o_ref[...] = x * lax.rsqrt(jnp.mean(x * x, axis=-1, keepdims=True) + NORM_EPS) * g_ref[...]


def _route(logits):
    assert MOE_TOPK == 2
    g_logit = logits[:, :MOE_GROUPS]
    g_prob = jax.nn.softmax(g_logit, axis=-1)
    g_idx = jnp.argmax(g_prob, axis=-1)
    g_p = jnp.max(g_prob, axis=-1)
    e_logit = logits[:, MOE_GROUPS:MOE_GROUPS + MOE_GROUPS * MOE_PER_GROUP]
    e_logit = e_logit.reshape(-1, MOE_GROUPS, MOE_PER_GROUP)
    sel = (jnp.arange(MOE_GROUPS)[None, :] == g_idx[:, None])[:, :, None]
    e_logit = jnp.sum(jnp.where(sel, e_logit, 0.0), axis=1)
    e_prob = jax.nn.softmax(e_logit, axis=-1)
    i0 = jnp.argmax(e_prob, axis=-1)
    p0 = jnp.max(e_prob, axis=-1)
    rest = jnp.where(jnp.arange(MOE_PER_GROUP)[None, :] == i0[:, None], -1.0, e_prob)
    i1 = jnp.argmax(rest, axis=-1)
    p1 = jnp.max(rest, axis=-1)
    e_p = jnp.stack([p0, p1], axis=-1)
    w = g_p[:, None] * e_p / jnp.sum(e_p, axis=-1, keepdims=True)
    ids = g_idx[:, None] * MOE_PER_GROUP + jnp.stack([i0, i1], axis=-1)
    return ids.astype(jnp.int32), w


def _moe(h, logits, x, gate, w_gate, w_up, w_down, layer, *, rows_per_batch, ctx_len, final_g=None):
    t, d = h.shape
    _, n_exp, _, dff = w_gate.shape
    tile = MOE_TILE
    ids, wts = _route(logits)
    flat_e = ids.reshape(-1)
    onehot = (flat_e[:, None] == jnp.arange(n_exp)[None, :]).astype(jnp.int32)
    counts = onehot.sum(0)
    rank = jnp.take_along_axis(jnp.cumsum(onehot, axis=0) - onehot, flat_e[:, None], axis=1)[:, 0]
    padded = (counts + tile - 1) // tile * tile
    starts = jnp.cumsum(padded) - padded
    pos = starts[flat_e] + rank
    n_rows = (t * MOE_TOPK // tile + n_exp) * tile
    n_tiles = n_rows // tile
    token = (jnp.arange(t * MOE_TOPK, dtype=jnp.int32) // MOE_TOPK).astype(F32)
    table = jnp.zeros((n_rows, 2), F32).at[pos].set(jnp.stack([token, wts.reshape(-1)], axis=1),
                                                    unique_indices=True, mode="promise_in_bounds")
    src = table[:, 0].astype(jnp.int32)
    row_w = table[:, 1]
    tile_start = jnp.arange(n_tiles, dtype=jnp.int32) * tile
    ends = starts + padded
    tile_e = jnp.minimum(jnp.sum(tile_start[:, None] >= ends[None, :], axis=1), n_exp - 1).astype(jnp.int32)

    xs = h.at[src].get(mode="promise_in_bounds")
    tn_up = _pick_tile(dff, 512, LANES)
    w_up_spec = pl.BlockSpec((1, 1, d, tn_up), lambda j, i, te: (layer, te[i], 0, j))
    hid = pl.pallas_call(
        _moe_up_body,
        grid_spec=pltpu.PrefetchScalarGridSpec(
            num_scalar_prefetch=1,
            grid=(dff // tn_up, n_tiles),
            in_specs=[pl.BlockSpec((tile, d), lambda j, i, te: (i, 0)), w_up_spec, w_up_spec],
            out_specs=pl.BlockSpec((tile, tn_up), lambda j, i, te: (i, j)),
            scratch_shapes=[pltpu.VMEM((d, tn_up), BF16), pltpu.VMEM((d, tn_up), BF16)]),
        out_shape=jax.ShapeDtypeStruct((n_rows, dff), BF16),
        compiler_params=_cp("arbitrary", "arbitrary"),
        name="moe_up",
    )(tile_e, xs, w_gate, w_up)
    tn_dn = _pick_tile(d, 4096, LANES)
    ys = pl.pallas_call(
        _moe_down_body,
        grid_spec=pltpu.PrefetchScalarGridSpec(
            num_scalar_prefetch=1,
            grid=(d // tn_dn, n_tiles),
            in_specs=[pl.BlockSpec((tile, dff), lambda j, i, te: (i, 0)),
                      pl.BlockSpec((1, 1, dff, tn_dn), lambda j, i, te: (layer, te[i], 0, j)),
                      pl.BlockSpec((tile, 1), lambda j, i, te: (i, 0))],
            out_specs=pl.BlockSpec((tile, tn_dn), lambda j, i, te: (i, j)),
            scratch_shapes=[pltpu.VMEM((dff, tn_dn), BF16)]),
        out_shape=jax.ShapeDtypeStruct((n_rows, d), BF16),
        compiler_params=_cp("arbitrary", "arbitrary"),
        name="moe_down",
    )(tile_e, hid, w_down, row_w.reshape(n_rows, 1))

    pos2 = pos.reshape(t, MOE_TOPK)
    y0 = ys.at[pos2[:, 0]].get(mode="promise_in_bounds")
    y1 = ys.at[pos2[:, 1]].get(mode="promise_in_bounds")
    tm = _pick_tile(math.gcd(rows_per_batch, ctx_len), 256, 8)
    tpb = rows_per_batch // tm
    if final_g is not None:
        batch = t // rows_per_batch
        ctx_tiles, lat_tiles = ctx_len // tm, (rows_per_batch - ctx_len) // tm
        lat_spec = pl.BlockSpec((tm, d), lambda b, i: (b * tpb + ctx_tiles + i, 0))
        out = pl.pallas_call(
            _moe_final_body,
            grid=(batch, lat_tiles),
            in_specs=[lat_spec, lat_spec, lat_spec, pl.BlockSpec((1, 2, d), lambda b, i: (b, 0, 0)),
                      pl.BlockSpec((1, d), lambda b, i: (0, 0))],
            out_specs=pl.BlockSpec((tm, d), lambda b, i: (b * lat_tiles + i, 0)),
            out_shape=jax.ShapeDtypeStruct((batch * lat_tiles * tm, d), F32),
            compiler_params=_cp("parallel", "parallel"),
            name="moe_combine_final",
        )(x, y0, y1, gate, final_g.reshape(1, d).astype(F32))
        return out.reshape(batch, lat_tiles * tm, d)
    row_spec = pl.BlockSpec((tm, d), lambda i: (i, 0))
    return pl.pallas_call(
        functools.partial(_moe_combine_body, tm=tm, tiles_per_batch=tpb, ctx_len=ctx_len),
        grid=(t // tm,),
        in_specs=[row_spec, row_spec, row_spec, pl.BlockSpec((1, 2, d), lambda i: (i // tpb, 0, 0))],
        out_specs=row_spec,
        out_shape=jax.ShapeDtypeStruct((t, d), F32),
        compiler_params=_cp("parallel"),
        name="moe_combine",
    )(x, y0, y1, gate)


def kernel(x, c, ctx, c_ctx, ada_w, ada_b, norm_mix, norm_ffn, w_in, w_out, diff_lambda, diff_subln,
           s5_a_re, s5_a_im, s5_log_dt, s5_b_re, s5_b_im, s5_c_re, s5_c_im, s5_d, s5_glu_w, s5_glu_b,
           mla_q_norm, mla_kv_norm, mla_w_uq, mla_w_ukv, ret_decay, ret_norm,
           moe_wg, moe_bg, moe_we, moe_be, moe_w_gate, moe_w_up, moe_w_down, final_norm):
    batch, n_lat, d = x.shape
    ctx_len = ctx.shape[1]
    depth = ada_w.shape[0]
    seq = ctx_len + n_lat
    rows = batch * seq
    gw = d // 4
    heads = gw // LANES
    q_rank, kv_rank = 3 * d // 16, d // 16
    ret_qk = heads * RET_K
    n_route = MOE_GROUPS + MOE_GROUPS * MOE_PER_GROUP
    assert heads % 2 == 0 and ctx_len % RET_CHUNK == 0 and n_lat % RET_CHUNK == 0

    splits = (gw, gw, gw, gw, q_rank, kv_rank, ROPE_DIM, ret_qk, ret_qk, gw, gw)
    offs = [0]
    for s_ in splits:
        offs.append(offs[-1] + s_)
    names = ("dq", "dk", "dv", "su", "mcq", "mckv", "mkr", "rq", "rk", "rv", "rg")
    src_col = {n_: (offs[i], offs[i + 1]) for i, n_ in enumerate(names)}
    order = ("dq", "dk", "rq", "rk", "dv", "su", "rv", "rg", "mcq", "mckv")
    col = {}
    pos = 0
    for n_ in order:
        col[n_] = pos
        pos += src_col[n_][1] - src_col[n_][0]
    n_main = pos
    n_rope = col["dv"]
    uq_cols = jnp.arange(heads * (MLA_NOPE + ROPE_DIM)).reshape(heads, MLA_NOPE + ROPE_DIM)
    uq_perm = jnp.concatenate([uq_cols[:, :MLA_NOPE].reshape(-1), uq_cols[:, MLA_NOPE:].reshape(-1)])

    tables = _rope_tables(n_lat, ctx_len)
    log2e = math.log2(math.e)
    rope_scale = jnp.ones((n_rope,), F32).at[col["rk"]:col["rk"] + ret_qk].set(RET_K ** -0.5)
    rope_scale = rope_scale.at[col["dq"]:col["dq"] + gw].set(DIFF_HEAD_DIM ** -0.5 * log2e)
    mla_q_scale = (MLA_NOPE + ROPE_DIM) ** -0.5 * log2e

    cond = jnp.concatenate([c_ctx[None, :], c], axis=0)
    cond = jnp.pad(cond * _sigmoid(cond), ((0, 8 - (batch + 1) % 8 if (batch + 1) % 8 else 0), (0, 0)))

    mod_all = _ada_mod(cond, ada_w, ada_b)

    xa = jnp.concatenate([ctx, x], axis=1).reshape(rows, d)
    tm_big = _pick_tile(seq, 1088)
    tn = lambda n_: _pick_tile(n_, 512, LANES)

    for l in range(depth):
        lam_init = 0.8 - 0.6 * math.exp(-0.3 * l)
        mod = mod_all[l].reshape(cond.shape[0], 6, d)
        mods = [jnp.stack([jnp.broadcast_to(mod[0, i], (batch, d)), mod[1:batch + 1, i]], axis=1)
                for i in range(6)]

        w_main = jnp.concatenate([w_in[l, :, src_col[n_][0]:src_col[n_][1]] for n_ in order],
                                 axis=1).astype(BF16)
        w_kr = jnp.concatenate([w_in[l, :, src_col["mkr"][0]:src_col["mkr"][1]]] * (LANES // ROPE_DIM),
                               axis=1).astype(BF16)

        h = _norm_mod(xa, norm_mix[l], mods[0], mods[1], rows_per_batch=seq, ctx_len=ctx_len)
        proj = _mm([h], w_main, name="in_proj", out_dtype=BF16, tm=tm_big, tn=tn(n_main))
        kr = _mm([h], w_kr, name="in_proj_kr", out_dtype=BF16, tm=tm_big, tn=LANES)
        qk = _rope(proj, n_rope, tables, rope_scale, rows_per_batch=seq)
        krr = _rope(kr, LANES, tables, jnp.ones((LANES,), F32), rows_per_batch=seq)

        lv = diff_lambda[l].astype(F32)
        lam = jnp.exp(jnp.sum(lv[0] * lv[1])) - jnp.exp(jnp.sum(lv[2] * lv[3])) + lam_init
        a_out = _diff_attn(qk, proj, lam, diff_subln[l], batch=batch, seq=seq, ctx_len=ctx_len,
                           heads=heads, q_blk=col["dq"] // LANES, k_blk=col["dk"] // LANES,
                           v_blk=col["dv"] // LANES, post=1.0 - lam_init)

        s5p = _s5n_params(s5_a_re[l], s5_a_im[l], s5_log_dt[l], s5_b_re[l], s5_b_im[l],
                          s5_c_re[l], s5_c_im[l], s5_d[l])
        s_act = _s5n_mix(proj, col["su"], s5p, batch=batch, seq=seq, ctx_len=ctx_len, gw=gw)
        s_out = _mm([s_act], s5_glu_w[l].astype(BF16), name="s5_glu", out_dtype=BF16, tm=tm_big, tn=tn(gw),
                    bias=s5_glu_b[l], glu_in=s_act)

        cq = proj[:, col["mcq"]:col["mcq"] + q_rank]
        ckv = proj[:, col["mckv"]:col["mckv"] + kv_rank]
        w_uq = (mla_w_uq[l][:, uq_perm] * mla_q_scale).astype(BF16)
        q_up = _mm([cq], w_uq, name="mla_q_up", out_dtype=BF16, tm=tm_big, tn=tn(w_uq.shape[1]), norm_g=mla_q_norm[l])
        kv_up = _mm([ckv], mla_w_ukv[l].astype(BF16), name="mla_kv_up", out_dtype=BF16, tm=tm_big,
                    tn=tn(mla_w_ukv.shape[2]), norm_g=mla_kv_norm[l])
        q_rope = _rope(q_up, heads * ROPE_DIM, tables, jnp.ones((heads * ROPE_DIM,), F32),
                       rows_per_batch=seq, col_off=heads * MLA_NOPE)
        m_out = _mla_attn(q_up, q_rope, kv_up, krr, batch=batch, seq=seq, ctx_len=ctx_len, heads=heads)

        log_g = jax.nn.log_sigmoid(ret_decay[l].astype(F32))
        r_out = _retention(qk, proj, log_g, ret_norm[l], batch=batch, seq=seq, ctx_len=ctx_len,
                           heads=heads, q_off=col["rq"], k_off=col["rk"], v_off=col["rv"], g_off=col["rg"])

        xa = _mm([a_out, s_out, m_out, r_out], w_out[l].astype(BF16), name="out_proj", out_dtype=F32, tm=tm_big, tn=tn(d),
                 res=xa, gate=mods[2], rows_per_batch=seq, ctx_len=ctx_len)

        w_r = jnp.concatenate([moe_wg[l], moe_we[l]], axis=1).astype(F32)
        w_r = jnp.pad(w_r, ((0, 0), (0, LANES - n_route)))
        w_r_hi = w_r.astype(BF16)
        w_r_lo = (w_r - w_r_hi.astype(F32)).astype(BF16)
        b_r = jnp.pad(jnp.concatenate([moe_bg[l], moe_be[l]]).astype(F32), (0, LANES - n_route))
        h, logits = _norm_mod(xa, norm_ffn[l], mods[3], mods[4], rows_per_batch=seq, ctx_len=ctx_len,
                              router=(w_r_hi, w_r_lo, b_r.reshape(1, LANES)))
        xa = _moe(h, logits, xa, mods[5], moe_w_gate, moe_w_up, moe_w_down, l,
                  rows_per_batch=seq, ctx_len=ctx_len, final_g=final_norm if l == depth - 1 else None)
    return xa
```

```python
import functools
import math

import jax
import jax.numpy as jnp
from jax import lax
from jax.experimental import pallas as pl
from jax.experimental.pallas import tpu as pltpu

BF16 = jnp.bfloat16
F32 = jnp.float32

V7X_VMEM_BYTES = 64 * 2**20
VMEM_LIMIT = V7X_VMEM_BYTES - 12 * 2**20
LANES = 128

GRID_W = 64
ROPE_DIM = 64
ROPE_BASE = 10000.0
NORM_EPS = 1e-6
DIFF_HEAD_DIM = 64
S5_CH = 16
S5_STATE = 64
S5_CHUNK = 16
MLA_NOPE = 128
MLA_V = 128
RET_K = 64
RET_V = 128
RET_CHUNK = 128
MOE_GROUPS = 4
MOE_PER_GROUP = 4
MOE_TOPK = 2
MOE_TILE = 256


def _cp(*sem):
    return pltpu.CompilerParams(dimension_semantics=sem, vmem_limit_bytes=VMEM_LIMIT)


def _pick_tile(n, target, mult=16):
    best = None
    for t in range(mult, min(n, target) + 1, mult):
        if n % t == 0:
            best = t
    assert best is not None, (n, target)
    return best


def _sigmoid(x):
    return 1.0 / (1.0 + jnp.exp(-x))


def _mm_body(*refs, nx, ksizes, has_norm, has_bias, epilogue, tm, tiles_per_batch, ctx_len):
    x_refs = refs[:nx]
    w_ref = refs[nx]
    idx = nx + 1
    g_ref = b_ref = e_ref = res_ref = gate_ref = None
    if has_norm:
        g_ref = refs[idx]; idx += 1
    if has_bias:
        b_ref = refs[idx]; idx += 1
    if epilogue == "glu":
        e_ref = refs[idx]; idx += 1
    if epilogue == "resgate":
        res_ref, gate_ref = refs[idx], refs[idx + 1]; idx += 2
    o_ref = refs[idx]

    acc = None
    off = 0
    for xr, ks in zip(x_refs, ksizes):
        x = xr[...]
        if has_norm:
            xf = x.astype(F32)
            xf = xf * lax.rsqrt(jnp.mean(xf * xf, axis=-1, keepdims=True) + NORM_EPS)
            x = xf * g_ref[...]
        x = x.astype(BF16)
        w = w_ref[off:off + ks, :].astype(BF16)
        part = jnp.dot(x, w, preferred_element_type=F32)
        acc = part if acc is None else acc + part
        off += ks
    if has_bias:
        acc = acc + b_ref[...]
    if epilogue == "glu":
        acc = e_ref[...].astype(F32) * _sigmoid(acc)
    elif epilogue == "resgate":
        i = pl.program_id(0)
        row = (i % tiles_per_batch) * tm + lax.broadcasted_iota(jnp.int32, (tm, 1), 0)
        gate = jnp.where(row < ctx_len, gate_ref[0, 0:1, :], gate_ref[0, 1:2, :])
        acc = res_ref[...] + gate * acc
    o_ref[...] = acc.astype(o_ref.dtype)


def _mm(xs, w, *, name, out_dtype, tm, tn, norm_g=None, bias=None, glu_in=None, res=None, gate=None,
        rows_per_batch=None, ctx_len=0):
    m = xs[0].shape[0]
    ksizes = tuple(x.shape[1] for x in xs)
    k, n = w.shape
    assert sum(ksizes) == k and m % tm == 0 and n % tn == 0
    epilogue = "glu" if glu_in is not None else ("resgate" if res is not None else None)
    tiles_per_batch = (rows_per_batch // tm) if rows_per_batch else 1
    in_specs = [pl.BlockSpec((tm, ks), lambda i, j: (i, 0)) for ks in ksizes]
    in_specs.append(pl.BlockSpec((k, tn), lambda i, j: (0, j)))
    args = list(xs) + [w]
    if norm_g is not None:
        in_specs.append(pl.BlockSpec((1, k), lambda i, j: (0, 0)))
        args.append(norm_g.reshape(1, k).astype(F32))
    if bias is not None:
        in_specs.append(pl.BlockSpec((1, tn), lambda i, j: (0, j)))
        args.append(bias.reshape(1, n).astype(F32))
    if epilogue == "glu":
        in_specs.append(pl.BlockSpec((tm, tn), lambda i, j: (i, j)))
        args.append(glu_in)
    if epilogue == "resgate":
        tpb = tiles_per_batch
        in_specs.append(pl.BlockSpec((tm, tn), lambda i, j: (i, j)))
        in_specs.append(pl.BlockSpec((1, 2, tn), lambda i, j: (i // tpb, 0, j)))
        args += [res, gate]
    body = functools.partial(_mm_body, nx=len(xs), ksizes=ksizes, has_norm=norm_g is not None,
                             has_bias=bias is not None, epilogue=epilogue, tm=tm,
                             tiles_per_batch=tiles_per_batch, ctx_len=ctx_len)
    return pl.pallas_call(
        body,
        grid=(m // tm, n // tn),
        in_specs=in_specs,
        out_specs=pl.BlockSpec((tm, tn), lambda i, j: (i, j)),
        out_shape=jax.ShapeDtypeStruct((m, n), out_dtype),
        compiler_params=_cp("parallel", "arbitrary"),
        name=name,
    )(*args)


def _ada_body(c_ref, w_ref, b_ref, o_ref):
    acc = jnp.dot(c_ref[...].astype(BF16), w_ref[0].astype(BF16), preferred_element_type=F32)
    o_ref[0] = acc + b_ref[0]


def _ada_mod(cond, ada_w, ada_b):
    depth, d, n6 = ada_w.shape
    rows = cond.shape[0]
    tn = _pick_tile(n6, 512, LANES)
    return pl.pallas_call(
        _ada_body,
        grid=(depth, n6 // tn),
        in_specs=[pl.BlockSpec((rows, d), lambda l, j: (0, 0)),
                  pl.BlockSpec((1, d, tn), lambda l, j: (l, 0, j)),
                  pl.BlockSpec((1, 1, tn), lambda l, j: (l, 0, j))],
        out_specs=pl.BlockSpec((1, rows, tn), lambda l, j: (l, 0, j)),
        out_shape=jax.ShapeDtypeStruct((depth, rows, n6), F32),
        compiler_params=_cp("parallel", "arbitrary"),
        name="ada_mod",
    )(cond, ada_w, ada_b.reshape(depth, 1, n6).astype(F32))


def _norm_mod_body(*refs, tm, tiles_per_batch, ctx_len, router):
    if router:
        x_ref, g_ref, sh_ref, sc_ref, whi_ref, wlo_ref, br_ref, h_ref, lg_ref = refs
    else:
        x_ref, g_ref, sh_ref, sc_ref, h_ref = refs
    i = pl.program_id(0)
    x = x_ref[...]
    y = x * lax.rsqrt(jnp.mean(x * x, axis=-1, keepdims=True) + NORM_EPS) * g_ref[...]
    is_ctx = (i % tiles_per_batch) * tm < ctx_len
    sh = jnp.where(is_ctx, sh_ref[0, 0:1, :], sh_ref[0, 1:2, :])
    sc = jnp.where(is_ctx, sc_ref[0, 0:1, :], sc_ref[0, 1:2, :])
    h = y * (1.0 + sc) + sh
    h_ref[...] = h.astype(BF16)
    if router:
        hi = h.astype(BF16)
        lo = (h - hi.astype(F32)).astype(BF16)
        lg = jnp.dot(hi, whi_ref[...], preferred_element_type=F32)
        lg = lg + jnp.dot(hi, wlo_ref[...], preferred_element_type=F32)
        lg = lg + jnp.dot(lo, whi_ref[...], preferred_element_type=F32)
        lg_ref[...] = lg + br_ref[...]


def _norm_mod(x, g, shift, scale, *, rows_per_batch, ctx_len, router=None):
    m, d = x.shape
    tm = _pick_tile(math.gcd(rows_per_batch, ctx_len), 256, 8)
    tpb = rows_per_batch // tm
    in_specs = [
        pl.BlockSpec((tm, d), lambda i: (i, 0)),
        pl.BlockSpec((1, d), lambda i: (0, 0)),
        pl.BlockSpec((1, 2, d), lambda i: (i // tpb, 0, 0)),
        pl.BlockSpec((1, 2, d), lambda i: (i // tpb, 0, 0)),
    ]
    args = [x, g.reshape(1, d), shift, scale]
    out_specs = [pl.BlockSpec((tm, d), lambda i: (i, 0))]
    out_shape = [jax.ShapeDtypeStruct((m, d), BF16)]
    if router is not None:
        whi, wlo, br = router
        in_specs += [pl.BlockSpec((d, LANES), lambda i: (0, 0)),
                     pl.BlockSpec((d, LANES), lambda i: (0, 0)),
                     pl.BlockSpec((1, LANES), lambda i: (0, 0))]
        args += [whi, wlo, br]
        out_specs.append(pl.BlockSpec((tm, LANES), lambda i: (i, 0)))
        out_shape.append(jax.ShapeDtypeStruct((m, LANES), F32))
    body = functools.partial(_norm_mod_body, tm=tm, tiles_per_batch=tpb, ctx_len=ctx_len,
                             router=router is not None)
    outs = pl.pallas_call(body, grid=(m // tm,), in_specs=in_specs, out_specs=out_specs,
                          out_shape=out_shape, compiler_params=_cp("parallel"),
                          name="norm_mod_router" if router is not None else "norm_mod")(*args)
    return outs if router is not None else outs[0]


def _final_norm_body(x_ref, g_ref, o_ref):
    x = x_ref[...]
    o_ref[...] = x * lax.rsqrt(jnp.mean(x * x, axis=-1, keepdims=True) + NORM_EPS) * g_ref[...]


def _final_norm(x, g, *, batch, rows_per_batch, ctx_len):
    d = x.shape[1]
    n_lat = rows_per_batch - ctx_len
    tm = _pick_tile(math.gcd(n_lat, ctx_len), 256, 8)
    tpb, ctx_tiles, lat_tiles = rows_per_batch // tm, ctx_len // tm, n_lat // tm
    out = pl.pallas_call(
        _final_norm_body,
        grid=(batch, lat_tiles),
        in_specs=[pl.BlockSpec((tm, d), lambda b, i: (b * tpb + ctx_tiles + i, 0)),
                  pl.BlockSpec((1, d), lambda b, i: (0, 0))],
        out_specs=pl.BlockSpec((tm, d), lambda b, i: (b * lat_tiles + i, 0)),
        out_shape=jax.ShapeDtypeStruct((batch * n_lat, d), F32),
        compiler_params=_cp("parallel", "parallel"),
        name="final_norm",
    )(x, g.reshape(1, d))
    return out.reshape(batch, n_lat, d)


def _rope_body(x_ref, cos_ref, sa_ref, sb_ref, cs_ref, o_ref, *, reps):
    x = x_ref[...].astype(F32)
    width = x.shape[1]
    x_dn = pltpu.roll(x, width - 16, 1)
    x_up = pltpu.roll(x, 16, 1)
    tile = lambda r: jnp.tile(r[...], (1, reps))
    y = x * tile(cos_ref) + x_dn * tile(sa_ref) + x_up * tile(sb_ref)
    o_ref[...] = (y * cs_ref[...]).astype(o_ref.dtype)


def _rope(x, ncols, tables, col_scale, *, rows_per_batch, col_off=0):
    m = x.shape[0]
    cos, sa, sb = tables
    wb = max(w for w in (512, 256, 128) if ncols % w == 0 and col_off % w == 0)
    tm = _pick_tile(rows_per_batch, 1088)
    tpb = rows_per_batch // tm
    ob = col_off // wb
    body = functools.partial(_rope_body, reps=wb // LANES)
    tspec = pl.BlockSpec((tm, LANES), lambda i, j: (i % tpb, 0))
    return pl.pallas_call(
        body,
        grid=(m // tm, ncols // wb),
        in_specs=[pl.BlockSpec((tm, wb), lambda i, j: (i, j + ob)), tspec, tspec, tspec,
                  pl.BlockSpec((1, wb), lambda i, j: (0, j))],
        out_specs=pl.BlockSpec((tm, wb), lambda i, j: (i, j)),
        out_shape=jax.ShapeDtypeStruct((m, ncols), x.dtype),
        compiler_params=_cp("parallel", "arbitrary"),
        name="rope",
    )(x, cos, sa, sb, col_scale.reshape(1, ncols).astype(F32))


def _rope_tables(n_lat, ctx_len):
    rows = n_lat // GRID_W
    row = jnp.repeat(jnp.arange(rows, dtype=F32), GRID_W)
    col = jnp.tile(jnp.arange(GRID_W, dtype=F32), rows)
    quarter = ROPE_DIM // 4
    inv = ROPE_BASE ** (-jnp.arange(quarter, dtype=F32) / quarter)
    ar = row[:, None] * inv
    ac = col[:, None] * inv
    ang = jnp.concatenate([ar, ar, ac, ac], axis=-1)
    ang = jnp.concatenate([jnp.zeros((ctx_len, ROPE_DIM), F32), ang], axis=0)
    ang = jnp.tile(ang, (1, LANES // ROPE_DIM))
    cos, sin = jnp.cos(ang), jnp.sin(ang)
    lane = jnp.arange(LANES)
    even = ((lane // quarter) % 2 == 0)[None, :]
    sa = jnp.where(even, -sin, 0.0)
    sb = jnp.where(even, 0.0, sin)
    return cos, sa, sb


ATTN_ALIGN = 256


def _softmax_pv(q, k_ref, va_ref, bounds):
    ms, ovs = [], []
    for lo, hi in bounds:
        s = lax.dot_general(q, k_ref[lo:hi, :], (((1,), (1,)), ((), ())), preferred_element_type=F32)
        m = jnp.max(s, axis=-1, keepdims=True)
        e = jnp.exp2(s - m).astype(BF16)
        ovs.append(jnp.dot(e, va_ref[lo:hi, :], preferred_element_type=F32))
        ms.append(m)
    m_all = functools.reduce(jnp.maximum, ms)
    acc = sum(ov * jnp.exp2(m - m_all) for m, ov in zip(ms, ovs))
    return acc[:, :LANES] / acc[:, LANES:]


def _key_chunks(n_keys):
    if n_keys < 2 * ATTN_ALIGN:
        return ((0, n_keys),)
    half = (n_keys // ATTN_ALIGN + 1) // 2 * ATTN_ALIGN
    return ((0, half), (half, n_keys))


def _lat_tile(seq, ctx_len):
    n_lat = seq - ctx_len
    assert ctx_len % ATTN_ALIGN == 0 and n_lat % ATTN_ALIGN == 0
    return _pick_tile(n_lat, 512, ATTN_ALIGN)


def _lat_rows(seq, ctx_len, tq):
    return lambda b, i: pl.multiple_of(b * seq + ctx_len + i * tq, ATTN_ALIGN)


def _diff_attn_body(lam_ref, q_ref, k_ref, v_ref, sub_ref, *rest, post, bounds):
    o_ref, va_ref = rest[-2:]

    def fill():
        va_ref[:, :LANES] = v_ref[...]
        va_ref[:, LANES:] = jnp.ones(v_ref.shape, BF16)

    if len(rest) == 2:
        pl.when(pl.program_id(2) == 0)(fill)
    else:
        fill()

    q = q_ref[...]
    lane = lax.broadcasted_iota(jnp.int32, (1, LANES), 1)
    first = lane < DIFF_HEAD_DIM
    zero = jnp.zeros_like(q)
    q0 = jnp.where(first, q, zero)
    q1 = jnp.where(first, zero, q)
    o = _softmax_pv(q0, k_ref, va_ref, bounds) - lam_ref[0] * _softmax_pv(q1, k_ref, va_ref, bounds)
    o = o * lax.rsqrt(jnp.mean(o * o, axis=-1, keepdims=True) + NORM_EPS) * sub_ref[...] * post
    o_ref[...] = o.astype(o_ref.dtype)


def _diff_attn(qk, proj, lam, subln, *, batch, seq, ctx_len, heads, q_blk, k_blk, v_blk, post):
    tq = _lat_tile(seq, ctx_len)
    rows = _lat_rows(seq, ctx_len, tq)
    smem = pl.BlockSpec(memory_space=pltpu.SMEM)
    args = (lam.reshape(1).astype(F32), qk, qk, proj, subln.reshape(1, LANES).astype(F32))
    out_shape = jax.ShapeDtypeStruct((batch * seq, heads * LANES), BF16)
    elem = (pl.Element(tq), pl.Element(LANES))
    lat = pl.pallas_call(
        functools.partial(_diff_attn_body, post=post, bounds=_key_chunks(seq)),
        grid=(batch, heads, (seq - ctx_len) // tq),
        in_specs=[
            smem,
            pl.BlockSpec(elem, lambda b, h, i: (rows(b, i), pl.multiple_of((q_blk + h) * LANES, LANES))),
            pl.BlockSpec((seq, LANES), lambda b, h, i: (b, k_blk + h)),
            pl.BlockSpec((seq, LANES), lambda b, h, i: (b, v_blk + h)),
            pl.BlockSpec((1, LANES), lambda b, h, i: (0, 0)),
        ],
        out_specs=pl.BlockSpec(elem, lambda b, h, i: (rows(b, i), pl.multiple_of(h * LANES, LANES))),
        out_shape=out_shape,
        scratch_shapes=[pltpu.VMEM((seq, 2 * LANES), BF16)],
        compiler_params=_cp("parallel", "parallel", "arbitrary"),
        name="diff_attn",
    )(*args)
    cpb = seq // ctx_len
    ctx_spec = lambda blk: pl.BlockSpec((ctx_len, LANES), lambda b, h: (b * cpb, blk + h))
    return pl.pallas_call(
        functools.partial(_diff_attn_body, post=post, bounds=_key_chunks(ctx_len)),
        grid=(batch, heads),
        in_specs=[smem, ctx_spec(q_blk), ctx_spec(k_blk), ctx_spec(v_blk),
                  pl.BlockSpec((1, LANES), lambda b, h: (0, 0)), pl.BlockSpec(memory_space=pl.ANY)],
        out_specs=ctx_spec(0),
        out_shape=out_shape,
        scratch_shapes=[pltpu.VMEM((ctx_len, 2 * LANES), BF16)],
        input_output_aliases={5: 0},
        compiler_params=_cp("parallel", "parallel"),
        name="diff_attn_ctx",
    )(*args, lat)


def _mla_attn_body(qn_ref, qr_ref, kn_ref, kr_ref, v_ref, *rest, bounds):
    o_ref, ka_ref, va_ref = rest[-3:]
    h = pl.program_id(1)

    def fill():
        ka_ref[:, :LANES] = kn_ref[...]
        ka_ref[:, LANES:] = kr_ref[...]
        va_ref[:, :LANES] = v_ref[...]
        va_ref[:, LANES:] = jnp.ones(v_ref.shape, BF16)

    if len(rest) == 3:
        pl.when(pl.program_id(2) == 0)(fill)
    else:
        fill()

    qr = qr_ref[...]
    lane = lax.broadcasted_iota(jnp.int32, (1, LANES), 1)
    mine = (lane < ROPE_DIM) == (h % 2 == 0)
    qr = jnp.where(mine, qr, jnp.zeros_like(qr))
    q = jnp.concatenate([qn_ref[...], qr], axis=1)
    o_ref[...] = _softmax_pv(q, ka_ref, va_ref, bounds).astype(o_ref.dtype)


def _mla_attn(q_up, q_rope, kv_up, k_rope, *, batch, seq, ctx_len, heads):
    tq = _lat_tile(seq, ctx_len)
    rows = _lat_rows(seq, ctx_len, tq)
    args = (q_up, q_rope, kv_up, k_rope, kv_up)
    out_shape = jax.ShapeDtypeStruct((batch * seq, heads * LANES), BF16)
    elem = (pl.Element(tq), pl.Element(LANES))
    lat = pl.pallas_call(
        functools.partial(_mla_attn_body, bounds=_key_chunks(seq)),
        grid=(batch, heads, (seq - ctx_len) // tq),
        in_specs=[
            pl.BlockSpec(elem, lambda b, h, i: (rows(b, i), pl.multiple_of(h * LANES, LANES))),
            pl.BlockSpec(elem, lambda b, h, i: (rows(b, i), pl.multiple_of((h // 2) * LANES, LANES))),
            pl.BlockSpec((seq, LANES), lambda b, h, i: (b, 2 * h)),
            pl.BlockSpec((seq, LANES), lambda b, h, i: (b, 0)),
            pl.BlockSpec((seq, LANES), lambda b, h, i: (b, 2 * h + 1)),
        ],
        out_specs=pl.BlockSpec(elem, lambda b, h, i: (rows(b, i), pl.multiple_of(h * LANES, LANES))),
        out_shape=out_shape,
        scratch_shapes=[pltpu.VMEM((seq, 2 * LANES), BF16), pltpu.VMEM((seq, 2 * LANES), BF16)],
        compiler_params=_cp("parallel", "parallel", "arbitrary"),
        name="mla_attn",
    )(*args)
    cpb = seq // ctx_len
    ctx_spec = lambda col: pl.BlockSpec((ctx_len, LANES), lambda b, h: (b * cpb, col(h)))
    return pl.pallas_call(
        functools.partial(_mla_attn_body, bounds=_key_chunks(ctx_len)),
        grid=(batch, heads),
        in_specs=[ctx_spec(lambda h: h), ctx_spec(lambda h: h // 2), ctx_spec(lambda h: 2 * h),
                  ctx_spec(lambda h: 0), ctx_spec(lambda h: 2 * h + 1), pl.BlockSpec(memory_space=pl.ANY)],
        out_specs=ctx_spec(lambda h: h),
        out_shape=out_shape,
        scratch_shapes=[pltpu.VMEM((ctx_len, 2 * LANES), BF16), pltpu.VMEM((ctx_len, 2 * LANES), BF16)],
        input_output_aliases={5: 0},
        compiler_params=_cp("parallel", "parallel"),
        name="mla_attn_ctx",
    )(*args, lat)


def _retention_body(*refs, backward, cs, heads):
    if backward:
        lg_ref, q_ref, k_ref, v_ref, yf_ref, gate_ref, ng_ref, o_ref, s_ref, d_ref, qd_ref, kd_ref = refs
    else:
        lg_ref, q_ref, k_ref, v_ref, o_ref, s_ref, d_ref, qd_ref, kd_ref = refs
    t = pl.program_id(1)

    @pl.when(t == 0)
    def _():
        s_ref[...] = jnp.zeros_like(s_ref)
        pos_r = lax.broadcasted_iota(jnp.int32, (cs, cs), 0).astype(F32)
        pos_c = lax.broadcasted_iota(jnp.int32, (cs, cs), 1).astype(F32)
        pos = lax.broadcasted_iota(jnp.int32, (cs, 1), 0).astype(F32)
        for h in range(heads):
            lg = lg_ref[h]
            if backward:
                diff = pos_c - pos_r
                keep = diff > 0
                qd_ref[h] = jnp.exp(lg * (cs - pos))
                kd_ref[h] = jnp.exp(lg * pos)
            else:
                diff = pos_r - pos_c
                keep = diff >= 0
                qd_ref[h] = jnp.exp(lg * (pos + 1.0))
                kd_ref[h] = jnp.exp(lg * (cs - 1.0 - pos))
            d_ref[h] = jnp.where(keep, jnp.exp(lg * jnp.maximum(diff, 0.0)), 0.0)

    lane = lax.broadcasted_iota(jnp.int32, (1, LANES), 1)
    nt = (((1,), (1,)), ((), ()))
    tn = (((0,), (0,)), ((), ()))
    for h in range(heads):
        blk = slice((h // 2) * LANES, (h // 2 + 1) * LANES)
        col = slice(h * RET_V, (h + 1) * RET_V)
        mine = (lane < RET_K) == (h % 2 == 0)
        q = q_ref[:, blk]
        q = jnp.where(mine, q, jnp.zeros_like(q))
        k = k_ref[:, blk]
        v = v_ref[:, col]
        scores = lax.dot_general(q, k, nt, preferred_element_type=F32) * d_ref[h]
        intra = jnp.dot(scores.astype(BF16), v, preferred_element_type=F32)
        state = s_ref[h]
        q_w = (q.astype(F32) * qd_ref[h]).astype(BF16)
        cross = jnp.dot(q_w, state.astype(BF16), preferred_element_type=F32)
        k_w = (k.astype(F32) * kd_ref[h]).astype(BF16)
        upd = lax.dot_general(k_w, v, tn, preferred_element_type=F32)
        s_ref[h] = jnp.exp(lg_ref[h] * cs) * state + upd
        y = intra + cross
        if backward:
            y = y + yf_ref[:, col]
            y = y * lax.rsqrt(jnp.mean(y * y, axis=-1, keepdims=True) + NORM_EPS) * ng_ref[:, col]
            g = gate_ref[:, col].astype(F32)
            o_ref[:, col] = (g * _sigmoid(g) * y).astype(o_ref.dtype)
        else:
            o_ref[:, col] = y


def _retention(qk, proj, log_g, norm_g, *, batch, seq, ctx_len, heads, q_off, k_off, v_off, g_off):
    cs = RET_CHUNK
    nc, nc_ctx = seq // cs, ctx_len // cs
    wqk, wv = heads * RET_K, heads * RET_V
    assert q_off % wqk == 0 and k_off % wqk == 0 and v_off % wv == 0 and g_off % wv == 0
    smem = pl.BlockSpec(memory_space=pltpu.SMEM)

    def fwd_chunk(t):
        return t

    def bwd_chunk(t):
        return jnp.where(t < nc_ctx, nc_ctx - 1 - t, nc - 1 - (t - nc_ctx))

    def specs(chunk):
        row = lambda b, t: b * nc + chunk(t)
        return (pl.BlockSpec((cs, wqk), lambda b, t: (row(b, t), q_off // wqk)),
                pl.BlockSpec((cs, wqk), lambda b, t: (row(b, t), k_off // wqk)),
                pl.BlockSpec((cs, wv), lambda b, t: (row(b, t), v_off // wv)),
                pl.BlockSpec((cs, wv), lambda b, t: (row(b, t), 0)),
                pl.BlockSpec((cs, wv), lambda b, t: (row(b, t), g_off // wv)))

    scratch = [pltpu.VMEM((heads, LANES, RET_V), F32), pltpu.VMEM((heads, cs, cs), F32),
               pltpu.VMEM((heads, cs, 1), F32), pltpu.VMEM((heads, cs, 1), F32)]
    qs, ks, vs, ys, gs = specs(fwd_chunk)
    y_f = pl.pallas_call(
        functools.partial(_retention_body, backward=False, cs=cs, heads=heads),
        grid=(batch, nc),
        in_specs=[smem, qs, ks, vs],
        out_specs=ys,
        out_shape=jax.ShapeDtypeStruct((batch * seq, wv), F32),
        scratch_shapes=scratch,
        compiler_params=_cp("parallel", "arbitrary"),
        name="retention_fwd",
    )(log_g[0].astype(F32), qk, qk, proj)
    qs, ks, vs, ys, gs = specs(bwd_chunk)
    return pl.pallas_call(
        functools.partial(_retention_body, backward=True, cs=cs, heads=heads),
        grid=(batch, nc),
        in_specs=[smem, qs, ks, vs, ys, gs, pl.BlockSpec((1, wv), lambda b, t: (0, 0))],
        out_specs=ys,
        out_shape=jax.ShapeDtypeStruct((batch * seq, wv), BF16),
        scratch_shapes=scratch,
        compiler_params=_cp("parallel", "arbitrary"),
        name="retention_bwd",
    )(log_g[1].astype(F32), qk, qk, proj, y_f, proj, norm_g.reshape(1, wv).astype(F32))


def _s5_params(a_re, a_im, log_dt, b_re, b_im, c_re, c_im):
    L, ch = S5_CHUNK, S5_CH
    a_re, a_im = a_re.astype(F32), a_im.astype(F32)
    dt = jnp.exp(log_dt.astype(F32))[..., None]
    e = jnp.arange(L + 1, dtype=F32)[:, None, None, None]
    mag = jnp.exp(a_re * dt * e)
    pw_re, pw_im = mag * jnp.cos(a_im * dt * e), mag * jnp.sin(a_im * dt * e)
    ab_re, ab_im = pw_re[1], pw_im[1]
    den = a_re * a_re + a_im * a_im
    f_re = ((ab_re - 1.0) * a_re + ab_im * a_im) / den
    f_im = (ab_im * a_re - (ab_re - 1.0) * a_im) / den
    bb_re = f_re[..., None] * b_re - f_im[..., None] * b_im
    bb_im = f_re[..., None] * b_im + f_im[..., None] * b_re
    c_re, c_im = c_re.astype(F32), c_im.astype(F32)

    cp_re = c_re[None] * pw_re[:L, :, :, None, :] - c_im[None] * pw_im[:L, :, :, None, :]
    cp_im = c_re[None] * pw_im[:L, :, :, None, :] + c_im[None] * pw_re[:L, :, :, None, :]
    hp = lax.Precision.HIGHEST
    kmat = (jnp.einsum("ldgcp,dgpk->dglck", cp_re, bb_re, precision=hp)
            - jnp.einsum("ldgcp,dgpk->dglck", cp_im, bb_im, precision=hp))
    idx = jnp.arange(L)
    lag_f = idx[None, :] - idx[:, None]
    lag_b = -lag_f

    def toeplitz(k, lag):
        t = k[:, jnp.clip(lag, 0, L - 1)]
        t = jnp.where((lag >= 0)[None, :, :, None, None], t, 0.0)
        return t.transpose(0, 1, 4, 2, 3).reshape(k.shape[0], L * ch, L * ch)

    m_both = jnp.concatenate([toeplitz(kmat[0], lag_f), toeplitz(kmat[1], lag_b)], axis=-1)

    def state_in(d, exps):
        p_re, p_im = pw_re[exps, d], pw_im[exps, d]
        w_re = p_re[..., None] * bb_re[d][None] - p_im[..., None] * bb_im[d][None]
        w_im = p_re[..., None] * bb_im[d][None] + p_im[..., None] * bb_re[d][None]
        w = jnp.concatenate([w_re, w_im], axis=2)
        return w.transpose(1, 0, 3, 2).reshape(w.shape[1], L * ch, 2 * S5_STATE)

    w_both = jnp.concatenate([state_in(0, L - 1 - idx), state_in(1, idx)], axis=-1)

    def state_out(d, exps):
        p_re, p_im = pw_re[exps, d], pw_im[exps, d]
        v_re = c_re[d][None] * p_re[:, :, None, :] - c_im[d][None] * p_im[:, :, None, :]
        v_im = c_re[d][None] * p_im[:, :, None, :] + c_im[d][None] * p_re[:, :, None, :]
        v = jnp.concatenate([v_re, -v_im], axis=-1)
        return v.transpose(1, 3, 0, 2).reshape(v.shape[1], 2 * S5_STATE, L * ch)

    v_mat = jnp.stack([state_out(0, idx + 1), state_out(1, L - idx)])
    a_pow = jnp.concatenate([pw_re[L], pw_im[L]], axis=-1)
    return m_both.astype(BF16), w_both.astype(BF16), v_mat.astype(BF16), a_pow


def _s5_state_in_body(u_ref, w_ref, o_ref):
    s = jnp.dot(u_ref[0], w_ref[0], preferred_element_type=F32)
    o_ref[0] = s[:, :LANES]
    o_ref[1] = s[:, LANES:]


def _s5_scan_body(s_ref, a_ref, o_ref, *, nk, nk_ctx):
    d = pl.program_id(0)
    a = a_ref[0]
    lane = lax.broadcasted_iota(jnp.int32, a.shape, 1)
    first = lane < S5_STATE
    a_sw = pltpu.roll(a, S5_STATE, 1)
    a_same = jnp.where(first, a, a_sw)
    a_cross = jnp.where(first, -a_sw, a)
    a_same, a_cross = a_same[None], a_cross[None]

    def step(k, carry):
        hstate, hswap = carry
        o_ref[0, k] = hstate
        s = s_ref[0, k]
        return (a_same * hstate + a_cross * hswap + s,
                a_same * hswap - a_cross * hstate + pltpu.roll(s, S5_STATE, 2))

    zero = jnp.zeros(o_ref.shape[2:], F32)
    unroll = 8 if (nk % 8 == 0 and nk_ctx % 8 == 0) else 1

    @pl.when(d == 0)
    def _():
        lax.fori_loop(0, nk, step, (zero, zero), unroll=unroll)

    @pl.when(d == 1)
    def _():
        hc = lax.fori_loop(0, nk_ctx, lambda t, c: step(nk_ctx - 1 - t, c), (zero, zero), unroll=unroll)
        lax.fori_loop(0, nk - nk_ctx, lambda t, c: step(nk - 1 - t, c), hc, unroll=unroll)


def _s5_out_body(u_ref, m_ref, h_ref, v_ref, d_ref, o_ref):
    u = u_ref[0]
    y = jnp.dot(u, m_ref[0], preferred_element_type=F32)
    y = y[:, :2 * LANES] + y[:, 2 * LANES:]
    y = y + jnp.dot(h_ref[0].astype(BF16), v_ref[0, 0], preferred_element_type=F32)
    y = y + jnp.dot(h_ref[1].astype(BF16), v_ref[1, 0], preferred_element_type=F32)
    y = y + d_ref[0] * u.astype(F32)
    o_ref[0] = jax.nn.gelu(y).astype(o_ref.dtype)


def _s5_mix(proj, su_off, params, d_skip, *, batch, seq, ctx_len, gw):
    m_both, w_both, v_mat, a_pow = params
    L, ch, st2 = S5_CHUNK, S5_CH, 2 * S5_STATE
    groups = gw // ch
    nk, nk_ctx = seq // L, ctx_len // L
    rows = nk * batch
    cols = L * ch
    u = proj[:, su_off:su_off + gw].reshape(batch, nk, L, groups, ch)
    u = u.transpose(3, 1, 0, 2, 4).reshape(groups, rows, cols)

    s_in = pl.pallas_call(
        _s5_state_in_body,
        grid=(groups,),
        in_specs=[pl.BlockSpec((1, rows, cols), lambda g: (g, 0, 0)),
                  pl.BlockSpec((1, cols, 2 * st2), lambda g: (g, 0, 0))],
        out_specs=pl.BlockSpec((2, rows, st2), lambda g: (0, 0, g)),
        out_shape=jax.ShapeDtypeStruct((2, rows, groups * st2), F32),
        compiler_params=_cp("parallel"),
        name="s5_state_in",
    )(u, w_both)

    gb = 8
    s4 = s_in.reshape(2, nk, batch, groups, st2)
    h_prev = pl.pallas_call(
        functools.partial(_s5_scan_body, nk=nk, nk_ctx=nk_ctx),
        grid=(2, groups // gb),
        in_specs=[pl.BlockSpec((1, nk, batch, gb, st2), lambda d, g: (d, 0, 0, g, 0)),
                  pl.BlockSpec((1, gb, st2), lambda d, g: (d, g, 0))],
        out_specs=pl.BlockSpec((1, nk, batch, gb, st2), lambda d, g: (d, 0, 0, g, 0)),
        out_shape=jax.ShapeDtypeStruct((2, nk, batch, groups, st2), F32),
        compiler_params=_cp("parallel", "parallel"),
        name="s5_scan",
    )(s4, a_pow)
    h2 = h_prev.reshape(2, rows, groups * st2)

    d_rep = jnp.tile(d_skip.astype(F32), (1, L)).reshape(groups, 1, cols)
    y = pl.pallas_call(
        _s5_out_body,
        grid=(groups,),
        in_specs=[pl.BlockSpec((1, rows, cols), lambda g: (g, 0, 0)),
                  pl.BlockSpec((1, cols, 2 * cols), lambda g: (g, 0, 0)),
                  pl.BlockSpec((2, rows, st2), lambda g: (0, 0, g)),
                  pl.BlockSpec((2, 1, st2, cols), lambda g: (0, g, 0, 0)),
                  pl.BlockSpec((1, 1, cols), lambda g: (g, 0, 0))],
        out_specs=pl.BlockSpec((1, rows, cols), lambda g: (g, 0, 0)),
        out_shape=jax.ShapeDtypeStruct((groups, rows, cols), BF16),
        compiler_params=_cp("parallel"),
        name="s5_out",
    )(u, m_both, h2, v_mat, d_rep)
    y = y.reshape(groups, nk, batch, L, ch).transpose(2, 1, 3, 0, 4)
    return y.reshape(batch * seq, gw)


S5N_CHUNK = 8
S5N_GROUPS = LANES // S5_CH
S5N_HALF = S5N_GROUPS * S5_STATE


def _s5n_params(a_re, a_im, log_dt, b_re, b_im, c_re, c_im, d_skip):
    L, ch, gl = S5N_CHUNK, S5_CH, S5N_GROUPS
    a_re, a_im = a_re.astype(F32), a_im.astype(F32)
    groups = a_re.shape[1]
    nb = groups // gl
    dt = jnp.exp(log_dt.astype(F32))[..., None]
    e = jnp.arange(L + 1, dtype=F32)[:, None, None, None]
    mag = jnp.exp(a_re * dt * e)
    pw_re, pw_im = mag * jnp.cos(a_im * dt * e), mag * jnp.sin(a_im * dt * e)
    ab_re, ab_im = pw_re[1], pw_im[1]
    den = a_re * a_re + a_im * a_im
    f_re = ((ab_re - 1.0) * a_re + ab_im * a_im) / den
    f_im = (ab_im * a_re - (ab_re - 1.0) * a_im) / den
    bb_re = f_re[..., None] * b_re - f_im[..., None] * b_im
    bb_im = f_re[..., None] * b_im + f_im[..., None] * b_re
    c_re, c_im = c_re.astype(F32), c_im.astype(F32)
    hp = lax.Precision.HIGHEST
    idx = jnp.arange(L)
    eye = jnp.eye(gl, dtype=F32)

    cp_re = c_re[None] * pw_re[:L, :, :, None, :] - c_im[None] * pw_im[:L, :, :, None, :]
    cp_im = c_re[None] * pw_im[:L, :, :, None, :] + c_im[None] * pw_re[:L, :, :, None, :]
    kmat = (jnp.einsum("ldgcp,dgpk->dglck", cp_re, bb_re, precision=hp)
            - jnp.einsum("ldgcp,dgpk->dglck", cp_im, bb_im, precision=hp))
    lag = idx[None, :] - idx[:, None]

    def toeplitz(k, lg):
        return jnp.where((lg >= 0)[None, :, :, None, None], k[:, jnp.clip(lg, 0, L - 1)], 0.0)

    t = toeplitz(kmat[0], lag) + toeplitz(kmat[1], -lag)
    skip = (idx[:, None] == idx[None, :])[None, :, :, None, None] * (
        jnp.eye(ch, dtype=F32)[None, None, None] * d_skip.astype(F32)[:, None, None, :, None])
    t = (t + skip).reshape(nb, gl, L, L, ch, ch)
    m_mat = t.transpose(0, 2, 1, 5, 3, 4).reshape(nb, L, LANES, L * ch)

    def state_in(d, exps):
        p_re, p_im = pw_re[exps, d], pw_im[exps, d]
        w_re = p_re[..., None] * bb_re[d][None] - p_im[..., None] * bb_im[d][None]
        w_im = p_re[..., None] * bb_im[d][None] + p_im[..., None] * bb_re[d][None]
        return jnp.stack([w_re, w_im], axis=0).transpose(2, 1, 4, 0, 3)

    w_full = jnp.stack([state_in(0, L - 1 - idx), state_in(1, idx)], axis=3)
    w_full = w_full.reshape(nb, gl, L, ch, 2, 2, S5_STATE)
    w_mat = w_full.transpose(0, 2, 1, 3, 4, 5, 6).reshape(nb, L, LANES, 4 * S5_STATE)

    def state_out(d, exps):
        p_re, p_im = pw_re[exps, d], pw_im[exps, d]
        v_re = c_re[d][None] * p_re[:, :, None, :] - c_im[d][None] * p_im[:, :, None, :]
        v_im = c_re[d][None] * p_im[:, :, None, :] + c_im[d][None] * p_re[:, :, None, :]
        return jnp.stack([v_re, -v_im], axis=0).transpose(2, 0, 4, 1, 3)

    v_full = jnp.stack([state_out(0, idx + 1), state_out(1, L - idx)], axis=1)
    v_full = v_full.reshape(nb, gl, 2, 2, S5_STATE, L, ch)
    v_mat = v_full.transpose(0, 2, 3, 1, 4, 5, 6).reshape(nb, 4 * S5N_HALF, L * ch)

    a_pow = jnp.stack([pw_re[L], pw_im[L]], axis=1).reshape(2, 2, nb * S5N_HALF)
    return m_mat.astype(BF16), w_mat.astype(BF16), v_mat.astype(BF16), a_pow


def _s5n_spreaders():
    L, ch, gl, st = S5N_CHUNK, S5_CH, S5N_GROUPS, S5_STATE
    e_out = jnp.einsum("ij,cd->icjd", jnp.eye(L), jnp.eye(ch))
    e_out = jnp.broadcast_to(e_out[:, :, :, None, :], (L, ch, L, gl, ch)).reshape(L * ch, L * LANES)
    e_st = jnp.broadcast_to(jnp.eye(4 * st).reshape(4 * st, 4, 1, st), (4 * st, 4, gl, st))
    return e_out.astype(BF16), e_st.reshape(4 * st, 4 * S5N_HALF).astype(BF16)


def _s5n_expand(compact, spread, row_group, col_group):
    full = jnp.dot(compact, spread, preferred_element_type=F32)
    rows = lax.broadcasted_iota(jnp.int32, full.shape, 0)
    cols = lax.broadcasted_iota(jnp.int32, full.shape, 1)
    return jnp.where(row_group(rows) == col_group(cols), full, 0.0).astype(BF16)


_S5N_IN_ROW_GROUP = lambda r: r // S5_CH
_S5N_STATE_GROUP = lambda c: (c % S5N_HALF) // S5_STATE
_S5N_OUT_COL_GROUP = lambda c: (c % LANES) // S5_CH


def _s5n_state_in_body(u_ref, w_ref, e_ref, o_ref, w_exp):
    @pl.when(pl.program_id(1) == 0)
    def _():
        for j in range(S5N_CHUNK):
            w_exp[j * LANES:(j + 1) * LANES, :] = _s5n_expand(w_ref[0, j], e_ref[...],
                                                              _S5N_IN_ROW_GROUP, _S5N_STATE_GROUP)

    u_cat = jnp.concatenate([u_ref[:, j, :] for j in range(S5N_CHUNK)], axis=1)
    acc = jnp.dot(u_cat, w_exp[...], preferred_element_type=F32)
    for d in range(2):
        for r in range(2):
            lo = (2 * d + r) * S5N_HALF
            o_ref[d, r] = acc[:, lo:lo + S5N_HALF]


def _s5n_scan_body(sr_ref, si_ref, a_ref, o_ref, *, nk, nk_ctx):
    d = pl.program_id(0)
    ar, ai = a_ref[0, 0], a_ref[0, 1]

    def step(k, carry):
        hr, hi = carry
        o_ref[0, 0, pl.ds(k, 1), :] = hr
        o_ref[0, 1, pl.ds(k, 1), :] = hi
        sr = sr_ref[0, 0, pl.ds(k, 1), :]
        si = si_ref[0, 0, pl.ds(k, 1), :]
        return ar * hr - ai * hi + sr, ar * hi + ai * hr + si

    zero = jnp.zeros_like(ar)
    unroll = 8 if (nk % 8 == 0 and nk_ctx % 8 == 0) else 1

    @pl.when(d == 0)
    def _():
        lax.fori_loop(0, nk, step, (zero, zero), unroll=unroll)

    @pl.when(d == 1)
    def _():
        hc = lax.fori_loop(0, nk_ctx, lambda t, c: step(nk_ctx - 1 - t, c), (zero, zero), unroll=unroll)
        lax.fori_loop(0, nk - nk_ctx, lambda t, c: step(nk - 1 - t, c), hc, unroll=unroll)


def _s5n_out_body(u_ref, m_ref, h_ref, v_ref, e_ref, o_ref, m_exp, v_exp):
    @pl.when(pl.program_id(1) == 0)
    def _():
        for j in range(S5N_CHUNK):
            m_exp[j * LANES:(j + 1) * LANES, :] = _s5n_expand(m_ref[0, j], e_ref[...],
                                                              _S5N_IN_ROW_GROUP, _S5N_OUT_COL_GROUP)
        v_exp[...] = _s5n_expand(v_ref[0], e_ref[...], _S5N_STATE_GROUP, _S5N_OUT_COL_GROUP)

    u_cat = jnp.concatenate([u_ref[:, j, :] for j in range(S5N_CHUNK)], axis=1)
    h_cat = jnp.concatenate([h_ref[d, r].astype(BF16) for d in range(2) for r in range(2)], axis=1)
    acc = (jnp.dot(u_cat, m_exp[...], preferred_element_type=F32)
           + jnp.dot(h_cat, v_exp[...], preferred_element_type=F32))
    y = jax.nn.gelu(acc).astype(o_ref.dtype)
    for i in range(S5N_CHUNK):
        o_ref[:, i, :] = y[:, i * LANES:(i + 1) * LANES]


def _s5n_mix(proj, su_off, params, *, batch, seq, ctx_len, gw):
    m_mat, w_mat, v_mat, a_pow = params
    L, half = S5N_CHUNK, S5N_HALF
    nb = gw // LANES
    nk, nk_ctx = seq // L, ctx_len // L
    assert su_off % LANES == 0 and seq % L == 0 and ctx_len % L == 0
    ub = su_off // LANES
    u3 = proj.reshape(batch * nk, L, proj.shape[1])
    lanes = batch * nb * half
    u_spec = pl.BlockSpec((nk, L, LANES), lambda n, b: (b, 0, ub + n))
    h_spec = pl.BlockSpec((2, 2, nk, half), lambda n, b: (0, 0, 0, b * nb + n))

    e_out, e_st = _s5n_spreaders()
    whole = lambda a: pl.BlockSpec(a.shape, lambda n, b: (0,) * a.ndim)
    s_in = pl.pallas_call(
        _s5n_state_in_body,
        grid=(nb, batch),
        in_specs=[u_spec, pl.BlockSpec((1,) + w_mat.shape[1:], lambda n, b: (n, 0, 0, 0)), whole(e_st)],
        out_specs=h_spec,
        out_shape=jax.ShapeDtypeStruct((2, 2, nk, lanes), F32),
        scratch_shapes=[pltpu.VMEM((L * LANES, 4 * half), BF16)],
        compiler_params=_cp("parallel", "arbitrary"),
        name="s5_state_in",
    )(u3, w_mat, e_st)

    wl = _pick_tile(lanes, 2048, LANES)
    part_spec = lambda r: pl.BlockSpec((1, 1, nk, wl), lambda d, w: (d, r, 0, w))
    a_lanes = jnp.tile(a_pow.reshape(2, 2, 1, nb * half), (1, 1, 1, batch))
    h_prev = pl.pallas_call(
        functools.partial(_s5n_scan_body, nk=nk, nk_ctx=nk_ctx),
        grid=(2, lanes // wl),
        in_specs=[part_spec(0), part_spec(1), pl.BlockSpec((1, 2, 1, wl), lambda d, w: (d, 0, 0, w))],
        out_specs=pl.BlockSpec((1, 2, nk, wl), lambda d, w: (d, 0, 0, w)),
        out_shape=jax.ShapeDtypeStruct((2, 2, nk, lanes), F32),
        compiler_params=_cp("parallel", "parallel"),
        name="s5_scan",
    )(s_in, s_in, a_lanes)

    y3 = pl.pallas_call(
        _s5n_out_body,
        grid=(nb, batch),
        in_specs=[u_spec,
                  pl.BlockSpec((1,) + m_mat.shape[1:], lambda n, b: (n, 0, 0, 0)),
                  h_spec,
                  pl.BlockSpec((1,) + v_mat.shape[1:], lambda n, b: (n, 0, 0)),
                  whole(e_out)],
        out_specs=pl.BlockSpec((nk, L, LANES), lambda n, b: (b, 0, n)),
        out_shape=jax.ShapeDtypeStruct((batch * nk, L, gw), BF16),
        scratch_shapes=[pltpu.VMEM((L * LANES, L * LANES), BF16), pltpu.VMEM((4 * half, L * LANES), BF16)],
        compiler_params=_cp("parallel", "arbitrary"),
        name="s5_out",
    )(u3, m_mat, h_prev, v_mat, e_out)
    return y3.reshape(batch * seq, gw)


def _expert_changed(te_ref):
    i = pl.program_id(1)
    return (i == 0) | (te_ref[i] != te_ref[jnp.maximum(i - 1, 0)])


def _moe_up_body(te_ref, x_ref, wg_ref, wu_ref, o_ref, wg_bf, wu_bf):
    @pl.when(_expert_changed(te_ref))
    def _():
        wg_bf[...] = wg_ref[0, 0].astype(BF16)
        wu_bf[...] = wu_ref[0, 0].astype(BF16)

    x = x_ref[...]
    a = jnp.dot(x, wg_bf[...], preferred_element_type=F32)
    b = jnp.dot(x, wu_bf[...], preferred_element_type=F32)
    o_ref[...] = (a * _sigmoid(a) * b).astype(o_ref.dtype)


def _moe_down_body(te_ref, h_ref, w_ref, rw_ref, o_ref, w_bf):
    @pl.when(_expert_changed(te_ref))
    def _():
        w_bf[...] = w_ref[0, 0].astype(BF16)

    y = jnp.dot(h_ref[...], w_bf[...], preferred_element_type=F32)
    o_ref[...] = (rw_ref[...] * y).astype(o_ref.dtype)


def _moe_combine_body(x_ref, y0_ref, y1_ref, gate_ref, o_ref, *, tm, tiles_per_batch, ctx_len):
    i = pl.program_id(0)
    is_ctx = (i % tiles_per_batch) * tm < ctx_len
    gate = jnp.where(is_ctx, gate_ref[0, 0:1, :], gate_ref[0, 1:2, :])
    o_ref[...] = x_ref[...] + gate * (y0_ref[...].astype(F32) + y1_ref[...].astype(F32))


def _moe_final_body(x_ref, y0_ref, y1_ref, gate_ref, g_ref, o_ref):
    x = x_ref[...] + gate_ref[0, 1:2, :] * (y0_ref[...].astype(F32) + y1_ref[...].astype(F32))
    o_ref[...] = x * lax.rsqrt(jnp.mean(x * x, axis=-1, keepdims=True) + NORM_EPS) * g_ref[...]


def _route(logits):
    assert MOE_TOPK == 2
    g_logit = logits[:, :MOE_GROUPS]
    g_prob = jax.nn.softmax(g_logit, axis=-1)
    g_idx = jnp.argmax(g_prob, axis=-1)
    g_p = jnp.max(g_prob, axis=-1)
    e_logit = logits[:, MOE_GROUPS:MOE_GROUPS + MOE_GROUPS * MOE_PER_GROUP]
    e_logit = e_logit.reshape(-1, MOE_GROUPS, MOE_PER_GROUP)
    sel = (jnp.arange(MOE_GROUPS)[None, :] == g_idx[:, None])[:, :, None]
    e_logit = jnp.sum(jnp.where(sel, e_logit, 0.0), axis=1)
    e_prob = jax.nn.softmax(e_logit, axis=-1)
    i0 = jnp.argmax(e_prob, axis=-1)
    p0 = jnp.max(e_prob, axis=-1)
    rest = jnp.where(jnp.arange(MOE_PER_GROUP)[None, :] == i0[:, None], -1.0, e_prob)
    i1 = jnp.argmax(rest, axis=-1)
    p1 = jnp.max(rest, axis=-1)
    e_p = jnp.stack([p0, p1], axis=-1)
    w = g_p[:, None] * e_p / jnp.sum(e_p, axis=-1, keepdims=True)
    ids = g_idx[:, None] * MOE_PER_GROUP + jnp.stack([i0, i1], axis=-1)
    return ids.astype(jnp.int32), w


def _moe(h, logits, x, gate, w_gate, w_up, w_down, layer, *, rows_per_batch, ctx_len, final_g=None):
    t, d = h.shape
    _, n_exp, _, dff = w_gate.shape
    tile = MOE_TILE
    ids, wts = _route(logits)
    flat_e = ids.reshape(-1)
    onehot = (flat_e[:, None] == jnp.arange(n_exp)[None, :]).astype(jnp.int32)
    counts = onehot.sum(0)
    rank = jnp.take_along_axis(jnp.cumsum(onehot, axis=0) - onehot, flat_e[:, None], axis=1)[:, 0]
    padded = (counts + tile - 1) // tile * tile
    starts = jnp.cumsum(padded) - padded
    pos = starts[flat_e] + rank
    n_rows = (t * MOE_TOPK // tile + n_exp) * tile
    n_tiles = n_rows // tile
    token = (jnp.arange(t * MOE_TOPK, dtype=jnp.int32) // MOE_TOPK).astype(F32)
    table = jnp.zeros((n_rows, 2), F32).at[pos].set(jnp.stack([token, wts.reshape(-1)], axis=1),
                                                    unique_indices=True, mode="promise_in_bounds")
    src = table[:, 0].astype(jnp.int32)
    row_w = table[:, 1]
    tile_start = jnp.arange(n_tiles, dtype=jnp.int32) * tile
    ends = starts + padded
    tile_e = jnp.minimum(jnp.sum(tile_start[:, None] >= ends[None, :], axis=1), n_exp - 1).astype(jnp.int32)

    xs = h.at[src].get(mode="promise_in_bounds")
    tn_up = _pick_tile(dff, 512, LANES)
    w_up_spec = pl.BlockSpec((1, 1, d, tn_up), lambda j, i, te: (layer, te[i], 0, j))
    hid = pl.pallas_call(
        _moe_up_body,
        grid_spec=pltpu.PrefetchScalarGridSpec(
            num_scalar_prefetch=1,
            grid=(dff // tn_up, n_tiles),
            in_specs=[pl.BlockSpec((tile, d), lambda j, i, te: (i, 0)), w_up_spec, w_up_spec],
            out_specs=pl.BlockSpec((tile, tn_up), lambda j, i, te: (i, j)),
            scratch_shapes=[pltpu.VMEM((d, tn_up), BF16), pltpu.VMEM((d, tn_up), BF16)]),
        out_shape=jax.ShapeDtypeStruct((n_rows, dff), BF16),
        compiler_params=_cp("arbitrary", "arbitrary"),
        name="moe_up",
    )(tile_e, xs, w_gate, w_up)
    tn_dn = _pick_tile(d, 4096, LANES)
    ys = pl.pallas_call(
        _moe_down_body,
        grid_spec=pltpu.PrefetchScalarGridSpec(
            num_scalar_prefetch=1,
            grid=(d // tn_dn, n_tiles),
            in_specs=[pl.BlockSpec((tile, dff), lambda j, i, te: (i, 0)),
                      pl.BlockSpec((1, 1, dff, tn_dn), lambda j, i, te: (layer, te[i], 0, j)),
                      pl.BlockSpec((tile, 1), lambda j, i, te: (i, 0))],
            out_specs=pl.BlockSpec((tile, tn_dn), lambda j, i, te: (i, j)),
            scratch_shapes=[pltpu.VMEM((dff, tn_dn), BF16)]),
        out_shape=jax.ShapeDtypeStruct((n_rows, d), BF16),
        compiler_params=_cp("arbitrary", "arbitrary"),
        name="moe_down",
    )(tile_e, hid, w_down, row_w.reshape(n_rows, 1))

    pos2 = pos.reshape(t, MOE_TOPK)
    y0 = ys.at[pos2[:, 0]].get(mode="promise_in_bounds")
    y1 = ys.at[pos2[:, 1]].get(mode="promise_in_bounds")
    tm = _pick_tile(math.gcd(rows_per_batch, ctx_len), 256, 8)
    tpb = rows_per_batch // tm
    if final_g is not None:
        batch = t // rows_per_batch
        ctx_tiles, lat_tiles = ctx_len // tm, (rows_per_batch - ctx_len) // tm
        lat_spec = pl.BlockSpec((tm, d), lambda b, i: (b * tpb + ctx_tiles + i, 0))
        out = pl.pallas_call(
            _moe_final_body,
            grid=(batch, lat_tiles),
            in_specs=[lat_spec, lat_spec, lat_spec, pl.BlockSpec((1, 2, d), lambda b, i: (b, 0, 0)),
                      pl.BlockSpec((1, d), lambda b, i: (0, 0))],
            out_specs=pl.BlockSpec((tm, d), lambda b, i: (b * lat_tiles + i, 0)),
            out_shape=jax.ShapeDtypeStruct((batch * lat_tiles * tm, d), F32),
            compiler_params=_cp("parallel", "parallel"),
            name="moe_combine_final",
        )(x, y0, y1, gate, final_g.reshape(1, d).astype(F32))
        return out.reshape(batch, lat_tiles * tm, d)
    row_spec = pl.BlockSpec((tm, d), lambda i: (i, 0))
    return pl.pallas_call(
        functools.partial(_moe_combine_body, tm=tm, tiles_per_batch=tpb, ctx_len=ctx_len),
        grid=(t // tm,),
        in_specs=[row_spec, row_spec, row_spec, pl.BlockSpec((1, 2, d), lambda i: (i // tpb, 0, 0))],
        out_specs=row_spec,
        out_shape=jax.ShapeDtypeStruct((t, d), F32),
        compiler_params=_cp("parallel"),
        name="moe_combine",
    )(x, y0, y1, gate)


def kernel(x, c, ctx, c_ctx, ada_w, ada_b, norm_mix, norm_ffn, w_in, w_out, diff_lambda, diff_subln,
           s5_a_re, s5_a_im, s5_log_dt, s5_b_re, s5_b_im, s5_c_re, s5_c_im, s5_d, s5_glu_w, s5_glu_b,
           mla_q_norm, mla_kv_norm, mla_w_uq, mla_w_ukv, ret_decay, ret_norm,
           moe_wg, moe_bg, moe_we, moe_be, moe_w_gate, moe_w_up, moe_w_down, final_norm):
    batch, n_lat, d = x.shape
    ctx_len = ctx.shape[1]
    depth = ada_w.shape[0]
    seq = ctx_len + n_lat
    rows = batch * seq
    gw = d // 4
    heads = gw // LANES
    q_rank, kv_rank = 3 * d // 16, d // 16
    ret_qk = heads * RET_K
    n_route = MOE_GROUPS + MOE_GROUPS * MOE_PER_GROUP
    assert heads % 2 == 0 and ctx_len % RET_CHUNK == 0 and n_lat % RET_CHUNK == 0

    splits = (gw, gw, gw, gw, q_rank, kv_rank, ROPE_DIM, ret_qk, ret_qk, gw, gw)
    offs = [0]
    for s_ in splits:
        offs.append(offs[-1] + s_)
    names = ("dq", "dk", "dv", "su", "mcq", "mckv", "mkr", "rq", "rk", "rv", "rg")
    src_col = {n_: (offs[i], offs[i + 1]) for i, n_ in enumerate(names)}
    order = ("dq", "dk", "rq", "rk", "dv", "su", "rv", "rg", "mcq", "mckv")
    col = {}
    pos = 0
    for n_ in order:
        col[n_] = pos
        pos += src_col[n_][1] - src_col[n_][0]
    n_main = pos
    n_rope = col["dv"]
    uq_cols = jnp.arange(heads * (MLA_NOPE + ROPE_DIM)).reshape(heads, MLA_NOPE + ROPE_DIM)
    uq_perm = jnp.concatenate([uq_cols[:, :MLA_NOPE].reshape(-1), uq_cols[:, MLA_NOPE:].reshape(-1)])

    tables = _rope_tables(n_lat, ctx_len)
    log2e = math.log2(math.e)
    rope_scale = jnp.ones((n_rope,), F32).at[col["rk"]:col["rk"] + ret_qk].set(RET_K ** -0.5)
    rope_scale = rope_scale.at[col["dq"]:col["dq"] + gw].set(DIFF_HEAD_DIM ** -0.5 * log2e)
    mla_q_scale = (MLA_NOPE + ROPE_DIM) ** -0.5 * log2e

    cond = jnp.concatenate([c_ctx[None, :], c], axis=0)
    cond = jnp.pad(cond * _sigmoid(cond), ((0, 8 - (batch + 1) % 8 if (batch + 1) % 8 else 0), (0, 0)))

    mod_all = _ada_mod(cond, ada_w, ada_b)

    xa = jnp.concatenate([ctx, x], axis=1).reshape(rows, d)
    tm_big = _pick_tile(seq, 1088)
    tn = lambda n_: _pick_tile(n_, 512, LANES)

    for l in range(depth):
        lam_init = 0.8 - 0.6 * math.exp(-0.3 * l)
        mod = mod_all[l].reshape(cond.shape[0], 6, d)
        mods = [jnp.stack([jnp.broadcast_to(mod[0, i], (batch, d)), mod[1:batch + 1, i]], axis=1)
                for i in range(6)]

        w_main = jnp.concatenate([w_in[l, :, src_col[n_][0]:src_col[n_][1]] for n_ in order],
                                 axis=1).astype(BF16)
        w_kr = jnp.concatenate([w_in[l, :, src_col["mkr"][0]:src_col["mkr"][1]]] * (LANES // ROPE_DIM),
                               axis=1).astype(BF16)

        h = _norm_mod(xa, norm_mix[l], mods[0], mods[1], rows_per_batch=seq, ctx_len=ctx_len)
        proj = _mm([h], w_main, name="in_proj", out_dtype=BF16, tm=tm_big, tn=tn(n_main))
        kr = _mm([h], w_kr, name="in_proj_kr", out_dtype=BF16, tm=tm_big, tn=LANES)
        qk = _rope(proj, n_rope, tables, rope_scale, rows_per_batch=seq)
        krr = _rope(kr, LANES, tables, jnp.ones((LANES,), F32), rows_per_batch=seq)

        lv = diff_lambda[l].astype(F32)
        lam = jnp.exp(jnp.sum(lv[0] * lv[1])) - jnp.exp(jnp.sum(lv[2] * lv[3])) + lam_init
        a_out = _diff_attn(qk, proj, lam, diff_subln[l], batch=batch, seq=seq, ctx_len=ctx_len,
                           heads=heads, q_blk=col["dq"] // LANES, k_blk=col["dk"] // LANES,
                           v_blk=col["dv"] // LANES, post=1.0 - lam_init)

        s5p = _s5n_params(s5_a_re[l], s5_a_im[l], s5_log_dt[l], s5_b_re[l], s5_b_im[l],
                          s5_c_re[l], s5_c_im[l], s5_d[l])
        s_act = _s5n_mix(proj, col["su"], s5p, batch=batch, seq=seq, ctx_len=ctx_len, gw=gw)
        s_out = _mm([s_act], s5_glu_w[l].astype(BF16), name="s5_glu", out_dtype=BF16, tm=tm_big, tn=tn(gw),
                    bias=s5_glu_b[l], glu_in=s_act)

        cq = proj[:, col["mcq"]:col["mcq"] + q_rank]
        ckv = proj[:, col["mckv"]:col["mckv"] + kv_rank]
        w_uq = (mla_w_uq[l][:, uq_perm] * mla_q_scale).astype(BF16)
        q_up = _mm([cq], w_uq, name="mla_q_up", out_dtype=BF16, tm=tm_big, tn=tn(w_uq.shape[1]), norm_g=mla_q_norm[l])
        kv_up = _mm([ckv], mla_w_ukv[l].astype(BF16), name="mla_kv_up", out_dtype=BF16, tm=tm_big,
                    tn=tn(mla_w_ukv.shape[2]), norm_g=mla_kv_norm[l])
        q_rope = _rope(q_up, heads * ROPE_DIM, tables, jnp.ones((heads * ROPE_DIM,), F32),
                       rows_per_batch=seq, col_off=heads * MLA_NOPE)
        m_out = _mla_attn(q_up, q_rope, kv_up, krr, batch=batch, seq=seq, ctx_len=ctx_len, heads=heads)

        log_g = jax.nn.log_sigmoid(ret_decay[l].astype(F32))
        r_out = _retention(qk, proj, log_g, ret_norm[l], batch=batch, seq=seq, ctx_len=ctx_len,
                           heads=heads, q_off=col["rq"], k_off=col["rk"], v_off=col["rv"], g_off=col["rg"])

        xa = _mm([a_out, s_out, m_out, r_out], w_out[l].astype(BF16), name="out_proj", out_dtype=F32, tm=tm_big, tn=tn(d),
                 res=xa, gate=mods[2], rows_per_batch=seq, ctx_len=ctx_len)

        w_r = jnp.concatenate([moe_wg[l], moe_we[l]], axis=1).astype(F32)
        w_r = jnp.pad(w_r, ((0, 0), (0, LANES - n_route)))
        w_r_hi = w_r.astype(BF16)
        w_r_lo = (w_r - w_r_hi.astype(F32)).astype(BF16)
        b_r = jnp.pad(jnp.concatenate([moe_bg[l], moe_be[l]]).astype(F32), (0, LANES - n_route))
        h, logits = _norm_mod(xa, norm_ffn[l], mods[3], mods[4], rows_per_batch=seq, ctx_len=ctx_len,
                              router=(w_r_hi, w_r_lo, b_r.reshape(1, LANES)))
        xa = _moe(h, logits, xa, mods[5], moe_w_gate, moe_w_up, moe_w_down, l,
                  rows_per_batch=seq, ctx_len=ctx_len, final_g=final_norm if l == depth - 1 else None)
    return xa
```

```python
import functools
import math

import jax
import jax.numpy as jnp
from jax import lax
from jax.experimental import pallas as pl
from jax.experimental.pallas import tpu as pltpu

BF16 = jnp.bfloat16
F32 = jnp.float32

V7X_VMEM_BYTES = 64 * 2**20
VMEM_LIMIT = V7X_VMEM_BYTES - 12 * 2**20
LANES = 128

GRID_W = 64
ROPE_DIM = 64
ROPE_BASE = 10000.0
NORM_EPS = 1e-6
DIFF_HEAD_DIM = 64
S5_CH = 16
S5_STATE = 64
MLA_NOPE = 128
MLA_V = 128
RET_K = 64
RET_V = 128
RET_CHUNK = 128
MOE_GROUPS = 4
MOE_PER_GROUP = 4
MOE_TOPK = 2
MOE_TILE = 256


def _cp(*sem):
    return pltpu.CompilerParams(dimension_semantics=sem, vmem_limit_bytes=VMEM_LIMIT)


def _pick_tile(n, target, mult=16):
    best = None
    for t in range(mult, min(n, target) + 1, mult):
        if n % t == 0:
            best = t
    assert best is not None, (n, target)
    return best


def _sigmoid(x):
    return 1.0 / (1.0 + jnp.exp(-x))


def _rotate(x, cos, sa, sb):
    width = x.shape[1]
    reps = width // LANES
    tile = lambda t: jnp.tile(t, (1, reps))
    x_dn = pltpu.roll(x, width - 16, 1)
    x_up = pltpu.roll(x, 16, 1)
    return x * tile(cos) + x_dn * tile(sa) + x_up * tile(sb)


def _mm_body(*refs, nx, ksizes, has_norm, has_bias, epilogue, tm, tiles_per_batch, ctx_len, rope_tiles):
    x_refs = refs[:nx]
    w_ref = refs[nx]
    idx = nx + 1
    g_ref = b_ref = e_ref = res_ref = gate_ref = None
    if has_norm:
        g_ref = refs[idx]; idx += 1
    if has_bias:
        b_ref = refs[idx]; idx += 1
    if epilogue == "glu":
        e_ref = refs[idx]; idx += 1
    if epilogue == "resgate":
        res_ref, gate_ref = refs[idx], refs[idx + 1]; idx += 2
    if epilogue == "rope":
        cos_ref, sa_ref, sb_ref, cs_ref = refs[idx:idx + 4]; idx += 4
    o_ref = refs[idx]

    acc = None
    off = 0
    for xr, ks in zip(x_refs, ksizes):
        x = xr[...]
        if has_norm:
            xf = x.astype(F32)
            xf = xf * lax.rsqrt(jnp.mean(xf * xf, axis=-1, keepdims=True) + NORM_EPS)
            x = xf * g_ref[...]
        x = x.astype(BF16)
        w = w_ref[off:off + ks, :].astype(BF16)
        part = jnp.dot(x, w, preferred_element_type=F32)
        acc = part if acc is None else acc + part
        off += ks
    if has_bias:
        acc = acc + b_ref[...]
    if epilogue == "glu":
        acc = e_ref[...].astype(F32) * _sigmoid(acc)
    elif epilogue == "resgate":
        i = pl.program_id(0)
        row = (i % tiles_per_batch) * tm + lax.broadcasted_iota(jnp.int32, (tm, 1), 0)
        gate = jnp.where(row < ctx_len, gate_ref[0, 0:1, :], gate_ref[0, 1:2, :])
        acc = res_ref[...] + gate * acc
    if epilogue == "rope":
        j = pl.program_id(1)
        roped = (j >= rope_tiles[0]) & (j < rope_tiles[1])

        @pl.when(roped)
        def _():
            y = _rotate(acc, cos_ref[...], sa_ref[...], sb_ref[...]) * cs_ref[...]
            o_ref[...] = y.astype(o_ref.dtype)

        @pl.when(jnp.logical_not(roped))
        def _():
            o_ref[...] = acc.astype(o_ref.dtype)
    else:
        o_ref[...] = acc.astype(o_ref.dtype)


def _mm(xs, w, *, name, out_dtype, tm, tn, norm_g=None, bias=None, glu_in=None, res=None, gate=None,
        rope=None, rows_per_batch=None, ctx_len=0):
    m = xs[0].shape[0]
    ksizes = tuple(x.shape[1] for x in xs)
    k, n = w.shape
    assert sum(ksizes) == k and m % tm == 0 and n % tn == 0
    epilogue = ("glu" if glu_in is not None else "resgate" if res is not None
                else "rope" if rope is not None else None)
    tiles_per_batch = (rows_per_batch // tm) if rows_per_batch else 1
    rope_tiles = None
    in_specs = [pl.BlockSpec((tm, ks), lambda i, j: (i, 0)) for ks in ksizes]
    in_specs.append(pl.BlockSpec((k, tn), lambda i, j: (0, j)))
    args = list(xs) + [w]
    if norm_g is not None:
        in_specs.append(pl.BlockSpec((1, k), lambda i, j: (0, 0)))
        args.append(norm_g.reshape(1, k).astype(F32))
    if bias is not None:
        in_specs.append(pl.BlockSpec((1, tn), lambda i, j: (0, j)))
        args.append(bias.reshape(1, n).astype(F32))
    if epilogue == "glu":
        in_specs.append(pl.BlockSpec((tm, tn), lambda i, j: (i, j)))
        args.append(glu_in)
    if epilogue == "resgate":
        tpb = tiles_per_batch
        in_specs.append(pl.BlockSpec((tm, tn), lambda i, j: (i, j)))
        in_specs.append(pl.BlockSpec((1, 2, tn), lambda i, j: (i // tpb, 0, j)))
        args += [res, gate]
    if epilogue == "rope":
        tables, col_scale, (lo, hi) = rope
        assert lo % tn == 0 and hi % tn == 0
        rope_tiles = (lo // tn, hi // tn)
        tpb = tiles_per_batch
        in_specs += [pl.BlockSpec((tm, LANES), lambda i, j: (i % tpb, 0))] * 3
        in_specs.append(pl.BlockSpec((1, tn), lambda i, j: (0, j)))
        args += list(tables) + [col_scale.reshape(1, n).astype(F32)]
    body = functools.partial(_mm_body, nx=len(xs), ksizes=ksizes, has_norm=norm_g is not None,
                             has_bias=bias is not None, epilogue=epilogue, tm=tm,
                             tiles_per_batch=tiles_per_batch, ctx_len=ctx_len, rope_tiles=rope_tiles)
    return pl.pallas_call(
        body,
        grid=(m // tm, n // tn),
        in_specs=in_specs,
        out_specs=pl.BlockSpec((tm, tn), lambda i, j: (i, j)),
        out_shape=jax.ShapeDtypeStruct((m, n), out_dtype),
        compiler_params=_cp("parallel", "arbitrary"),
        name=name,
    )(*args)


def _ada_body(c_ref, w_ref, b_ref, o_ref):
    acc = jnp.dot(c_ref[...].astype(BF16), w_ref[0].astype(BF16), preferred_element_type=F32)
    o_ref[0] = acc + b_ref[0]


def _ada_mod(cond, ada_w, ada_b):
    depth, d, n6 = ada_w.shape
    rows = cond.shape[0]
    tn = _pick_tile(n6, 512, LANES)
    return pl.pallas_call(
        _ada_body,
        grid=(depth, n6 // tn),
        in_specs=[pl.BlockSpec((rows, d), lambda l, j: (0, 0)),
                  pl.BlockSpec((1, d, tn), lambda l, j: (l, 0, j)),
                  pl.BlockSpec((1, 1, tn), lambda l, j: (l, 0, j))],
        out_specs=pl.BlockSpec((1, rows, tn), lambda l, j: (l, 0, j)),
        out_shape=jax.ShapeDtypeStruct((depth, rows, n6), F32),
        compiler_params=_cp("parallel", "arbitrary"),
        name="ada_mod",
    )(cond, ada_w, ada_b.reshape(depth, 1, n6).astype(F32))


def _norm_mod_body(*refs, tm, tiles_per_batch, ctx_len, router):
    if router:
        x_ref, g_ref, sh_ref, sc_ref, whi_ref, wlo_ref, br_ref, h_ref, lg_ref = refs
    else:
        x_ref, g_ref, sh_ref, sc_ref, h_ref = refs
    i = pl.program_id(0)
    x = x_ref[...]
    y = x * lax.rsqrt(jnp.mean(x * x, axis=-1, keepdims=True) + NORM_EPS) * g_ref[...]
    is_ctx = (i % tiles_per_batch) * tm < ctx_len
    sh = jnp.where(is_ctx, sh_ref[0, 0:1, :], sh_ref[0, 1:2, :])
    sc = jnp.where(is_ctx, sc_ref[0, 0:1, :], sc_ref[0, 1:2, :])
    h = y * (1.0 + sc) + sh
    h_ref[...] = h.astype(BF16)
    if router:
        hi = h.astype(BF16)
        lo = (h - hi.astype(F32)).astype(BF16)
        lg = jnp.dot(hi, whi_ref[...], preferred_element_type=F32)
        lg = lg + jnp.dot(hi, wlo_ref[...], preferred_element_type=F32)
        lg = lg + jnp.dot(lo, whi_ref[...], preferred_element_type=F32)
        lg_ref[...] = lg + br_ref[...]


def _norm_mod(x, g, shift, scale, *, rows_per_batch, ctx_len, router=None):
    m, d = x.shape
    tm = _pick_tile(math.gcd(rows_per_batch, ctx_len), 256, 8)
    tpb = rows_per_batch // tm
    in_specs = [
        pl.BlockSpec((tm, d), lambda i: (i, 0)),
        pl.BlockSpec((1, d), lambda i: (0, 0)),
        pl.BlockSpec((1, 2, d), lambda i: (i // tpb, 0, 0)),
        pl.BlockSpec((1, 2, d), lambda i: (i // tpb, 0, 0)),
    ]
    args = [x, g.reshape(1, d), shift, scale]
    out_specs = [pl.BlockSpec((tm, d), lambda i: (i, 0))]
    out_shape = [jax.ShapeDtypeStruct((m, d), BF16)]
    if router is not None:
        whi, wlo, br = router
        in_specs += [pl.BlockSpec((d, LANES), lambda i: (0, 0)),
                     pl.BlockSpec((d, LANES), lambda i: (0, 0)),
                     pl.BlockSpec((1, LANES), lambda i: (0, 0))]
        args += [whi, wlo, br]
        out_specs.append(pl.BlockSpec((tm, LANES), lambda i: (i, 0)))
        out_shape.append(jax.ShapeDtypeStruct((m, LANES), F32))
    body = functools.partial(_norm_mod_body, tm=tm, tiles_per_batch=tpb, ctx_len=ctx_len,
                             router=router is not None)
    outs = pl.pallas_call(body, grid=(m // tm,), in_specs=in_specs, out_specs=out_specs,
                          out_shape=out_shape, compiler_params=_cp("parallel"),
                          name="norm_mod_router" if router is not None else "norm_mod")(*args)
    return outs if router is not None else outs[0]


def _rope_tables(n_lat, ctx_len):
    rows = n_lat // GRID_W
    row = jnp.repeat(jnp.arange(rows, dtype=F32), GRID_W)
    col = jnp.tile(jnp.arange(GRID_W, dtype=F32), rows)
    quarter = ROPE_DIM // 4
    inv = ROPE_BASE ** (-jnp.arange(quarter, dtype=F32) / quarter)
    ar = row[:, None] * inv
    ac = col[:, None] * inv
    ang = jnp.concatenate([ar, ar, ac, ac], axis=-1)
    ang = jnp.concatenate([jnp.zeros((ctx_len, ROPE_DIM), F32), ang], axis=0)
    ang = jnp.tile(ang, (1, LANES // ROPE_DIM))
    cos, sin = jnp.cos(ang), jnp.sin(ang)
    lane = jnp.arange(LANES)
    even = ((lane // quarter) % 2 == 0)[None, :]
    sa = jnp.where(even, -sin, 0.0)
    sb = jnp.where(even, 0.0, sin)
    return cos, sa, sb


ATTN_ALIGN = 256


def _softmax_pv(q, k_ref, va_ref, bounds):
    ms, ovs = [], []
    for lo, hi in bounds:
        s = lax.dot_general(q, k_ref[lo:hi, :], (((1,), (1,)), ((), ())), preferred_element_type=F32)
        m = jnp.max(s, axis=-1, keepdims=True)
        e = jnp.exp2(s - m).astype(BF16)
        ovs.append(jnp.dot(e, va_ref[lo:hi, :], preferred_element_type=F32))
        ms.append(m)
    m_all = functools.reduce(jnp.maximum, ms)
    acc = sum(ov * jnp.exp2(m - m_all) for m, ov in zip(ms, ovs))
    return acc[:, :LANES] / acc[:, LANES:]


def _key_chunks(n_keys):
    if n_keys < 2 * ATTN_ALIGN:
        return ((0, n_keys),)
    half = (n_keys // ATTN_ALIGN + 1) // 2 * ATTN_ALIGN
    return ((0, half), (half, n_keys))


def _lat_tile(seq, ctx_len):
    n_lat = seq - ctx_len
    assert ctx_len % ATTN_ALIGN == 0 and n_lat % ATTN_ALIGN == 0
    return _pick_tile(n_lat, 512, ATTN_ALIGN)


def _lat_rows(seq, ctx_len, tq):
    return lambda b, i: pl.multiple_of(b * seq + ctx_len + i * tq, ATTN_ALIGN)


def _diff_attn_body(lam_ref, q_ref, k_ref, v_ref, sub_ref, *rest, post, bounds):
    o_ref, va_ref = rest[-2:]

    def fill():
        va_ref[:, :LANES] = v_ref[...]
        va_ref[:, LANES:] = jnp.ones(v_ref.shape, BF16)

    if len(rest) == 2:
        pl.when(pl.program_id(2) == 0)(fill)
    else:
        fill()

    q = q_ref[...]
    lane = lax.broadcasted_iota(jnp.int32, (1, LANES), 1)
    first = lane < DIFF_HEAD_DIM
    zero = jnp.zeros_like(q)
    q0 = jnp.where(first, q, zero)
    q1 = jnp.where(first, zero, q)
    o = _softmax_pv(q0, k_ref, va_ref, bounds) - lam_ref[0] * _softmax_pv(q1, k_ref, va_ref, bounds)
    o = o * lax.rsqrt(jnp.mean(o * o, axis=-1, keepdims=True) + NORM_EPS) * sub_ref[...] * post
    o_ref[...] = o.astype(o_ref.dtype)


def _diff_attn(qk, proj, lam, subln, *, batch, seq, ctx_len, heads, q_blk, k_blk, v_blk, post):
    tq = _lat_tile(seq, ctx_len)
    rows = _lat_rows(seq, ctx_len, tq)
    smem = pl.BlockSpec(memory_space=pltpu.SMEM)
    args = (lam.reshape(1).astype(F32), qk, qk, proj, subln.reshape(1, LANES).astype(F32))
    out_shape = jax.ShapeDtypeStruct((batch * seq, heads * LANES), BF16)
    elem = (pl.Element(tq), pl.Element(LANES))
    lat = pl.pallas_call(
        functools.partial(_diff_attn_body, post=post, bounds=_key_chunks(seq)),
        grid=(batch, heads, (seq - ctx_len) // tq),
        in_specs=[
            smem,
            pl.BlockSpec(elem, lambda b, h, i: (rows(b, i), pl.multiple_of((q_blk + h) * LANES, LANES))),
            pl.BlockSpec((seq, LANES), lambda b, h, i: (b, k_blk + h)),
            pl.BlockSpec((seq, LANES), lambda b, h, i: (b, v_blk + h)),
            pl.BlockSpec((1, LANES), lambda b, h, i: (0, 0)),
        ],
        out_specs=pl.BlockSpec(elem, lambda b, h, i: (rows(b, i), pl.multiple_of(h * LANES, LANES))),
        out_shape=out_shape,
        scratch_shapes=[pltpu.VMEM((seq, 2 * LANES), BF16)],
        compiler_params=_cp("parallel", "parallel", "arbitrary"),
        name="diff_attn",
    )(*args)
    cpb = seq // ctx_len
    ctx_spec = lambda blk: pl.BlockSpec((ctx_len, LANES), lambda b, h: (b * cpb, blk + h))
    return pl.pallas_call(
        functools.partial(_diff_attn_body, post=post, bounds=_key_chunks(ctx_len)),
        grid=(batch, heads),
        in_specs=[smem, ctx_spec(q_blk), ctx_spec(k_blk), ctx_spec(v_blk),
                  pl.BlockSpec((1, LANES), lambda b, h: (0, 0)), pl.BlockSpec(memory_space=pl.ANY)],
        out_specs=ctx_spec(0),
        out_shape=out_shape,
        scratch_shapes=[pltpu.VMEM((ctx_len, 2 * LANES), BF16)],
        input_output_aliases={5: 0},
        compiler_params=_cp("parallel", "parallel"),
        name="diff_attn_ctx",
    )(*args, lat)


def _mla_attn_body(qn_ref, qr_ref, kn_ref, kr_ref, v_ref, *rest, bounds):
    o_ref, ka_ref, va_ref = rest[-3:]
    h = pl.program_id(1)

    def fill():
        ka_ref[:, :LANES] = kn_ref[...]
        ka_ref[:, LANES:] = kr_ref[...]
        va_ref[:, :LANES] = v_ref[...]
        va_ref[:, LANES:] = jnp.ones(v_ref.shape, BF16)

    if len(rest) == 3:
        pl.when(pl.program_id(2) == 0)(fill)
    else:
        fill()

    qr = qr_ref[...]
    lane = lax.broadcasted_iota(jnp.int32, (1, LANES), 1)
    mine = (lane < ROPE_DIM) == (h % 2 == 0)
    qr = jnp.where(mine, qr, jnp.zeros_like(qr))
    q = jnp.concatenate([qn_ref[...], qr], axis=1)
    o_ref[...] = _softmax_pv(q, ka_ref, va_ref, bounds).astype(o_ref.dtype)


def _mla_attn(q_up, kv_up, k_rope, *, batch, seq, ctx_len, heads):
    tq = _lat_tile(seq, ctx_len)
    rows = _lat_rows(seq, ctx_len, tq)
    args = (q_up, q_up, kv_up, k_rope, kv_up)
    out_shape = jax.ShapeDtypeStruct((batch * seq, heads * LANES), BF16)
    elem = (pl.Element(tq), pl.Element(LANES))
    lat = pl.pallas_call(
        functools.partial(_mla_attn_body, bounds=_key_chunks(seq)),
        grid=(batch, heads, (seq - ctx_len) // tq),
        in_specs=[
            pl.BlockSpec(elem, lambda b, h, i: (rows(b, i), pl.multiple_of(h * LANES, LANES))),
            pl.BlockSpec(elem, lambda b, h, i: (rows(b, i), pl.multiple_of((heads + h // 2) * LANES, LANES))),
            pl.BlockSpec((seq, LANES), lambda b, h, i: (b, 2 * h)),
            pl.BlockSpec((seq, LANES), lambda b, h, i: (b, 0)),
            pl.BlockSpec((seq, LANES), lambda b, h, i: (b, 2 * h + 1)),
        ],
        out_specs=pl.BlockSpec(elem, lambda b, h, i: (rows(b, i), pl.multiple_of(h * LANES, LANES))),
        out_shape=out_shape,
        scratch_shapes=[pltpu.VMEM((seq, 2 * LANES), BF16), pltpu.VMEM((seq, 2 * LANES), BF16)],
        compiler_params=_cp("parallel", "parallel", "arbitrary"),
        name="mla_attn",
    )(*args)
    cpb = seq // ctx_len
    ctx_spec = lambda col: pl.BlockSpec((ctx_len, LANES), lambda b, h: (b * cpb, col(h)))
    return pl.pallas_call(
        functools.partial(_mla_attn_body, bounds=_key_chunks(ctx_len)),
        grid=(batch, heads),
        in_specs=[ctx_spec(lambda h: h), ctx_spec(lambda h: heads + h // 2), ctx_spec(lambda h: 2 * h),
                  ctx_spec(lambda h: 0), ctx_spec(lambda h: 2 * h + 1), pl.BlockSpec(memory_space=pl.ANY)],
        out_specs=ctx_spec(lambda h: h),
        out_shape=out_shape,
        scratch_shapes=[pltpu.VMEM((ctx_len, 2 * LANES), BF16), pltpu.VMEM((ctx_len, 2 * LANES), BF16)],
        input_output_aliases={5: 0},
        compiler_params=_cp("parallel", "parallel"),
        name="mla_attn_ctx",
    )(*args, lat)


def _retention_body(*refs, backward, cs, heads):
    if backward:
        lg_ref, q_ref, k_ref, v_ref, yf_ref, gate_ref, ng_ref, o_ref, s_ref, d_ref, qd_ref, kd_ref = refs
    else:
        lg_ref, q_ref, k_ref, v_ref, o_ref, s_ref, d_ref, qd_ref, kd_ref = refs
    t = pl.program_id(1)

    @pl.when(t == 0)
    def _():
        s_ref[...] = jnp.zeros_like(s_ref)
        pos_r = lax.broadcasted_iota(jnp.int32, (cs, cs), 0).astype(F32)
        pos_c = lax.broadcasted_iota(jnp.int32, (cs, cs), 1).astype(F32)
        pos = lax.broadcasted_iota(jnp.int32, (cs, 1), 0).astype(F32)
        for h in range(heads):
            lg = lg_ref[h]
            if backward:
                diff = pos_c - pos_r
                keep = diff > 0
                qd_ref[h] = jnp.exp(lg * (cs - pos))
                kd_ref[h] = jnp.exp(lg * pos)
            else:
                diff = pos_r - pos_c
                keep = diff >= 0
                qd_ref[h] = jnp.exp(lg * (pos + 1.0))
                kd_ref[h] = jnp.exp(lg * (cs - 1.0 - pos))
            d_ref[h] = jnp.where(keep, jnp.exp(lg * jnp.maximum(diff, 0.0)), 0.0)

    lane = lax.broadcasted_iota(jnp.int32, (1, LANES), 1)
    nt = (((1,), (1,)), ((), ()))
    tn = (((0,), (0,)), ((), ()))
    for h in range(heads):
        blk = slice((h // 2) * LANES, (h // 2 + 1) * LANES)
        col = slice(h * RET_V, (h + 1) * RET_V)
        mine = (lane < RET_K) == (h % 2 == 0)
        q = q_ref[:, blk]
        q = jnp.where(mine, q, jnp.zeros_like(q))
        k = k_ref[:, blk]
        v = v_ref[:, col]
        scores = lax.dot_general(q, k, nt, preferred_element_type=F32) * d_ref[h]
        intra = jnp.dot(scores.astype(BF16), v, preferred_element_type=F32)
        state = s_ref[h]
        q_w = (q.astype(F32) * qd_ref[h]).astype(BF16)
        cross = jnp.dot(q_w, state.astype(BF16), preferred_element_type=F32)
        k_w = (k.astype(F32) * kd_ref[h]).astype(BF16)
        upd = lax.dot_general(k_w, v, tn, preferred_element_type=F32)
        s_ref[h] = jnp.exp(lg_ref[h] * cs) * state + upd
        y = intra + cross
        if backward:
            y = y + yf_ref[:, col]
            y = y * lax.rsqrt(jnp.mean(y * y, axis=-1, keepdims=True) + NORM_EPS) * ng_ref[:, col]
            g = gate_ref[:, col].astype(F32)
            o_ref[:, col] = (g * _sigmoid(g) * y).astype(o_ref.dtype)
        else:
            o_ref[:, col] = y


def _retention(qk, proj, log_g, norm_g, *, batch, seq, ctx_len, heads, q_off, k_off, v_off, g_off):
    cs = RET_CHUNK
    nc, nc_ctx = seq // cs, ctx_len // cs
    wqk, wv = heads * RET_K, heads * RET_V
    assert q_off % wqk == 0 and k_off % wqk == 0 and v_off % wv == 0 and g_off % wv == 0
    smem = pl.BlockSpec(memory_space=pltpu.SMEM)

    def fwd_chunk(t):
        return t

    def bwd_chunk(t):
        return jnp.where(t < nc_ctx, nc_ctx - 1 - t, nc - 1 - (t - nc_ctx))

    def specs(chunk):
        row = lambda b, t: b * nc + chunk(t)
        return (pl.BlockSpec((cs, wqk), lambda b, t: (row(b, t), q_off // wqk)),
                pl.BlockSpec((cs, wqk), lambda b, t: (row(b, t), k_off // wqk)),
                pl.BlockSpec((cs, wv), lambda b, t: (row(b, t), v_off // wv)),
                pl.BlockSpec((cs, wv), lambda b, t: (row(b, t), 0)),
                pl.BlockSpec((cs, wv), lambda b, t: (row(b, t), g_off // wv)))

    scratch = [pltpu.VMEM((heads, LANES, RET_V), F32), pltpu.VMEM((heads, cs, cs), F32),
               pltpu.VMEM((heads, cs, 1), F32), pltpu.VMEM((heads, cs, 1), F32)]
    qs, ks, vs, ys, gs = specs(fwd_chunk)
    y_f = pl.pallas_call(
        functools.partial(_retention_body, backward=False, cs=cs, heads=heads),
        grid=(batch, nc),
        in_specs=[smem, qs, ks, vs],
        out_specs=ys,
        out_shape=jax.ShapeDtypeStruct((batch * seq, wv), F32),
        scratch_shapes=scratch,
        compiler_params=_cp("parallel", "arbitrary"),
        name="retention_fwd",
    )(log_g[0].astype(F32), qk, qk, proj)
    qs, ks, vs, ys, gs = specs(bwd_chunk)
    return pl.pallas_call(
        functools.partial(_retention_body, backward=True, cs=cs, heads=heads),
        grid=(batch, nc),
        in_specs=[smem, qs, ks, vs, ys, gs, pl.BlockSpec((1, wv), lambda b, t: (0, 0))],
        out_specs=ys,
        out_shape=jax.ShapeDtypeStruct((batch * seq, wv), BF16),
        scratch_shapes=scratch,
        compiler_params=_cp("parallel", "arbitrary"),
        name="retention_bwd",
    )(log_g[1].astype(F32), qk, qk, proj, y_f, proj, norm_g.reshape(1, wv).astype(F32))


S5N_CHUNK = 8
S5N_GROUPS = LANES // S5_CH
S5N_HALF = S5N_GROUPS * S5_STATE


def _s5n_params(a_re, a_im, log_dt, b_re, b_im, c_re, c_im, d_skip):
    L, ch, gl = S5N_CHUNK, S5_CH, S5N_GROUPS
    a_re, a_im = a_re.astype(F32), a_im.astype(F32)
    groups = a_re.shape[1]
    nb = groups // gl
    dt = jnp.exp(log_dt.astype(F32))[..., None]
    e = jnp.arange(L + 1, dtype=F32)[:, None, None, None]
    mag = jnp.exp(a_re * dt * e)
    pw_re, pw_im = mag * jnp.cos(a_im * dt * e), mag * jnp.sin(a_im * dt * e)
    ab_re, ab_im = pw_re[1], pw_im[1]
    den = a_re * a_re + a_im * a_im
    f_re = ((ab_re - 1.0) * a_re + ab_im * a_im) / den
    f_im = (ab_im * a_re - (ab_re - 1.0) * a_im) / den
    bb_re = f_re[..., None] * b_re - f_im[..., None] * b_im
    bb_im = f_re[..., None] * b_im + f_im[..., None] * b_re
    c_re, c_im = c_re.astype(F32), c_im.astype(F32)
    hp = lax.Precision.HIGHEST
    idx = jnp.arange(L)

    cp_re = c_re[None] * pw_re[:L, :, :, None, :] - c_im[None] * pw_im[:L, :, :, None, :]
    cp_im = c_re[None] * pw_im[:L, :, :, None, :] + c_im[None] * pw_re[:L, :, :, None, :]
    kmat = (jnp.einsum("ldgcp,dgpk->dglck", cp_re, bb_re, precision=hp)
            - jnp.einsum("ldgcp,dgpk->dglck", cp_im, bb_im, precision=hp))
    lag = idx[None, :] - idx[:, None]

    def toeplitz(k, lg):
        return jnp.where((lg >= 0)[None, :, :, None, None], k[:, jnp.clip(lg, 0, L - 1)], 0.0)

    t = toeplitz(kmat[0], lag) + toeplitz(kmat[1], -lag)
    skip = (idx[:, None] == idx[None, :])[None, :, :, None, None] * (
        jnp.eye(ch, dtype=F32)[None, None, None] * d_skip.astype(F32)[:, None, None, :, None])
    t = (t + skip).reshape(nb, gl, L, L, ch, ch)
    m_mat = t.transpose(0, 2, 1, 5, 3, 4).reshape(nb, L, LANES, L * ch)

    def state_in(d, exps):
        p_re, p_im = pw_re[exps, d], pw_im[exps, d]
        w_re = p_re[..., None] * bb_re[d][None] - p_im[..., None] * bb_im[d][None]
        w_im = p_re[..., None] * bb_im[d][None] + p_im[..., None] * bb_re[d][None]
        return jnp.stack([w_re, w_im], axis=0).transpose(2, 1, 4, 0, 3)

    w_full = jnp.stack([state_in(0, L - 1 - idx), state_in(1, idx)], axis=3)
    w_full = w_full.reshape(nb, gl, L, ch, 2, 2, S5_STATE)
    w_mat = w_full.transpose(0, 2, 1, 3, 4, 5, 6).reshape(nb, L, LANES, 4 * S5_STATE)

    def state_out(d, exps):
        p_re, p_im = pw_re[exps, d], pw_im[exps, d]
        v_re = c_re[d][None] * p_re[:, :, None, :] - c_im[d][None] * p_im[:, :, None, :]
        v_im = c_re[d][None] * p_im[:, :, None, :] + c_im[d][None] * p_re[:, :, None, :]
        return jnp.stack([v_re, -v_im], axis=0).transpose(2, 0, 4, 1, 3)

    v_full = jnp.stack([state_out(0, idx + 1), state_out(1, L - idx)], axis=1)
    v_full = v_full.reshape(nb, gl, 2, 2, S5_STATE, L, ch)
    v_mat = v_full.transpose(0, 2, 3, 1, 4, 5, 6).reshape(nb, 4 * S5N_HALF, L * ch)

    a_pow = jnp.stack([pw_re[L], pw_im[L]], axis=1).reshape(2, 2, nb * S5N_HALF)
    return m_mat.astype(BF16), w_mat.astype(BF16), v_mat.astype(BF16), a_pow


def _s5n_spreaders():
    L, ch, gl, st = S5N_CHUNK, S5_CH, S5N_GROUPS, S5_STATE
    e_out = jnp.einsum("ij,cd->icjd", jnp.eye(L), jnp.eye(ch))
    e_out = jnp.broadcast_to(e_out[:, :, :, None, :], (L, ch, L, gl, ch)).reshape(L * ch, L * LANES)
    e_st = jnp.broadcast_to(jnp.eye(4 * st).reshape(4 * st, 4, 1, st), (4 * st, 4, gl, st))
    return e_out.astype(BF16), e_st.reshape(4 * st, 4 * S5N_HALF).astype(BF16)


def _s5n_expand(compact, spread, row_group, col_group):
    full = jnp.dot(compact, spread, preferred_element_type=F32)
    rows = lax.broadcasted_iota(jnp.int32, full.shape, 0)
    cols = lax.broadcasted_iota(jnp.int32, full.shape, 1)
    return jnp.where(row_group(rows) == col_group(cols), full, 0.0).astype(BF16)


_S5N_IN_ROW_GROUP = lambda r: r // S5_CH
_S5N_STATE_GROUP = lambda c: (c % S5N_HALF) // S5_STATE
_S5N_OUT_COL_GROUP = lambda c: (c % LANES) // S5_CH


def _s5n_state_in_body(u_ref, w_ref, e_ref, o_ref, w_exp):
    @pl.when(pl.program_id(1) == 0)
    def _():
        for j in range(S5N_CHUNK):
            w_exp[j * LANES:(j + 1) * LANES, :] = _s5n_expand(w_ref[0, j], e_ref[...],
                                                              _S5N_IN_ROW_GROUP, _S5N_STATE_GROUP)

    u_cat = jnp.concatenate([u_ref[:, j, :] for j in range(S5N_CHUNK)], axis=1)
    acc = jnp.dot(u_cat, w_exp[...], preferred_element_type=F32)
    for d in range(2):
        for r in range(2):
            lo = (2 * d + r) * S5N_HALF
            o_ref[d, r] = acc[:, lo:lo + S5N_HALF]


def _s5n_scan_body(sr_ref, si_ref, a_ref, o_ref, *, nk, nk_ctx):
    d = pl.program_id(0)
    ar, ai = a_ref[0, 0], a_ref[0, 1]

    def step(k, carry):
        hr, hi = carry
        o_ref[0, 0, pl.ds(k, 1), :] = hr
        o_ref[0, 1, pl.ds(k, 1), :] = hi
        sr = sr_ref[0, 0, pl.ds(k, 1), :]
        si = si_ref[0, 0, pl.ds(k, 1), :]
        return ar * hr - ai * hi + sr, ar * hi + ai * hr + si

    zero = jnp.zeros_like(ar)
    unroll = 8 if (nk % 8 == 0 and nk_ctx % 8 == 0) else 1

    @pl.when(d == 0)
    def _():
        lax.fori_loop(0, nk, step, (zero, zero), unroll=unroll)

    @pl.when(d == 1)
    def _():
        hc = lax.fori_loop(0, nk_ctx, lambda t, c: step(nk_ctx - 1 - t, c), (zero, zero), unroll=unroll)
        lax.fori_loop(0, nk - nk_ctx, lambda t, c: step(nk - 1 - t, c), hc, unroll=unroll)


def _s5n_out_body(u_ref, m_ref, h_ref, v_ref, e_ref, o_ref, m_exp, v_exp):
    @pl.when(pl.program_id(1) == 0)
    def _():
        for j in range(S5N_CHUNK):
            m_exp[j * LANES:(j + 1) * LANES, :] = _s5n_expand(m_ref[0, j], e_ref[...],
                                                              _S5N_IN_ROW_GROUP, _S5N_OUT_COL_GROUP)
        v_exp[...] = _s5n_expand(v_ref[0], e_ref[...], _S5N_STATE_GROUP, _S5N_OUT_COL_GROUP)

    u_cat = jnp.concatenate([u_ref[:, j, :] for j in range(S5N_CHUNK)], axis=1)
    h_cat = jnp.concatenate([h_ref[d, r].astype(BF16) for d in range(2) for r in range(2)], axis=1)
    acc = (jnp.dot(u_cat, m_exp[...], preferred_element_type=F32)
           + jnp.dot(h_cat, v_exp[...], preferred_element_type=F32))
    y = jax.nn.gelu(acc).astype(o_ref.dtype)
    for i in range(S5N_CHUNK):
        o_ref[:, i, :] = y[:, i * LANES:(i + 1) * LANES]


def _s5n_mix(proj, su_off, params, *, batch, seq, ctx_len, gw):
    m_mat, w_mat, v_mat, a_pow = params
    L, half = S5N_CHUNK, S5N_HALF
    nb = gw // LANES
    nk, nk_ctx = seq // L, ctx_len // L
    assert su_off % LANES == 0 and seq % L == 0 and ctx_len % L == 0
    ub = su_off // LANES
    u3 = proj.reshape(batch * nk, L, proj.shape[1])
    lanes = batch * nb * half
    u_spec = pl.BlockSpec((nk, L, LANES), lambda n, b: (b, 0, ub + n))
    h_spec = pl.BlockSpec((2, 2, nk, half), lambda n, b: (0, 0, 0, b * nb + n))

    e_out, e_st = _s5n_spreaders()
    whole = lambda a: pl.BlockSpec(a.shape, lambda n, b: (0,) * a.ndim)
    s_in = pl.pallas_call(
        _s5n_state_in_body,
        grid=(nb, batch),
        in_specs=[u_spec, pl.BlockSpec((1,) + w_mat.shape[1:], lambda n, b: (n, 0, 0, 0)), whole(e_st)],
        out_specs=h_spec,
        out_shape=jax.ShapeDtypeStruct((2, 2, nk, lanes), F32),
        scratch_shapes=[pltpu.VMEM((L * LANES, 4 * half), BF16)],
        compiler_params=_cp("parallel", "arbitrary"),
        name="s5_state_in",
    )(u3, w_mat, e_st)

    wl = _pick_tile(lanes, 2048, LANES)
    part_spec = lambda r: pl.BlockSpec((1, 1, nk, wl), lambda d, w: (d, r, 0, w))
    a_lanes = jnp.tile(a_pow.reshape(2, 2, 1, nb * half), (1, 1, 1, batch))
    h_prev = pl.pallas_call(
        functools.partial(_s5n_scan_body, nk=nk, nk_ctx=nk_ctx),
        grid=(2, lanes // wl),
        in_specs=[part_spec(0), part_spec(1), pl.BlockSpec((1, 2, 1, wl), lambda d, w: (d, 0, 0, w))],
        out_specs=pl.BlockSpec((1, 2, nk, wl), lambda d, w: (d, 0, 0, w)),
        out_shape=jax.ShapeDtypeStruct((2, 2, nk, lanes), F32),
        compiler_params=_cp("parallel", "parallel"),
        name="s5_scan",
    )(s_in, s_in, a_lanes)

    y3 = pl.pallas_call(
        _s5n_out_body,
        grid=(nb, batch),
        in_specs=[u_spec,
                  pl.BlockSpec((1,) + m_mat.shape[1:], lambda n, b: (n, 0, 0, 0)),
                  h_spec,
                  pl.BlockSpec((1,) + v_mat.shape[1:], lambda n, b: (n, 0, 0)),
                  whole(e_out)],
        out_specs=pl.BlockSpec((nk, L, LANES), lambda n, b: (b, 0, n)),
        out_shape=jax.ShapeDtypeStruct((batch * nk, L, gw), BF16),
        scratch_shapes=[pltpu.VMEM((L * LANES, L * LANES), BF16), pltpu.VMEM((4 * half, L * LANES), BF16)],
        compiler_params=_cp("parallel", "arbitrary"),
        name="s5_out",
    )(u3, m_mat, h_prev, v_mat, e_out)
    return y3.reshape(batch * seq, gw)


def _expert_changed(te_ref):
    i = pl.program_id(1)
    return (i == 0) | (te_ref[i] != te_ref[jnp.maximum(i - 1, 0)])


def _moe_up_body(te_ref, x_ref, wg_ref, wu_ref, o_ref, wg_bf, wu_bf):
    @pl.when(_expert_changed(te_ref))
    def _():
        wg_bf[...] = wg_ref[0, 0].astype(BF16)
        wu_bf[...] = wu_ref[0, 0].astype(BF16)

    x = x_ref[...]
    a = jnp.dot(x, wg_bf[...], preferred_element_type=F32)
    b = jnp.dot(x, wu_bf[...], preferred_element_type=F32)
    o_ref[...] = (a * _sigmoid(a) * b).astype(o_ref.dtype)


def _moe_down_body(te_ref, h_ref, w_ref, rw_ref, o_ref, w_bf):
    @pl.when(_expert_changed(te_ref))
    def _():
        w_bf[...] = w_ref[0, 0].astype(BF16)

    y = jnp.dot(h_ref[...], w_bf[...], preferred_element_type=F32)
    o_ref[...] = (rw_ref[...] * y).astype(o_ref.dtype)


def _moe_down_into_body(te_ref, h_ref, w_ref, rw_ref, prev_ref, o_ref, w_bf):
    _moe_down_body(te_ref, h_ref, w_ref, rw_ref, o_ref, w_bf)


def _moe_combine_body(x_ref, y0_ref, y1_ref, gate_ref, o_ref, *, tm, tiles_per_batch, ctx_len):
    i = pl.program_id(0)
    is_ctx = (i % tiles_per_batch) * tm < ctx_len
    gate = jnp.where(is_ctx, gate_ref[0, 0:1, :], gate_ref[0, 1:2, :])
    o_ref[...] = x_ref[...] + gate * (y0_ref[...].astype(F32) + y1_ref[...].astype(F32))


def _moe_final_body(x_ref, y0_ref, y1_ref, gate_ref, g_ref, o_ref):
    x = x_ref[...] + gate_ref[0, 1:2, :] * (y0_ref[...].astype(F32) + y1_ref[...].astype(F32))
    o_ref[...] = x * lax.rsqrt(jnp.mean(x * x, axis=-1, keepdims=True) + NORM_EPS) * g_ref[...]


def _route(logits):
    assert MOE_TOPK == 2
    g_logit = logits[:, :MOE_GROUPS]
    g_prob = jax.nn.softmax(g_logit, axis=-1)
    g_idx = jnp.argmax(g_prob, axis=-1)
    g_p = jnp.max(g_prob, axis=-1)
    e_logit = logits[:, MOE_GROUPS:MOE_GROUPS + MOE_GROUPS * MOE_PER_GROUP]
    e_logit = e_logit.reshape(-1, MOE_GROUPS, MOE_PER_GROUP)
    sel = (jnp.arange(MOE_GROUPS)[None, :] == g_idx[:, None])[:, :, None]
    e_logit = jnp.sum(jnp.where(sel, e_logit, 0.0), axis=1)
    e_prob = jax.nn.softmax(e_logit, axis=-1)
    i0 = jnp.argmax(e_prob, axis=-1)
    p0 = jnp.max(e_prob, axis=-1)
    rest = jnp.where(jnp.arange(MOE_PER_GROUP)[None, :] == i0[:, None], -1.0, e_prob)
    i1 = jnp.argmax(rest, axis=-1)
    p1 = jnp.max(rest, axis=-1)
    e_p = jnp.stack([p0, p1], axis=-1)
    w = g_p[:, None] * e_p / jnp.sum(e_p, axis=-1, keepdims=True)
    ids = g_idx[:, None] * MOE_PER_GROUP + jnp.stack([i0, i1], axis=-1)
    return ids.astype(jnp.int32), w


def _moe(h, logits, x, gate, w_gate, w_up, w_down, layer, *, rows_per_batch, ctx_len, final_g=None):
    t, d = h.shape
    _, n_exp, _, dff = w_gate.shape
    tile = MOE_TILE
    ids, wts = _route(logits)
    flat_e = ids.reshape(-1)
    onehot = (flat_e[:, None] == jnp.arange(n_exp)[None, :]).astype(jnp.int32)
    counts = onehot.sum(0)
    rank = jnp.take_along_axis(jnp.cumsum(onehot, axis=0) - onehot, flat_e[:, None], axis=1)[:, 0]
    padded = (counts + tile - 1) // tile * tile
    starts = jnp.cumsum(padded) - padded
    pos = starts[flat_e] + rank
    n_rows = (t * MOE_TOPK // tile + n_exp) * tile
    n_tiles = n_rows // tile
    token = (jnp.arange(t * MOE_TOPK, dtype=jnp.int32) // MOE_TOPK).astype(F32)
    table = jnp.zeros((n_rows, 2), F32).at[pos].set(jnp.stack([token, wts.reshape(-1)], axis=1),
                                                    unique_indices=True, mode="promise_in_bounds")
    src = table[:, 0].astype(jnp.int32)
    row_w = table[:, 1]
    tile_start = jnp.arange(n_tiles, dtype=jnp.int32) * tile
    ends = starts + padded
    tile_e = jnp.minimum(jnp.sum(tile_start[:, None] >= ends[None, :], axis=1), n_exp - 1).astype(jnp.int32)

    n_chunks = max(c for c in (4, 2, 1) if n_tiles % c == 0)
    tpc = n_tiles // n_chunks
    tn_up = _pick_tile(dff, 512, LANES)
    tn_dn = _pick_tile(d, 4096, LANES)
    w_up_spec = pl.BlockSpec((1, 1, d, tn_up), lambda j, i, te: (layer, te[i], 0, j))
    ys = None
    for c in range(n_chunks):
        rows_c = slice(c * tpc * tile, (c + 1) * tpc * tile)
        te_c = tile_e[c * tpc:(c + 1) * tpc]
        xs = h.at[src[rows_c]].get(mode="promise_in_bounds")
        hid = pl.pallas_call(
            _moe_up_body,
            grid_spec=pltpu.PrefetchScalarGridSpec(
                num_scalar_prefetch=1,
                grid=(dff // tn_up, tpc),
                in_specs=[pl.BlockSpec((tile, d), lambda j, i, te: (i, 0)), w_up_spec, w_up_spec],
                out_specs=pl.BlockSpec((tile, tn_up), lambda j, i, te: (i, j)),
                scratch_shapes=[pltpu.VMEM((d, tn_up), BF16), pltpu.VMEM((d, tn_up), BF16)]),
            out_shape=jax.ShapeDtypeStruct((tpc * tile, dff), BF16),
            compiler_params=_cp("arbitrary", "arbitrary"),
            name="moe_up",
        )(te_c, xs, w_gate, w_up)
        first = ys is None
        ys = pl.pallas_call(
            _moe_down_body if first else _moe_down_into_body,
            grid_spec=pltpu.PrefetchScalarGridSpec(
                num_scalar_prefetch=1,
                grid=(d // tn_dn, tpc),
                in_specs=[pl.BlockSpec((tile, dff), lambda j, i, te: (i, 0)),
                          pl.BlockSpec((1, 1, dff, tn_dn), lambda j, i, te: (layer, te[i], 0, j)),
                          pl.BlockSpec((tile, 1), lambda j, i, te: (i, 0))]
                         + ([] if first else [pl.BlockSpec(memory_space=pl.ANY)]),
                out_specs=pl.BlockSpec((tile, tn_dn), lambda j, i, te, c=c: (c * tpc + i, j)),
                scratch_shapes=[pltpu.VMEM((dff, tn_dn), BF16)]),
            out_shape=jax.ShapeDtypeStruct((n_rows, d), BF16),
            input_output_aliases={} if first else {4: 0},
            compiler_params=_cp("arbitrary", "arbitrary"),
            name="moe_down",
        )(te_c, hid, w_down, row_w[rows_c].reshape(tpc * tile, 1), *(() if first else (ys,)))

    pos2 = pos.reshape(t, MOE_TOPK)
    y0 = ys.at[pos2[:, 0]].get(mode="promise_in_bounds")
    y1 = ys.at[pos2[:, 1]].get(mode="promise_in_bounds")
    tm = _pick_tile(math.gcd(rows_per_batch, ctx_len), 256, 8)
    tpb = rows_per_batch // tm
    if final_g is not None:
        batch = t // rows_per_batch
        ctx_tiles, lat_tiles = ctx_len // tm, (rows_per_batch - ctx_len) // tm
        lat_spec = pl.BlockSpec((tm, d), lambda b, i: (b * tpb + ctx_tiles + i, 0))
        out = pl.pallas_call(
            _moe_final_body,
            grid=(batch, lat_tiles),
            in_specs=[lat_spec, lat_spec, lat_spec, pl.BlockSpec((1, 2, d), lambda b, i: (b, 0, 0)),
                      pl.BlockSpec((1, d), lambda b, i: (0, 0))],
            out_specs=pl.BlockSpec((tm, d), lambda b, i: (b * lat_tiles + i, 0)),
            out_shape=jax.ShapeDtypeStruct((batch * lat_tiles * tm, d), F32),
            compiler_params=_cp("parallel", "parallel"),
            name="moe_combine_final",
        )(x, y0, y1, gate, final_g.reshape(1, d).astype(F32))
        return out.reshape(batch, lat_tiles * tm, d)
    row_spec = pl.BlockSpec((tm, d), lambda i: (i, 0))
    return pl.pallas_call(
        functools.partial(_moe_combine_body, tm=tm, tiles_per_batch=tpb, ctx_len=ctx_len),
        grid=(t // tm,),
        in_specs=[row_spec, row_spec, row_spec, pl.BlockSpec((1, 2, d), lambda i: (i // tpb, 0, 0))],
        out_specs=row_spec,
        out_shape=jax.ShapeDtypeStruct((t, d), F32),
        compiler_params=_cp("parallel"),
        name="moe_combine",
    )(x, y0, y1, gate)


def kernel(x, c, ctx, c_ctx, ada_w, ada_b, norm_mix, norm_ffn, w_in, w_out, diff_lambda, diff_subln,
           s5_a_re, s5_a_im, s5_log_dt, s5_b_re, s5_b_im, s5_c_re, s5_c_im, s5_d, s5_glu_w, s5_glu_b,
           mla_q_norm, mla_kv_norm, mla_w_uq, mla_w_ukv, ret_decay, ret_norm,
           moe_wg, moe_bg, moe_we, moe_be, moe_w_gate, moe_w_up, moe_w_down, final_norm):
    batch, n_lat, d = x.shape
    ctx_len = ctx.shape[1]
    depth = ada_w.shape[0]
    seq = ctx_len + n_lat
    rows = batch * seq
    gw = d // 4
    heads = gw // LANES
    q_rank, kv_rank = 3 * d // 16, d // 16
    ret_qk = heads * RET_K
    n_route = MOE_GROUPS + MOE_GROUPS * MOE_PER_GROUP
    assert heads % 2 == 0 and ctx_len % RET_CHUNK == 0 and n_lat % RET_CHUNK == 0

    splits = (gw, gw, gw, gw, q_rank, kv_rank, ROPE_DIM, ret_qk, ret_qk, gw, gw)
    offs = [0]
    for s_ in splits:
        offs.append(offs[-1] + s_)
    names = ("dq", "dk", "dv", "su", "mcq", "mckv", "mkr", "rq", "rk", "rv", "rg")
    src_col = {n_: (offs[i], offs[i + 1]) for i, n_ in enumerate(names)}
    order = ("dq", "dk", "rq", "rk", "dv", "su", "rv", "rg", "mcq", "mckv")
    col = {}
    pos = 0
    for n_ in order:
        col[n_] = pos
        pos += src_col[n_][1] - src_col[n_][0]
    n_main = pos
    n_rope = col["dv"]
    uq_cols = jnp.arange(heads * (MLA_NOPE + ROPE_DIM)).reshape(heads, MLA_NOPE + ROPE_DIM)
    uq_perm = jnp.concatenate([uq_cols[:, :MLA_NOPE].reshape(-1), uq_cols[:, MLA_NOPE:].reshape(-1)])

    tables = _rope_tables(n_lat, ctx_len)
    log2e = math.log2(math.e)
    rope_scale = jnp.ones((n_rope,), F32).at[col["rk"]:col["rk"] + ret_qk].set(RET_K ** -0.5)
    rope_scale = rope_scale.at[col["dq"]:col["dq"] + gw].set(DIFF_HEAD_DIM ** -0.5 * log2e)
    mla_q_scale = (MLA_NOPE + ROPE_DIM) ** -0.5 * log2e
    main_scale = jnp.concatenate([rope_scale, jnp.ones((n_main - n_rope,), F32)])

    cond = jnp.concatenate([c_ctx[None, :], c], axis=0)
    cond = jnp.pad(cond * _sigmoid(cond), ((0, 8 - (batch + 1) % 8 if (batch + 1) % 8 else 0), (0, 0)))

    mod_all = _ada_mod(cond, ada_w, ada_b)

    xa = jnp.concatenate([ctx, x], axis=1).reshape(rows, d)
    tm_big = _pick_tile(seq, 1088)
    tn = lambda n_: _pick_tile(n_, 512, LANES)

    for l in range(depth):
        lam_init = 0.8 - 0.6 * math.exp(-0.3 * l)
        mod = mod_all[l].reshape(cond.shape[0], 6, d)
        mods = [jnp.stack([jnp.broadcast_to(mod[0, i], (batch, d)), mod[1:batch + 1, i]], axis=1)
                for i in range(6)]

        w_main = jnp.concatenate([w_in[l, :, src_col[n_][0]:src_col[n_][1]] for n_ in order],
                                 axis=1).astype(BF16)
        w_kr = jnp.concatenate([w_in[l, :, src_col["mkr"][0]:src_col["mkr"][1]]] * (LANES // ROPE_DIM),
                               axis=1).astype(BF16)

        h = _norm_mod(xa, norm_mix[l], mods[0], mods[1], rows_per_batch=seq, ctx_len=ctx_len)
        proj = _mm([h], w_main, name="in_proj", out_dtype=BF16, tm=tm_big,
                   tn=tn(math.gcd(n_main, n_rope)), rope=(tables, main_scale, (0, n_rope)),
                   rows_per_batch=seq)
        krr = _mm([h], w_kr, name="in_proj_kr", out_dtype=BF16, tm=tm_big, tn=LANES,
                  rope=(tables, jnp.ones((LANES,), F32), (0, LANES)), rows_per_batch=seq)
        qk = proj

        lv = diff_lambda[l].astype(F32)
        lam = jnp.exp(jnp.sum(lv[0] * lv[1])) - jnp.exp(jnp.sum(lv[2] * lv[3])) + lam_init
        a_out = _diff_attn(qk, proj, lam, diff_subln[l], batch=batch, seq=seq, ctx_len=ctx_len,
                           heads=heads, q_blk=col["dq"] // LANES, k_blk=col["dk"] // LANES,
                           v_blk=col["dv"] // LANES, post=1.0 - lam_init)

        s5p = _s5n_params(s5_a_re[l], s5_a_im[l], s5_log_dt[l], s5_b_re[l], s5_b_im[l],
                          s5_c_re[l], s5_c_im[l], s5_d[l])
        s_act = _s5n_mix(proj, col["su"], s5p, batch=batch, seq=seq, ctx_len=ctx_len, gw=gw)
        s_out = _mm([s_act], s5_glu_w[l].astype(BF16), name="s5_glu", out_dtype=BF16, tm=tm_big, tn=tn(gw),
                    bias=s5_glu_b[l], glu_in=s_act)

        cq = proj[:, col["mcq"]:col["mcq"] + q_rank]
        ckv = proj[:, col["mckv"]:col["mckv"] + kv_rank]
        w_uq = (mla_w_uq[l][:, uq_perm] * mla_q_scale).astype(BF16)
        n_qn, n_q = heads * MLA_NOPE, heads * (MLA_NOPE + ROPE_DIM)
        q_up = _mm([cq], w_uq, name="mla_q_up", out_dtype=BF16, tm=tm_big, tn=tn(math.gcd(n_qn, n_q)),
                   norm_g=mla_q_norm[l], rope=(tables, jnp.ones((n_q,), F32), (n_qn, n_q)),
                   rows_per_batch=seq)
        kv_up = _mm([ckv], mla_w_ukv[l].astype(BF16), name="mla_kv_up", out_dtype=BF16, tm=tm_big,
                    tn=tn(mla_w_ukv.shape[2]), norm_g=mla_kv_norm[l])
        m_out = _mla_attn(q_up, kv_up, krr, batch=batch, seq=seq, ctx_len=ctx_len, heads=heads)

        log_g = jax.nn.log_sigmoid(ret_decay[l].astype(F32))
        r_out = _retention(qk, proj, log_g, ret_norm[l], batch=batch, seq=seq, ctx_len=ctx_len,
                           heads=heads, q_off=col["rq"], k_off=col["rk"], v_off=col["rv"], g_off=col["rg"])

        xa = _mm([a_out, s_out, m_out, r_out], w_out[l].astype(BF16), name="out_proj", out_dtype=F32, tm=tm_big, tn=tn(d),
                 res=xa, gate=mods[2], rows_per_batch=seq, ctx_len=ctx_len)

        w_r = jnp.concatenate([moe_wg[l], moe_we[l]], axis=1).astype(F32)
        w_r = jnp.pad(w_r, ((0, 0), (0, LANES - n_route)))
        w_r_hi = w_r.astype(BF16)
        w_r_lo = (w_r - w_r_hi.astype(F32)).astype(BF16)
        b_r = jnp.pad(jnp.concatenate([moe_bg[l], moe_be[l]]).astype(F32), (0, LANES - n_route))
        h, logits = _norm_mod(xa, norm_ffn[l], mods[3], mods[4], rows_per_batch=seq, ctx_len=ctx_len,
                              router=(w_r_hi, w_r_lo, b_r.reshape(1, LANES)))
        xa = _moe(h, logits, xa, mods[5], moe_w_gate, moe_w_up, moe_w_down, l,
                  rows_per_batch=seq, ctx_len=ctx_len, final_g=final_norm if l == depth - 1 else None)
    return xa
```

```python
import functools
import math

import jax
import jax.numpy as jnp
from jax import lax
from jax.experimental import pallas as pl
from jax.experimental.pallas import tpu as pltpu

BF16 = jnp.bfloat16
F32 = jnp.float32

V7X_VMEM_BYTES = 64 * 2**20
VMEM_LIMIT = V7X_VMEM_BYTES - 12 * 2**20
LANES = 128

GRID_W = 64
ROPE_DIM = 64
ROPE_BASE = 10000.0
NORM_EPS = 1e-6
DIFF_HEAD_DIM = 64
S5_CH = 16
S5_STATE = 64
MLA_NOPE = 128
MLA_V = 128
RET_K = 64
RET_V = 128
RET_CHUNK = 128
MOE_GROUPS = 4
MOE_PER_GROUP = 4
MOE_TOPK = 2
ROPE_QUARTER = ROPE_DIM // 4

MM_ROW_TILE = 1088
MM_COL_TILE = 512
ROW_TILE = 256
ATTN_Q_TILE = 512
S5_SCAN_LANES = 2048
MOE_TILE = 256
MOE_UP_COL_TILE = 512
MOE_DOWN_COL_TILE = 4096


def _cp(*sem):
    return pltpu.CompilerParams(dimension_semantics=sem, vmem_limit_bytes=VMEM_LIMIT)


def _pick_tile(n, target, mult=16):
    best = None
    for t in range(mult, min(n, target) + 1, mult):
        if n % t == 0:
            best = t
    assert best is not None, (n, target)
    return best


def _sigmoid(x):
    return 1.0 / (1.0 + jnp.exp(-x))


def _rotate(x, cos, sa, sb):
    width = x.shape[1]
    reps = width // LANES
    tile = lambda t: jnp.tile(t, (1, reps))
    x_dn = pltpu.roll(x, width - ROPE_QUARTER, 1)
    x_up = pltpu.roll(x, ROPE_QUARTER, 1)
    return x * tile(cos) + x_dn * tile(sa) + x_up * tile(sb)


def _mm_body(*refs, nx, ksizes, has_norm, has_bias, epilogue, tm, tiles_per_batch, ctx_len, rope_tiles):
    x_refs = refs[:nx]
    w_ref = refs[nx]
    idx = nx + 1
    g_ref = b_ref = e_ref = res_ref = gate_ref = None
    if has_norm:
        g_ref = refs[idx]; idx += 1
    if has_bias:
        b_ref = refs[idx]; idx += 1
    if epilogue == "glu":
        e_ref = refs[idx]; idx += 1
    if epilogue == "resgate":
        res_ref, gate_ref = refs[idx], refs[idx + 1]; idx += 2
    if epilogue == "rope":
        cos_ref, sa_ref, sb_ref, cs_ref = refs[idx:idx + 4]; idx += 4
    o_ref = refs[idx]

    acc = None
    off = 0
    for xr, ks in zip(x_refs, ksizes):
        x = xr[...]
        if has_norm:
            xf = x.astype(F32)
            xf = xf * lax.rsqrt(jnp.mean(xf * xf, axis=-1, keepdims=True) + NORM_EPS)
            x = xf * g_ref[...]
        x = x.astype(BF16)
        w = w_ref[off:off + ks, :].astype(BF16)
        part = jnp.dot(x, w, preferred_element_type=F32)
        acc = part if acc is None else acc + part
        off += ks
    if has_bias:
        acc = acc + b_ref[...]
    if epilogue == "glu":
        acc = e_ref[...].astype(F32) * _sigmoid(acc)
    elif epilogue == "resgate":
        i = pl.program_id(0)
        row = (i % tiles_per_batch) * tm + lax.broadcasted_iota(jnp.int32, (tm, 1), 0)
        gate = jnp.where(row < ctx_len, gate_ref[0, 0:1, :], gate_ref[0, 1:2, :])
        acc = res_ref[...] + gate * acc
    if epilogue == "rope":
        j = pl.program_id(1)
        roped = (j >= rope_tiles[0]) & (j < rope_tiles[1])

        @pl.when(roped)
        def _():
            y = _rotate(acc, cos_ref[...], sa_ref[...], sb_ref[...]) * cs_ref[...]
            o_ref[...] = y.astype(o_ref.dtype)

        @pl.when(jnp.logical_not(roped))
        def _():
            o_ref[...] = acc.astype(o_ref.dtype)
    else:
        o_ref[...] = acc.astype(o_ref.dtype)


def _mm(xs, w, *, name, out_dtype, tm, tn, norm_g=None, bias=None, glu_in=None, res=None, gate=None,
        rope=None, rows_per_batch=None, ctx_len=0):
    m = xs[0].shape[0]
    ksizes = tuple(x.shape[1] for x in xs)
    k, n = w.shape
    assert sum(ksizes) == k and m % tm == 0 and n % tn == 0
    epilogue = ("glu" if glu_in is not None else "resgate" if res is not None
                else "rope" if rope is not None else None)
    tiles_per_batch = (rows_per_batch // tm) if rows_per_batch else 1
    rope_tiles = None
    in_specs = [pl.BlockSpec((tm, ks), lambda i, j: (i, 0)) for ks in ksizes]
    in_specs.append(pl.BlockSpec((k, tn), lambda i, j: (0, j)))
    args = list(xs) + [w]
    if norm_g is not None:
        in_specs.append(pl.BlockSpec((1, k), lambda i, j: (0, 0)))
        args.append(norm_g.reshape(1, k).astype(F32))
    if bias is not None:
        in_specs.append(pl.BlockSpec((1, tn), lambda i, j: (0, j)))
        args.append(bias.reshape(1, n).astype(F32))
    if epilogue == "glu":
        in_specs.append(pl.BlockSpec((tm, tn), lambda i, j: (i, j)))
        args.append(glu_in)
    if epilogue == "resgate":
        tpb = tiles_per_batch
        in_specs.append(pl.BlockSpec((tm, tn), lambda i, j: (i, j)))
        in_specs.append(pl.BlockSpec((1, 2, tn), lambda i, j: (i // tpb, 0, j)))
        args += [res, gate]
    if epilogue == "rope":
        tables, col_scale, (lo, hi) = rope
        assert lo % tn == 0 and hi % tn == 0
        rope_tiles = (lo // tn, hi // tn)
        tpb = tiles_per_batch
        in_specs += [pl.BlockSpec((tm, LANES), lambda i, j: (i % tpb, 0))] * 3
        in_specs.append(pl.BlockSpec((1, tn), lambda i, j: (0, j)))
        args += list(tables) + [col_scale.reshape(1, n).astype(F32)]
    body = functools.partial(_mm_body, nx=len(xs), ksizes=ksizes, has_norm=norm_g is not None,
                             has_bias=bias is not None, epilogue=epilogue, tm=tm,
                             tiles_per_batch=tiles_per_batch, ctx_len=ctx_len, rope_tiles=rope_tiles)
    return pl.pallas_call(
        body,
        grid=(m // tm, n // tn),
        in_specs=in_specs,
        out_specs=pl.BlockSpec((tm, tn), lambda i, j: (i, j)),
        out_shape=jax.ShapeDtypeStruct((m, n), out_dtype),
        compiler_params=_cp("parallel", "arbitrary"),
        name=name,
    )(*args)


def _ada_body(c_ref, w_ref, b_ref, o_ref):
    acc = jnp.dot(c_ref[...].astype(BF16), w_ref[0].astype(BF16), preferred_element_type=F32)
    o_ref[0] = acc + b_ref[0]


def _ada_mod(cond, ada_w, ada_b):
    depth, d, n6 = ada_w.shape
    rows = cond.shape[0]
    tn = _pick_tile(n6, MM_COL_TILE, LANES)
    return pl.pallas_call(
        _ada_body,
        grid=(depth, n6 // tn),
        in_specs=[pl.BlockSpec((rows, d), lambda l, j: (0, 0)),
                  pl.BlockSpec((1, d, tn), lambda l, j: (l, 0, j)),
                  pl.BlockSpec((1, 1, tn), lambda l, j: (l, 0, j))],
        out_specs=pl.BlockSpec((1, rows, tn), lambda l, j: (l, 0, j)),
        out_shape=jax.ShapeDtypeStruct((depth, rows, n6), F32),
        compiler_params=_cp("parallel", "arbitrary"),
        name="ada_mod",
    )(cond, ada_w, ada_b.reshape(depth, 1, n6).astype(F32))


def _norm_mod_body(*refs, tm, tiles_per_batch, ctx_len, router):
    if router:
        x_ref, g_ref, sh_ref, sc_ref, whi_ref, wlo_ref, br_ref, h_ref, lg_ref = refs
    else:
        x_ref, g_ref, sh_ref, sc_ref, h_ref = refs
    i = pl.program_id(0)
    x = x_ref[...]
    y = x * lax.rsqrt(jnp.mean(x * x, axis=-1, keepdims=True) + NORM_EPS) * g_ref[...]
    is_ctx = (i % tiles_per_batch) * tm < ctx_len
    sh = jnp.where(is_ctx, sh_ref[0, 0:1, :], sh_ref[0, 1:2, :])
    sc = jnp.where(is_ctx, sc_ref[0, 0:1, :], sc_ref[0, 1:2, :])
    h = y * (1.0 + sc) + sh
    h_ref[...] = h.astype(BF16)
    if router:
        hi = h.astype(BF16)
        lo = (h - hi.astype(F32)).astype(BF16)
        lg = jnp.dot(hi, whi_ref[...], preferred_element_type=F32)
        lg = lg + jnp.dot(hi, wlo_ref[...], preferred_element_type=F32)
        lg = lg + jnp.dot(lo, whi_ref[...], preferred_element_type=F32)
        lg_ref[...] = lg + br_ref[...]


def _norm_mod(x, g, shift, scale, *, rows_per_batch, ctx_len, router=None):
    m, d = x.shape
    tm = _pick_tile(math.gcd(rows_per_batch, ctx_len), ROW_TILE, 8)
    tpb = rows_per_batch // tm
    in_specs = [
        pl.BlockSpec((tm, d), lambda i: (i, 0)),
        pl.BlockSpec((1, d), lambda i: (0, 0)),
        pl.BlockSpec((1, 2, d), lambda i: (i // tpb, 0, 0)),
        pl.BlockSpec((1, 2, d), lambda i: (i // tpb, 0, 0)),
    ]
    args = [x, g.reshape(1, d), shift, scale]
    out_specs = [pl.BlockSpec((tm, d), lambda i: (i, 0))]
    out_shape = [jax.ShapeDtypeStruct((m, d), BF16)]
    if router is not None:
        whi, wlo, br = router
        in_specs += [pl.BlockSpec((d, LANES), lambda i: (0, 0)),
                     pl.BlockSpec((d, LANES), lambda i: (0, 0)),
                     pl.BlockSpec((1, LANES), lambda i: (0, 0))]
        args += [whi, wlo, br]
        out_specs.append(pl.BlockSpec((tm, LANES), lambda i: (i, 0)))
        out_shape.append(jax.ShapeDtypeStruct((m, LANES), F32))
    body = functools.partial(_norm_mod_body, tm=tm, tiles_per_batch=tpb, ctx_len=ctx_len,
                             router=router is not None)
    outs = pl.pallas_call(body, grid=(m // tm,), in_specs=in_specs, out_specs=out_specs,
                          out_shape=out_shape, compiler_params=_cp("parallel"),
                          name="norm_mod_router" if router is not None else "norm_mod")(*args)
    return outs if router is not None else outs[0]


def _rope_tables(n_lat, ctx_len):
    rows = n_lat // GRID_W
    row = jnp.repeat(jnp.arange(rows, dtype=F32), GRID_W)
    col = jnp.tile(jnp.arange(GRID_W, dtype=F32), rows)
    quarter = ROPE_DIM // 4
    inv = ROPE_BASE ** (-jnp.arange(quarter, dtype=F32) / quarter)
    ar = row[:, None] * inv
    ac = col[:, None] * inv
    ang = jnp.concatenate([ar, ar, ac, ac], axis=-1)
    ang = jnp.concatenate([jnp.zeros((ctx_len, ROPE_DIM), F32), ang], axis=0)
    ang = jnp.tile(ang, (1, LANES // ROPE_DIM))
    cos, sin = jnp.cos(ang), jnp.sin(ang)
    lane = jnp.arange(LANES)
    even = ((lane // quarter) % 2 == 0)[None, :]
    sa = jnp.where(even, -sin, 0.0)
    sb = jnp.where(even, 0.0, sin)
    return cos, sa, sb


ATTN_ALIGN = 256


def _softmax_pv(q, k_ref, va_ref, bounds):
    ms, ovs = [], []
    for lo, hi in bounds:
        s = lax.dot_general(q, k_ref[lo:hi, :], (((1,), (1,)), ((), ())), preferred_element_type=F32)
        m = jnp.max(s, axis=-1, keepdims=True)
        e = jnp.exp2(s - m).astype(BF16)
        ovs.append(jnp.dot(e, va_ref[lo:hi, :], preferred_element_type=F32))
        ms.append(m)
    m_all = functools.reduce(jnp.maximum, ms)
    acc = sum(ov * jnp.exp2(m - m_all) for m, ov in zip(ms, ovs))
    return acc[:, :LANES] / acc[:, LANES:]


def _key_chunks(n_keys):
    if n_keys < 2 * ATTN_ALIGN:
        return ((0, n_keys),)
    half = (n_keys // ATTN_ALIGN + 1) // 2 * ATTN_ALIGN
    return ((0, half), (half, n_keys))


def _lat_tile(seq, ctx_len):
    n_lat = seq - ctx_len
    assert ctx_len % ATTN_ALIGN == 0 and n_lat % ATTN_ALIGN == 0
    return _pick_tile(n_lat, ATTN_Q_TILE, ATTN_ALIGN)


def _lat_rows(seq, ctx_len, tq):
    return lambda b, i: pl.multiple_of(b * seq + ctx_len + i * tq, ATTN_ALIGN)


def _diff_attn_body(lam_ref, q_ref, k_ref, v_ref, sub_ref, *rest, post, bounds):
    o_ref, va_ref = rest[-2:]

    def fill():
        va_ref[:, :LANES] = v_ref[...]
        va_ref[:, LANES:] = jnp.ones(v_ref.shape, BF16)

    if len(rest) == 2:
        pl.when(pl.program_id(2) == 0)(fill)
    else:
        fill()

    q = q_ref[...]
    lane = lax.broadcasted_iota(jnp.int32, (1, LANES), 1)
    first = lane < DIFF_HEAD_DIM
    zero = jnp.zeros_like(q)
    q0 = jnp.where(first, q, zero)
    q1 = jnp.where(first, zero, q)
    o = _softmax_pv(q0, k_ref, va_ref, bounds) - lam_ref[0] * _softmax_pv(q1, k_ref, va_ref, bounds)
    o = o * lax.rsqrt(jnp.mean(o * o, axis=-1, keepdims=True) + NORM_EPS) * sub_ref[...] * post
    o_ref[...] = o.astype(o_ref.dtype)


def _diff_attn(qk, proj, lam, subln, *, batch, seq, ctx_len, heads, q_blk, k_blk, v_blk, post):
    tq = _lat_tile(seq, ctx_len)
    rows = _lat_rows(seq, ctx_len, tq)
    smem = pl.BlockSpec(memory_space=pltpu.SMEM)
    args = (lam.reshape(1).astype(F32), qk, qk, proj, subln.reshape(1, LANES).astype(F32))
    out_shape = jax.ShapeDtypeStruct((batch * seq, heads * LANES), BF16)
    elem = (pl.Element(tq), pl.Element(LANES))
    lat = pl.pallas_call(
        functools.partial(_diff_attn_body, post=post, bounds=_key_chunks(seq)),
        grid=(batch, heads, (seq - ctx_len) // tq),
        in_specs=[
            smem,
            pl.BlockSpec(elem, lambda b, h, i: (rows(b, i), pl.multiple_of((q_blk + h) * LANES, LANES))),
            pl.BlockSpec((seq, LANES), lambda b, h, i: (b, k_blk + h)),
            pl.BlockSpec((seq, LANES), lambda b, h, i: (b, v_blk + h)),
            pl.BlockSpec((1, LANES), lambda b, h, i: (0, 0)),
        ],
        out_specs=pl.BlockSpec(elem, lambda b, h, i: (rows(b, i), pl.multiple_of(h * LANES, LANES))),
        out_shape=out_shape,
        scratch_shapes=[pltpu.VMEM((seq, 2 * LANES), BF16)],
        compiler_params=_cp("parallel", "parallel", "arbitrary"),
        name="diff_attn",
    )(*args)
    cpb = seq // ctx_len
    ctx_spec = lambda blk: pl.BlockSpec((ctx_len, LANES), lambda b, h: (b * cpb, blk + h))
    return pl.pallas_call(
        functools.partial(_diff_attn_body, post=post, bounds=_key_chunks(ctx_len)),
        grid=(batch, heads),
        in_specs=[smem, ctx_spec(q_blk), ctx_spec(k_blk), ctx_spec(v_blk),
                  pl.BlockSpec((1, LANES), lambda b, h: (0, 0)), pl.BlockSpec(memory_space=pl.ANY)],
        out_specs=ctx_spec(0),
        out_shape=out_shape,
        scratch_shapes=[pltpu.VMEM((ctx_len, 2 * LANES), BF16)],
        input_output_aliases={5: 0},
        compiler_params=_cp("parallel", "parallel"),
        name="diff_attn_ctx",
    )(*args, lat)


def _mla_attn_body(qn_ref, qr_ref, kn_ref, kr_ref, v_ref, *rest, bounds):
    o_ref, ka_ref, va_ref = rest[-3:]
    h = pl.program_id(1)

    def fill():
        ka_ref[:, :LANES] = kn_ref[...]
        ka_ref[:, LANES:] = kr_ref[...]
        va_ref[:, :LANES] = v_ref[...]
        va_ref[:, LANES:] = jnp.ones(v_ref.shape, BF16)

    if len(rest) == 3:
        pl.when(pl.program_id(2) == 0)(fill)
    else:
        fill()

    qr = qr_ref[...]
    lane = lax.broadcasted_iota(jnp.int32, (1, LANES), 1)
    mine = (lane < ROPE_DIM) == (h % 2 == 0)
    qr = jnp.where(mine, qr, jnp.zeros_like(qr))
    q = jnp.concatenate([qn_ref[...], qr], axis=1)
    o_ref[...] = _softmax_pv(q, ka_ref, va_ref, bounds).astype(o_ref.dtype)


def _mla_attn(q_up, kv_up, k_rope, *, batch, seq, ctx_len, heads):
    tq = _lat_tile(seq, ctx_len)
    rows = _lat_rows(seq, ctx_len, tq)
    args = (q_up, q_up, kv_up, k_rope, kv_up)
    out_shape = jax.ShapeDtypeStruct((batch * seq, heads * LANES), BF16)
    elem = (pl.Element(tq), pl.Element(LANES))
    lat = pl.pallas_call(
        functools.partial(_mla_attn_body, bounds=_key_chunks(seq)),
        grid=(batch, heads, (seq - ctx_len) // tq),
        in_specs=[
            pl.BlockSpec(elem, lambda b, h, i: (rows(b, i), pl.multiple_of(h * LANES, LANES))),
            pl.BlockSpec(elem, lambda b, h, i: (rows(b, i), pl.multiple_of((heads + h // 2) * LANES, LANES))),
            pl.BlockSpec((seq, LANES), lambda b, h, i: (b, 2 * h)),
            pl.BlockSpec((seq, LANES), lambda b, h, i: (b, 0)),
            pl.BlockSpec((seq, LANES), lambda b, h, i: (b, 2 * h + 1)),
        ],
        out_specs=pl.BlockSpec(elem, lambda b, h, i: (rows(b, i), pl.multiple_of(h * LANES, LANES))),
        out_shape=out_shape,
        scratch_shapes=[pltpu.VMEM((seq, 2 * LANES), BF16), pltpu.VMEM((seq, 2 * LANES), BF16)],
        compiler_params=_cp("parallel", "parallel", "arbitrary"),
        name="mla_attn",
    )(*args)
    cpb = seq // ctx_len
    ctx_spec = lambda col: pl.BlockSpec((ctx_len, LANES), lambda b, h: (b * cpb, col(h)))
    return pl.pallas_call(
        functools.partial(_mla_attn_body, bounds=_key_chunks(ctx_len)),
        grid=(batch, heads),
        in_specs=[ctx_spec(lambda h: h), ctx_spec(lambda h: heads + h // 2), ctx_spec(lambda h: 2 * h),
                  ctx_spec(lambda h: 0), ctx_spec(lambda h: 2 * h + 1), pl.BlockSpec(memory_space=pl.ANY)],
        out_specs=ctx_spec(lambda h: h),
        out_shape=out_shape,
        scratch_shapes=[pltpu.VMEM((ctx_len, 2 * LANES), BF16), pltpu.VMEM((ctx_len, 2 * LANES), BF16)],
        input_output_aliases={5: 0},
        compiler_params=_cp("parallel", "parallel"),
        name="mla_attn_ctx",
    )(*args, lat)


def _retention_body(*refs, backward, cs, heads):
    if backward:
        lg_ref, q_ref, k_ref, v_ref, yf_ref, gate_ref, ng_ref, o_ref, s_ref, d_ref, qd_ref, kd_ref = refs
    else:
        lg_ref, q_ref, k_ref, v_ref, o_ref, s_ref, d_ref, qd_ref, kd_ref = refs
    t = pl.program_id(1)

    @pl.when(t == 0)
    def _():
        s_ref[...] = jnp.zeros_like(s_ref)
        pos_r = lax.broadcasted_iota(jnp.int32, (cs, cs), 0).astype(F32)
        pos_c = lax.broadcasted_iota(jnp.int32, (cs, cs), 1).astype(F32)
        pos = lax.broadcasted_iota(jnp.int32, (cs, 1), 0).astype(F32)
        for h in range(heads):
            lg = lg_ref[h]
            if backward:
                diff = pos_c - pos_r
                keep = diff > 0
                qd_ref[h] = jnp.exp(lg * (cs - pos))
                kd_ref[h] = jnp.exp(lg * pos)
            else:
                diff = pos_r - pos_c
                keep = diff >= 0
                qd_ref[h] = jnp.exp(lg * (pos + 1.0))
                kd_ref[h] = jnp.exp(lg * (cs - 1.0 - pos))
            d_ref[h] = jnp.where(keep, jnp.exp(lg * jnp.maximum(diff, 0.0)), 0.0)

    lane = lax.broadcasted_iota(jnp.int32, (1, LANES), 1)
    nt = (((1,), (1,)), ((), ()))
    tn = (((0,), (0,)), ((), ()))
    for h in range(heads):
        blk = slice((h // 2) * LANES, (h // 2 + 1) * LANES)
        col = slice(h * RET_V, (h + 1) * RET_V)
        mine = (lane < RET_K) == (h % 2 == 0)
        q = q_ref[:, blk]
        q = jnp.where(mine, q, jnp.zeros_like(q))
        k = k_ref[:, blk]
        v = v_ref[:, col]
        scores = lax.dot_general(q, k, nt, preferred_element_type=F32) * d_ref[h]
        intra = jnp.dot(scores.astype(BF16), v, preferred_element_type=F32)
        state = s_ref[h]
        q_w = (q.astype(F32) * qd_ref[h]).astype(BF16)
        cross = jnp.dot(q_w, state.astype(BF16), preferred_element_type=F32)
        k_w = (k.astype(F32) * kd_ref[h]).astype(BF16)
        upd = lax.dot_general(k_w, v, tn, preferred_element_type=F32)
        s_ref[h] = jnp.exp(lg_ref[h] * cs) * state + upd
        y = intra + cross
        if backward:
            y = y + yf_ref[:, col]
            y = y * lax.rsqrt(jnp.mean(y * y, axis=-1, keepdims=True) + NORM_EPS) * ng_ref[:, col]
            g = gate_ref[:, col].astype(F32)
            o_ref[:, col] = (g * _sigmoid(g) * y).astype(o_ref.dtype)
        else:
            o_ref[:, col] = y


def _retention(qk, proj, log_g, norm_g, *, batch, seq, ctx_len, heads, q_off, k_off, v_off, g_off):
    cs = RET_CHUNK
    nc, nc_ctx = seq // cs, ctx_len // cs
    wqk, wv = heads * RET_K, heads * RET_V
    assert q_off % wqk == 0 and k_off % wqk == 0 and v_off % wv == 0 and g_off % wv == 0
    smem = pl.BlockSpec(memory_space=pltpu.SMEM)

    def fwd_chunk(t):
        return t

    def bwd_chunk(t):
        return jnp.where(t < nc_ctx, nc_ctx - 1 - t, nc - 1 - (t - nc_ctx))

    def specs(chunk):
        row = lambda b, t: b * nc + chunk(t)
        return (pl.BlockSpec((cs, wqk), lambda b, t: (row(b, t), q_off // wqk)),
                pl.BlockSpec((cs, wqk), lambda b, t: (row(b, t), k_off // wqk)),
                pl.BlockSpec((cs, wv), lambda b, t: (row(b, t), v_off // wv)),
                pl.BlockSpec((cs, wv), lambda b, t: (row(b, t), 0)),
                pl.BlockSpec((cs, wv), lambda b, t: (row(b, t), g_off // wv)))

    scratch = [pltpu.VMEM((heads, LANES, RET_V), F32), pltpu.VMEM((heads, cs, cs), F32),
               pltpu.VMEM((heads, cs, 1), F32), pltpu.VMEM((heads, cs, 1), F32)]
    qs, ks, vs, ys, gs = specs(fwd_chunk)
    y_f = pl.pallas_call(
        functools.partial(_retention_body, backward=False, cs=cs, heads=heads),
        grid=(batch, nc),
        in_specs=[smem, qs, ks, vs],
        out_specs=ys,
        out_shape=jax.ShapeDtypeStruct((batch * seq, wv), F32),
        scratch_shapes=scratch,
        compiler_params=_cp("parallel", "arbitrary"),
        name="retention_fwd",
    )(log_g[0].astype(F32), qk, qk, proj)
    qs, ks, vs, ys, gs = specs(bwd_chunk)
    return pl.pallas_call(
        functools.partial(_retention_body, backward=True, cs=cs, heads=heads),
        grid=(batch, nc),
        in_specs=[smem, qs, ks, vs, ys, gs, pl.BlockSpec((1, wv), lambda b, t: (0, 0))],
        out_specs=ys,
        out_shape=jax.ShapeDtypeStruct((batch * seq, wv), BF16),
        scratch_shapes=scratch,
        compiler_params=_cp("parallel", "arbitrary"),
        name="retention_bwd",
    )(log_g[1].astype(F32), qk, qk, proj, y_f, proj, norm_g.reshape(1, wv).astype(F32))


S5N_CHUNK = 8
S5N_GROUPS = LANES // S5_CH
S5N_HALF = S5N_GROUPS * S5_STATE


def _s5n_params(a_re, a_im, log_dt, b_re, b_im, c_re, c_im, d_skip):
    L, ch, gl = S5N_CHUNK, S5_CH, S5N_GROUPS
    a_re, a_im = a_re.astype(F32), a_im.astype(F32)
    groups = a_re.shape[1]
    nb = groups // gl
    dt = jnp.exp(log_dt.astype(F32))[..., None]
    e = jnp.arange(L + 1, dtype=F32)[:, None, None, None]
    mag = jnp.exp(a_re * dt * e)
    pw_re, pw_im = mag * jnp.cos(a_im * dt * e), mag * jnp.sin(a_im * dt * e)
    ab_re, ab_im = pw_re[1], pw_im[1]
    den = a_re * a_re + a_im * a_im
    f_re = ((ab_re - 1.0) * a_re + ab_im * a_im) / den
    f_im = (ab_im * a_re - (ab_re - 1.0) * a_im) / den
    bb_re = f_re[..., None] * b_re - f_im[..., None] * b_im
    bb_im = f_re[..., None] * b_im + f_im[..., None] * b_re
    c_re, c_im = c_re.astype(F32), c_im.astype(F32)
    hp = lax.Precision.HIGHEST
    idx = jnp.arange(L)

    cp_re = c_re[None] * pw_re[:L, :, :, None, :] - c_im[None] * pw_im[:L, :, :, None, :]
    cp_im = c_re[None] * pw_im[:L, :, :, None, :] + c_im[None] * pw_re[:L, :, :, None, :]
    kmat = (jnp.einsum("ldgcp,dgpk->dglck", cp_re, bb_re, precision=hp)
            - jnp.einsum("ldgcp,dgpk->dglck", cp_im, bb_im, precision=hp))
    lag = idx[None, :] - idx[:, None]

    def toeplitz(k, lg):
        return jnp.where((lg >= 0)[None, :, :, None, None], k[:, jnp.clip(lg, 0, L - 1)], 0.0)

    t = toeplitz(kmat[0], lag) + toeplitz(kmat[1], -lag)
    skip = (idx[:, None] == idx[None, :])[None, :, :, None, None] * (
        jnp.eye(ch, dtype=F32)[None, None, None] * d_skip.astype(F32)[:, None, None, :, None])
    t = (t + skip).reshape(nb, gl, L, L, ch, ch)
    m_mat = t.transpose(0, 2, 1, 5, 3, 4).reshape(nb, L, LANES, L * ch)

    def state_in(d, exps):
        p_re, p_im = pw_re[exps, d], pw_im[exps, d]
        w_re = p_re[..., None] * bb_re[d][None] - p_im[..., None] * bb_im[d][None]
        w_im = p_re[..., None] * bb_im[d][None] + p_im[..., None] * bb_re[d][None]
        return jnp.stack([w_re, w_im], axis=0).transpose(2, 1, 4, 0, 3)

    w_full = jnp.stack([state_in(0, L - 1 - idx), state_in(1, idx)], axis=3)
    w_full = w_full.reshape(nb, gl, L, ch, 2, 2, S5_STATE)
    w_mat = w_full.transpose(0, 2, 1, 3, 4, 5, 6).reshape(nb, L, LANES, 4 * S5_STATE)

    def state_out(d, exps):
        p_re, p_im = pw_re[exps, d], pw_im[exps, d]
        v_re = c_re[d][None] * p_re[:, :, None, :] - c_im[d][None] * p_im[:, :, None, :]
        v_im = c_re[d][None] * p_im[:, :, None, :] + c_im[d][None] * p_re[:, :, None, :]
        return jnp.stack([v_re, -v_im], axis=0).transpose(2, 0, 4, 1, 3)

    v_full = jnp.stack([state_out(0, idx + 1), state_out(1, L - idx)], axis=1)
    v_full = v_full.reshape(nb, gl, 2, 2, S5_STATE, L, ch)
    v_mat = v_full.transpose(0, 2, 3, 1, 4, 5, 6).reshape(nb, 4 * S5N_HALF, L * ch)

    a_pow = jnp.stack([pw_re[L], pw_im[L]], axis=1).reshape(2, 2, nb * S5N_HALF)
    return m_mat.astype(BF16), w_mat.astype(BF16), v_mat.astype(BF16), a_pow


def _s5n_spreaders():
    L, ch, gl, st = S5N_CHUNK, S5_CH, S5N_GROUPS, S5_STATE
    e_out = jnp.einsum("ij,cd->icjd", jnp.eye(L), jnp.eye(ch))
    e_out = jnp.broadcast_to(e_out[:, :, :, None, :], (L, ch, L, gl, ch)).reshape(L * ch, L * LANES)
    e_st = jnp.broadcast_to(jnp.eye(4 * st).reshape(4 * st, 4, 1, st), (4 * st, 4, gl, st))
    return e_out.astype(BF16), e_st.reshape(4 * st, 4 * S5N_HALF).astype(BF16)


def _s5n_expand(compact, spread, row_group, col_group):
    full = jnp.dot(compact, spread, preferred_element_type=F32)
    rows = lax.broadcasted_iota(jnp.int32, full.shape, 0)
    cols = lax.broadcasted_iota(jnp.int32, full.shape, 1)
    return jnp.where(row_group(rows) == col_group(cols), full, 0.0).astype(BF16)


_S5N_IN_ROW_GROUP = lambda r: r // S5_CH
_S5N_STATE_GROUP = lambda c: (c % S5N_HALF) // S5_STATE
_S5N_OUT_COL_GROUP = lambda c: (c % LANES) // S5_CH


def _s5n_state_in_body(u_ref, w_ref, e_ref, o_ref, w_exp):
    @pl.when(pl.program_id(1) == 0)
    def _():
        for j in range(S5N_CHUNK):
            w_exp[j * LANES:(j + 1) * LANES, :] = _s5n_expand(w_ref[0, j], e_ref[...],
                                                              _S5N_IN_ROW_GROUP, _S5N_STATE_GROUP)

    u_cat = jnp.concatenate([u_ref[:, j, :] for j in range(S5N_CHUNK)], axis=1)
    acc = jnp.dot(u_cat, w_exp[...], preferred_element_type=F32)
    for d in range(2):
        for r in range(2):
            lo = (2 * d + r) * S5N_HALF
            o_ref[d, r] = acc[:, lo:lo + S5N_HALF]


def _s5n_scan_body(sr_ref, si_ref, a_ref, o_ref, *, nk, nk_ctx):
    d = pl.program_id(0)
    ar, ai = a_ref[0, 0], a_ref[0, 1]

    def step(k, carry):
        hr, hi = carry
        o_ref[0, 0, pl.ds(k, 1), :] = hr
        o_ref[0, 1, pl.ds(k, 1), :] = hi
        sr = sr_ref[0, 0, pl.ds(k, 1), :]
        si = si_ref[0, 0, pl.ds(k, 1), :]
        return ar * hr - ai * hi + sr, ar * hi + ai * hr + si

    zero = jnp.zeros_like(ar)
    unroll = 8 if (nk % 8 == 0 and nk_ctx % 8 == 0) else 1

    @pl.when(d == 0)
    def _():
        lax.fori_loop(0, nk, step, (zero, zero), unroll=unroll)

    @pl.when(d == 1)
    def _():
        hc = lax.fori_loop(0, nk_ctx, lambda t, c: step(nk_ctx - 1 - t, c), (zero, zero), unroll=unroll)
        lax.fori_loop(0, nk - nk_ctx, lambda t, c: step(nk - 1 - t, c), hc, unroll=unroll)


def _s5n_out_body(u_ref, m_ref, h_ref, v_ref, e_ref, o_ref, m_exp, v_exp):
    @pl.when(pl.program_id(1) == 0)
    def _():
        for j in range(S5N_CHUNK):
            m_exp[j * LANES:(j + 1) * LANES, :] = _s5n_expand(m_ref[0, j], e_ref[...],
                                                              _S5N_IN_ROW_GROUP, _S5N_OUT_COL_GROUP)
        v_exp[...] = _s5n_expand(v_ref[0], e_ref[...], _S5N_STATE_GROUP, _S5N_OUT_COL_GROUP)

    u_cat = jnp.concatenate([u_ref[:, j, :] for j in range(S5N_CHUNK)], axis=1)
    h_cat = jnp.concatenate([h_ref[d, r].astype(BF16) for d in range(2) for r in range(2)], axis=1)
    acc = (jnp.dot(u_cat, m_exp[...], preferred_element_type=F32)
           + jnp.dot(h_cat, v_exp[...], preferred_element_type=F32))
    y = jax.nn.gelu(acc).astype(o_ref.dtype)
    for i in range(S5N_CHUNK):
        o_ref[:, i, :] = y[:, i * LANES:(i + 1) * LANES]


def _s5n_mix(proj, su_off, params, *, batch, seq, ctx_len, gw):
    m_mat, w_mat, v_mat, a_pow = params
    L, half = S5N_CHUNK, S5N_HALF
    nb = gw // LANES
    nk, nk_ctx = seq // L, ctx_len // L
    assert su_off % LANES == 0 and seq % L == 0 and ctx_len % L == 0
    ub = su_off // LANES
    u3 = proj.reshape(batch * nk, L, proj.shape[1])
    lanes = batch * nb * half
    u_spec = pl.BlockSpec((nk, L, LANES), lambda n, b: (b, 0, ub + n))
    h_spec = pl.BlockSpec((2, 2, nk, half), lambda n, b: (0, 0, 0, b * nb + n))

    e_out, e_st = _s5n_spreaders()
    whole = lambda a: pl.BlockSpec(a.shape, lambda n, b: (0,) * a.ndim)
    s_in = pl.pallas_call(
        _s5n_state_in_body,
        grid=(nb, batch),
        in_specs=[u_spec, pl.BlockSpec((1,) + w_mat.shape[1:], lambda n, b: (n, 0, 0, 0)), whole(e_st)],
        out_specs=h_spec,
        out_shape=jax.ShapeDtypeStruct((2, 2, nk, lanes), F32),
        scratch_shapes=[pltpu.VMEM((L * LANES, 4 * half), BF16)],
        compiler_params=_cp("parallel", "arbitrary"),
        name="s5_state_in",
    )(u3, w_mat, e_st)

    wl = _pick_tile(lanes, S5_SCAN_LANES, LANES)
    part_spec = lambda r: pl.BlockSpec((1, 1, nk, wl), lambda d, w: (d, r, 0, w))
    a_lanes = jnp.tile(a_pow.reshape(2, 2, 1, nb * half), (1, 1, 1, batch))
    h_prev = pl.pallas_call(
        functools.partial(_s5n_scan_body, nk=nk, nk_ctx=nk_ctx),
        grid=(2, lanes // wl),
        in_specs=[part_spec(0), part_spec(1), pl.BlockSpec((1, 2, 1, wl), lambda d, w: (d, 0, 0, w))],
        out_specs=pl.BlockSpec((1, 2, nk, wl), lambda d, w: (d, 0, 0, w)),
        out_shape=jax.ShapeDtypeStruct((2, 2, nk, lanes), F32),
        compiler_params=_cp("parallel", "parallel"),
        name="s5_scan",
    )(s_in, s_in, a_lanes)

    y3 = pl.pallas_call(
        _s5n_out_body,
        grid=(nb, batch),
        in_specs=[u_spec,
                  pl.BlockSpec((1,) + m_mat.shape[1:], lambda n, b: (n, 0, 0, 0)),
                  h_spec,
                  pl.BlockSpec((1,) + v_mat.shape[1:], lambda n, b: (n, 0, 0)),
                  whole(e_out)],
        out_specs=pl.BlockSpec((nk, L, LANES), lambda n, b: (b, 0, n)),
        out_shape=jax.ShapeDtypeStruct((batch * nk, L, gw), BF16),
        scratch_shapes=[pltpu.VMEM((L * LANES, L * LANES), BF16), pltpu.VMEM((4 * half, L * LANES), BF16)],
        compiler_params=_cp("parallel", "arbitrary"),
        name="s5_out",
    )(u3, m_mat, h_prev, v_mat, e_out)
    return y3.reshape(batch * seq, gw)


def _expert_changed(te_ref):
    i = pl.program_id(1)
    return (i == 0) | (te_ref[i] != te_ref[jnp.maximum(i - 1, 0)])


def _moe_up_body(te_ref, x_ref, wg_ref, wu_ref, o_ref, wg_bf, wu_bf):
    @pl.when(_expert_changed(te_ref))
    def _():
        wg_bf[...] = wg_ref[0, 0].astype(BF16)
        wu_bf[...] = wu_ref[0, 0].astype(BF16)

    x = x_ref[...]
    a = jnp.dot(x, wg_bf[...], preferred_element_type=F32)
    b = jnp.dot(x, wu_bf[...], preferred_element_type=F32)
    o_ref[...] = (a * _sigmoid(a) * b).astype(o_ref.dtype)


def _moe_down_body(te_ref, h_ref, w_ref, rw_ref, o_ref, w_bf):
    @pl.when(_expert_changed(te_ref))
    def _():
        w_bf[...] = w_ref[0, 0].astype(BF16)

    y = jnp.dot(h_ref[...], w_bf[...], preferred_element_type=F32)
    o_ref[...] = (rw_ref[...] * y).astype(o_ref.dtype)


def _moe_down_into_body(te_ref, h_ref, w_ref, rw_ref, prev_ref, o_ref, w_bf):
    _moe_down_body(te_ref, h_ref, w_ref, rw_ref, o_ref, w_bf)


def _moe_combine_body(x_ref, y0_ref, y1_ref, gate_ref, o_ref, *, tm, tiles_per_batch, ctx_len):
    i = pl.program_id(0)
    is_ctx = (i % tiles_per_batch) * tm < ctx_len
    gate = jnp.where(is_ctx, gate_ref[0, 0:1, :], gate_ref[0, 1:2, :])
    o_ref[...] = x_ref[...] + gate * (y0_ref[...].astype(F32) + y1_ref[...].astype(F32))


def _moe_final_body(x_ref, y0_ref, y1_ref, gate_ref, g_ref, o_ref):
    x = x_ref[...] + gate_ref[0, 1:2, :] * (y0_ref[...].astype(F32) + y1_ref[...].astype(F32))
    o_ref[...] = x * lax.rsqrt(jnp.mean(x * x, axis=-1, keepdims=True) + NORM_EPS) * g_ref[...]


def _route(logits):
    assert MOE_TOPK == 2
    g_logit = logits[:, :MOE_GROUPS]
    g_prob = jax.nn.softmax(g_logit, axis=-1)
    g_idx = jnp.argmax(g_prob, axis=-1)
    g_p = jnp.max(g_prob, axis=-1)
    e_logit = logits[:, MOE_GROUPS:MOE_GROUPS + MOE_GROUPS * MOE_PER_GROUP]
    e_logit = e_logit.reshape(-1, MOE_GROUPS, MOE_PER_GROUP)
    sel = (jnp.arange(MOE_GROUPS)[None, :] == g_idx[:, None])[:, :, None]
    e_logit = jnp.sum(jnp.where(sel, e_logit, 0.0), axis=1)
    e_prob = jax.nn.softmax(e_logit, axis=-1)
    i0 = jnp.argmax(e_prob, axis=-1)
    p0 = jnp.max(e_prob, axis=-1)
    rest = jnp.where(jnp.arange(MOE_PER_GROUP)[None, :] == i0[:, None], -1.0, e_prob)
    i1 = jnp.argmax(rest, axis=-1)
    p1 = jnp.max(rest, axis=-1)
    e_p = jnp.stack([p0, p1], axis=-1)
    w = g_p[:, None] * e_p / jnp.sum(e_p, axis=-1, keepdims=True)
    ids = g_idx[:, None] * MOE_PER_GROUP + jnp.stack([i0, i1], axis=-1)
    return ids.astype(jnp.int32), w


def _moe(h, logits, x, gate, w_gate, w_up, w_down, layer, *, rows_per_batch, ctx_len, final_g=None):
    t, d = h.shape
    _, n_exp, _, dff = w_gate.shape
    tile = MOE_TILE
    ids, wts = _route(logits)
    flat_e = ids.reshape(-1)
    onehot = (flat_e[:, None] == jnp.arange(n_exp)[None, :]).astype(jnp.int32)
    counts = onehot.sum(0)
    rank = jnp.take_along_axis(jnp.cumsum(onehot, axis=0) - onehot, flat_e[:, None], axis=1)[:, 0]
    padded = (counts + tile - 1) // tile * tile
    starts = jnp.cumsum(padded) - padded
    pos = starts[flat_e] + rank
    n_rows = (t * MOE_TOPK // tile + n_exp) * tile
    n_tiles = n_rows // tile
    token = (jnp.arange(t * MOE_TOPK, dtype=jnp.int32) // MOE_TOPK).astype(F32)
    table = jnp.zeros((n_rows, 2), F32).at[pos].set(jnp.stack([token, wts.reshape(-1)], axis=1),
                                                    unique_indices=True, mode="promise_in_bounds")
    src = table[:, 0].astype(jnp.int32)
    row_w = table[:, 1]
    tile_start = jnp.arange(n_tiles, dtype=jnp.int32) * tile
    ends = starts + padded
    tile_e = jnp.minimum(jnp.sum(tile_start[:, None] >= ends[None, :], axis=1), n_exp - 1).astype(jnp.int32)

    n_chunks = max(c for c in (4, 2, 1) if n_tiles % c == 0)
    tpc = n_tiles // n_chunks
    tn_up = _pick_tile(dff, MOE_UP_COL_TILE, LANES)
    tn_dn = _pick_tile(d, MOE_DOWN_COL_TILE, LANES)
    w_up_spec = pl.BlockSpec((1, 1, d, tn_up), lambda j, i, te: (layer, te[i], 0, j))
    ys = None
    for c in range(n_chunks):
        rows_c = slice(c * tpc * tile, (c + 1) * tpc * tile)
        te_c = tile_e[c * tpc:(c + 1) * tpc]
        xs = h.at[src[rows_c]].get(mode="promise_in_bounds")
        hid = pl.pallas_call(
            _moe_up_body,
            grid_spec=pltpu.PrefetchScalarGridSpec(
                num_scalar_prefetch=1,
                grid=(dff // tn_up, tpc),
                in_specs=[pl.BlockSpec((tile, d), lambda j, i, te: (i, 0)), w_up_spec, w_up_spec],
                out_specs=pl.BlockSpec((tile, tn_up), lambda j, i, te: (i, j)),
                scratch_shapes=[pltpu.VMEM((d, tn_up), BF16), pltpu.VMEM((d, tn_up), BF16)]),
            out_shape=jax.ShapeDtypeStruct((tpc * tile, dff), BF16),
            compiler_params=_cp("arbitrary", "arbitrary"),
            name="moe_up",
        )(te_c, xs, w_gate, w_up)
        first = ys is None
        ys = pl.pallas_call(
            _moe_down_body if first else _moe_down_into_body,
            grid_spec=pltpu.PrefetchScalarGridSpec(
                num_scalar_prefetch=1,
                grid=(d // tn_dn, tpc),
                in_specs=[pl.BlockSpec((tile, dff), lambda j, i, te: (i, 0)),
                          pl.BlockSpec((1, 1, dff, tn_dn), lambda j, i, te: (layer, te[i], 0, j)),
                          pl.BlockSpec((tile, 1), lambda j, i, te: (i, 0))]
                         + ([] if first else [pl.BlockSpec(memory_space=pl.ANY)]),
                out_specs=pl.BlockSpec((tile, tn_dn), lambda j, i, te, c=c: (c * tpc + i, j)),
                scratch_shapes=[pltpu.VMEM((dff, tn_dn), BF16)]),
            out_shape=jax.ShapeDtypeStruct((n_rows, d), BF16),
            input_output_aliases={} if first else {4: 0},
            compiler_params=_cp("arbitrary", "arbitrary"),
            name="moe_down",
        )(te_c, hid, w_down, row_w[rows_c].reshape(tpc * tile, 1), *(() if first else (ys,)))

    pos2 = pos.reshape(t, MOE_TOPK)
    y0 = ys.at[pos2[:, 0]].get(mode="promise_in_bounds")
    y1 = ys.at[pos2[:, 1]].get(mode="promise_in_bounds")
    tm = _pick_tile(math.gcd(rows_per_batch, ctx_len), ROW_TILE, 8)
    tpb = rows_per_batch // tm
    if final_g is not None:
        batch = t // rows_per_batch
        ctx_tiles, lat_tiles = ctx_len // tm, (rows_per_batch - ctx_len) // tm
        lat_spec = pl.BlockSpec((tm, d), lambda b, i: (b * tpb + ctx_tiles + i, 0))
        out = pl.pallas_call(
            _moe_final_body,
            grid=(batch, lat_tiles),
            in_specs=[lat_spec, lat_spec, lat_spec, pl.BlockSpec((1, 2, d), lambda b, i: (b, 0, 0)),
                      pl.BlockSpec((1, d), lambda b, i: (0, 0))],
            out_specs=pl.BlockSpec((tm, d), lambda b, i: (b * lat_tiles + i, 0)),
            out_shape=jax.ShapeDtypeStruct((batch * lat_tiles * tm, d), F32),
            compiler_params=_cp("parallel", "parallel"),
            name="moe_combine_final",
        )(x, y0, y1, gate, final_g.reshape(1, d).astype(F32))
        return out.reshape(batch, lat_tiles * tm, d)
    row_spec = pl.BlockSpec((tm, d), lambda i: (i, 0))
    return pl.pallas_call(
        functools.partial(_moe_combine_body, tm=tm, tiles_per_batch=tpb, ctx_len=ctx_len),
        grid=(t // tm,),
        in_specs=[row_spec, row_spec, row_spec, pl.BlockSpec((1, 2, d), lambda i: (i // tpb, 0, 0))],
        out_specs=row_spec,
        out_shape=jax.ShapeDtypeStruct((t, d), F32),
        compiler_params=_cp("parallel"),
        name="moe_combine",
    )(x, y0, y1, gate)


def kernel(x, c, ctx, c_ctx, ada_w, ada_b, norm_mix, norm_ffn, w_in, w_out, diff_lambda, diff_subln,
           s5_a_re, s5_a_im, s5_log_dt, s5_b_re, s5_b_im, s5_c_re, s5_c_im, s5_d, s5_glu_w, s5_glu_b,
           mla_q_norm, mla_kv_norm, mla_w_uq, mla_w_ukv, ret_decay, ret_norm,
           moe_wg, moe_bg, moe_we, moe_be, moe_w_gate, moe_w_up, moe_w_down, final_norm):
    batch, n_lat, d = x.shape
    ctx_len = ctx.shape[1]
    depth = ada_w.shape[0]
    seq = ctx_len + n_lat
    rows = batch * seq
    gw = d // 4
    heads = gw // LANES
    q_rank, kv_rank = 3 * d // 16, d // 16
    ret_qk = heads * RET_K
    n_route = MOE_GROUPS + MOE_GROUPS * MOE_PER_GROUP
    assert heads % 2 == 0 and ctx_len % RET_CHUNK == 0 and n_lat % RET_CHUNK == 0

    splits = (gw, gw, gw, gw, q_rank, kv_rank, ROPE_DIM, ret_qk, ret_qk, gw, gw)
    offs = [0]
    for s_ in splits:
        offs.append(offs[-1] + s_)
    names = ("dq", "dk", "dv", "su", "mcq", "mckv", "mkr", "rq", "rk", "rv", "rg")
    src_col = {n_: (offs[i], offs[i + 1]) for i, n_ in enumerate(names)}
    order = ("dq", "dk", "rq", "rk", "dv", "su", "rv", "rg", "mcq", "mckv")
    col = {}
    pos = 0
    for n_ in order:
        col[n_] = pos
        pos += src_col[n_][1] - src_col[n_][0]
    n_main = pos
    n_rope = col["dv"]
    uq_cols = jnp.arange(heads * (MLA_NOPE + ROPE_DIM)).reshape(heads, MLA_NOPE + ROPE_DIM)
    uq_perm = jnp.concatenate([uq_cols[:, :MLA_NOPE].reshape(-1), uq_cols[:, MLA_NOPE:].reshape(-1)])

    tables = _rope_tables(n_lat, ctx_len)
    log2e = math.log2(math.e)
    rope_scale = jnp.ones((n_rope,), F32).at[col["rk"]:col["rk"] + ret_qk].set(RET_K ** -0.5)
    rope_scale = rope_scale.at[col["dq"]:col["dq"] + gw].set(DIFF_HEAD_DIM ** -0.5 * log2e)
    mla_q_scale = (MLA_NOPE + ROPE_DIM) ** -0.5 * log2e
    main_scale = jnp.concatenate([rope_scale, jnp.ones((n_main - n_rope,), F32)])

    cond = jnp.concatenate([c_ctx[None, :], c], axis=0)
    cond = jnp.pad(cond * _sigmoid(cond), ((0, 8 - (batch + 1) % 8 if (batch + 1) % 8 else 0), (0, 0)))

    mod_all = _ada_mod(cond, ada_w, ada_b)

    xa = jnp.concatenate([ctx, x], axis=1).reshape(rows, d)
    tm_big = _pick_tile(seq, MM_ROW_TILE)
    tn = lambda n_: _pick_tile(n_, MM_COL_TILE, LANES)

    for l in range(depth):
        lam_init = 0.8 - 0.6 * math.exp(-0.3 * l)
        mod = mod_all[l].reshape(cond.shape[0], 6, d)
        mods = [jnp.stack([jnp.broadcast_to(mod[0, i], (batch, d)), mod[1:batch + 1, i]], axis=1)
                for i in range(6)]

        w_main = jnp.concatenate([w_in[l, :, src_col[n_][0]:src_col[n_][1]] for n_ in order],
                                 axis=1).astype(BF16)
        w_kr = jnp.concatenate([w_in[l, :, src_col["mkr"][0]:src_col["mkr"][1]]] * (LANES // ROPE_DIM),
                               axis=1).astype(BF16)

        h = _norm_mod(xa, norm_mix[l], mods[0], mods[1], rows_per_batch=seq, ctx_len=ctx_len)
        proj = _mm([h], w_main, name="in_proj", out_dtype=BF16, tm=tm_big,
                   tn=tn(math.gcd(n_main, n_rope)), rope=(tables, main_scale, (0, n_rope)),
                   rows_per_batch=seq)
        krr = _mm([h], w_kr, name="in_proj_kr", out_dtype=BF16, tm=tm_big, tn=LANES,
                  rope=(tables, jnp.ones((LANES,), F32), (0, LANES)), rows_per_batch=seq)
        qk = proj

        lv = diff_lambda[l].astype(F32)
        lam = jnp.exp(jnp.sum(lv[0] * lv[1])) - jnp.exp(jnp.sum(lv[2] * lv[3])) + lam_init
        a_out = _diff_attn(qk, proj, lam, diff_subln[l], batch=batch, seq=seq, ctx_len=ctx_len,
                           heads=heads, q_blk=col["dq"] // LANES, k_blk=col["dk"] // LANES,
                           v_blk=col["dv"] // LANES, post=1.0 - lam_init)

        s5p = _s5n_params(s5_a_re[l], s5_a_im[l], s5_log_dt[l], s5_b_re[l], s5_b_im[l],
                          s5_c_re[l], s5_c_im[l], s5_d[l])
        s_act = _s5n_mix(proj, col["su"], s5p, batch=batch, seq=seq, ctx_len=ctx_len, gw=gw)
        s_out = _mm([s_act], s5_glu_w[l].astype(BF16), name="s5_glu", out_dtype=BF16, tm=tm_big, tn=tn(gw),
                    bias=s5_glu_b[l], glu_in=s_act)

        cq = proj[:, col["mcq"]:col["mcq"] + q_rank]
        ckv = proj[:, col["mckv"]:col["mckv"] + kv_rank]
        w_uq = (mla_w_uq[l][:, uq_perm] * mla_q_scale).astype(BF16)
        n_qn, n_q = heads * MLA_NOPE, heads * (MLA_NOPE + ROPE_DIM)
        q_up = _mm([cq], w_uq, name="mla_q_up", out_dtype=BF16, tm=tm_big, tn=tn(math.gcd(n_qn, n_q)),
                   norm_g=mla_q_norm[l], rope=(tables, jnp.ones((n_q,), F32), (n_qn, n_q)),
                   rows_per_batch=seq)
        kv_up = _mm([ckv], mla_w_ukv[l].astype(BF16), name="mla_kv_up", out_dtype=BF16, tm=tm_big,
                    tn=tn(mla_w_ukv.shape[2]), norm_g=mla_kv_norm[l])
        m_out = _mla_attn(q_up, kv_up, krr, batch=batch, seq=seq, ctx_len=ctx_len, heads=heads)

        log_g = jax.nn.log_sigmoid(ret_decay[l].astype(F32))
        r_out = _retention(qk, proj, log_g, ret_norm[l], batch=batch, seq=seq, ctx_len=ctx_len,
                           heads=heads, q_off=col["rq"], k_off=col["rk"], v_off=col["rv"], g_off=col["rg"])

        xa = _mm([a_out, s_out, m_out, r_out], w_out[l].astype(BF16), name="out_proj", out_dtype=F32, tm=tm_big, tn=tn(d),
                 res=xa, gate=mods[2], rows_per_batch=seq, ctx_len=ctx_len)

        w_r = jnp.concatenate([moe_wg[l], moe_we[l]], axis=1).astype(F32)
        w_r = jnp.pad(w_r, ((0, 0), (0, LANES - n_route)))
        w_r_hi = w_r.astype(BF16)
        w_r_lo = (w_r - w_r_hi.astype(F32)).astype(BF16)
        b_r = jnp.pad(jnp.concatenate([moe_bg[l], moe_be[l]]).astype(F32), (0, LANES - n_route))
        h, logits = _norm_mod(xa, norm_ffn[l], mods[3], mods[4], rows_per_batch=seq, ctx_len=ctx_len,
                              router=(w_r_hi, w_r_lo, b_r.reshape(1, LANES)))
        xa = _moe(h, logits, xa, mods[5], moe_w_gate, moe_w_up, moe_w_down, l,
                  rows_per_batch=seq, ctx_len=ctx_len, final_g=final_norm if l == depth - 1 else None)
    return xa
```

```python
import functools
import math

import jax
import jax.numpy as jnp
import numpy as np
from jax import lax
from jax.experimental import pallas as pl
from jax.experimental.pallas import tpu as pltpu

BF16 = jnp.bfloat16
F32 = jnp.float32

V7X_VMEM_BYTES = 64 * 2**20
VMEM_LIMIT = V7X_VMEM_BYTES - 12 * 2**20
LANES = 128

GRID_W = 64
ROPE_DIM = 64
ROPE_BASE = 10000.0
NORM_EPS = 1e-6
DIFF_HEAD_DIM = 64
S5_CH = 16
S5_STATE = 64
MLA_NOPE = 128
MLA_V = 128
RET_K = 64
RET_V = 128
RET_CHUNK = 128
MOE_GROUPS = 4
MOE_PER_GROUP = 4
MOE_TOPK = 2
ROPE_QUARTER = ROPE_DIM // 4

MM_ROW_TILE = 1088
MM_COL_TILE = 512
ROW_TILE = 256
ATTN_Q_TILE = 512
S5_SCAN_LANES = 2048
MOE_TILE = 256
MOE_UP_COL_TILE = 512
MOE_DOWN_COL_TILE = 4096


def _cp(*sem):
    return pltpu.CompilerParams(dimension_semantics=sem, vmem_limit_bytes=VMEM_LIMIT)


def _pick_tile(n, target, mult=16):
    best = None
    for t in range(mult, min(n, target) + 1, mult):
        if n % t == 0:
            best = t
    assert best is not None, (n, target)
    return best


def _sigmoid(x):
    return 1.0 / (1.0 + jnp.exp(-x))


def _rotate(x, cos, sin, swap):
    reps = x.shape[1] // LANES
    tile = lambda t: jnp.tile(t, (1, reps))
    partner = jnp.dot(x.astype(BF16), swap, preferred_element_type=F32)
    return x * tile(cos) + partner * tile(sin)


def _rope_swap(width):
    lane = np.arange(width)
    first = (lane // ROPE_QUARTER) % 2 == 0
    src = np.where(first, lane + ROPE_QUARTER, lane - ROPE_QUARTER)
    swap = np.zeros((width, width), np.float32)
    swap[src, lane] = np.where(first, -1.0, 1.0)
    return jnp.asarray(swap, BF16)


def _mm_body(*refs, nx, ksizes, has_norm, has_bias, epilogue, tm, tiles_per_batch, ctx_len, rope_tiles):
    x_refs = refs[:nx]
    w_ref = refs[nx]
    idx = nx + 1
    g_ref = b_ref = e_ref = res_ref = gate_ref = None
    if has_norm:
        g_ref = refs[idx]; idx += 1
    if has_bias:
        b_ref = refs[idx]; idx += 1
    if epilogue == "glu":
        e_ref = refs[idx]; idx += 1
    if epilogue == "resgate":
        res_ref, gate_ref = refs[idx], refs[idx + 1]; idx += 2
    if epilogue == "rope":
        cos_ref, sin_ref, swap_ref, cs_ref = refs[idx:idx + 4]; idx += 4
    o_ref = refs[idx]

    acc = None
    off = 0
    for xr, ks in zip(x_refs, ksizes):
        x = xr[...]
        if has_norm:
            xf = x.astype(F32)
            xf = xf * lax.rsqrt(jnp.mean(xf * xf, axis=-1, keepdims=True) + NORM_EPS)
            x = xf * g_ref[...]
        x = x.astype(BF16)
        w = w_ref[off:off + ks, :].astype(BF16)
        part = jnp.dot(x, w, preferred_element_type=F32)
        acc = part if acc is None else acc + part
        off += ks
    if has_bias:
        acc = acc + b_ref[...]
    if epilogue == "glu":
        acc = e_ref[...].astype(F32) * _sigmoid(acc)
    elif epilogue == "resgate":
        i = pl.program_id(0)
        row = (i % tiles_per_batch) * tm + lax.broadcasted_iota(jnp.int32, (tm, 1), 0)
        gate = jnp.where(row < ctx_len, gate_ref[0, 0:1, :], gate_ref[0, 1:2, :])
        acc = res_ref[...] + gate * acc
    if epilogue == "rope":
        j = pl.program_id(1)
        roped = (j >= rope_tiles[0]) & (j < rope_tiles[1])

        @pl.when(roped)
        def _():
            y = _rotate(acc, cos_ref[...], sin_ref[...], swap_ref[...]) * cs_ref[...]
            o_ref[...] = y.astype(o_ref.dtype)

        @pl.when(jnp.logical_not(roped))
        def _():
            o_ref[...] = acc.astype(o_ref.dtype)
    else:
        o_ref[...] = acc.astype(o_ref.dtype)


def _mm(xs, w, *, name, out_dtype, tm, tn, norm_g=None, bias=None, glu_in=None, res=None, gate=None,
        rope=None, rows_per_batch=None, ctx_len=0):
    m = xs[0].shape[0]
    ksizes = tuple(x.shape[1] for x in xs)
    k, n = w.shape
    assert sum(ksizes) == k and m % tm == 0 and n % tn == 0
    epilogue = ("glu" if glu_in is not None else "resgate" if res is not None
                else "rope" if rope is not None else None)
    tiles_per_batch = (rows_per_batch // tm) if rows_per_batch else 1
    rope_tiles = None
    in_specs = [pl.BlockSpec((tm, ks), lambda i, j: (i, 0)) for ks in ksizes]
    in_specs.append(pl.BlockSpec((k, tn), lambda i, j: (0, j)))
    args = list(xs) + [w]
    if norm_g is not None:
        in_specs.append(pl.BlockSpec((1, k), lambda i, j: (0, 0)))
        args.append(norm_g.reshape(1, k).astype(F32))
    if bias is not None:
        in_specs.append(pl.BlockSpec((1, tn), lambda i, j: (0, j)))
        args.append(bias.reshape(1, n).astype(F32))
    if epilogue == "glu":
        in_specs.append(pl.BlockSpec((tm, tn), lambda i, j: (i, j)))
        args.append(glu_in)
    if epilogue == "resgate":
        tpb = tiles_per_batch
        in_specs.append(pl.BlockSpec((tm, tn), lambda i, j: (i, j)))
        in_specs.append(pl.BlockSpec((1, 2, tn), lambda i, j: (i // tpb, 0, j)))
        args += [res, gate]
    if epilogue == "rope":
        tables, col_scale, (lo, hi) = rope
        assert lo % tn == 0 and hi % tn == 0
        rope_tiles = (lo // tn, hi // tn)
        tpb = tiles_per_batch
        in_specs += [pl.BlockSpec((tm, LANES), lambda i, j: (i % tpb, 0))] * 2
        in_specs.append(pl.BlockSpec((tn, tn), lambda i, j: (0, 0)))
        in_specs.append(pl.BlockSpec((1, tn), lambda i, j: (0, j)))
        args += list(tables) + [_rope_swap(tn), col_scale.reshape(1, n).astype(F32)]
    body = functools.partial(_mm_body, nx=len(xs), ksizes=ksizes, has_norm=norm_g is not None,
                             has_bias=bias is not None, epilogue=epilogue, tm=tm,
                             tiles_per_batch=tiles_per_batch, ctx_len=ctx_len, rope_tiles=rope_tiles)
    return pl.pallas_call(
        body,
        grid=(m // tm, n // tn),
        in_specs=in_specs,
        out_specs=pl.BlockSpec((tm, tn), lambda i, j: (i, j)),
        out_shape=jax.ShapeDtypeStruct((m, n), out_dtype),
        compiler_params=_cp("parallel", "arbitrary"),
        name=name,
    )(*args)


def _ada_body(c_ref, w_ref, b_ref, o_ref):
    acc = jnp.dot(c_ref[...].astype(BF16), w_ref[0].astype(BF16), preferred_element_type=F32)
    o_ref[0] = acc + b_ref[0]


def _ada_mod(cond, ada_w, ada_b):
    depth, d, n6 = ada_w.shape
    rows = cond.shape[0]
    tn = _pick_tile(n6, MM_COL_TILE, LANES)
    return pl.pallas_call(
        _ada_body,
        grid=(depth, n6 // tn),
        in_specs=[pl.BlockSpec((rows, d), lambda l, j: (0, 0)),
                  pl.BlockSpec((1, d, tn), lambda l, j: (l, 0, j)),
                  pl.BlockSpec((1, 1, tn), lambda l, j: (l, 0, j))],
        out_specs=pl.BlockSpec((1, rows, tn), lambda l, j: (l, 0, j)),
        out_shape=jax.ShapeDtypeStruct((depth, rows, n6), F32),
        compiler_params=_cp("parallel", "arbitrary"),
        name="ada_mod",
    )(cond, ada_w, ada_b.reshape(depth, 1, n6).astype(F32))


def _norm_mod_body(*refs, tm, tiles_per_batch, ctx_len, router):
    if router:
        x_ref, g_ref, sh_ref, sc_ref, whi_ref, wlo_ref, br_ref, h_ref, lg_ref = refs
    else:
        x_ref, g_ref, sh_ref, sc_ref, h_ref = refs
    i = pl.program_id(0)
    x = x_ref[...]
    y = x * lax.rsqrt(jnp.mean(x * x, axis=-1, keepdims=True) + NORM_EPS) * g_ref[...]
    is_ctx = (i % tiles_per_batch) * tm < ctx_len
    sh = jnp.where(is_ctx, sh_ref[0, 0:1, :], sh_ref[0, 1:2, :])
    sc = jnp.where(is_ctx, sc_ref[0, 0:1, :], sc_ref[0, 1:2, :])
    h = y * (1.0 + sc) + sh
    h_ref[...] = h.astype(BF16)
    if router:
        hi = h.astype(BF16)
        lo = (h - hi.astype(F32)).astype(BF16)
        lg = jnp.dot(hi, whi_ref[...], preferred_element_type=F32)
        lg = lg + jnp.dot(hi, wlo_ref[...], preferred_element_type=F32)
        lg = lg + jnp.dot(lo, whi_ref[...], preferred_element_type=F32)
        lg_ref[...] = lg + br_ref[...]


def _norm_mod(x, g, shift, scale, *, rows_per_batch, ctx_len, router=None):
    m, d = x.shape
    tm = _pick_tile(math.gcd(rows_per_batch, ctx_len), ROW_TILE, 8)
    tpb = rows_per_batch // tm
    in_specs = [
        pl.BlockSpec((tm, d), lambda i: (i, 0)),
        pl.BlockSpec((1, d), lambda i: (0, 0)),
        pl.BlockSpec((1, 2, d), lambda i: (i // tpb, 0, 0)),
        pl.BlockSpec((1, 2, d), lambda i: (i // tpb, 0, 0)),
    ]
    args = [x, g.reshape(1, d), shift, scale]
    out_specs = [pl.BlockSpec((tm, d), lambda i: (i, 0))]
    out_shape = [jax.ShapeDtypeStruct((m, d), BF16)]
    if router is not None:
        whi, wlo, br = router
        in_specs += [pl.BlockSpec((d, LANES), lambda i: (0, 0)),
                     pl.BlockSpec((d, LANES), lambda i: (0, 0)),
                     pl.BlockSpec((1, LANES), lambda i: (0, 0))]
        args += [whi, wlo, br]
        out_specs.append(pl.BlockSpec((tm, LANES), lambda i: (i, 0)))
        out_shape.append(jax.ShapeDtypeStruct((m, LANES), F32))
    body = functools.partial(_norm_mod_body, tm=tm, tiles_per_batch=tpb, ctx_len=ctx_len,
                             router=router is not None)
    outs = pl.pallas_call(body, grid=(m // tm,), in_specs=in_specs, out_specs=out_specs,
                          out_shape=out_shape, compiler_params=_cp("parallel"),
                          name="norm_mod_router" if router is not None else "norm_mod")(*args)
    return outs if router is not None else outs[0]


def _rope_tables(n_lat, ctx_len):
    rows = n_lat // GRID_W
    row = jnp.repeat(jnp.arange(rows, dtype=F32), GRID_W)
    col = jnp.tile(jnp.arange(GRID_W, dtype=F32), rows)
    quarter = ROPE_QUARTER
    inv = ROPE_BASE ** (-jnp.arange(quarter, dtype=F32) / quarter)
    ar = row[:, None] * inv
    ac = col[:, None] * inv
    ang = jnp.concatenate([ar, ar, ac, ac], axis=-1)
    ang = jnp.concatenate([jnp.zeros((ctx_len, ROPE_DIM), F32), ang], axis=0)
    ang = jnp.tile(ang, (1, LANES // ROPE_DIM))
    return jnp.cos(ang), jnp.sin(ang)


ATTN_ALIGN = 256


def _softmax_pv(q, k_ref, va_ref, bounds):
    ms, ovs = [], []
    for lo, hi in bounds:
        s = lax.dot_general(q, k_ref[lo:hi, :], (((1,), (1,)), ((), ())), preferred_element_type=F32)
        m = jnp.max(s, axis=-1, keepdims=True)
        e = jnp.exp2(s - m).astype(BF16)
        ovs.append(jnp.dot(e, va_ref[lo:hi, :], preferred_element_type=F32))
        ms.append(m)
    m_all = functools.reduce(jnp.maximum, ms)
    acc = sum(ov * jnp.exp2(m - m_all) for m, ov in zip(ms, ovs))
    return acc[:, :LANES] / acc[:, LANES:]


def _key_chunks(n_keys):
    if n_keys < 2 * ATTN_ALIGN:
        return ((0, n_keys),)
    half = (n_keys // ATTN_ALIGN + 1) // 2 * ATTN_ALIGN
    return ((0, half), (half, n_keys))


def _lat_tile(seq, ctx_len):
    n_lat = seq - ctx_len
    assert ctx_len % ATTN_ALIGN == 0 and n_lat % ATTN_ALIGN == 0
    return _pick_tile(n_lat, ATTN_Q_TILE, ATTN_ALIGN)


def _lat_rows(seq, ctx_len, tq):
    return lambda b, i: pl.multiple_of(b * seq + ctx_len + i * tq, ATTN_ALIGN)


def _diff_attn_body(lam_ref, q_ref, k_ref, v_ref, sub_ref, *rest, post, bounds):
    o_ref, va_ref = rest[-2:]

    def fill():
        va_ref[:, :LANES] = v_ref[...]
        va_ref[:, LANES:] = jnp.ones(v_ref.shape, BF16)

    if len(rest) == 2:
        pl.when(pl.program_id(2) == 0)(fill)
    else:
        fill()

    q = q_ref[...]
    lane = lax.broadcasted_iota(jnp.int32, (1, LANES), 1)
    first = lane < DIFF_HEAD_DIM
    zero = jnp.zeros_like(q)
    q0 = jnp.where(first, q, zero)
    q1 = jnp.where(first, zero, q)
    o = _softmax_pv(q0, k_ref, va_ref, bounds) - lam_ref[0] * _softmax_pv(q1, k_ref, va_ref, bounds)
    o = o * lax.rsqrt(jnp.mean(o * o, axis=-1, keepdims=True) + NORM_EPS) * sub_ref[...] * post
    o_ref[...] = o.astype(o_ref.dtype)


def _diff_attn(qk, proj, lam, subln, *, batch, seq, ctx_len, heads, q_blk, k_blk, v_blk, post):
    tq = _lat_tile(seq, ctx_len)
    rows = _lat_rows(seq, ctx_len, tq)
    smem = pl.BlockSpec(memory_space=pltpu.SMEM)
    args = (lam.reshape(1).astype(F32), qk, qk, proj, subln.reshape(1, LANES).astype(F32))
    out_shape = jax.ShapeDtypeStruct((batch * seq, heads * LANES), BF16)
    elem = (pl.Element(tq), pl.Element(LANES))
    lat = pl.pallas_call(
        functools.partial(_diff_attn_body, post=post, bounds=_key_chunks(seq)),
        grid=(batch, heads, (seq - ctx_len) // tq),
        in_specs=[
            smem,
            pl.BlockSpec(elem, lambda b, h, i: (rows(b, i), pl.multiple_of((q_blk + h) * LANES, LANES))),
            pl.BlockSpec((seq, LANES), lambda b, h, i: (b, k_blk + h)),
            pl.BlockSpec((seq, LANES), lambda b, h, i: (b, v_blk + h)),
            pl.BlockSpec((1, LANES), lambda b, h, i: (0, 0)),
        ],
        out_specs=pl.BlockSpec(elem, lambda b, h, i: (rows(b, i), pl.multiple_of(h * LANES, LANES))),
        out_shape=out_shape,
        scratch_shapes=[pltpu.VMEM((seq, 2 * LANES), BF16)],
        compiler_params=_cp("parallel", "parallel", "arbitrary"),
        name="diff_attn",
    )(*args)
    cpb = seq // ctx_len
    ctx_spec = lambda blk: pl.BlockSpec((ctx_len, LANES), lambda b, h: (b * cpb, blk + h))
    return pl.pallas_call(
        functools.partial(_diff_attn_body, post=post, bounds=_key_chunks(ctx_len)),
        grid=(batch, heads),
        in_specs=[smem, ctx_spec(q_blk), ctx_spec(k_blk), ctx_spec(v_blk),
                  pl.BlockSpec((1, LANES), lambda b, h: (0, 0)), pl.BlockSpec(memory_space=pl.ANY)],
        out_specs=ctx_spec(0),
        out_shape=out_shape,
        scratch_shapes=[pltpu.VMEM((ctx_len, 2 * LANES), BF16)],
        input_output_aliases={5: 0},
        compiler_params=_cp("parallel", "parallel"),
        name="diff_attn_ctx",
    )(*args, lat)


def _mla_attn_body(qn_ref, qr_ref, kn_ref, kr_ref, v_ref, *rest, bounds):
    o_ref, ka_ref, va_ref = rest[-3:]
    h = pl.program_id(1)

    def fill():
        ka_ref[:, :LANES] = kn_ref[...]
        ka_ref[:, LANES:] = kr_ref[...]
        va_ref[:, :LANES] = v_ref[...]
        va_ref[:, LANES:] = jnp.ones(v_ref.shape, BF16)

    if len(rest) == 3:
        pl.when(pl.program_id(2) == 0)(fill)
    else:
        fill()

    qr = qr_ref[...]
    lane = lax.broadcasted_iota(jnp.int32, (1, LANES), 1)
    mine = (lane < ROPE_DIM) == (h % 2 == 0)
    qr = jnp.where(mine, qr, jnp.zeros_like(qr))
    q = jnp.concatenate([qn_ref[...], qr], axis=1)
    o_ref[...] = _softmax_pv(q, ka_ref, va_ref, bounds).astype(o_ref.dtype)


def _mla_attn(q_up, kv_up, k_rope, *, batch, seq, ctx_len, heads):
    tq = _lat_tile(seq, ctx_len)
    rows = _lat_rows(seq, ctx_len, tq)
    args = (q_up, q_up, kv_up, k_rope, kv_up)
    out_shape = jax.ShapeDtypeStruct((batch * seq, heads * LANES), BF16)
    elem = (pl.Element(tq), pl.Element(LANES))
    lat = pl.pallas_call(
        functools.partial(_mla_attn_body, bounds=_key_chunks(seq)),
        grid=(batch, heads, (seq - ctx_len) // tq),
        in_specs=[
            pl.BlockSpec(elem, lambda b, h, i: (rows(b, i), pl.multiple_of(h * LANES, LANES))),
            pl.BlockSpec(elem, lambda b, h, i: (rows(b, i), pl.multiple_of((heads + h // 2) * LANES, LANES))),
            pl.BlockSpec((seq, LANES), lambda b, h, i: (b, 2 * h)),
            pl.BlockSpec((seq, LANES), lambda b, h, i: (b, 0)),
            pl.BlockSpec((seq, LANES), lambda b, h, i: (b, 2 * h + 1)),
        ],
        out_specs=pl.BlockSpec(elem, lambda b, h, i: (rows(b, i), pl.multiple_of(h * LANES, LANES))),
        out_shape=out_shape,
        scratch_shapes=[pltpu.VMEM((seq, 2 * LANES), BF16), pltpu.VMEM((seq, 2 * LANES), BF16)],
        compiler_params=_cp("parallel", "parallel", "arbitrary"),
        name="mla_attn",
    )(*args)
    cpb = seq // ctx_len
    ctx_spec = lambda col: pl.BlockSpec((ctx_len, LANES), lambda b, h: (b * cpb, col(h)))
    return pl.pallas_call(
        functools.partial(_mla_attn_body, bounds=_key_chunks(ctx_len)),
        grid=(batch, heads),
        in_specs=[ctx_spec(lambda h: h), ctx_spec(lambda h: heads + h // 2), ctx_spec(lambda h: 2 * h),
                  ctx_spec(lambda h: 0), ctx_spec(lambda h: 2 * h + 1), pl.BlockSpec(memory_space=pl.ANY)],
        out_specs=ctx_spec(lambda h: h),
        out_shape=out_shape,
        scratch_shapes=[pltpu.VMEM((ctx_len, 2 * LANES), BF16), pltpu.VMEM((ctx_len, 2 * LANES), BF16)],
        input_output_aliases={5: 0},
        compiler_params=_cp("parallel", "parallel"),
        name="mla_attn_ctx",
    )(*args, lat)


def _retention_body(*refs, backward, cs, heads):
    if backward:
        lg_ref, q_ref, k_ref, v_ref, yf_ref, gate_ref, ng_ref, o_ref, s_ref, d_ref, qd_ref, kd_ref = refs
    else:
        lg_ref, q_ref, k_ref, v_ref, o_ref, s_ref, d_ref, qd_ref, kd_ref = refs
    t = pl.program_id(1)

    @pl.when(t == 0)
    def _():
        s_ref[...] = jnp.zeros_like(s_ref)
        pos_r = lax.broadcasted_iota(jnp.int32, (cs, cs), 0).astype(F32)
        pos_c = lax.broadcasted_iota(jnp.int32, (cs, cs), 1).astype(F32)
        pos = lax.broadcasted_iota(jnp.int32, (cs, 1), 0).astype(F32)
        for h in range(heads):
            lg = lg_ref[h]
            if backward:
                diff = pos_c - pos_r
                keep = diff > 0
                qd_ref[h] = jnp.exp(lg * (cs - pos))
                kd_ref[h] = jnp.exp(lg * pos)
            else:
                diff = pos_r - pos_c
                keep = diff >= 0
                qd_ref[h] = jnp.exp(lg * (pos + 1.0))
                kd_ref[h] = jnp.exp(lg * (cs - 1.0 - pos))
            d_ref[h] = jnp.where(keep, jnp.exp(lg * jnp.maximum(diff, 0.0)), 0.0)

    lane = lax.broadcasted_iota(jnp.int32, (1, LANES), 1)
    nt = (((1,), (1,)), ((), ()))
    tn = (((0,), (0,)), ((), ()))
    for h in range(heads):
        blk = slice((h // 2) * LANES, (h // 2 + 1) * LANES)
        col = slice(h * RET_V, (h + 1) * RET_V)
        mine = (lane < RET_K) == (h % 2 == 0)
        q = q_ref[:, blk]
        q = jnp.where(mine, q, jnp.zeros_like(q))
        k = k_ref[:, blk]
        v = v_ref[:, col]
        scores = lax.dot_general(q, k, nt, preferred_element_type=F32) * d_ref[h]
        intra = jnp.dot(scores.astype(BF16), v, preferred_element_type=F32)
        state = s_ref[h]
        q_w = (q.astype(F32) * qd_ref[h]).astype(BF16)
        cross = jnp.dot(q_w, state.astype(BF16), preferred_element_type=F32)
        k_w = (k.astype(F32) * kd_ref[h]).astype(BF16)
        upd = lax.dot_general(k_w, v, tn, preferred_element_type=F32)
        s_ref[h] = jnp.exp(lg_ref[h] * cs) * state + upd
        y = intra + cross
        if backward:
            y = y + yf_ref[:, col]
            y = y * lax.rsqrt(jnp.mean(y * y, axis=-1, keepdims=True) + NORM_EPS) * ng_ref[:, col]
            g = gate_ref[:, col].astype(F32)
            o_ref[:, col] = (g * _sigmoid(g) * y).astype(o_ref.dtype)
        else:
            o_ref[:, col] = y


def _retention(qk, proj, log_g, norm_g, *, batch, seq, ctx_len, heads, q_off, k_off, v_off, g_off):
    cs = RET_CHUNK
    nc, nc_ctx = seq // cs, ctx_len // cs
    wqk, wv = heads * RET_K, heads * RET_V
    assert q_off % wqk == 0 and k_off % wqk == 0 and v_off % wv == 0 and g_off % wv == 0
    smem = pl.BlockSpec(memory_space=pltpu.SMEM)

    def fwd_chunk(t):
        return t

    def bwd_chunk(t):
        return jnp.where(t < nc_ctx, nc_ctx - 1 - t, nc - 1 - (t - nc_ctx))

    def specs(chunk):
        row = lambda b, t: b * nc + chunk(t)
        return (pl.BlockSpec((cs, wqk), lambda b, t: (row(b, t), q_off // wqk)),
                pl.BlockSpec((cs, wqk), lambda b, t: (row(b, t), k_off // wqk)),
                pl.BlockSpec((cs, wv), lambda b, t: (row(b, t), v_off // wv)),
                pl.BlockSpec((cs, wv), lambda b, t: (row(b, t), 0)),
                pl.BlockSpec((cs, wv), lambda b, t: (row(b, t), g_off // wv)))

    scratch = [pltpu.VMEM((heads, LANES, RET_V), F32), pltpu.VMEM((heads, cs, cs), F32),
               pltpu.VMEM((heads, cs, 1), F32), pltpu.VMEM((heads, cs, 1), F32)]
    qs, ks, vs, ys, gs = specs(fwd_chunk)
    y_f = pl.pallas_call(
        functools.partial(_retention_body, backward=False, cs=cs, heads=heads),
        grid=(batch, nc),
        in_specs=[smem, qs, ks, vs],
        out_specs=ys,
        out_shape=jax.ShapeDtypeStruct((batch * seq, wv), F32),
        scratch_shapes=scratch,
        compiler_params=_cp("parallel", "arbitrary"),
        name="retention_fwd",
    )(log_g[0].astype(F32), qk, qk, proj)
    qs, ks, vs, ys, gs = specs(bwd_chunk)
    return pl.pallas_call(
        functools.partial(_retention_body, backward=True, cs=cs, heads=heads),
        grid=(batch, nc),
        in_specs=[smem, qs, ks, vs, ys, gs, pl.BlockSpec((1, wv), lambda b, t: (0, 0))],
        out_specs=ys,
        out_shape=jax.ShapeDtypeStruct((batch * seq, wv), BF16),
        scratch_shapes=scratch,
        compiler_params=_cp("parallel", "arbitrary"),
        name="retention_bwd",
    )(log_g[1].astype(F32), qk, qk, proj, y_f, proj, norm_g.reshape(1, wv).astype(F32))


S5N_CHUNK = 8
S5N_GROUPS = LANES // S5_CH
S5N_HALF = S5N_GROUPS * S5_STATE


def _s5n_params(a_re, a_im, log_dt, b_re, b_im, c_re, c_im, d_skip):
    L, ch, gl = S5N_CHUNK, S5_CH, S5N_GROUPS
    a_re, a_im = a_re.astype(F32), a_im.astype(F32)
    groups = a_re.shape[1]
    nb = groups // gl
    dt = jnp.exp(log_dt.astype(F32))[..., None]
    e = jnp.arange(L + 1, dtype=F32)[:, None, None, None]
    mag = jnp.exp(a_re * dt * e)
    pw_re, pw_im = mag * jnp.cos(a_im * dt * e), mag * jnp.sin(a_im * dt * e)
    ab_re, ab_im = pw_re[1], pw_im[1]
    den = a_re * a_re + a_im * a_im
    f_re = ((ab_re - 1.0) * a_re + ab_im * a_im) / den
    f_im = (ab_im * a_re - (ab_re - 1.0) * a_im) / den
    bb_re = f_re[..., None] * b_re - f_im[..., None] * b_im
    bb_im = f_re[..., None] * b_im + f_im[..., None] * b_re
    c_re, c_im = c_re.astype(F32), c_im.astype(F32)
    hp = lax.Precision.HIGHEST
    idx = jnp.arange(L)

    cp_re = c_re[None] * pw_re[:L, :, :, None, :] - c_im[None] * pw_im[:L, :, :, None, :]
    cp_im = c_re[None] * pw_im[:L, :, :, None, :] + c_im[None] * pw_re[:L, :, :, None, :]
    kmat = (jnp.einsum("ldgcp,dgpk->dglck", cp_re, bb_re, precision=hp)
            - jnp.einsum("ldgcp,dgpk->dglck", cp_im, bb_im, precision=hp))
    lag = idx[None, :] - idx[:, None]

    def toeplitz(k, lg):
        return jnp.where((lg >= 0)[None, :, :, None, None], k[:, jnp.clip(lg, 0, L - 1)], 0.0)

    t = toeplitz(kmat[0], lag) + toeplitz(kmat[1], -lag)
    skip = (idx[:, None] == idx[None, :])[None, :, :, None, None] * (
        jnp.eye(ch, dtype=F32)[None, None, None] * d_skip.astype(F32)[:, None, None, :, None])
    t = (t + skip).reshape(nb, gl, L, L, ch, ch)
    m_mat = t.transpose(0, 2, 1, 5, 3, 4).reshape(nb, L, LANES, L * ch)

    def state_in(d, exps):
        p_re, p_im = pw_re[exps, d], pw_im[exps, d]
        w_re = p_re[..., None] * bb_re[d][None] - p_im[..., None] * bb_im[d][None]
        w_im = p_re[..., None] * bb_im[d][None] + p_im[..., None] * bb_re[d][None]
        return jnp.stack([w_re, w_im], axis=0).transpose(2, 1, 4, 0, 3)

    w_full = jnp.stack([state_in(0, L - 1 - idx), state_in(1, idx)], axis=3)
    w_full = w_full.reshape(nb, gl, L, ch, 2, 2, S5_STATE)
    w_mat = w_full.transpose(0, 2, 1, 3, 4, 5, 6).reshape(nb, L, LANES, 4 * S5_STATE)

    def state_out(d, exps):
        p_re, p_im = pw_re[exps, d], pw_im[exps, d]
        v_re = c_re[d][None] * p_re[:, :, None, :] - c_im[d][None] * p_im[:, :, None, :]
        v_im = c_re[d][None] * p_im[:, :, None, :] + c_im[d][None] * p_re[:, :, None, :]
        return jnp.stack([v_re, -v_im], axis=0).transpose(2, 0, 4, 1, 3)

    v_full = jnp.stack([state_out(0, idx + 1), state_out(1, L - idx)], axis=1)
    v_full = v_full.reshape(nb, gl, 2, 2, S5_STATE, L, ch)
    v_mat = v_full.transpose(0, 2, 3, 1, 4, 5, 6).reshape(nb, 4 * S5N_HALF, L * ch)

    a_pow = jnp.stack([pw_re[L], pw_im[L]], axis=1).reshape(2, 2, nb * S5N_HALF)
    return m_mat.astype(BF16), w_mat.astype(BF16), v_mat.astype(BF16), a_pow


def _s5n_spreaders():
    L, ch, gl, st = S5N_CHUNK, S5_CH, S5N_GROUPS, S5_STATE
    e_out = jnp.einsum("ij,cd->icjd", jnp.eye(L), jnp.eye(ch))
    e_out = jnp.broadcast_to(e_out[:, :, :, None, :], (L, ch, L, gl, ch)).reshape(L * ch, L * LANES)
    e_st = jnp.broadcast_to(jnp.eye(4 * st).reshape(4 * st, 4, 1, st), (4 * st, 4, gl, st))
    return e_out.astype(BF16), e_st.reshape(4 * st, 4 * S5N_HALF).astype(BF16)


def _s5n_expand(compact, spread, row_group, col_group):
    full = jnp.dot(compact, spread, preferred_element_type=F32)
    rows = lax.broadcasted_iota(jnp.int32, full.shape, 0)
    cols = lax.broadcasted_iota(jnp.int32, full.shape, 1)
    return jnp.where(row_group(rows) == col_group(cols), full, 0.0).astype(BF16)


_S5N_IN_ROW_GROUP = lambda r: r // S5_CH
_S5N_STATE_GROUP = lambda c: (c % S5N_HALF) // S5_STATE
_S5N_OUT_COL_GROUP = lambda c: (c % LANES) // S5_CH


def _s5n_state_in_body(u_ref, w_ref, e_ref, o_ref, w_exp):
    @pl.when(pl.program_id(1) == 0)
    def _():
        for j in range(S5N_CHUNK):
            w_exp[j * LANES:(j + 1) * LANES, :] = _s5n_expand(w_ref[0, j], e_ref[...],
                                                              _S5N_IN_ROW_GROUP, _S5N_STATE_GROUP)

    u_cat = jnp.concatenate([u_ref[:, j, :] for j in range(S5N_CHUNK)], axis=1)
    acc = jnp.dot(u_cat, w_exp[...], preferred_element_type=F32)
    for d in range(2):
        for r in range(2):
            lo = (2 * d + r) * S5N_HALF
            o_ref[d, r] = acc[:, lo:lo + S5N_HALF]


def _s5n_scan_body(sr_ref, si_ref, a_ref, o_ref, *, nk, nk_ctx):
    d = pl.program_id(0)
    ar, ai = a_ref[0, 0], a_ref[0, 1]

    def step(k, carry):
        hr, hi = carry
        o_ref[0, 0, pl.ds(k, 1), :] = hr
        o_ref[0, 1, pl.ds(k, 1), :] = hi
        sr = sr_ref[0, 0, pl.ds(k, 1), :]
        si = si_ref[0, 0, pl.ds(k, 1), :]
        return ar * hr - ai * hi + sr, ar * hi + ai * hr + si

    zero = jnp.zeros_like(ar)
    unroll = 8 if (nk % 8 == 0 and nk_ctx % 8 == 0) else 1

    @pl.when(d == 0)
    def _():
        lax.fori_loop(0, nk, step, (zero, zero), unroll=unroll)

    @pl.when(d == 1)
    def _():
        hc = lax.fori_loop(0, nk_ctx, lambda t, c: step(nk_ctx - 1 - t, c), (zero, zero), unroll=unroll)
        lax.fori_loop(0, nk - nk_ctx, lambda t, c: step(nk - 1 - t, c), hc, unroll=unroll)


def _s5n_out_body(u_ref, m_ref, h_ref, v_ref, e_ref, o_ref, m_exp, v_exp):
    @pl.when(pl.program_id(1) == 0)
    def _():
        for j in range(S5N_CHUNK):
            m_exp[j * LANES:(j + 1) * LANES, :] = _s5n_expand(m_ref[0, j], e_ref[...],
                                                              _S5N_IN_ROW_GROUP, _S5N_OUT_COL_GROUP)
        v_exp[...] = _s5n_expand(v_ref[0], e_ref[...], _S5N_STATE_GROUP, _S5N_OUT_COL_GROUP)

    u_cat = jnp.concatenate([u_ref[:, j, :] for j in range(S5N_CHUNK)], axis=1)
    h_cat = jnp.concatenate([h_ref[d, r].astype(BF16) for d in range(2) for r in range(2)], axis=1)
    acc = (jnp.dot(u_cat, m_exp[...], preferred_element_type=F32)
           + jnp.dot(h_cat, v_exp[...], preferred_element_type=F32))
    y = jax.nn.gelu(acc).astype(o_ref.dtype)
    for i in range(S5N_CHUNK):
        o_ref[:, i, :] = y[:, i * LANES:(i + 1) * LANES]


def _s5n_mix(proj, su_off, params, *, batch, seq, ctx_len, gw):
    m_mat, w_mat, v_mat, a_pow = params
    L, half = S5N_CHUNK, S5N_HALF
    nb = gw // LANES
    nk, nk_ctx = seq // L, ctx_len // L
    assert su_off % LANES == 0 and seq % L == 0 and ctx_len % L == 0
    ub = su_off // LANES
    u3 = proj.reshape(batch * nk, L, proj.shape[1])
    lanes = batch * nb * half
    u_spec = pl.BlockSpec((nk, L, LANES), lambda n, b: (b, 0, ub + n))
    h_spec = pl.BlockSpec((2, 2, nk, half), lambda n, b: (0, 0, 0, b * nb + n))

    e_out, e_st = _s5n_spreaders()
    whole = lambda a: pl.BlockSpec(a.shape, lambda n, b: (0,) * a.ndim)
    s_in = pl.pallas_call(
        _s5n_state_in_body,
        grid=(nb, batch),
        in_specs=[u_spec, pl.BlockSpec((1,) + w_mat.shape[1:], lambda n, b: (n, 0, 0, 0)), whole(e_st)],
        out_specs=h_spec,
        out_shape=jax.ShapeDtypeStruct((2, 2, nk, lanes), F32),
        scratch_shapes=[pltpu.VMEM((L * LANES, 4 * half), BF16)],
        compiler_params=_cp("parallel", "arbitrary"),
        name="s5_state_in",
    )(u3, w_mat, e_st)

    wl = _pick_tile(lanes, S5_SCAN_LANES, LANES)
    part_spec = lambda r: pl.BlockSpec((1, 1, nk, wl), lambda d, w: (d, r, 0, w))
    a_lanes = jnp.tile(a_pow.reshape(2, 2, 1, nb * half), (1, 1, 1, batch))
    h_prev = pl.pallas_call(
        functools.partial(_s5n_scan_body, nk=nk, nk_ctx=nk_ctx),
        grid=(2, lanes // wl),
        in_specs=[part_spec(0), part_spec(1), pl.BlockSpec((1, 2, 1, wl), lambda d, w: (d, 0, 0, w))],
        out_specs=pl.BlockSpec((1, 2, nk, wl), lambda d, w: (d, 0, 0, w)),
        out_shape=jax.ShapeDtypeStruct((2, 2, nk, lanes), F32),
        compiler_params=_cp("parallel", "parallel"),
        name="s5_scan",
    )(s_in, s_in, a_lanes)

    y3 = pl.pallas_call(
        _s5n_out_body,
        grid=(nb, batch),
        in_specs=[u_spec,
                  pl.BlockSpec((1,) + m_mat.shape[1:], lambda n, b: (n, 0, 0, 0)),
                  h_spec,
                  pl.BlockSpec((1,) + v_mat.shape[1:], lambda n, b: (n, 0, 0)),
                  whole(e_out)],
        out_specs=pl.BlockSpec((nk, L, LANES), lambda n, b: (b, 0, n)),
        out_shape=jax.ShapeDtypeStruct((batch * nk, L, gw), BF16),
        scratch_shapes=[pltpu.VMEM((L * LANES, L * LANES), BF16), pltpu.VMEM((4 * half, L * LANES), BF16)],
        compiler_params=_cp("parallel", "arbitrary"),
        name="s5_out",
    )(u3, m_mat, h_prev, v_mat, e_out)
    return y3.reshape(batch * seq, gw)


def _expert_changed(te_ref):
    i = pl.program_id(1)
    return (i == 0) | (te_ref[i] != te_ref[jnp.maximum(i - 1, 0)])


def _moe_up_body(te_ref, x_ref, wg_ref, wu_ref, o_ref, wg_bf, wu_bf):
    @pl.when(_expert_changed(te_ref))
    def _():
        wg_bf[...] = wg_ref[0, 0].astype(BF16)
        wu_bf[...] = wu_ref[0, 0].astype(BF16)

    x = x_ref[...]
    a = jnp.dot(x, wg_bf[...], preferred_element_type=F32)
    b = jnp.dot(x, wu_bf[...], preferred_element_type=F32)
    o_ref[...] = (a * _sigmoid(a) * b).astype(o_ref.dtype)


def _moe_down_body(te_ref, h_ref, w_ref, rw_ref, o_ref, w_bf):
    @pl.when(_expert_changed(te_ref))
    def _():
        w_bf[...] = w_ref[0, 0].astype(BF16)

    y = jnp.dot(h_ref[...], w_bf[...], preferred_element_type=F32)
    o_ref[...] = (rw_ref[...] * y).astype(o_ref.dtype)


def _moe_down_into_body(te_ref, h_ref, w_ref, rw_ref, prev_ref, o_ref, w_bf):
    _moe_down_body(te_ref, h_ref, w_ref, rw_ref, o_ref, w_bf)


def _moe_combine_body(x_ref, y0_ref, y1_ref, gate_ref, o_ref, *, tm, tiles_per_batch, ctx_len):
    i = pl.program_id(0)
    is_ctx = (i % tiles_per_batch) * tm < ctx_len
    gate = jnp.where(is_ctx, gate_ref[0, 0:1, :], gate_ref[0, 1:2, :])
    o_ref[...] = x_ref[...] + gate * (y0_ref[...].astype(F32) + y1_ref[...].astype(F32))


def _moe_final_body(x_ref, y0_ref, y1_ref, gate_ref, g_ref, o_ref):
    x = x_ref[...] + gate_ref[0, 1:2, :] * (y0_ref[...].astype(F32) + y1_ref[...].astype(F32))
    o_ref[...] = x * lax.rsqrt(jnp.mean(x * x, axis=-1, keepdims=True) + NORM_EPS) * g_ref[...]


def _route(logits):
    assert MOE_TOPK == 2
    g_logit = logits[:, :MOE_GROUPS]
    g_prob = jax.nn.softmax(g_logit, axis=-1)
    g_idx = jnp.argmax(g_prob, axis=-1)
    g_p = jnp.max(g_prob, axis=-1)
    e_logit = logits[:, MOE_GROUPS:MOE_GROUPS + MOE_GROUPS * MOE_PER_GROUP]
    e_logit = e_logit.reshape(-1, MOE_GROUPS, MOE_PER_GROUP)
    sel = (jnp.arange(MOE_GROUPS)[None, :] == g_idx[:, None])[:, :, None]
    e_logit = jnp.sum(jnp.where(sel, e_logit, 0.0), axis=1)
    e_prob = jax.nn.softmax(e_logit, axis=-1)
    i0 = jnp.argmax(e_prob, axis=-1)
    p0 = jnp.max(e_prob, axis=-1)
    rest = jnp.where(jnp.arange(MOE_PER_GROUP)[None, :] == i0[:, None], -1.0, e_prob)
    i1 = jnp.argmax(rest, axis=-1)
    p1 = jnp.max(rest, axis=-1)
    e_p = jnp.stack([p0, p1], axis=-1)
    w = g_p[:, None] * e_p / jnp.sum(e_p, axis=-1, keepdims=True)
    ids = g_idx[:, None] * MOE_PER_GROUP + jnp.stack([i0, i1], axis=-1)
    return ids.astype(jnp.int32), w


def _moe(h, logits, x, gate, w_gate, w_up, w_down, layer, *, rows_per_batch, ctx_len, final_g=None):
    t, d = h.shape
    _, n_exp, _, dff = w_gate.shape
    tile = MOE_TILE
    ids, wts = _route(logits)
    flat_e = ids.reshape(-1)
    onehot = (flat_e[:, None] == jnp.arange(n_exp)[None, :]).astype(jnp.int32)
    counts = onehot.sum(0)
    rank = jnp.take_along_axis(jnp.cumsum(onehot, axis=0) - onehot, flat_e[:, None], axis=1)[:, 0]
    padded = (counts + tile - 1) // tile * tile
    starts = jnp.cumsum(padded) - padded
    pos = starts[flat_e] + rank
    n_rows = (t * MOE_TOPK // tile + n_exp) * tile
    n_tiles = n_rows // tile
    token = (jnp.arange(t * MOE_TOPK, dtype=jnp.int32) // MOE_TOPK).astype(F32)
    table = jnp.zeros((n_rows, 2), F32).at[pos].set(jnp.stack([token, wts.reshape(-1)], axis=1),
                                                    unique_indices=True, mode="promise_in_bounds")
    src = table[:, 0].astype(jnp.int32)
    row_w = table[:, 1]
    tile_start = jnp.arange(n_tiles, dtype=jnp.int32) * tile
    ends = starts + padded
    tile_e = jnp.minimum(jnp.sum(tile_start[:, None] >= ends[None, :], axis=1), n_exp - 1).astype(jnp.int32)

    n_chunks = max(c for c in (4, 2, 1) if n_tiles % c == 0)
    tpc = n_tiles // n_chunks
    tn_up = _pick_tile(dff, MOE_UP_COL_TILE, LANES)
    tn_dn = _pick_tile(d, MOE_DOWN_COL_TILE, LANES)
    w_up_spec = pl.BlockSpec((1, 1, d, tn_up), lambda j, i, te: (layer, te[i], 0, j))
    ys = None
    for c in range(n_chunks):
        rows_c = slice(c * tpc * tile, (c + 1) * tpc * tile)
        te_c = tile_e[c * tpc:(c + 1) * tpc]
        xs = h.at[src[rows_c]].get(mode="promise_in_bounds")
        hid = pl.pallas_call(
            _moe_up_body,
            grid_spec=pltpu.PrefetchScalarGridSpec(
                num_scalar_prefetch=1,
                grid=(dff // tn_up, tpc),
                in_specs=[pl.BlockSpec((tile, d), lambda j, i, te: (i, 0)), w_up_spec, w_up_spec],
                out_specs=pl.BlockSpec((tile, tn_up), lambda j, i, te: (i, j)),
                scratch_shapes=[pltpu.VMEM((d, tn_up), BF16), pltpu.VMEM((d, tn_up), BF16)]),
            out_shape=jax.ShapeDtypeStruct((tpc * tile, dff), BF16),
            compiler_params=_cp("arbitrary", "arbitrary"),
            name="moe_up",
        )(te_c, xs, w_gate, w_up)
        first = ys is None
        ys = pl.pallas_call(
            _moe_down_body if first else _moe_down_into_body,
            grid_spec=pltpu.PrefetchScalarGridSpec(
                num_scalar_prefetch=1,
                grid=(d // tn_dn, tpc),
                in_specs=[pl.BlockSpec((tile, dff), lambda j, i, te: (i, 0)),
                          pl.BlockSpec((1, 1, dff, tn_dn), lambda j, i, te: (layer, te[i], 0, j)),
                          pl.BlockSpec((tile, 1), lambda j, i, te: (i, 0))]
                         + ([] if first else [pl.BlockSpec(memory_space=pl.ANY)]),
                out_specs=pl.BlockSpec((tile, tn_dn), lambda j, i, te, c=c: (c * tpc + i, j)),
                scratch_shapes=[pltpu.VMEM((dff, tn_dn), BF16)]),
            out_shape=jax.ShapeDtypeStruct((n_rows, d), BF16),
            input_output_aliases={} if first else {4: 0},
            compiler_params=_cp("arbitrary", "arbitrary"),
            name="moe_down",
        )(te_c, hid, w_down, row_w[rows_c].reshape(tpc * tile, 1), *(() if first else (ys,)))

    pos2 = pos.reshape(t, MOE_TOPK)
    y0 = ys.at[pos2[:, 0]].get(mode="promise_in_bounds")
    y1 = ys.at[pos2[:, 1]].get(mode="promise_in_bounds")
    tm = _pick_tile(math.gcd(rows_per_batch, ctx_len), ROW_TILE, 8)
    tpb = rows_per_batch // tm
    if final_g is not None:
        batch = t // rows_per_batch
        ctx_tiles, lat_tiles = ctx_len // tm, (rows_per_batch - ctx_len) // tm
        lat_spec = pl.BlockSpec((tm, d), lambda b, i: (b * tpb + ctx_tiles + i, 0))
        out = pl.pallas_call(
            _moe_final_body,
            grid=(batch, lat_tiles),
            in_specs=[lat_spec, lat_spec, lat_spec, pl.BlockSpec((1, 2, d), lambda b, i: (b, 0, 0)),
                      pl.BlockSpec((1, d), lambda b, i: (0, 0))],
            out_specs=pl.BlockSpec((tm, d), lambda b, i: (b * lat_tiles + i, 0)),
            out_shape=jax.ShapeDtypeStruct((batch * lat_tiles * tm, d), F32),
            compiler_params=_cp("parallel", "parallel"),
            name="moe_combine_final",
        )(x, y0, y1, gate, final_g.reshape(1, d).astype(F32))
        return out.reshape(batch, lat_tiles * tm, d)
    row_spec = pl.BlockSpec((tm, d), lambda i: (i, 0))
    return pl.pallas_call(
        functools.partial(_moe_combine_body, tm=tm, tiles_per_batch=tpb, ctx_len=ctx_len),
        grid=(t // tm,),
        in_specs=[row_spec, row_spec, row_spec, pl.BlockSpec((1, 2, d), lambda i: (i // tpb, 0, 0))],
        out_specs=row_spec,
        out_shape=jax.ShapeDtypeStruct((t, d), F32),
        compiler_params=_cp("parallel"),
        name="moe_combine",
    )(x, y0, y1, gate)


def kernel(x, c, ctx, c_ctx, ada_w, ada_b, norm_mix, norm_ffn, w_in, w_out, diff_lambda, diff_subln,
           s5_a_re, s5_a_im, s5_log_dt, s5_b_re, s5_b_im, s5_c_re, s5_c_im, s5_d, s5_glu_w, s5_glu_b,
           mla_q_norm, mla_kv_norm, mla_w_uq, mla_w_ukv, ret_decay, ret_norm,
           moe_wg, moe_bg, moe_we, moe_be, moe_w_gate, moe_w_up, moe_w_down, final_norm):
    batch, n_lat, d = x.shape
    ctx_len = ctx.shape[1]
    depth = ada_w.shape[0]
    seq = ctx_len + n_lat
    rows = batch * seq
    gw = d // 4
    heads = gw // LANES
    q_rank, kv_rank = 3 * d // 16, d // 16
    ret_qk = heads * RET_K
    n_route = MOE_GROUPS + MOE_GROUPS * MOE_PER_GROUP
    assert heads % 2 == 0 and ctx_len % RET_CHUNK == 0 and n_lat % RET_CHUNK == 0

    splits = (gw, gw, gw, gw, q_rank, kv_rank, ROPE_DIM, ret_qk, ret_qk, gw, gw)
    offs = [0]
    for s_ in splits:
        offs.append(offs[-1] + s_)
    names = ("dq", "dk", "dv", "su", "mcq", "mckv", "mkr", "rq", "rk", "rv", "rg")
    src_col = {n_: (offs[i], offs[i + 1]) for i, n_ in enumerate(names)}
    order = ("dq", "dk", "rq", "rk", "dv", "su", "rv", "rg", "mcq", "mckv")
    col = {}
    pos = 0
    for n_ in order:
        col[n_] = pos
        pos += src_col[n_][1] - src_col[n_][0]
    n_main = pos
    n_rope = col["dv"]
    uq_cols = jnp.arange(heads * (MLA_NOPE + ROPE_DIM)).reshape(heads, MLA_NOPE + ROPE_DIM)
    uq_perm = jnp.concatenate([uq_cols[:, :MLA_NOPE].reshape(-1), uq_cols[:, MLA_NOPE:].reshape(-1)])

    tables = _rope_tables(n_lat, ctx_len)
    log2e = math.log2(math.e)
    rope_scale = jnp.ones((n_rope,), F32).at[col["rk"]:col["rk"] + ret_qk].set(RET_K ** -0.5)
    rope_scale = rope_scale.at[col["dq"]:col["dq"] + gw].set(DIFF_HEAD_DIM ** -0.5 * log2e)
    mla_q_scale = (MLA_NOPE + ROPE_DIM) ** -0.5 * log2e
    main_scale = jnp.concatenate([rope_scale, jnp.ones((n_main - n_rope,), F32)])

    cond = jnp.concatenate([c_ctx[None, :], c], axis=0)
    cond = jnp.pad(cond * _sigmoid(cond), ((0, 8 - (batch + 1) % 8 if (batch + 1) % 8 else 0), (0, 0)))

    mod_all = _ada_mod(cond, ada_w, ada_b)

    xa = jnp.concatenate([ctx, x], axis=1).reshape(rows, d)
    tm_big = _pick_tile(seq, MM_ROW_TILE)
    tn = lambda n_: _pick_tile(n_, MM_COL_TILE, LANES)

    for l in range(depth):
        lam_init = 0.8 - 0.6 * math.exp(-0.3 * l)
        mod = mod_all[l].reshape(cond.shape[0], 6, d)
        mods = [jnp.stack([jnp.broadcast_to(mod[0, i], (batch, d)), mod[1:batch + 1, i]], axis=1)
                for i in range(6)]

        w_main = jnp.concatenate([w_in[l, :, src_col[n_][0]:src_col[n_][1]] for n_ in order],
                                 axis=1).astype(BF16)
        w_kr = jnp.concatenate([w_in[l, :, src_col["mkr"][0]:src_col["mkr"][1]]] * (LANES // ROPE_DIM),
                               axis=1).astype(BF16)

        h = _norm_mod(xa, norm_mix[l], mods[0], mods[1], rows_per_batch=seq, ctx_len=ctx_len)
        proj = _mm([h], w_main, name="in_proj", out_dtype=BF16, tm=tm_big,
                   tn=tn(math.gcd(n_main, n_rope)), rope=(tables, main_scale, (0, n_rope)),
                   rows_per_batch=seq)
        krr = _mm([h], w_kr, name="in_proj_kr", out_dtype=BF16, tm=tm_big, tn=LANES,
                  rope=(tables, jnp.ones((LANES,), F32), (0, LANES)), rows_per_batch=seq)
        qk = proj

        lv = diff_lambda[l].astype(F32)
        lam = jnp.exp(jnp.sum(lv[0] * lv[1])) - jnp.exp(jnp.sum(lv[2] * lv[3])) + lam_init
        a_out = _diff_attn(qk, proj, lam, diff_subln[l], batch=batch, seq=seq, ctx_len=ctx_len,
                           heads=heads, q_blk=col["dq"] // LANES, k_blk=col["dk"] // LANES,
                           v_blk=col["dv"] // LANES, post=1.0 - lam_init)

        s5p = _s5n_params(s5_a_re[l], s5_a_im[l], s5_log_dt[l], s5_b_re[l], s5_b_im[l],
                          s5_c_re[l], s5_c_im[l], s5_d[l])
        s_act = _s5n_mix(proj, col["su"], s5p, batch=batch, seq=seq, ctx_len=ctx_len, gw=gw)
        s_out = _mm([s_act], s5_glu_w[l].astype(BF16), name="s5_glu", out_dtype=BF16, tm=tm_big, tn=tn(gw),
                    bias=s5_glu_b[l], glu_in=s_act)

        cq = proj[:, col["mcq"]:col["mcq"] + q_rank]
        ckv = proj[:, col["mckv"]:col["mckv"] + kv_rank]
        w_uq = (mla_w_uq[l][:, uq_perm] * mla_q_scale).astype(BF16)
        n_qn, n_q = heads * MLA_NOPE, heads * (MLA_NOPE + ROPE_DIM)
        q_up = _mm([cq], w_uq, name="mla_q_up", out_dtype=BF16, tm=tm_big, tn=tn(math.gcd(n_qn, n_q)),
                   norm_g=mla_q_norm[l], rope=(tables, jnp.ones((n_q,), F32), (n_qn, n_q)),
                   rows_per_batch=seq)
        kv_up = _mm([ckv], mla_w_ukv[l].astype(BF16), name="mla_kv_up", out_dtype=BF16, tm=tm_big,
                    tn=tn(mla_w_ukv.shape[2]), norm_g=mla_kv_norm[l])
        m_out = _mla_attn(q_up, kv_up, krr, batch=batch, seq=seq, ctx_len=ctx_len, heads=heads)

        log_g = jax.nn.log_sigmoid(ret_decay[l].astype(F32))
        r_out = _retention(qk, proj, log_g, ret_norm[l], batch=batch, seq=seq, ctx_len=ctx_len,
                           heads=heads, q_off=col["rq"], k_off=col["rk"], v_off=col["rv"], g_off=col["rg"])

        xa = _mm([a_out, s_out, m_out, r_out], w_out[l].astype(BF16), name="out_proj", out_dtype=F32, tm=tm_big, tn=tn(d),
                 res=xa, gate=mods[2], rows_per_batch=seq, ctx_len=ctx_len)

        w_r = jnp.concatenate([moe_wg[l], moe_we[l]], axis=1).astype(F32)
        w_r = jnp.pad(w_r, ((0, 0), (0, LANES - n_route)))
        w_r_hi = w_r.astype(BF16)
        w_r_lo = (w_r - w_r_hi.astype(F32)).astype(BF16)
        b_r = jnp.pad(jnp.concatenate([moe_bg[l], moe_be[l]]).astype(F32), (0, LANES - n_route))
        h, logits = _norm_mod(xa, norm_ffn[l], mods[3], mods[4], rows_per_batch=seq, ctx_len=ctx_len,
                              router=(w_r_hi, w_r_lo, b_r.reshape(1, LANES)))
        xa = _moe(h, logits, xa, mods[5], moe_w_gate, moe_w_up, moe_w_down, l,
                  rows_per_batch=seq, ctx_len=ctx_len, final_g=final_norm if l == depth - 1 else None)
    return xa
```

```python
import functools
import math

import jax
import jax.numpy as jnp
import numpy as np
from jax import lax
from jax.experimental import pallas as pl
from jax.experimental.pallas import tpu as pltpu

BF16 = jnp.bfloat16
F32 = jnp.float32

V7X_VMEM_BYTES = 64 * 2**20
VMEM_LIMIT = V7X_VMEM_BYTES - 12 * 2**20
LANES = 128

GRID_W = 64
ROPE_DIM = 64
ROPE_BASE = 10000.0
NORM_EPS = 1e-6
DIFF_HEAD_DIM = 64
S5_CH = 16
S5_STATE = 64
MLA_NOPE = 128
MLA_V = 128
RET_K = 64
RET_V = 128
RET_CHUNK = 128
MOE_GROUPS = 4
MOE_PER_GROUP = 4
MOE_TOPK = 2
ROPE_QUARTER = ROPE_DIM // 4

MM_ROW_TILE = 1088
MM_COL_TILE = 512
ROW_TILE = 256
ATTN_Q_TILE = 512
S5_SCAN_LANES = 2048
MOE_TILE = 256
MOE_UP_COL_TILE = 512
MOE_DOWN_COL_TILE = 4096


def _cp(*sem):
    return pltpu.CompilerParams(dimension_semantics=sem, vmem_limit_bytes=VMEM_LIMIT)


def _pick_tile(n, target, mult=16):
    best = None
    for t in range(mult, min(n, target) + 1, mult):
        if n % t == 0:
            best = t
    assert best is not None, (n, target)
    return best


def _sigmoid(x):
    return 1.0 / (1.0 + jnp.exp(-x))


def _rotate(x, cos, sin, swap):
    reps = x.shape[1] // LANES
    tile = lambda t: jnp.tile(t, (1, reps))
    partner = jnp.dot(x.astype(BF16), swap, preferred_element_type=F32)
    return x * tile(cos) + partner * tile(sin)


def _rope_swap(width):
    lane = np.arange(width)
    first = (lane // ROPE_QUARTER) % 2 == 0
    src = np.where(first, lane + ROPE_QUARTER, lane - ROPE_QUARTER)
    swap = np.zeros((width, width), np.float32)
    swap[src, lane] = np.where(first, -1.0, 1.0)
    return jnp.asarray(swap, BF16)


def _mm_body(*refs, nx, ksizes, has_norm, has_bias, epilogue, tm, tiles_per_batch, ctx_len, rope_tiles):
    x_refs = refs[:nx]
    w_ref = refs[nx]
    idx = nx + 1
    g_ref = b_ref = e_ref = res_ref = gate_ref = None
    if has_norm:
        g_ref = refs[idx]; idx += 1
    if has_bias:
        b_ref = refs[idx]; idx += 1
    if epilogue == "glu":
        e_ref = refs[idx]; idx += 1
    if epilogue == "resgate":
        res_ref, gate_ref = refs[idx], refs[idx + 1]; idx += 2
    if epilogue == "rope":
        cos_ref, sin_ref, swap_ref, cs_ref = refs[idx:idx + 4]; idx += 4
    o_ref = refs[idx]

    acc = None
    off = 0
    for xr, ks in zip(x_refs, ksizes):
        x = xr[...]
        if has_norm:
            xf = x.astype(F32)
            xf = xf * lax.rsqrt(jnp.mean(xf * xf, axis=-1, keepdims=True) + NORM_EPS)
            x = xf * g_ref[...]
        x = x.astype(BF16)
        w = w_ref[off:off + ks, :].astype(BF16)
        part = jnp.dot(x, w, preferred_element_type=F32)
        acc = part if acc is None else acc + part
        off += ks
    if has_bias:
        acc = acc + b_ref[...]
    if epilogue == "glu":
        acc = e_ref[...].astype(F32) * _sigmoid(acc)
    elif epilogue == "resgate":
        i = pl.program_id(0)
        row = (i % tiles_per_batch) * tm + lax.broadcasted_iota(jnp.int32, (tm, 1), 0)
        gate = jnp.where(row < ctx_len, gate_ref[0, 0:1, :], gate_ref[0, 1:2, :])
        acc = res_ref[...] + gate * acc
    if epilogue == "rope":
        j = pl.program_id(1)
        roped = (j >= rope_tiles[0]) & (j < rope_tiles[1])

        @pl.when(roped)
        def _():
            y = _rotate(acc, cos_ref[...], sin_ref[...], swap_ref[...]) * cs_ref[...]
            o_ref[...] = y.astype(o_ref.dtype)

        @pl.when(jnp.logical_not(roped))
        def _():
            o_ref[...] = acc.astype(o_ref.dtype)
    else:
        o_ref[...] = acc.astype(o_ref.dtype)


def _mm(xs, w, *, name, out_dtype, tm, tn, norm_g=None, bias=None, glu_in=None, res=None, gate=None,
        rope=None, rows_per_batch=None, ctx_len=0):
    m = xs[0].shape[0]
    ksizes = tuple(x.shape[1] for x in xs)
    k, n = w.shape
    assert sum(ksizes) == k and m % tm == 0 and n % tn == 0
    epilogue = ("glu" if glu_in is not None else "resgate" if res is not None
                else "rope" if rope is not None else None)
    tiles_per_batch = (rows_per_batch // tm) if rows_per_batch else 1
    rope_tiles = None
    in_specs = [pl.BlockSpec((tm, ks), lambda i, j: (i, 0)) for ks in ksizes]
    in_specs.append(pl.BlockSpec((k, tn), lambda i, j: (0, j)))
    args = list(xs) + [w]
    if norm_g is not None:
        in_specs.append(pl.BlockSpec((1, k), lambda i, j: (0, 0)))
        args.append(norm_g.reshape(1, k).astype(F32))
    if bias is not None:
        in_specs.append(pl.BlockSpec((1, tn), lambda i, j: (0, j)))
        args.append(bias.reshape(1, n).astype(F32))
    if epilogue == "glu":
        in_specs.append(pl.BlockSpec((tm, tn), lambda i, j: (i, j)))
        args.append(glu_in)
    if epilogue == "resgate":
        tpb = tiles_per_batch
        in_specs.append(pl.BlockSpec((tm, tn), lambda i, j: (i, j)))
        in_specs.append(pl.BlockSpec((1, 2, tn), lambda i, j: (i // tpb, 0, j)))
        args += [res, gate]
    if epilogue == "rope":
        tables, col_scale, (lo, hi) = rope
        assert lo % tn == 0 and hi % tn == 0
        rope_tiles = (lo // tn, hi // tn)
        tpb = tiles_per_batch
        in_specs += [pl.BlockSpec((tm, LANES), lambda i, j: (i % tpb, 0))] * 2
        in_specs.append(pl.BlockSpec((tn, tn), lambda i, j: (0, 0)))
        in_specs.append(pl.BlockSpec((1, tn), lambda i, j: (0, j)))
        args += list(tables) + [_rope_swap(tn), col_scale.reshape(1, n).astype(F32)]
    body = functools.partial(_mm_body, nx=len(xs), ksizes=ksizes, has_norm=norm_g is not None,
                             has_bias=bias is not None, epilogue=epilogue, tm=tm,
                             tiles_per_batch=tiles_per_batch, ctx_len=ctx_len, rope_tiles=rope_tiles)
    return pl.pallas_call(
        body,
        grid=(m // tm, n // tn),
        in_specs=in_specs,
        out_specs=pl.BlockSpec((tm, tn), lambda i, j: (i, j)),
        out_shape=jax.ShapeDtypeStruct((m, n), out_dtype),
        compiler_params=_cp("parallel", "arbitrary"),
        name=name,
    )(*args)


def _ada_body(c_ref, w_ref, b_ref, o_ref):
    acc = jnp.dot(c_ref[...].astype(BF16), w_ref[0].astype(BF16), preferred_element_type=F32)
    o_ref[0] = acc + b_ref[0]


def _ada_mod(cond, ada_w, ada_b):
    depth, d, n6 = ada_w.shape
    rows = cond.shape[0]
    tn = _pick_tile(n6, MM_COL_TILE, LANES)
    return pl.pallas_call(
        _ada_body,
        grid=(depth, n6 // tn),
        in_specs=[pl.BlockSpec((rows, d), lambda l, j: (0, 0)),
                  pl.BlockSpec((1, d, tn), lambda l, j: (l, 0, j)),
                  pl.BlockSpec((1, 1, tn), lambda l, j: (l, 0, j))],
        out_specs=pl.BlockSpec((1, rows, tn), lambda l, j: (l, 0, j)),
        out_shape=jax.ShapeDtypeStruct((depth, rows, n6), F32),
        compiler_params=_cp("parallel", "arbitrary"),
        name="ada_mod",
    )(cond, ada_w, ada_b.reshape(depth, 1, n6).astype(F32))


def _norm_mod_body(*refs, tm, tiles_per_batch, ctx_len, router):
    if router:
        x_ref, g_ref, sh_ref, sc_ref, whi_ref, wlo_ref, br_ref, h_ref, lg_ref = refs
    else:
        x_ref, g_ref, sh_ref, sc_ref, h_ref = refs
    i = pl.program_id(0)
    x = x_ref[...]
    y = x * lax.rsqrt(jnp.mean(x * x, axis=-1, keepdims=True) + NORM_EPS) * g_ref[...]
    is_ctx = (i % tiles_per_batch) * tm < ctx_len
    sh = jnp.where(is_ctx, sh_ref[0, 0:1, :], sh_ref[0, 1:2, :])
    sc = jnp.where(is_ctx, sc_ref[0, 0:1, :], sc_ref[0, 1:2, :])
    h = y * (1.0 + sc) + sh
    h_ref[...] = h.astype(BF16)
    if router:
        hi = h.astype(BF16)
        lo = (h - hi.astype(F32)).astype(BF16)
        lg = jnp.dot(hi, whi_ref[...], preferred_element_type=F32)
        lg = lg + jnp.dot(hi, wlo_ref[...], preferred_element_type=F32)
        lg = lg + jnp.dot(lo, whi_ref[...], preferred_element_type=F32)
        lg_ref[...] = lg + br_ref[...]


def _norm_mod_first_body(c_ref, x_ref, g_ref, sh_ref, sc_ref, h_ref, xa_ref, *, ctx_tiles):
    is_ctx = pl.program_id(1) < ctx_tiles
    x = jnp.where(is_ctx, c_ref[...], x_ref[...])
    xa_ref[...] = x
    y = x * lax.rsqrt(jnp.mean(x * x, axis=-1, keepdims=True) + NORM_EPS) * g_ref[...]
    sh = jnp.where(is_ctx, sh_ref[0, 0:1, :], sh_ref[0, 1:2, :])
    sc = jnp.where(is_ctx, sc_ref[0, 0:1, :], sc_ref[0, 1:2, :])
    h_ref[...] = (y * (1.0 + sc) + sh).astype(BF16)


def _norm_mod_first(ctx, x, g, shift, scale):
    batch, ctx_len, d = ctx.shape
    n_lat = x.shape[1]
    tm = _pick_tile(math.gcd(n_lat, ctx_len), ROW_TILE, 8)
    ctx_tiles, lat_tiles = ctx_len // tm, n_lat // tm
    tpb = ctx_tiles + lat_tiles
    rows = batch * tpb * tm
    out_spec = pl.BlockSpec((tm, d), lambda b, i: (b * tpb + i, 0))
    mod_spec = pl.BlockSpec((1, 2, d), lambda b, i: (b, 0, 0))
    return pl.pallas_call(
        functools.partial(_norm_mod_first_body, ctx_tiles=ctx_tiles),
        grid=(batch, tpb),
        in_specs=[pl.BlockSpec((tm, d), lambda b, i: (b * ctx_tiles + jnp.minimum(i, ctx_tiles - 1), 0)),
                  pl.BlockSpec((tm, d), lambda b, i: (b * lat_tiles + jnp.maximum(i - ctx_tiles, 0), 0)),
                  pl.BlockSpec((1, d), lambda b, i: (0, 0)), mod_spec, mod_spec],
        out_specs=[out_spec, out_spec],
        out_shape=[jax.ShapeDtypeStruct((rows, d), BF16), jax.ShapeDtypeStruct((rows, d), F32)],
        compiler_params=_cp("parallel", "arbitrary"),
        name="norm_mod_first",
    )(ctx.reshape(batch * ctx_len, d), x.reshape(batch * n_lat, d), g.reshape(1, d), shift, scale)


def _norm_mod(x, g, shift, scale, *, rows_per_batch, ctx_len, router=None):
    m, d = x.shape
    tm = _pick_tile(math.gcd(rows_per_batch, ctx_len), ROW_TILE, 8)
    tpb = rows_per_batch // tm
    in_specs = [
        pl.BlockSpec((tm, d), lambda i: (i, 0)),
        pl.BlockSpec((1, d), lambda i: (0, 0)),
        pl.BlockSpec((1, 2, d), lambda i: (i // tpb, 0, 0)),
        pl.BlockSpec((1, 2, d), lambda i: (i // tpb, 0, 0)),
    ]
    args = [x, g.reshape(1, d), shift, scale]
    out_specs = [pl.BlockSpec((tm, d), lambda i: (i, 0))]
    out_shape = [jax.ShapeDtypeStruct((m, d), BF16)]
    if router is not None:
        whi, wlo, br = router
        in_specs += [pl.BlockSpec((d, LANES), lambda i: (0, 0)),
                     pl.BlockSpec((d, LANES), lambda i: (0, 0)),
                     pl.BlockSpec((1, LANES), lambda i: (0, 0))]
        args += [whi, wlo, br]
        out_specs.append(pl.BlockSpec((tm, LANES), lambda i: (i, 0)))
        out_shape.append(jax.ShapeDtypeStruct((m, LANES), F32))
    body = functools.partial(_norm_mod_body, tm=tm, tiles_per_batch=tpb, ctx_len=ctx_len,
                             router=router is not None)
    outs = pl.pallas_call(body, grid=(m // tm,), in_specs=in_specs, out_specs=out_specs,
                          out_shape=out_shape, compiler_params=_cp("parallel"),
                          name="norm_mod_router" if router is not None else "norm_mod")(*args)
    return outs if router is not None else outs[0]


def _rope_tables(n_lat, ctx_len):
    rows = n_lat // GRID_W
    row = jnp.repeat(jnp.arange(rows, dtype=F32), GRID_W)
    col = jnp.tile(jnp.arange(GRID_W, dtype=F32), rows)
    quarter = ROPE_QUARTER
    inv = ROPE_BASE ** (-jnp.arange(quarter, dtype=F32) / quarter)
    ar = row[:, None] * inv
    ac = col[:, None] * inv
    ang = jnp.concatenate([ar, ar, ac, ac], axis=-1)
    ang = jnp.concatenate([jnp.zeros((ctx_len, ROPE_DIM), F32), ang], axis=0)
    ang = jnp.tile(ang, (1, LANES // ROPE_DIM))
    return jnp.cos(ang), jnp.sin(ang)


ATTN_ALIGN = 256


def _softmax_pv(q, k_ref, va_ref, bounds):
    ms, ovs = [], []
    for lo, hi in bounds:
        s = lax.dot_general(q, k_ref[lo:hi, :], (((1,), (1,)), ((), ())), preferred_element_type=F32)
        m = jnp.max(s, axis=-1, keepdims=True)
        e = jnp.exp2(s - m).astype(BF16)
        ovs.append(jnp.dot(e, va_ref[lo:hi, :], preferred_element_type=F32))
        ms.append(m)
    m_all = functools.reduce(jnp.maximum, ms)
    acc = sum(ov * jnp.exp2(m - m_all) for m, ov in zip(ms, ovs))
    return acc[:, :LANES] / acc[:, LANES:]


def _key_chunks(n_keys):
    if n_keys < 2 * ATTN_ALIGN:
        return ((0, n_keys),)
    half = (n_keys // ATTN_ALIGN + 1) // 2 * ATTN_ALIGN
    return ((0, half), (half, n_keys))


def _lat_tile(seq, ctx_len):
    n_lat = seq - ctx_len
    assert ctx_len % ATTN_ALIGN == 0 and n_lat % ATTN_ALIGN == 0
    return _pick_tile(n_lat, ATTN_Q_TILE, ATTN_ALIGN)


def _lat_rows(seq, ctx_len, tq):
    return lambda b, i: pl.multiple_of(b * seq + ctx_len + i * tq, ATTN_ALIGN)


def _diff_attn_body(lam_ref, q_ref, k_ref, v_ref, sub_ref, *rest, post, bounds):
    o_ref, va_ref = rest[-2:]

    def fill():
        va_ref[:, :LANES] = v_ref[...]
        va_ref[:, LANES:] = jnp.ones(v_ref.shape, BF16)

    if len(rest) == 2:
        pl.when(pl.program_id(2) == 0)(fill)
    else:
        fill()

    q = q_ref[...]
    lane = lax.broadcasted_iota(jnp.int32, (1, LANES), 1)
    first = lane < DIFF_HEAD_DIM
    zero = jnp.zeros_like(q)
    q0 = jnp.where(first, q, zero)
    q1 = jnp.where(first, zero, q)
    o = _softmax_pv(q0, k_ref, va_ref, bounds) - lam_ref[0] * _softmax_pv(q1, k_ref, va_ref, bounds)
    o = o * lax.rsqrt(jnp.mean(o * o, axis=-1, keepdims=True) + NORM_EPS) * sub_ref[...] * post
    o_ref[...] = o.astype(o_ref.dtype)


def _diff_attn(qk, proj, lam, subln, *, batch, seq, ctx_len, heads, q_blk, k_blk, v_blk, post):
    tq = _lat_tile(seq, ctx_len)
    rows = _lat_rows(seq, ctx_len, tq)
    smem = pl.BlockSpec(memory_space=pltpu.SMEM)
    args = (lam.reshape(1).astype(F32), qk, qk, proj, subln.reshape(1, LANES).astype(F32))
    out_shape = jax.ShapeDtypeStruct((batch * seq, heads * LANES), BF16)
    elem = (pl.Element(tq), pl.Element(LANES))
    lat = pl.pallas_call(
        functools.partial(_diff_attn_body, post=post, bounds=_key_chunks(seq)),
        grid=(batch, heads, (seq - ctx_len) // tq),
        in_specs=[
            smem,
            pl.BlockSpec(elem, lambda b, h, i: (rows(b, i), pl.multiple_of((q_blk + h) * LANES, LANES))),
            pl.BlockSpec((seq, LANES), lambda b, h, i: (b, k_blk + h)),
            pl.BlockSpec((seq, LANES), lambda b, h, i: (b, v_blk + h)),
            pl.BlockSpec((1, LANES), lambda b, h, i: (0, 0)),
        ],
        out_specs=pl.BlockSpec(elem, lambda b, h, i: (rows(b, i), pl.multiple_of(h * LANES, LANES))),
        out_shape=out_shape,
        scratch_shapes=[pltpu.VMEM((seq, 2 * LANES), BF16)],
        compiler_params=_cp("parallel", "parallel", "arbitrary"),
        name="diff_attn",
    )(*args)
    cpb = seq // ctx_len
    ctx_spec = lambda blk: pl.BlockSpec((ctx_len, LANES), lambda b, h: (b * cpb, blk + h))
    return pl.pallas_call(
        functools.partial(_diff_attn_body, post=post, bounds=_key_chunks(ctx_len)),
        grid=(batch, heads),
        in_specs=[smem, ctx_spec(q_blk), ctx_spec(k_blk), ctx_spec(v_blk),
                  pl.BlockSpec((1, LANES), lambda b, h: (0, 0)), pl.BlockSpec(memory_space=pl.ANY)],
        out_specs=ctx_spec(0),
        out_shape=out_shape,
        scratch_shapes=[pltpu.VMEM((ctx_len, 2 * LANES), BF16)],
        input_output_aliases={5: 0},
        compiler_params=_cp("parallel", "parallel"),
        name="diff_attn_ctx",
    )(*args, lat)


def _mla_attn_body(qn_ref, qr_ref, kn_ref, kr_ref, v_ref, *rest, bounds):
    o_ref, ka_ref, va_ref = rest[-3:]
    h = pl.program_id(1)

    def fill():
        ka_ref[:, :LANES] = kn_ref[...]
        ka_ref[:, LANES:] = kr_ref[...]
        va_ref[:, :LANES] = v_ref[...]
        va_ref[:, LANES:] = jnp.ones(v_ref.shape, BF16)

    if len(rest) == 3:
        pl.when(pl.program_id(2) == 0)(fill)
    else:
        fill()

    qr = qr_ref[...]
    lane = lax.broadcasted_iota(jnp.int32, (1, LANES), 1)
    mine = (lane < ROPE_DIM) == (h % 2 == 0)
    qr = jnp.where(mine, qr, jnp.zeros_like(qr))
    q = jnp.concatenate([qn_ref[...], qr], axis=1)
    o_ref[...] = _softmax_pv(q, ka_ref, va_ref, bounds).astype(o_ref.dtype)


def _mla_attn(q_up, kv_up, k_rope, *, batch, seq, ctx_len, heads):
    tq = _lat_tile(seq, ctx_len)
    rows = _lat_rows(seq, ctx_len, tq)
    args = (q_up, q_up, kv_up, k_rope, kv_up)
    out_shape = jax.ShapeDtypeStruct((batch * seq, heads * LANES), BF16)
    elem = (pl.Element(tq), pl.Element(LANES))
    lat = pl.pallas_call(
        functools.partial(_mla_attn_body, bounds=_key_chunks(seq)),
        grid=(batch, heads, (seq - ctx_len) // tq),
        in_specs=[
            pl.BlockSpec(elem, lambda b, h, i: (rows(b, i), pl.multiple_of(h * LANES, LANES))),
            pl.BlockSpec(elem, lambda b, h, i: (rows(b, i), pl.multiple_of((heads + h // 2) * LANES, LANES))),
            pl.BlockSpec((seq, LANES), lambda b, h, i: (b, 2 * h)),
            pl.BlockSpec((seq, LANES), lambda b, h, i: (b, 0)),
            pl.BlockSpec((seq, LANES), lambda b, h, i: (b, 2 * h + 1)),
        ],
        out_specs=pl.BlockSpec(elem, lambda b, h, i: (rows(b, i), pl.multiple_of(h * LANES, LANES))),
        out_shape=out_shape,
        scratch_shapes=[pltpu.VMEM((seq, 2 * LANES), BF16), pltpu.VMEM((seq, 2 * LANES), BF16)],
        compiler_params=_cp("parallel", "parallel", "arbitrary"),
        name="mla_attn",
    )(*args)
    cpb = seq // ctx_len
    ctx_spec = lambda col: pl.BlockSpec((ctx_len, LANES), lambda b, h: (b * cpb, col(h)))
    return pl.pallas_call(
        functools.partial(_mla_attn_body, bounds=_key_chunks(ctx_len)),
        grid=(batch, heads),
        in_specs=[ctx_spec(lambda h: h), ctx_spec(lambda h: heads + h // 2), ctx_spec(lambda h: 2 * h),
                  ctx_spec(lambda h: 0), ctx_spec(lambda h: 2 * h + 1), pl.BlockSpec(memory_space=pl.ANY)],
        out_specs=ctx_spec(lambda h: h),
        out_shape=out_shape,
        scratch_shapes=[pltpu.VMEM((ctx_len, 2 * LANES), BF16), pltpu.VMEM((ctx_len, 2 * LANES), BF16)],
        input_output_aliases={5: 0},
        compiler_params=_cp("parallel", "parallel"),
        name="mla_attn_ctx",
    )(*args, lat)


def _retention_body(*refs, backward, cs, heads):
    if backward:
        lg_ref, q_ref, k_ref, v_ref, yf_ref, gate_ref, ng_ref, o_ref, s_ref, d_ref, qd_ref, kd_ref = refs
    else:
        lg_ref, q_ref, k_ref, v_ref, o_ref, s_ref, d_ref, qd_ref, kd_ref = refs
    t = pl.program_id(1)

    @pl.when(t == 0)
    def _():
        s_ref[...] = jnp.zeros_like(s_ref)
        pos_r = lax.broadcasted_iota(jnp.int32, (cs, cs), 0).astype(F32)
        pos_c = lax.broadcasted_iota(jnp.int32, (cs, cs), 1).astype(F32)
        pos = lax.broadcasted_iota(jnp.int32, (cs, 1), 0).astype(F32)
        for h in range(heads):
            lg = lg_ref[h]
            if backward:
                diff = pos_c - pos_r
                keep = diff > 0
                qd_ref[h] = jnp.exp(lg * (cs - pos))
                kd_ref[h] = jnp.exp(lg * pos)
            else:
                diff = pos_r - pos_c
                keep = diff >= 0
                qd_ref[h] = jnp.exp(lg * (pos + 1.0))
                kd_ref[h] = jnp.exp(lg * (cs - 1.0 - pos))
            d_ref[h] = jnp.where(keep, jnp.exp(lg * jnp.maximum(diff, 0.0)), 0.0)

    lane = lax.broadcasted_iota(jnp.int32, (1, LANES), 1)
    nt = (((1,), (1,)), ((), ()))
    tn = (((0,), (0,)), ((), ()))
    for h in range(heads):
        blk = slice((h // 2) * LANES, (h // 2 + 1) * LANES)
        col = slice(h * RET_V, (h + 1) * RET_V)
        mine = (lane < RET_K) == (h % 2 == 0)
        q = q_ref[:, blk]
        q = jnp.where(mine, q, jnp.zeros_like(q))
        k = k_ref[:, blk]
        v = v_ref[:, col]
        scores = lax.dot_general(q, k, nt, preferred_element_type=F32) * d_ref[h]
        intra = jnp.dot(scores.astype(BF16), v, preferred_element_type=F32)
        state = s_ref[h]
        q_w = (q.astype(F32) * qd_ref[h]).astype(BF16)
        cross = jnp.dot(q_w, state.astype(BF16), preferred_element_type=F32)
        k_w = (k.astype(F32) * kd_ref[h]).astype(BF16)
        upd = lax.dot_general(k_w, v, tn, preferred_element_type=F32)
        s_ref[h] = jnp.exp(lg_ref[h] * cs) * state + upd
        y = intra + cross
        if backward:
            y = y + yf_ref[:, col].astype(F32)
            y = y * lax.rsqrt(jnp.mean(y * y, axis=-1, keepdims=True) + NORM_EPS) * ng_ref[:, col]
            g = gate_ref[:, col].astype(F32)
            o_ref[:, col] = (g * _sigmoid(g) * y).astype(o_ref.dtype)
        else:
            o_ref[:, col] = y.astype(o_ref.dtype)


def _retention(qk, proj, log_g, norm_g, *, batch, seq, ctx_len, heads, q_off, k_off, v_off, g_off):
    cs = RET_CHUNK
    nc, nc_ctx = seq // cs, ctx_len // cs
    wqk, wv = heads * RET_K, heads * RET_V
    assert q_off % wqk == 0 and k_off % wqk == 0 and v_off % wv == 0 and g_off % wv == 0
    smem = pl.BlockSpec(memory_space=pltpu.SMEM)

    def fwd_chunk(t):
        return t

    def bwd_chunk(t):
        return jnp.where(t < nc_ctx, nc_ctx - 1 - t, nc - 1 - (t - nc_ctx))

    def specs(chunk):
        row = lambda b, t: b * nc + chunk(t)
        return (pl.BlockSpec((cs, wqk), lambda b, t: (row(b, t), q_off // wqk)),
                pl.BlockSpec((cs, wqk), lambda b, t: (row(b, t), k_off // wqk)),
                pl.BlockSpec((cs, wv), lambda b, t: (row(b, t), v_off // wv)),
                pl.BlockSpec((cs, wv), lambda b, t: (row(b, t), 0)),
                pl.BlockSpec((cs, wv), lambda b, t: (row(b, t), g_off // wv)))

    scratch = [pltpu.VMEM((heads, LANES, RET_V), F32), pltpu.VMEM((heads, cs, cs), F32),
               pltpu.VMEM((heads, cs, 1), F32), pltpu.VMEM((heads, cs, 1), F32)]
    qs, ks, vs, ys, gs = specs(fwd_chunk)
    y_f = pl.pallas_call(
        functools.partial(_retention_body, backward=False, cs=cs, heads=heads),
        grid=(batch, nc),
        in_specs=[smem, qs, ks, vs],
        out_specs=ys,
        out_shape=jax.ShapeDtypeStruct((batch * seq, wv), BF16),
        scratch_shapes=scratch,
        compiler_params=_cp("parallel", "arbitrary"),
        name="retention_fwd",
    )(log_g[0].astype(F32), qk, qk, proj)
    qs, ks, vs, ys, gs = specs(bwd_chunk)
    return pl.pallas_call(
        functools.partial(_retention_body, backward=True, cs=cs, heads=heads),
        grid=(batch, nc),
        in_specs=[smem, qs, ks, vs, ys, gs, pl.BlockSpec((1, wv), lambda b, t: (0, 0))],
        out_specs=ys,
        out_shape=jax.ShapeDtypeStruct((batch * seq, wv), BF16),
        scratch_shapes=scratch,
        compiler_params=_cp("parallel", "arbitrary"),
        name="retention_bwd",
    )(log_g[1].astype(F32), qk, qk, proj, y_f, proj, norm_g.reshape(1, wv).astype(F32))


S5N_CHUNK = 8
S5N_GROUPS = LANES // S5_CH
S5N_HALF = S5N_GROUPS * S5_STATE


def _s5n_params(a_re, a_im, log_dt, b_re, b_im, c_re, c_im, d_skip):
    L, ch, gl = S5N_CHUNK, S5_CH, S5N_GROUPS
    a_re, a_im = a_re.astype(F32), a_im.astype(F32)
    groups = a_re.shape[1]
    nb = groups // gl
    dt = jnp.exp(log_dt.astype(F32))[..., None]
    e = jnp.arange(L + 1, dtype=F32)[:, None, None, None]
    mag = jnp.exp(a_re * dt * e)
    pw_re, pw_im = mag * jnp.cos(a_im * dt * e), mag * jnp.sin(a_im * dt * e)
    ab_re, ab_im = pw_re[1], pw_im[1]
    den = a_re * a_re + a_im * a_im
    f_re = ((ab_re - 1.0) * a_re + ab_im * a_im) / den
    f_im = (ab_im * a_re - (ab_re - 1.0) * a_im) / den
    bb_re = f_re[..., None] * b_re - f_im[..., None] * b_im
    bb_im = f_re[..., None] * b_im + f_im[..., None] * b_re
    c_re, c_im = c_re.astype(F32), c_im.astype(F32)
    hp = lax.Precision.HIGHEST
    idx = jnp.arange(L)

    cp_re = c_re[None] * pw_re[:L, :, :, None, :] - c_im[None] * pw_im[:L, :, :, None, :]
    cp_im = c_re[None] * pw_im[:L, :, :, None, :] + c_im[None] * pw_re[:L, :, :, None, :]
    kmat = (jnp.einsum("ldgcp,dgpk->dglck", cp_re, bb_re, precision=hp)
            - jnp.einsum("ldgcp,dgpk->dglck", cp_im, bb_im, precision=hp))
    lag = idx[None, :] - idx[:, None]

    def toeplitz(k, lg):
        return jnp.where((lg >= 0)[None, :, :, None, None], k[:, jnp.clip(lg, 0, L - 1)], 0.0)

    t = toeplitz(kmat[0], lag) + toeplitz(kmat[1], -lag)
    skip = (idx[:, None] == idx[None, :])[None, :, :, None, None] * (
        jnp.eye(ch, dtype=F32)[None, None, None] * d_skip.astype(F32)[:, None, None, :, None])
    t = (t + skip).reshape(nb, gl, L, L, ch, ch)
    m_mat = t.transpose(0, 2, 1, 5, 3, 4).reshape(nb, L, LANES, L * ch)

    def state_in(d, exps):
        p_re, p_im = pw_re[exps, d], pw_im[exps, d]
        w_re = p_re[..., None] * bb_re[d][None] - p_im[..., None] * bb_im[d][None]
        w_im = p_re[..., None] * bb_im[d][None] + p_im[..., None] * bb_re[d][None]
        return jnp.stack([w_re, w_im], axis=0).transpose(2, 1, 4, 0, 3)

    w_full = jnp.stack([state_in(0, L - 1 - idx), state_in(1, idx)], axis=3)
    w_full = w_full.reshape(nb, gl, L, ch, 2, 2, S5_STATE)
    w_mat = w_full.transpose(0, 2, 1, 3, 4, 5, 6).reshape(nb, L, LANES, 4 * S5_STATE)

    def state_out(d, exps):
        p_re, p_im = pw_re[exps, d], pw_im[exps, d]
        v_re = c_re[d][None] * p_re[:, :, None, :] - c_im[d][None] * p_im[:, :, None, :]
        v_im = c_re[d][None] * p_im[:, :, None, :] + c_im[d][None] * p_re[:, :, None, :]
        return jnp.stack([v_re, -v_im], axis=0).transpose(2, 0, 4, 1, 3)

    v_full = jnp.stack([state_out(0, idx + 1), state_out(1, L - idx)], axis=1)
    v_full = v_full.reshape(nb, gl, 2, 2, S5_STATE, L, ch)
    v_mat = v_full.transpose(0, 2, 3, 1, 4, 5, 6).reshape(nb, 4 * S5N_HALF, L * ch)

    a_pow = jnp.stack([pw_re[L], pw_im[L]], axis=1).reshape(2, 2, nb * S5N_HALF)
    return m_mat.astype(BF16), w_mat.astype(BF16), v_mat.astype(BF16), a_pow


def _s5n_spreaders():
    L, ch, gl, st = S5N_CHUNK, S5_CH, S5N_GROUPS, S5_STATE
    e_out = jnp.einsum("ij,cd->icjd", jnp.eye(L), jnp.eye(ch))
    e_out = jnp.broadcast_to(e_out[:, :, :, None, :], (L, ch, L, gl, ch)).reshape(L * ch, L * LANES)
    e_st = jnp.broadcast_to(jnp.eye(4 * st).reshape(4 * st, 4, 1, st), (4 * st, 4, gl, st))
    return e_out.astype(BF16), e_st.reshape(4 * st, 4 * S5N_HALF).astype(BF16)


def _s5n_expand(compact, spread, row_group, col_group):
    full = jnp.dot(compact, spread, preferred_element_type=F32)
    rows = lax.broadcasted_iota(jnp.int32, full.shape, 0)
    cols = lax.broadcasted_iota(jnp.int32, full.shape, 1)
    return jnp.where(row_group(rows) == col_group(cols), full, 0.0).astype(BF16)


_S5N_IN_ROW_GROUP = lambda r: r // S5_CH
_S5N_STATE_GROUP = lambda c: (c % S5N_HALF) // S5_STATE
_S5N_OUT_COL_GROUP = lambda c: (c % LANES) // S5_CH


def _s5n_state_in_body(u_ref, w_ref, e_ref, o_ref, w_exp):
    @pl.when(pl.program_id(1) == 0)
    def _():
        for j in range(S5N_CHUNK):
            w_exp[j * LANES:(j + 1) * LANES, :] = _s5n_expand(w_ref[0, j], e_ref[...],
                                                              _S5N_IN_ROW_GROUP, _S5N_STATE_GROUP)

    u_cat = jnp.concatenate([u_ref[:, j, :] for j in range(S5N_CHUNK)], axis=1)
    acc = jnp.dot(u_cat, w_exp[...], preferred_element_type=F32)
    for d in range(2):
        for r in range(2):
            lo = (2 * d + r) * S5N_HALF
            o_ref[d, r] = acc[:, lo:lo + S5N_HALF]


def _s5n_scan_body(sr_ref, si_ref, a_ref, o_ref, *, nk, nk_ctx):
    d = pl.program_id(0)
    ar, ai = a_ref[0, 0], a_ref[0, 1]

    def step(k, carry):
        hr, hi = carry
        o_ref[0, 0, pl.ds(k, 1), :] = hr
        o_ref[0, 1, pl.ds(k, 1), :] = hi
        sr = sr_ref[0, 0, pl.ds(k, 1), :]
        si = si_ref[0, 0, pl.ds(k, 1), :]
        return ar * hr - ai * hi + sr, ar * hi + ai * hr + si

    zero = jnp.zeros_like(ar)
    unroll = 8 if (nk % 8 == 0 and nk_ctx % 8 == 0) else 1

    @pl.when(d == 0)
    def _():
        lax.fori_loop(0, nk, step, (zero, zero), unroll=unroll)

    @pl.when(d == 1)
    def _():
        hc = lax.fori_loop(0, nk_ctx, lambda t, c: step(nk_ctx - 1 - t, c), (zero, zero), unroll=unroll)
        lax.fori_loop(0, nk - nk_ctx, lambda t, c: step(nk - 1 - t, c), hc, unroll=unroll)


def _s5n_out_body(u_ref, m_ref, h_ref, v_ref, e_ref, o_ref, m_exp, v_exp):
    @pl.when(pl.program_id(1) == 0)
    def _():
        for j in range(S5N_CHUNK):
            m_exp[j * LANES:(j + 1) * LANES, :] = _s5n_expand(m_ref[0, j], e_ref[...],
                                                              _S5N_IN_ROW_GROUP, _S5N_OUT_COL_GROUP)
        v_exp[...] = _s5n_expand(v_ref[0], e_ref[...], _S5N_STATE_GROUP, _S5N_OUT_COL_GROUP)

    u_cat = jnp.concatenate([u_ref[:, j, :] for j in range(S5N_CHUNK)], axis=1)
    h_cat = jnp.concatenate([h_ref[d, r].astype(BF16) for d in range(2) for r in range(2)], axis=1)
    acc = (jnp.dot(u_cat, m_exp[...], preferred_element_type=F32)
           + jnp.dot(h_cat, v_exp[...], preferred_element_type=F32))
    y = jax.nn.gelu(acc).astype(o_ref.dtype)
    for i in range(S5N_CHUNK):
        o_ref[:, i, :] = y[:, i * LANES:(i + 1) * LANES]


def _s5n_mix(proj, su_off, params, *, batch, seq, ctx_len, gw):
    m_mat, w_mat, v_mat, a_pow = params
    L, half = S5N_CHUNK, S5N_HALF
    nb = gw // LANES
    nk, nk_ctx = seq // L, ctx_len // L
    assert su_off % LANES == 0 and seq % L == 0 and ctx_len % L == 0
    ub = su_off // LANES
    u3 = proj.reshape(batch * nk, L, proj.shape[1])
    lanes = batch * nb * half
    u_spec = pl.BlockSpec((nk, L, LANES), lambda n, b: (b, 0, ub + n))
    h_spec = pl.BlockSpec((2, 2, nk, half), lambda n, b: (0, 0, 0, b * nb + n))

    e_out, e_st = _s5n_spreaders()
    whole = lambda a: pl.BlockSpec(a.shape, lambda n, b: (0,) * a.ndim)
    s_in = pl.pallas_call(
        _s5n_state_in_body,
        grid=(nb, batch),
        in_specs=[u_spec, pl.BlockSpec((1,) + w_mat.shape[1:], lambda n, b: (n, 0, 0, 0)), whole(e_st)],
        out_specs=h_spec,
        out_shape=jax.ShapeDtypeStruct((2, 2, nk, lanes), F32),
        scratch_shapes=[pltpu.VMEM((L * LANES, 4 * half), BF16)],
        compiler_params=_cp("parallel", "arbitrary"),
        name="s5_state_in",
    )(u3, w_mat, e_st)

    wl = _pick_tile(lanes, S5_SCAN_LANES, LANES)
    part_spec = lambda r: pl.BlockSpec((1, 1, nk, wl), lambda d, w: (d, r, 0, w))
    a_lanes = jnp.tile(a_pow.reshape(2, 2, 1, nb * half), (1, 1, 1, batch))
    h_prev = pl.pallas_call(
        functools.partial(_s5n_scan_body, nk=nk, nk_ctx=nk_ctx),
        grid=(2, lanes // wl),
        in_specs=[part_spec(0), part_spec(1), pl.BlockSpec((1, 2, 1, wl), lambda d, w: (d, 0, 0, w))],
        out_specs=pl.BlockSpec((1, 2, nk, wl), lambda d, w: (d, 0, 0, w)),
        out_shape=jax.ShapeDtypeStruct((2, 2, nk, lanes), F32),
        compiler_params=_cp("parallel", "parallel"),
        name="s5_scan",
    )(s_in, s_in, a_lanes)

    y3 = pl.pallas_call(
        _s5n_out_body,
        grid=(nb, batch),
        in_specs=[u_spec,
                  pl.BlockSpec((1,) + m_mat.shape[1:], lambda n, b: (n, 0, 0, 0)),
                  h_spec,
                  pl.BlockSpec((1,) + v_mat.shape[1:], lambda n, b: (n, 0, 0)),
                  whole(e_out)],
        out_specs=pl.BlockSpec((nk, L, LANES), lambda n, b: (b, 0, n)),
        out_shape=jax.ShapeDtypeStruct((batch * nk, L, gw), BF16),
        scratch_shapes=[pltpu.VMEM((L * LANES, L * LANES), BF16), pltpu.VMEM((4 * half, L * LANES), BF16)],
        compiler_params=_cp("parallel", "arbitrary"),
        name="s5_out",
    )(u3, m_mat, h_prev, v_mat, e_out)
    return y3.reshape(batch * seq, gw)


def _expert_changed(te_ref):
    i = pl.program_id(1)
    return (i == 0) | (te_ref[i] != te_ref[jnp.maximum(i - 1, 0)])


def _moe_up_body(te_ref, x_ref, wg_ref, wu_ref, o_ref, wg_bf, wu_bf):
    @pl.when(_expert_changed(te_ref))
    def _():
        wg_bf[...] = wg_ref[0, 0].astype(BF16)
        wu_bf[...] = wu_ref[0, 0].astype(BF16)

    x = x_ref[...]
    a = jnp.dot(x, wg_bf[...], preferred_element_type=F32)
    b = jnp.dot(x, wu_bf[...], preferred_element_type=F32)
    o_ref[...] = (a * _sigmoid(a) * b).astype(o_ref.dtype)


def _moe_down_body(te_ref, h_ref, w_ref, rw_ref, o_ref, w_bf):
    @pl.when(_expert_changed(te_ref))
    def _():
        w_bf[...] = w_ref[0, 0].astype(BF16)

    y = jnp.dot(h_ref[...], w_bf[...], preferred_element_type=F32)
    o_ref[...] = (rw_ref[...] * y).astype(o_ref.dtype)


def _moe_down_into_body(te_ref, h_ref, w_ref, rw_ref, prev_ref, o_ref, w_bf):
    _moe_down_body(te_ref, h_ref, w_ref, rw_ref, o_ref, w_bf)


def _moe_combine_body(x_ref, y0_ref, y1_ref, gate_ref, o_ref, *, tm, tiles_per_batch, ctx_len):
    i = pl.program_id(0)
    is_ctx = (i % tiles_per_batch) * tm < ctx_len
    gate = jnp.where(is_ctx, gate_ref[0, 0:1, :], gate_ref[0, 1:2, :])
    o_ref[...] = x_ref[...] + gate * (y0_ref[...].astype(F32) + y1_ref[...].astype(F32))


def _moe_final_body(x_ref, y0_ref, y1_ref, gate_ref, g_ref, o_ref):
    x = x_ref[...] + gate_ref[0, 1:2, :] * (y0_ref[...].astype(F32) + y1_ref[...].astype(F32))
    o_ref[...] = x * lax.rsqrt(jnp.mean(x * x, axis=-1, keepdims=True) + NORM_EPS) * g_ref[...]


def _route(logits):
    assert MOE_TOPK == 2
    g_logit = logits[:, :MOE_GROUPS]
    g_prob = jax.nn.softmax(g_logit, axis=-1)
    g_idx = jnp.argmax(g_prob, axis=-1)
    g_p = jnp.max(g_prob, axis=-1)
    e_logit = logits[:, MOE_GROUPS:MOE_GROUPS + MOE_GROUPS * MOE_PER_GROUP]
    e_logit = e_logit.reshape(-1, MOE_GROUPS, MOE_PER_GROUP)
    sel = (jnp.arange(MOE_GROUPS)[None, :] == g_idx[:, None])[:, :, None]
    e_logit = jnp.sum(jnp.where(sel, e_logit, 0.0), axis=1)
    e_prob = jax.nn.softmax(e_logit, axis=-1)
    i0 = jnp.argmax(e_prob, axis=-1)
    p0 = jnp.max(e_prob, axis=-1)
    rest = jnp.where(jnp.arange(MOE_PER_GROUP)[None, :] == i0[:, None], -1.0, e_prob)
    i1 = jnp.argmax(rest, axis=-1)
    p1 = jnp.max(rest, axis=-1)
    e_p = jnp.stack([p0, p1], axis=-1)
    w = g_p[:, None] * e_p / jnp.sum(e_p, axis=-1, keepdims=True)
    ids = g_idx[:, None] * MOE_PER_GROUP + jnp.stack([i0, i1], axis=-1)
    return ids.astype(jnp.int32), w


def _moe(h, logits, x, gate, w_gate, w_up, w_down, layer, *, rows_per_batch, ctx_len, final_g=None):
    t, d = h.shape
    _, n_exp, _, dff = w_gate.shape
    tile = MOE_TILE
    ids, wts = _route(logits)
    flat_e = ids.reshape(-1)
    onehot = (flat_e[:, None] == jnp.arange(n_exp)[None, :]).astype(jnp.int32)
    counts = onehot.sum(0)
    rank = jnp.take_along_axis(jnp.cumsum(onehot, axis=0) - onehot, flat_e[:, None], axis=1)[:, 0]
    padded = (counts + tile - 1) // tile * tile
    starts = jnp.cumsum(padded) - padded
    pos = starts[flat_e] + rank
    n_rows = (t * MOE_TOPK // tile + n_exp) * tile
    n_tiles = n_rows // tile
    token = (jnp.arange(t * MOE_TOPK, dtype=jnp.int32) // MOE_TOPK).astype(F32)
    table = jnp.zeros((n_rows, 2), F32).at[pos].set(jnp.stack([token, wts.reshape(-1)], axis=1),
                                                    unique_indices=True, mode="promise_in_bounds")
    src = table[:, 0].astype(jnp.int32)
    row_w = table[:, 1]
    tile_start = jnp.arange(n_tiles, dtype=jnp.int32) * tile
    ends = starts + padded
    tile_e = jnp.minimum(jnp.sum(tile_start[:, None] >= ends[None, :], axis=1), n_exp - 1).astype(jnp.int32)

    n_chunks = max(c for c in (4, 2, 1) if n_tiles % c == 0)
    tpc = n_tiles // n_chunks
    tn_up = _pick_tile(dff, MOE_UP_COL_TILE, LANES)
    tn_dn = _pick_tile(d, MOE_DOWN_COL_TILE, LANES)
    w_up_spec = pl.BlockSpec((1, 1, d, tn_up), lambda j, i, te: (layer, te[i], 0, j))
    ys = None
    for c in range(n_chunks):
        rows_c = slice(c * tpc * tile, (c + 1) * tpc * tile)
        te_c = tile_e[c * tpc:(c + 1) * tpc]
        xs = h.at[src[rows_c]].get(mode="promise_in_bounds")
        hid = pl.pallas_call(
            _moe_up_body,
            grid_spec=pltpu.PrefetchScalarGridSpec(
                num_scalar_prefetch=1,
                grid=(dff // tn_up, tpc),
                in_specs=[pl.BlockSpec((tile, d), lambda j, i, te: (i, 0)), w_up_spec, w_up_spec],
                out_specs=pl.BlockSpec((tile, tn_up), lambda j, i, te: (i, j)),
                scratch_shapes=[pltpu.VMEM((d, tn_up), BF16), pltpu.VMEM((d, tn_up), BF16)]),
            out_shape=jax.ShapeDtypeStruct((tpc * tile, dff), BF16),
            compiler_params=_cp("arbitrary", "arbitrary"),
            name="moe_up",
        )(te_c, xs, w_gate, w_up)
        first = ys is None
        ys = pl.pallas_call(
            _moe_down_body if first else _moe_down_into_body,
            grid_spec=pltpu.PrefetchScalarGridSpec(
                num_scalar_prefetch=1,
                grid=(d // tn_dn, tpc),
                in_specs=[pl.BlockSpec((tile, dff), lambda j, i, te: (i, 0)),
                          pl.BlockSpec((1, 1, dff, tn_dn), lambda j, i, te: (layer, te[i], 0, j)),
                          pl.BlockSpec((tile, 1), lambda j, i, te: (i, 0))]
                         + ([] if first else [pl.BlockSpec(memory_space=pl.ANY)]),
                out_specs=pl.BlockSpec((tile, tn_dn), lambda j, i, te, c=c: (c * tpc + i, j)),
                scratch_shapes=[pltpu.VMEM((dff, tn_dn), BF16)]),
            out_shape=jax.ShapeDtypeStruct((n_rows, d), BF16),
            input_output_aliases={} if first else {4: 0},
            compiler_params=_cp("arbitrary", "arbitrary"),
            name="moe_down",
        )(te_c, hid, w_down, row_w[rows_c].reshape(tpc * tile, 1), *(() if first else (ys,)))

    pos2 = pos.reshape(t, MOE_TOPK)
    y0 = ys.at[pos2[:, 0]].get(mode="promise_in_bounds")
    y1 = ys.at[pos2[:, 1]].get(mode="promise_in_bounds")
    tm = _pick_tile(math.gcd(rows_per_batch, ctx_len), ROW_TILE, 8)
    tpb = rows_per_batch // tm
    if final_g is not None:
        batch = t // rows_per_batch
        ctx_tiles, lat_tiles = ctx_len // tm, (rows_per_batch - ctx_len) // tm
        lat_spec = pl.BlockSpec((tm, d), lambda b, i: (b * tpb + ctx_tiles + i, 0))
        out = pl.pallas_call(
            _moe_final_body,
            grid=(batch, lat_tiles),
            in_specs=[lat_spec, lat_spec, lat_spec, pl.BlockSpec((1, 2, d), lambda b, i: (b, 0, 0)),
                      pl.BlockSpec((1, d), lambda b, i: (0, 0))],
            out_specs=pl.BlockSpec((tm, d), lambda b, i: (b * lat_tiles + i, 0)),
            out_shape=jax.ShapeDtypeStruct((batch * lat_tiles * tm, d), F32),
            compiler_params=_cp("parallel", "parallel"),
            name="moe_combine_final",
        )(x, y0, y1, gate, final_g.reshape(1, d).astype(F32))
        return out.reshape(batch, lat_tiles * tm, d)
    row_spec = pl.BlockSpec((tm, d), lambda i: (i, 0))
    return pl.pallas_call(
        functools.partial(_moe_combine_body, tm=tm, tiles_per_batch=tpb, ctx_len=ctx_len),
        grid=(t // tm,),
        in_specs=[row_spec, row_spec, row_spec, pl.BlockSpec((1, 2, d), lambda i: (i // tpb, 0, 0))],
        out_specs=row_spec,
        out_shape=jax.ShapeDtypeStruct((t, d), F32),
        compiler_params=_cp("parallel"),
        name="moe_combine",
    )(x, y0, y1, gate)


def kernel(x, c, ctx, c_ctx, ada_w, ada_b, norm_mix, norm_ffn, w_in, w_out, diff_lambda, diff_subln,
           s5_a_re, s5_a_im, s5_log_dt, s5_b_re, s5_b_im, s5_c_re, s5_c_im, s5_d, s5_glu_w, s5_glu_b,
           mla_q_norm, mla_kv_norm, mla_w_uq, mla_w_ukv, ret_decay, ret_norm,
           moe_wg, moe_bg, moe_we, moe_be, moe_w_gate, moe_w_up, moe_w_down, final_norm):
    batch, n_lat, d = x.shape
    ctx_len = ctx.shape[1]
    depth = ada_w.shape[0]
    seq = ctx_len + n_lat
    rows = batch * seq
    gw = d // 4
    heads = gw // LANES
    q_rank, kv_rank = 3 * d // 16, d // 16
    ret_qk = heads * RET_K
    n_route = MOE_GROUPS + MOE_GROUPS * MOE_PER_GROUP
    assert heads % 2 == 0 and ctx_len % RET_CHUNK == 0 and n_lat % RET_CHUNK == 0

    splits = (gw, gw, gw, gw, q_rank, kv_rank, ROPE_DIM, ret_qk, ret_qk, gw, gw)
    offs = [0]
    for s_ in splits:
        offs.append(offs[-1] + s_)
    names = ("dq", "dk", "dv", "su", "mcq", "mckv", "mkr", "rq", "rk", "rv", "rg")
    src_col = {n_: (offs[i], offs[i + 1]) for i, n_ in enumerate(names)}
    order = ("dq", "dk", "rq", "rk", "dv", "su", "rv", "rg", "mcq", "mckv")
    col = {}
    pos = 0
    for n_ in order:
        col[n_] = pos
        pos += src_col[n_][1] - src_col[n_][0]
    n_main = pos
    n_rope = col["dv"]
    uq_cols = jnp.arange(heads * (MLA_NOPE + ROPE_DIM)).reshape(heads, MLA_NOPE + ROPE_DIM)
    uq_perm = jnp.concatenate([uq_cols[:, :MLA_NOPE].reshape(-1), uq_cols[:, MLA_NOPE:].reshape(-1)])

    tables = _rope_tables(n_lat, ctx_len)
    log2e = math.log2(math.e)
    rope_scale = jnp.ones((n_rope,), F32).at[col["rk"]:col["rk"] + ret_qk].set(RET_K ** -0.5)
    rope_scale = rope_scale.at[col["dq"]:col["dq"] + gw].set(DIFF_HEAD_DIM ** -0.5 * log2e)
    mla_q_scale = (MLA_NOPE + ROPE_DIM) ** -0.5 * log2e
    main_scale = jnp.concatenate([rope_scale, jnp.ones((n_main - n_rope,), F32)])

    cond = jnp.concatenate([c_ctx[None, :], c], axis=0)
    cond = jnp.pad(cond * _sigmoid(cond), ((0, 8 - (batch + 1) % 8 if (batch + 1) % 8 else 0), (0, 0)))

    mod_all = _ada_mod(cond, ada_w, ada_b)

    xa = None
    tm_big = _pick_tile(seq, MM_ROW_TILE)
    tn = lambda n_: _pick_tile(n_, MM_COL_TILE, LANES)

    for l in range(depth):
        lam_init = 0.8 - 0.6 * math.exp(-0.3 * l)
        mod = mod_all[l].reshape(cond.shape[0], 6, d)
        mods = [jnp.stack([jnp.broadcast_to(mod[0, i], (batch, d)), mod[1:batch + 1, i]], axis=1)
                for i in range(6)]

        w_main = jnp.concatenate([w_in[l, :, src_col[n_][0]:src_col[n_][1]] for n_ in order],
                                 axis=1).astype(BF16)
        w_kr = jnp.concatenate([w_in[l, :, src_col["mkr"][0]:src_col["mkr"][1]]] * (LANES // ROPE_DIM),
                               axis=1).astype(BF16)

        if l == 0:
            h, xa = _norm_mod_first(ctx, x, norm_mix[l], mods[0], mods[1])
        else:
            h = _norm_mod(xa, norm_mix[l], mods[0], mods[1], rows_per_batch=seq, ctx_len=ctx_len)
        proj = _mm([h], w_main, name="in_proj", out_dtype=BF16, tm=tm_big,
                   tn=tn(math.gcd(n_main, n_rope)), rope=(tables, main_scale, (0, n_rope)),
                   rows_per_batch=seq)
        krr = _mm([h], w_kr, name="in_proj_kr", out_dtype=BF16, tm=tm_big, tn=LANES,
                  rope=(tables, jnp.ones((LANES,), F32), (0, LANES)), rows_per_batch=seq)
        qk = proj

        lv = diff_lambda[l].astype(F32)
        lam = jnp.exp(jnp.sum(lv[0] * lv[1])) - jnp.exp(jnp.sum(lv[2] * lv[3])) + lam_init
        a_out = _diff_attn(qk, proj, lam, diff_subln[l], batch=batch, seq=seq, ctx_len=ctx_len,
                           heads=heads, q_blk=col["dq"] // LANES, k_blk=col["dk"] // LANES,
                           v_blk=col["dv"] // LANES, post=1.0 - lam_init)

        s5p = _s5n_params(s5_a_re[l], s5_a_im[l], s5_log_dt[l], s5_b_re[l], s5_b_im[l],
                          s5_c_re[l], s5_c_im[l], s5_d[l])
        s_act = _s5n_mix(proj, col["su"], s5p, batch=batch, seq=seq, ctx_len=ctx_len, gw=gw)
        s_out = _mm([s_act], s5_glu_w[l].astype(BF16), name="s5_glu", out_dtype=BF16, tm=tm_big, tn=tn(gw),
                    bias=s5_glu_b[l], glu_in=s_act)

        cq = proj[:, col["mcq"]:col["mcq"] + q_rank]
        ckv = proj[:, col["mckv"]:col["mckv"] + kv_rank]
        w_uq = (mla_w_uq[l][:, uq_perm] * mla_q_scale).astype(BF16)
        n_qn, n_q = heads * MLA_NOPE, heads * (MLA_NOPE + ROPE_DIM)
        q_up = _mm([cq], w_uq, name="mla_q_up", out_dtype=BF16, tm=tm_big, tn=tn(math.gcd(n_qn, n_q)),
                   norm_g=mla_q_norm[l], rope=(tables, jnp.ones((n_q,), F32), (n_qn, n_q)),
                   rows_per_batch=seq)
        kv_up = _mm([ckv], mla_w_ukv[l].astype(BF16), name="mla_kv_up", out_dtype=BF16, tm=tm_big,
                    tn=tn(mla_w_ukv.shape[2]), norm_g=mla_kv_norm[l])
        m_out = _mla_attn(q_up, kv_up, krr, batch=batch, seq=seq, ctx_len=ctx_len, heads=heads)

        log_g = jax.nn.log_sigmoid(ret_decay[l].astype(F32))
        r_out = _retention(qk, proj, log_g, ret_norm[l], batch=batch, seq=seq, ctx_len=ctx_len,
                           heads=heads, q_off=col["rq"], k_off=col["rk"], v_off=col["rv"], g_off=col["rg"])

        xa = _mm([a_out, s_out, m_out, r_out], w_out[l].astype(BF16), name="out_proj", out_dtype=F32, tm=tm_big, tn=tn(d),
                 res=xa, gate=mods[2], rows_per_batch=seq, ctx_len=ctx_len)

        w_r = jnp.concatenate([moe_wg[l], moe_we[l]], axis=1).astype(F32)
        w_r = jnp.pad(w_r, ((0, 0), (0, LANES - n_route)))
        w_r_hi = w_r.astype(BF16)
        w_r_lo = (w_r - w_r_hi.astype(F32)).astype(BF16)
        b_r = jnp.pad(jnp.concatenate([moe_bg[l], moe_be[l]]).astype(F32), (0, LANES - n_route))
        h, logits = _norm_mod(xa, norm_ffn[l], mods[3], mods[4], rows_per_batch=seq, ctx_len=ctx_len,
                              router=(w_r_hi, w_r_lo, b_r.reshape(1, LANES)))
        xa = _moe(h, logits, xa, mods[5], moe_w_gate, moe_w_up, moe_w_down, l,
                  rows_per_batch=seq, ctx_len=ctx_len, final_g=final_norm if l == depth - 1 else None)
    return xa
```

```python
import functools
import math

import jax
import jax.numpy as jnp
import numpy as np
from jax import lax
from jax.experimental import pallas as pl
from jax.experimental.pallas import tpu as pltpu

BF16 = jnp.bfloat16
F32 = jnp.float32

V7X_VMEM_BYTES = 64 * 2**20
VMEM_LIMIT = V7X_VMEM_BYTES - 12 * 2**20
LANES = 128

GRID_W = 64
ROPE_DIM = 64
ROPE_BASE = 10000.0
NORM_EPS = 1e-6
DIFF_HEAD_DIM = 64
S5_CH = 16
S5_STATE = 64
MLA_NOPE = 128
MLA_V = 128
RET_K = 64
RET_V = 128
RET_CHUNK = 128
MOE_GROUPS = 4
MOE_PER_GROUP = 4
MOE_TOPK = 2
ROPE_QUARTER = ROPE_DIM // 4

MM_ROW_TILE = 1088
MM_COL_TILE = 512
ROW_TILE = 256
ATTN_Q_TILE = 512
S5_SCAN_LANES = 2048
MOE_TILE = 256
MOE_UP_COL_TILE = 512
MOE_DOWN_COL_TILE = 4096


def _cp(*sem):
    return pltpu.CompilerParams(dimension_semantics=sem, vmem_limit_bytes=VMEM_LIMIT)


def _pick_tile(n, target, mult=16):
    best = None
    for t in range(mult, min(n, target) + 1, mult):
        if n % t == 0:
            best = t
    assert best is not None, (n, target)
    return best


def _sigmoid(x):
    return 1.0 / (1.0 + jnp.exp(-x))


def _rotate(x, cos, sin, swap):
    reps = x.shape[1] // LANES
    tile = lambda t: jnp.tile(t, (1, reps))
    partner = jnp.dot(x.astype(BF16), swap, preferred_element_type=F32)
    return x * tile(cos) + partner * tile(sin)


def _rope_swap(width):
    lane = np.arange(width)
    first = (lane // ROPE_QUARTER) % 2 == 0
    src = np.where(first, lane + ROPE_QUARTER, lane - ROPE_QUARTER)
    swap = np.zeros((width, width), np.float32)
    swap[src, lane] = np.where(first, -1.0, 1.0)
    return jnp.asarray(swap, BF16)


def _mm_body(*refs, nx, ksizes, has_norm, has_bias, epilogue, tm, tiles_per_batch, ctx_len, rope_tiles):
    x_refs = refs[:nx]
    w_ref = refs[nx]
    idx = nx + 1
    g_ref = b_ref = e_ref = res_ref = gate_ref = None
    if has_norm:
        g_ref = refs[idx]; idx += 1
    if has_bias:
        b_ref = refs[idx]; idx += 1
    if epilogue == "glu":
        e_ref = refs[idx]; idx += 1
    if epilogue == "resgate":
        res_ref, gate_ref = refs[idx], refs[idx + 1]; idx += 2
    if epilogue == "rope":
        cos_ref, sin_ref, swap_ref, cs_ref = refs[idx:idx + 4]; idx += 4
    o_ref = refs[idx]

    acc = None
    off = 0
    for xr, ks in zip(x_refs, ksizes):
        x = xr[...]
        if has_norm:
            xf = x.astype(F32)
            xf = xf * lax.rsqrt(jnp.mean(xf * xf, axis=-1, keepdims=True) + NORM_EPS)
            x = xf * g_ref[...]
        x = x.astype(BF16)
        w = w_ref[off:off + ks, :].astype(BF16)
        part = jnp.dot(x, w, preferred_element_type=F32)
        acc = part if acc is None else acc + part
        off += ks
    if has_bias:
        acc = acc + b_ref[...]
    if epilogue == "glu":
        acc = e_ref[...].astype(F32) * _sigmoid(acc)
    elif epilogue == "resgate":
        i = pl.program_id(0)
        row = (i % tiles_per_batch) * tm + lax.broadcasted_iota(jnp.int32, (tm, 1), 0)
        gate = jnp.where(row < ctx_len, gate_ref[0, 0:1, :], gate_ref[0, 1:2, :])
        acc = res_ref[...] + gate * acc
    if epilogue == "rope":
        j = pl.program_id(1)
        roped = (j >= rope_tiles[0]) & (j < rope_tiles[1])

        @pl.when(roped)
        def _():
            y = _rotate(acc, cos_ref[...], sin_ref[...], swap_ref[...]) * cs_ref[...]
            o_ref[...] = y.astype(o_ref.dtype)

        @pl.when(jnp.logical_not(roped))
        def _():
            o_ref[...] = acc.astype(o_ref.dtype)
    else:
        o_ref[...] = acc.astype(o_ref.dtype)


def _mm(xs, w, *, name, out_dtype, tm, tn, norm_g=None, bias=None, glu_in=None, res=None, gate=None,
        rope=None, rows_per_batch=None, ctx_len=0):
    m = xs[0].shape[0]
    ksizes = tuple(x.shape[1] for x in xs)
    k, n = w.shape
    assert sum(ksizes) == k and m % tm == 0 and n % tn == 0
    epilogue = ("glu" if glu_in is not None else "resgate" if res is not None
                else "rope" if rope is not None else None)
    tiles_per_batch = (rows_per_batch // tm) if rows_per_batch else 1
    rope_tiles = None
    in_specs = [pl.BlockSpec((tm, ks), lambda i, j: (i, 0)) for ks in ksizes]
    in_specs.append(pl.BlockSpec((k, tn), lambda i, j: (0, j)))
    args = list(xs) + [w]
    if norm_g is not None:
        in_specs.append(pl.BlockSpec((1, k), lambda i, j: (0, 0)))
        args.append(norm_g.reshape(1, k).astype(F32))
    if bias is not None:
        in_specs.append(pl.BlockSpec((1, tn), lambda i, j: (0, j)))
        args.append(bias.reshape(1, n).astype(F32))
    if epilogue == "glu":
        in_specs.append(pl.BlockSpec((tm, tn), lambda i, j: (i, j)))
        args.append(glu_in)
    if epilogue == "resgate":
        tpb = tiles_per_batch
        in_specs.append(pl.BlockSpec((tm, tn), lambda i, j: (i, j)))
        in_specs.append(pl.BlockSpec((1, 2, tn), lambda i, j: (i // tpb, 0, j)))
        args += [res, gate]
    if epilogue == "rope":
        tables, col_scale, (lo, hi) = rope
        assert lo % tn == 0 and hi % tn == 0
        rope_tiles = (lo // tn, hi // tn)
        tpb = tiles_per_batch
        in_specs += [pl.BlockSpec((tm, LANES), lambda i, j: (i % tpb, 0))] * 2
        in_specs.append(pl.BlockSpec((tn, tn), lambda i, j: (0, 0)))
        in_specs.append(pl.BlockSpec((1, tn), lambda i, j: (0, j)))
        args += list(tables) + [_rope_swap(tn), col_scale.reshape(1, n).astype(F32)]
    body = functools.partial(_mm_body, nx=len(xs), ksizes=ksizes, has_norm=norm_g is not None,
                             has_bias=bias is not None, epilogue=epilogue, tm=tm,
                             tiles_per_batch=tiles_per_batch, ctx_len=ctx_len, rope_tiles=rope_tiles)
    return pl.pallas_call(
        body,
        grid=(m // tm, n // tn),
        in_specs=in_specs,
        out_specs=pl.BlockSpec((tm, tn), lambda i, j: (i, j)),
        out_shape=jax.ShapeDtypeStruct((m, n), out_dtype),
        compiler_params=_cp("parallel", "arbitrary"),
        name=name,
    )(*args)


def _ada_body(c_ref, w_ref, b_ref, o_ref):
    acc = jnp.dot(c_ref[...].astype(BF16), w_ref[0].astype(BF16), preferred_element_type=F32)
    o_ref[0] = acc + b_ref[0]


def _ada_mod(cond, ada_w, ada_b):
    depth, d, n6 = ada_w.shape
    rows = cond.shape[0]
    tn = _pick_tile(n6, MM_COL_TILE, LANES)
    return pl.pallas_call(
        _ada_body,
        grid=(depth, n6 // tn),
        in_specs=[pl.BlockSpec((rows, d), lambda l, j: (0, 0)),
                  pl.BlockSpec((1, d, tn), lambda l, j: (l, 0, j)),
                  pl.BlockSpec((1, 1, tn), lambda l, j: (l, 0, j))],
        out_specs=pl.BlockSpec((1, rows, tn), lambda l, j: (l, 0, j)),
        out_shape=jax.ShapeDtypeStruct((depth, rows, n6), F32),
        compiler_params=_cp("parallel", "arbitrary"),
        name="ada_mod",
    )(cond, ada_w, ada_b.reshape(depth, 1, n6).astype(F32))


def _norm_mod_body(*refs, tm, tiles_per_batch, ctx_len, router):
    if router:
        x_ref, g_ref, sh_ref, sc_ref, whi_ref, wlo_ref, br_ref, h_ref, lg_ref = refs
    else:
        x_ref, g_ref, sh_ref, sc_ref, h_ref = refs
    i = pl.program_id(0)
    x = x_ref[...]
    y = x * lax.rsqrt(jnp.mean(x * x, axis=-1, keepdims=True) + NORM_EPS) * g_ref[...]
    is_ctx = (i % tiles_per_batch) * tm < ctx_len
    sh = jnp.where(is_ctx, sh_ref[0, 0:1, :], sh_ref[0, 1:2, :])
    sc = jnp.where(is_ctx, sc_ref[0, 0:1, :], sc_ref[0, 1:2, :])
    h = y * (1.0 + sc) + sh
    h_ref[...] = h.astype(BF16)
    if router:
        hi = h.astype(BF16)
        lo = (h - hi.astype(F32)).astype(BF16)
        lg = jnp.dot(hi, whi_ref[...], preferred_element_type=F32)
        lg = lg + jnp.dot(hi, wlo_ref[...], preferred_element_type=F32)
        lg = lg + jnp.dot(lo, whi_ref[...], preferred_element_type=F32)
        lg_ref[...] = lg + br_ref[...]


def _norm_mod_first_body(c_ref, x_ref, g_ref, sh_ref, sc_ref, h_ref, xa_ref, *, ctx_tiles):
    is_ctx = pl.program_id(1) < ctx_tiles
    x = jnp.where(is_ctx, c_ref[...], x_ref[...])
    xa_ref[...] = x
    y = x * lax.rsqrt(jnp.mean(x * x, axis=-1, keepdims=True) + NORM_EPS) * g_ref[...]
    sh = jnp.where(is_ctx, sh_ref[0, 0:1, :], sh_ref[0, 1:2, :])
    sc = jnp.where(is_ctx, sc_ref[0, 0:1, :], sc_ref[0, 1:2, :])
    h_ref[...] = (y * (1.0 + sc) + sh).astype(BF16)


def _norm_mod_first(ctx, x, g, shift, scale):
    batch, ctx_len, d = ctx.shape
    n_lat = x.shape[1]
    tm = _pick_tile(math.gcd(n_lat, ctx_len), ROW_TILE, 8)
    ctx_tiles, lat_tiles = ctx_len // tm, n_lat // tm
    tpb = ctx_tiles + lat_tiles
    rows = batch * tpb * tm
    out_spec = pl.BlockSpec((tm, d), lambda b, i: (b * tpb + i, 0))
    mod_spec = pl.BlockSpec((1, 2, d), lambda b, i: (b, 0, 0))
    return pl.pallas_call(
        functools.partial(_norm_mod_first_body, ctx_tiles=ctx_tiles),
        grid=(batch, tpb),
        in_specs=[pl.BlockSpec((tm, d), lambda b, i: (b * ctx_tiles + jnp.minimum(i, ctx_tiles - 1), 0)),
                  pl.BlockSpec((tm, d), lambda b, i: (b * lat_tiles + jnp.maximum(i - ctx_tiles, 0), 0)),
                  pl.BlockSpec((1, d), lambda b, i: (0, 0)), mod_spec, mod_spec],
        out_specs=[out_spec, out_spec],
        out_shape=[jax.ShapeDtypeStruct((rows, d), BF16), jax.ShapeDtypeStruct((rows, d), F32)],
        compiler_params=_cp("parallel", "arbitrary"),
        name="norm_mod_first",
    )(ctx.reshape(batch * ctx_len, d), x.reshape(batch * n_lat, d), g.reshape(1, d), shift, scale)


def _norm_mod(x, g, shift, scale, *, rows_per_batch, ctx_len, router=None):
    m, d = x.shape
    tm = _pick_tile(math.gcd(rows_per_batch, ctx_len), ROW_TILE, 8)
    tpb = rows_per_batch // tm
    in_specs = [
        pl.BlockSpec((tm, d), lambda i: (i, 0)),
        pl.BlockSpec((1, d), lambda i: (0, 0)),
        pl.BlockSpec((1, 2, d), lambda i: (i // tpb, 0, 0)),
        pl.BlockSpec((1, 2, d), lambda i: (i // tpb, 0, 0)),
    ]
    args = [x, g.reshape(1, d), shift, scale]
    out_specs = [pl.BlockSpec((tm, d), lambda i: (i, 0))]
    out_shape = [jax.ShapeDtypeStruct((m, d), BF16)]
    if router is not None:
        whi, wlo, br = router
        in_specs += [pl.BlockSpec((d, LANES), lambda i: (0, 0)),
                     pl.BlockSpec((d, LANES), lambda i: (0, 0)),
                     pl.BlockSpec((1, LANES), lambda i: (0, 0))]
        args += [whi, wlo, br]
        out_specs.append(pl.BlockSpec((tm, LANES), lambda i: (i, 0)))
        out_shape.append(jax.ShapeDtypeStruct((m, LANES), F32))
    body = functools.partial(_norm_mod_body, tm=tm, tiles_per_batch=tpb, ctx_len=ctx_len,
                             router=router is not None)
    outs = pl.pallas_call(body, grid=(m // tm,), in_specs=in_specs, out_specs=out_specs,
                          out_shape=out_shape, compiler_params=_cp("parallel"),
                          name="norm_mod_router" if router is not None else "norm_mod")(*args)
    return outs if router is not None else outs[0]


def _rope_tables(n_lat, ctx_len):
    rows = n_lat // GRID_W
    row = jnp.repeat(jnp.arange(rows, dtype=F32), GRID_W)
    col = jnp.tile(jnp.arange(GRID_W, dtype=F32), rows)
    quarter = ROPE_QUARTER
    inv = ROPE_BASE ** (-jnp.arange(quarter, dtype=F32) / quarter)
    ar = row[:, None] * inv
    ac = col[:, None] * inv
    ang = jnp.concatenate([ar, ar, ac, ac], axis=-1)
    ang = jnp.concatenate([jnp.zeros((ctx_len, ROPE_DIM), F32), ang], axis=0)
    ang = jnp.tile(ang, (1, LANES // ROPE_DIM))
    return jnp.cos(ang), jnp.sin(ang)


ATTN_ALIGN = 256


def _softmax_pv(q, k_ref, va_ref, bounds):
    ms, ovs = [], []
    for lo, hi in bounds:
        s = lax.dot_general(q, k_ref[lo:hi, :], (((1,), (1,)), ((), ())), preferred_element_type=F32)
        m = jnp.max(s, axis=-1, keepdims=True)
        e = jnp.exp2(s - m).astype(BF16)
        ovs.append(jnp.dot(e, va_ref[lo:hi, :], preferred_element_type=F32))
        ms.append(m)
    m_all = functools.reduce(jnp.maximum, ms)
    acc = sum(ov * jnp.exp2(m - m_all) for m, ov in zip(ms, ovs))
    return acc[:, :LANES] / acc[:, LANES:]


def _key_chunks(n_keys):
    if n_keys < 2 * ATTN_ALIGN:
        return ((0, n_keys),)
    half = (n_keys // ATTN_ALIGN + 1) // 2 * ATTN_ALIGN
    return ((0, half), (half, n_keys))


def _lat_tile(seq, ctx_len):
    n_lat = seq - ctx_len
    assert ctx_len % ATTN_ALIGN == 0 and n_lat % ATTN_ALIGN == 0
    return _pick_tile(n_lat, ATTN_Q_TILE, ATTN_ALIGN)


def _lat_rows(seq, ctx_len, tq):
    return lambda b, i: pl.multiple_of(b * seq + ctx_len + i * tq, ATTN_ALIGN)


def _diff_attn_body(lam_ref, q_ref, k_ref, v_ref, sub_ref, *rest, post, bounds):
    o_ref, va_ref = rest[-2:]

    def fill():
        va_ref[:, :LANES] = v_ref[...]
        va_ref[:, LANES:] = jnp.ones(v_ref.shape, BF16)

    if len(rest) == 2:
        pl.when(pl.program_id(2) == 0)(fill)
    else:
        fill()

    q = q_ref[...]
    lane = lax.broadcasted_iota(jnp.int32, (1, LANES), 1)
    first = lane < DIFF_HEAD_DIM
    zero = jnp.zeros_like(q)
    q0 = jnp.where(first, q, zero)
    q1 = jnp.where(first, zero, q)
    o = _softmax_pv(q0, k_ref, va_ref, bounds) - lam_ref[0] * _softmax_pv(q1, k_ref, va_ref, bounds)
    o = o * lax.rsqrt(jnp.mean(o * o, axis=-1, keepdims=True) + NORM_EPS) * sub_ref[...] * post
    o_ref[...] = o.astype(o_ref.dtype)


def _diff_attn(qk, proj, lam, subln, *, batch, seq, ctx_len, heads, q_blk, k_blk, v_blk, post):
    tq = _lat_tile(seq, ctx_len)
    rows = _lat_rows(seq, ctx_len, tq)
    smem = pl.BlockSpec(memory_space=pltpu.SMEM)
    args = (lam.reshape(1).astype(F32), qk, qk, proj, subln.reshape(1, LANES).astype(F32))
    out_shape = jax.ShapeDtypeStruct((batch * seq, heads * LANES), BF16)
    elem = (pl.Element(tq), pl.Element(LANES))
    lat = pl.pallas_call(
        functools.partial(_diff_attn_body, post=post, bounds=_key_chunks(seq)),
        grid=(batch, heads, (seq - ctx_len) // tq),
        in_specs=[
            smem,
            pl.BlockSpec(elem, lambda b, h, i: (rows(b, i), pl.multiple_of((q_blk + h) * LANES, LANES))),
            pl.BlockSpec((seq, LANES), lambda b, h, i: (b, k_blk + h)),
            pl.BlockSpec((seq, LANES), lambda b, h, i: (b, v_blk + h)),
            pl.BlockSpec((1, LANES), lambda b, h, i: (0, 0)),
        ],
        out_specs=pl.BlockSpec(elem, lambda b, h, i: (rows(b, i), pl.multiple_of(h * LANES, LANES))),
        out_shape=out_shape,
        scratch_shapes=[pltpu.VMEM((seq, 2 * LANES), BF16)],
        compiler_params=_cp("parallel", "parallel", "arbitrary"),
        name="diff_attn",
    )(*args)
    cpb = seq // ctx_len
    ctx_spec = lambda blk: pl.BlockSpec((ctx_len, LANES), lambda b, h: (b * cpb, blk + h))
    return pl.pallas_call(
        functools.partial(_diff_attn_body, post=post, bounds=_key_chunks(ctx_len)),
        grid=(batch, heads),
        in_specs=[smem, ctx_spec(q_blk), ctx_spec(k_blk), ctx_spec(v_blk),
                  pl.BlockSpec((1, LANES), lambda b, h: (0, 0)), pl.BlockSpec(memory_space=pl.ANY)],
        out_specs=ctx_spec(0),
        out_shape=out_shape,
        scratch_shapes=[pltpu.VMEM((ctx_len, 2 * LANES), BF16)],
        input_output_aliases={5: 0},
        compiler_params=_cp("parallel", "parallel"),
        name="diff_attn_ctx",
    )(*args, lat)


def _mla_attn_body(qn_ref, qr_ref, kn_ref, kr_ref, v_ref, *rest, bounds):
    o_ref, ka_ref, va_ref = rest[-3:]
    h = pl.program_id(1)

    def fill():
        ka_ref[:, :LANES] = kn_ref[...]
        ka_ref[:, LANES:] = kr_ref[...]
        va_ref[:, :LANES] = v_ref[...]
        va_ref[:, LANES:] = jnp.ones(v_ref.shape, BF16)

    if len(rest) == 3:
        pl.when(pl.program_id(2) == 0)(fill)
    else:
        fill()

    qr = qr_ref[...]
    lane = lax.broadcasted_iota(jnp.int32, (1, LANES), 1)
    mine = (lane < ROPE_DIM) == (h % 2 == 0)
    qr = jnp.where(mine, qr, jnp.zeros_like(qr))
    q = jnp.concatenate([qn_ref[...], qr], axis=1)
    o_ref[...] = _softmax_pv(q, ka_ref, va_ref, bounds).astype(o_ref.dtype)


def _mla_attn(q_up, kv_up, k_rope, *, batch, seq, ctx_len, heads):
    tq = _lat_tile(seq, ctx_len)
    rows = _lat_rows(seq, ctx_len, tq)
    args = (q_up, q_up, kv_up, k_rope, kv_up)
    out_shape = jax.ShapeDtypeStruct((batch * seq, heads * LANES), BF16)
    elem = (pl.Element(tq), pl.Element(LANES))
    lat = pl.pallas_call(
        functools.partial(_mla_attn_body, bounds=_key_chunks(seq)),
        grid=(batch, heads, (seq - ctx_len) // tq),
        in_specs=[
            pl.BlockSpec(elem, lambda b, h, i: (rows(b, i), pl.multiple_of(h * LANES, LANES))),
            pl.BlockSpec(elem, lambda b, h, i: (rows(b, i), pl.multiple_of((heads + h // 2) * LANES, LANES))),
            pl.BlockSpec((seq, LANES), lambda b, h, i: (b, 2 * h)),
            pl.BlockSpec((seq, LANES), lambda b, h, i: (b, 0)),
            pl.BlockSpec((seq, LANES), lambda b, h, i: (b, 2 * h + 1)),
        ],
        out_specs=pl.BlockSpec(elem, lambda b, h, i: (rows(b, i), pl.multiple_of(h * LANES, LANES))),
        out_shape=out_shape,
        scratch_shapes=[pltpu.VMEM((seq, 2 * LANES), BF16), pltpu.VMEM((seq, 2 * LANES), BF16)],
        compiler_params=_cp("parallel", "parallel", "arbitrary"),
        name="mla_attn",
    )(*args)
    cpb = seq // ctx_len
    ctx_spec = lambda col: pl.BlockSpec((ctx_len, LANES), lambda b, h: (b * cpb, col(h)))
    return pl.pallas_call(
        functools.partial(_mla_attn_body, bounds=_key_chunks(ctx_len)),
        grid=(batch, heads),
        in_specs=[ctx_spec(lambda h: h), ctx_spec(lambda h: heads + h // 2), ctx_spec(lambda h: 2 * h),
                  ctx_spec(lambda h: 0), ctx_spec(lambda h: 2 * h + 1), pl.BlockSpec(memory_space=pl.ANY)],
        out_specs=ctx_spec(lambda h: h),
        out_shape=out_shape,
        scratch_shapes=[pltpu.VMEM((ctx_len, 2 * LANES), BF16), pltpu.VMEM((ctx_len, 2 * LANES), BF16)],
        input_output_aliases={5: 0},
        compiler_params=_cp("parallel", "parallel"),
        name="mla_attn_ctx",
    )(*args, lat)


def _retention_body(*refs, backward, cs, heads):
    if backward:
        lg_ref, q_ref, k_ref, v_ref, yf_ref, gate_ref, ng_ref, o_ref, s_ref, d_ref, qd_ref, kd_ref = refs
    else:
        lg_ref, q_ref, k_ref, v_ref, o_ref, s_ref, d_ref, qd_ref, kd_ref = refs
    t = pl.program_id(1)

    @pl.when(t == 0)
    def _():
        s_ref[...] = jnp.zeros_like(s_ref)
        pos_r = lax.broadcasted_iota(jnp.int32, (cs, cs), 0).astype(F32)
        pos_c = lax.broadcasted_iota(jnp.int32, (cs, cs), 1).astype(F32)
        pos = lax.broadcasted_iota(jnp.int32, (cs, 1), 0).astype(F32)
        for h in range(heads):
            lg = lg_ref[h]
            if backward:
                diff = pos_c - pos_r
                keep = diff > 0
                qd_ref[h] = jnp.exp(lg * (cs - pos))
                kd_ref[h] = jnp.exp(lg * pos)
            else:
                diff = pos_r - pos_c
                keep = diff >= 0
                qd_ref[h] = jnp.exp(lg * (pos + 1.0))
                kd_ref[h] = jnp.exp(lg * (cs - 1.0 - pos))
            d_ref[h] = jnp.where(keep, jnp.exp(lg * jnp.maximum(diff, 0.0)), 0.0)

    lane = lax.broadcasted_iota(jnp.int32, (1, LANES), 1)
    nt = (((1,), (1,)), ((), ()))
    tn = (((0,), (0,)), ((), ()))
    for h in range(heads):
        blk = slice((h // 2) * LANES, (h // 2 + 1) * LANES)
        col = slice(h * RET_V, (h + 1) * RET_V)
        mine = (lane < RET_K) == (h % 2 == 0)
        q = q_ref[:, blk]
        q = jnp.where(mine, q, jnp.zeros_like(q))
        k = k_ref[:, blk]
        v = v_ref[:, col]
        scores = lax.dot_general(q, k, nt, preferred_element_type=F32) * d_ref[h]
        intra = jnp.dot(scores.astype(BF16), v, preferred_element_type=F32)
        state = s_ref[h]
        q_w = (q.astype(F32) * qd_ref[h]).astype(BF16)
        cross = jnp.dot(q_w, state.astype(BF16), preferred_element_type=F32)
        k_w = (k.astype(F32) * kd_ref[h]).astype(BF16)
        upd = lax.dot_general(k_w, v, tn, preferred_element_type=F32)
        s_ref[h] = jnp.exp(lg_ref[h] * cs) * state + upd
        y = intra + cross
        if backward:
            y = y + yf_ref[:, col].astype(F32)
            y = y * lax.rsqrt(jnp.mean(y * y, axis=-1, keepdims=True) + NORM_EPS) * ng_ref[:, col]
            g = gate_ref[:, col].astype(F32)
            o_ref[:, col] = (g * _sigmoid(g) * y).astype(o_ref.dtype)
        else:
            o_ref[:, col] = y.astype(o_ref.dtype)


def _retention(qk, proj, log_g, norm_g, *, batch, seq, ctx_len, heads, q_off, k_off, v_off, g_off):
    cs = RET_CHUNK
    nc, nc_ctx = seq // cs, ctx_len // cs
    wqk, wv = heads * RET_K, heads * RET_V
    assert q_off % wqk == 0 and k_off % wqk == 0 and v_off % wv == 0 and g_off % wv == 0
    smem = pl.BlockSpec(memory_space=pltpu.SMEM)

    def fwd_chunk(t):
        return t

    def bwd_chunk(t):
        return jnp.where(t < nc_ctx, nc_ctx - 1 - t, nc - 1 - (t - nc_ctx))

    def specs(chunk):
        row = lambda b, t: b * nc + chunk(t)
        return (pl.BlockSpec((cs, wqk), lambda b, t: (row(b, t), q_off // wqk)),
                pl.BlockSpec((cs, wqk), lambda b, t: (row(b, t), k_off // wqk)),
                pl.BlockSpec((cs, wv), lambda b, t: (row(b, t), v_off // wv)),
                pl.BlockSpec((cs, wv), lambda b, t: (row(b, t), 0)),
                pl.BlockSpec((cs, wv), lambda b, t: (row(b, t), g_off // wv)))

    scratch = [pltpu.VMEM((heads, LANES, RET_V), F32), pltpu.VMEM((heads, cs, cs), F32),
               pltpu.VMEM((heads, cs, 1), F32), pltpu.VMEM((heads, cs, 1), F32)]
    qs, ks, vs, ys, gs = specs(fwd_chunk)
    y_f = pl.pallas_call(
        functools.partial(_retention_body, backward=False, cs=cs, heads=heads),
        grid=(batch, nc),
        in_specs=[smem, qs, ks, vs],
        out_specs=ys,
        out_shape=jax.ShapeDtypeStruct((batch * seq, wv), BF16),
        scratch_shapes=scratch,
        compiler_params=_cp("parallel", "arbitrary"),
        name="retention_fwd",
    )(log_g[0].astype(F32), qk, qk, proj)
    qs, ks, vs, ys, gs = specs(bwd_chunk)
    return pl.pallas_call(
        functools.partial(_retention_body, backward=True, cs=cs, heads=heads),
        grid=(batch, nc),
        in_specs=[smem, qs, ks, vs, ys, gs, pl.BlockSpec((1, wv), lambda b, t: (0, 0))],
        out_specs=ys,
        out_shape=jax.ShapeDtypeStruct((batch * seq, wv), BF16),
        scratch_shapes=scratch,
        compiler_params=_cp("parallel", "arbitrary"),
        name="retention_bwd",
    )(log_g[1].astype(F32), qk, qk, proj, y_f, proj, norm_g.reshape(1, wv).astype(F32))


S5N_CHUNK = 8
S5N_GROUPS = LANES // S5_CH
S5N_HALF = S5N_GROUPS * S5_STATE


def _s5n_params(a_re, a_im, log_dt, b_re, b_im, c_re, c_im, d_skip):
    L, ch, gl = S5N_CHUNK, S5_CH, S5N_GROUPS
    a_re, a_im = a_re.astype(F32), a_im.astype(F32)
    groups = a_re.shape[1]
    nb = groups // gl
    dt = jnp.exp(log_dt.astype(F32))[..., None]
    e = jnp.arange(L + 1, dtype=F32)[:, None, None, None]
    mag = jnp.exp(a_re * dt * e)
    pw_re, pw_im = mag * jnp.cos(a_im * dt * e), mag * jnp.sin(a_im * dt * e)
    ab_re, ab_im = pw_re[1], pw_im[1]
    den = a_re * a_re + a_im * a_im
    f_re = ((ab_re - 1.0) * a_re + ab_im * a_im) / den
    f_im = (ab_im * a_re - (ab_re - 1.0) * a_im) / den
    bb_re = f_re[..., None] * b_re - f_im[..., None] * b_im
    bb_im = f_re[..., None] * b_im + f_im[..., None] * b_re
    c_re, c_im = c_re.astype(F32), c_im.astype(F32)
    hp = lax.Precision.HIGHEST
    idx = jnp.arange(L)

    cp_re = c_re[None] * pw_re[:L, :, :, None, :] - c_im[None] * pw_im[:L, :, :, None, :]
    cp_im = c_re[None] * pw_im[:L, :, :, None, :] + c_im[None] * pw_re[:L, :, :, None, :]
    kmat = (jnp.einsum("ldgcp,dgpk->dglck", cp_re, bb_re, precision=hp)
            - jnp.einsum("ldgcp,dgpk->dglck", cp_im, bb_im, precision=hp))
    lag = idx[None, :] - idx[:, None]

    def toeplitz(k, lg):
        return jnp.where((lg >= 0)[None, :, :, None, None], k[:, jnp.clip(lg, 0, L - 1)], 0.0)

    t = toeplitz(kmat[0], lag) + toeplitz(kmat[1], -lag)
    skip = (idx[:, None] == idx[None, :])[None, :, :, None, None] * (
        jnp.eye(ch, dtype=F32)[None, None, None] * d_skip.astype(F32)[:, None, None, :, None])
    t = (t + skip).reshape(nb, gl, L, L, ch, ch)
    m_mat = t.transpose(0, 2, 1, 5, 3, 4).reshape(nb, L, LANES, L * ch)

    def state_in(d, exps):
        p_re, p_im = pw_re[exps, d], pw_im[exps, d]
        w_re = p_re[..., None] * bb_re[d][None] - p_im[..., None] * bb_im[d][None]
        w_im = p_re[..., None] * bb_im[d][None] + p_im[..., None] * bb_re[d][None]
        return jnp.stack([w_re, w_im], axis=0).transpose(2, 1, 4, 0, 3)

    w_full = jnp.stack([state_in(0, L - 1 - idx), state_in(1, idx)], axis=3)
    w_full = w_full.reshape(nb, gl, L, ch, 2, 2, S5_STATE)
    w_mat = w_full.transpose(0, 2, 1, 3, 4, 5, 6).reshape(nb, L, LANES, 4 * S5_STATE)

    def state_out(d, exps):
        p_re, p_im = pw_re[exps, d], pw_im[exps, d]
        v_re = c_re[d][None] * p_re[:, :, None, :] - c_im[d][None] * p_im[:, :, None, :]
        v_im = c_re[d][None] * p_im[:, :, None, :] + c_im[d][None] * p_re[:, :, None, :]
        return jnp.stack([v_re, -v_im], axis=0).transpose(2, 0, 4, 1, 3)

    v_full = jnp.stack([state_out(0, idx + 1), state_out(1, L - idx)], axis=1)
    v_full = v_full.reshape(nb, gl, 2, 2, S5_STATE, L, ch)
    v_mat = v_full.transpose(0, 2, 3, 1, 4, 5, 6).reshape(nb, 4 * S5N_HALF, L * ch)

    a_pow = jnp.stack([pw_re[L], pw_im[L]], axis=1).reshape(2, 2, nb * S5N_HALF)
    return m_mat.astype(BF16), w_mat.astype(BF16), v_mat.astype(BF16), a_pow


def _s5n_spreaders():
    L, ch, gl, st = S5N_CHUNK, S5_CH, S5N_GROUPS, S5_STATE
    e_out = jnp.einsum("ij,cd->icjd", jnp.eye(L), jnp.eye(ch))
    e_out = jnp.broadcast_to(e_out[:, :, :, None, :], (L, ch, L, gl, ch)).reshape(L * ch, L * LANES)
    e_st = jnp.broadcast_to(jnp.eye(4 * st).reshape(4 * st, 4, 1, st), (4 * st, 4, gl, st))
    return e_out.astype(BF16), e_st.reshape(4 * st, 4 * S5N_HALF).astype(BF16)


def _s5n_expand(compact, spread, row_group, col_group):
    full = jnp.dot(compact, spread, preferred_element_type=F32)
    rows = lax.broadcasted_iota(jnp.int32, full.shape, 0)
    cols = lax.broadcasted_iota(jnp.int32, full.shape, 1)
    return jnp.where(row_group(rows) == col_group(cols), full, 0.0).astype(BF16)


_S5N_IN_ROW_GROUP = lambda r: r // S5_CH
_S5N_STATE_GROUP = lambda c: (c % S5N_HALF) // S5_STATE
_S5N_OUT_COL_GROUP = lambda c: (c % LANES) // S5_CH


def _s5n_state_in_body(u_ref, w_ref, e_ref, o_ref, w_exp):
    @pl.when(pl.program_id(1) == 0)
    def _():
        for j in range(S5N_CHUNK):
            w_exp[j * LANES:(j + 1) * LANES, :] = _s5n_expand(w_ref[0, j], e_ref[...],
                                                              _S5N_IN_ROW_GROUP, _S5N_STATE_GROUP)

    u_cat = jnp.concatenate([u_ref[:, j, :] for j in range(S5N_CHUNK)], axis=1)
    acc = jnp.dot(u_cat, w_exp[...], preferred_element_type=F32)
    for d in range(2):
        for r in range(2):
            lo = (2 * d + r) * S5N_HALF
            o_ref[d, r] = acc[:, lo:lo + S5N_HALF]


def _s5n_scan_body(sr_ref, si_ref, a_ref, o_ref, *, nk, nk_ctx):
    d = pl.program_id(0)
    ar, ai = a_ref[0, 0], a_ref[0, 1]

    def step(k, carry):
        hr, hi = carry
        o_ref[0, 0, pl.ds(k, 1), :] = hr
        o_ref[0, 1, pl.ds(k, 1), :] = hi
        sr = sr_ref[0, 0, pl.ds(k, 1), :]
        si = si_ref[0, 0, pl.ds(k, 1), :]
        return ar * hr - ai * hi + sr, ar * hi + ai * hr + si

    zero = jnp.zeros_like(ar)
    unroll = 8 if (nk % 8 == 0 and nk_ctx % 8 == 0) else 1

    @pl.when(d == 0)
    def _():
        lax.fori_loop(0, nk, step, (zero, zero), unroll=unroll)

    @pl.when(d == 1)
    def _():
        hc = lax.fori_loop(0, nk_ctx, lambda t, c: step(nk_ctx - 1 - t, c), (zero, zero), unroll=unroll)
        lax.fori_loop(0, nk - nk_ctx, lambda t, c: step(nk - 1 - t, c), hc, unroll=unroll)


def _s5n_out_body(u_ref, m_ref, h_ref, v_ref, e_ref, o_ref, m_exp, v_exp):
    @pl.when(pl.program_id(1) == 0)
    def _():
        for j in range(S5N_CHUNK):
            m_exp[j * LANES:(j + 1) * LANES, :] = _s5n_expand(m_ref[0, j], e_ref[...],
                                                              _S5N_IN_ROW_GROUP, _S5N_OUT_COL_GROUP)
        v_exp[...] = _s5n_expand(v_ref[0], e_ref[...], _S5N_STATE_GROUP, _S5N_OUT_COL_GROUP)

    u_cat = jnp.concatenate([u_ref[:, j, :] for j in range(S5N_CHUNK)], axis=1)
    h_cat = jnp.concatenate([h_ref[d, r].astype(BF16) for d in range(2) for r in range(2)], axis=1)
    acc = (jnp.dot(u_cat, m_exp[...], preferred_element_type=F32)
           + jnp.dot(h_cat, v_exp[...], preferred_element_type=F32))
    y = jax.nn.gelu(acc).astype(o_ref.dtype)
    for i in range(S5N_CHUNK):
        o_ref[:, i, :] = y[:, i * LANES:(i + 1) * LANES]


def _s5n_mix(proj, su_off, params, *, batch, seq, ctx_len, gw):
    m_mat, w_mat, v_mat, a_pow = params
    L, half = S5N_CHUNK, S5N_HALF
    nb = gw // LANES
    nk, nk_ctx = seq // L, ctx_len // L
    assert su_off % LANES == 0 and seq % L == 0 and ctx_len % L == 0
    ub = su_off // LANES
    u3 = proj.reshape(batch * nk, L, proj.shape[1])
    lanes = batch * nb * half
    u_spec = pl.BlockSpec((nk, L, LANES), lambda n, b: (b, 0, ub + n))
    h_spec = pl.BlockSpec((2, 2, nk, half), lambda n, b: (0, 0, 0, b * nb + n))

    e_out, e_st = _s5n_spreaders()
    whole = lambda a: pl.BlockSpec(a.shape, lambda n, b: (0,) * a.ndim)
    s_in = pl.pallas_call(
        _s5n_state_in_body,
        grid=(nb, batch),
        in_specs=[u_spec, pl.BlockSpec((1,) + w_mat.shape[1:], lambda n, b: (n, 0, 0, 0)), whole(e_st)],
        out_specs=h_spec,
        out_shape=jax.ShapeDtypeStruct((2, 2, nk, lanes), F32),
        scratch_shapes=[pltpu.VMEM((L * LANES, 4 * half), BF16)],
        compiler_params=_cp("parallel", "arbitrary"),
        name="s5_state_in",
    )(u3, w_mat, e_st)

    wl = _pick_tile(lanes, S5_SCAN_LANES, LANES)
    part_spec = lambda r: pl.BlockSpec((1, 1, nk, wl), lambda d, w: (d, r, 0, w))
    a_lanes = jnp.tile(a_pow.reshape(2, 2, 1, nb * half), (1, 1, 1, batch))
    h_prev = pl.pallas_call(
        functools.partial(_s5n_scan_body, nk=nk, nk_ctx=nk_ctx),
        grid=(2, lanes // wl),
        in_specs=[part_spec(0), part_spec(1), pl.BlockSpec((1, 2, 1, wl), lambda d, w: (d, 0, 0, w))],
        out_specs=pl.BlockSpec((1, 2, nk, wl), lambda d, w: (d, 0, 0, w)),
        out_shape=jax.ShapeDtypeStruct((2, 2, nk, lanes), F32),
        compiler_params=_cp("parallel", "parallel"),
        name="s5_scan",
    )(s_in, s_in, a_lanes)

    y3 = pl.pallas_call(
        _s5n_out_body,
        grid=(nb, batch),
        in_specs=[u_spec,
                  pl.BlockSpec((1,) + m_mat.shape[1:], lambda n, b: (n, 0, 0, 0)),
                  h_spec,
                  pl.BlockSpec((1,) + v_mat.shape[1:], lambda n, b: (n, 0, 0)),
                  whole(e_out)],
        out_specs=pl.BlockSpec((nk, L, LANES), lambda n, b: (b, 0, n)),
        out_shape=jax.ShapeDtypeStruct((batch * nk, L, gw), BF16),
        scratch_shapes=[pltpu.VMEM((L * LANES, L * LANES), BF16), pltpu.VMEM((4 * half, L * LANES), BF16)],
        compiler_params=_cp("parallel", "arbitrary"),
        name="s5_out",
    )(u3, m_mat, h_prev, v_mat, e_out)
    return y3.reshape(batch * seq, gw)


def _expert_changed(te_ref):
    i = pl.program_id(1)
    return (i == 0) | (te_ref[i] != te_ref[jnp.maximum(i - 1, 0)])


def _moe_up_body(te_ref, x_ref, wg_ref, wu_ref, o_ref, wg_bf, wu_bf):
    @pl.when(_expert_changed(te_ref))
    def _():
        wg_bf[...] = wg_ref[0, 0].astype(BF16)
        wu_bf[...] = wu_ref[0, 0].astype(BF16)

    x = x_ref[...]
    a = jnp.dot(x, wg_bf[...], preferred_element_type=F32)
    b = jnp.dot(x, wu_bf[...], preferred_element_type=F32)
    o_ref[...] = (a * _sigmoid(a) * b).astype(o_ref.dtype)


def _moe_down_body(te_ref, h_ref, w_ref, rw_ref, o_ref, w_bf):
    @pl.when(_expert_changed(te_ref))
    def _():
        w_bf[...] = w_ref[0, 0].astype(BF16)

    y = jnp.dot(h_ref[...], w_bf[...], preferred_element_type=F32)
    o_ref[...] = (rw_ref[...] * y).astype(o_ref.dtype)


def _moe_down_into_body(te_ref, h_ref, w_ref, rw_ref, prev_ref, o_ref, w_bf):
    _moe_down_body(te_ref, h_ref, w_ref, rw_ref, o_ref, w_bf)


def _moe_combine_body(x_ref, y0_ref, y1_ref, gate_ref, g_ref, sh_ref, sc_ref, o_ref, h_ref, *,
                      tm, tiles_per_batch, ctx_len):
    i = pl.program_id(0)
    is_ctx = (i % tiles_per_batch) * tm < ctx_len
    pick = lambda r: jnp.where(is_ctx, r[0, 0:1, :], r[0, 1:2, :])
    x = x_ref[...] + pick(gate_ref) * (y0_ref[...].astype(F32) + y1_ref[...].astype(F32))
    o_ref[...] = x
    y = x * lax.rsqrt(jnp.mean(x * x, axis=-1, keepdims=True) + NORM_EPS) * g_ref[...]
    h_ref[...] = (y * (1.0 + pick(sc_ref)) + pick(sh_ref)).astype(BF16)


def _moe_final_body(x_ref, y0_ref, y1_ref, gate_ref, g_ref, o_ref):
    x = x_ref[...] + gate_ref[0, 1:2, :] * (y0_ref[...].astype(F32) + y1_ref[...].astype(F32))
    o_ref[...] = x * lax.rsqrt(jnp.mean(x * x, axis=-1, keepdims=True) + NORM_EPS) * g_ref[...]


def _route(logits):
    assert MOE_TOPK == 2
    g_logit = logits[:, :MOE_GROUPS]
    g_prob = jax.nn.softmax(g_logit, axis=-1)
    g_idx = jnp.argmax(g_prob, axis=-1)
    g_p = jnp.max(g_prob, axis=-1)
    e_logit = logits[:, MOE_GROUPS:MOE_GROUPS + MOE_GROUPS * MOE_PER_GROUP]
    e_logit = e_logit.reshape(-1, MOE_GROUPS, MOE_PER_GROUP)
    sel = (jnp.arange(MOE_GROUPS)[None, :] == g_idx[:, None])[:, :, None]
    e_logit = jnp.sum(jnp.where(sel, e_logit, 0.0), axis=1)
    e_prob = jax.nn.softmax(e_logit, axis=-1)
    i0 = jnp.argmax(e_prob, axis=-1)
    p0 = jnp.max(e_prob, axis=-1)
    rest = jnp.where(jnp.arange(MOE_PER_GROUP)[None, :] == i0[:, None], -1.0, e_prob)
    i1 = jnp.argmax(rest, axis=-1)
    p1 = jnp.max(rest, axis=-1)
    e_p = jnp.stack([p0, p1], axis=-1)
    w = g_p[:, None] * e_p / jnp.sum(e_p, axis=-1, keepdims=True)
    ids = g_idx[:, None] * MOE_PER_GROUP + jnp.stack([i0, i1], axis=-1)
    return ids.astype(jnp.int32), w


def _moe(h, logits, x, gate, w_gate, w_up, w_down, layer, *, rows_per_batch, ctx_len,
         final_g=None, next_norm=None):
    t, d = h.shape
    _, n_exp, _, dff = w_gate.shape
    tile = MOE_TILE
    ids, wts = _route(logits)
    flat_e = ids.reshape(-1)
    onehot = (flat_e[:, None] == jnp.arange(n_exp)[None, :]).astype(jnp.int32)
    counts = onehot.sum(0)
    rank = jnp.take_along_axis(jnp.cumsum(onehot, axis=0) - onehot, flat_e[:, None], axis=1)[:, 0]
    padded = (counts + tile - 1) // tile * tile
    starts = jnp.cumsum(padded) - padded
    pos = starts[flat_e] + rank
    n_rows = (t * MOE_TOPK // tile + n_exp) * tile
    n_tiles = n_rows // tile
    token = (jnp.arange(t * MOE_TOPK, dtype=jnp.int32) // MOE_TOPK).astype(F32)
    table = jnp.zeros((n_rows, 2), F32).at[pos].set(jnp.stack([token, wts.reshape(-1)], axis=1),
                                                    unique_indices=True, mode="promise_in_bounds")
    src = table[:, 0].astype(jnp.int32)
    row_w = table[:, 1]
    tile_start = jnp.arange(n_tiles, dtype=jnp.int32) * tile
    ends = starts + padded
    tile_e = jnp.minimum(jnp.sum(tile_start[:, None] >= ends[None, :], axis=1), n_exp - 1).astype(jnp.int32)

    n_chunks = max(c for c in (4, 2, 1) if n_tiles % c == 0)
    tpc = n_tiles // n_chunks
    tn_up = _pick_tile(dff, MOE_UP_COL_TILE, LANES)
    tn_dn = _pick_tile(d, MOE_DOWN_COL_TILE, LANES)
    w_up_spec = pl.BlockSpec((1, 1, d, tn_up), lambda j, i, te: (layer, te[i], 0, j))
    ys = None
    for c in range(n_chunks):
        rows_c = slice(c * tpc * tile, (c + 1) * tpc * tile)
        te_c = tile_e[c * tpc:(c + 1) * tpc]
        xs = h.at[src[rows_c]].get(mode="promise_in_bounds")
        hid = pl.pallas_call(
            _moe_up_body,
            grid_spec=pltpu.PrefetchScalarGridSpec(
                num_scalar_prefetch=1,
                grid=(dff // tn_up, tpc),
                in_specs=[pl.BlockSpec((tile, d), lambda j, i, te: (i, 0)), w_up_spec, w_up_spec],
                out_specs=pl.BlockSpec((tile, tn_up), lambda j, i, te: (i, j)),
                scratch_shapes=[pltpu.VMEM((d, tn_up), BF16), pltpu.VMEM((d, tn_up), BF16)]),
            out_shape=jax.ShapeDtypeStruct((tpc * tile, dff), BF16),
            compiler_params=_cp("arbitrary", "arbitrary"),
            name="moe_up",
        )(te_c, xs, w_gate, w_up)
        first = ys is None
        ys = pl.pallas_call(
            _moe_down_body if first else _moe_down_into_body,
            grid_spec=pltpu.PrefetchScalarGridSpec(
                num_scalar_prefetch=1,
                grid=(d // tn_dn, tpc),
                in_specs=[pl.BlockSpec((tile, dff), lambda j, i, te: (i, 0)),
                          pl.BlockSpec((1, 1, dff, tn_dn), lambda j, i, te: (layer, te[i], 0, j)),
                          pl.BlockSpec((tile, 1), lambda j, i, te: (i, 0))]
                         + ([] if first else [pl.BlockSpec(memory_space=pl.ANY)]),
                out_specs=pl.BlockSpec((tile, tn_dn), lambda j, i, te, c=c: (c * tpc + i, j)),
                scratch_shapes=[pltpu.VMEM((dff, tn_dn), BF16)]),
            out_shape=jax.ShapeDtypeStruct((n_rows, d), BF16),
            input_output_aliases={} if first else {4: 0},
            compiler_params=_cp("arbitrary", "arbitrary"),
            name="moe_down",
        )(te_c, hid, w_down, row_w[rows_c].reshape(tpc * tile, 1), *(() if first else (ys,)))

    pos2 = pos.reshape(t, MOE_TOPK)
    y0 = ys.at[pos2[:, 0]].get(mode="promise_in_bounds")
    y1 = ys.at[pos2[:, 1]].get(mode="promise_in_bounds")
    tm = _pick_tile(math.gcd(rows_per_batch, ctx_len), ROW_TILE, 8)
    tpb = rows_per_batch // tm
    if final_g is not None:
        batch = t // rows_per_batch
        ctx_tiles, lat_tiles = ctx_len // tm, (rows_per_batch - ctx_len) // tm
        lat_spec = pl.BlockSpec((tm, d), lambda b, i: (b * tpb + ctx_tiles + i, 0))
        out = pl.pallas_call(
            _moe_final_body,
            grid=(batch, lat_tiles),
            in_specs=[lat_spec, lat_spec, lat_spec, pl.BlockSpec((1, 2, d), lambda b, i: (b, 0, 0)),
                      pl.BlockSpec((1, d), lambda b, i: (0, 0))],
            out_specs=pl.BlockSpec((tm, d), lambda b, i: (b * lat_tiles + i, 0)),
            out_shape=jax.ShapeDtypeStruct((batch * lat_tiles * tm, d), F32),
            compiler_params=_cp("parallel", "parallel"),
            name="moe_combine_final",
        )(x, y0, y1, gate, final_g.reshape(1, d).astype(F32))
        return out.reshape(batch, lat_tiles * tm, d)
    row_spec = pl.BlockSpec((tm, d), lambda i: (i, 0))
    mod_spec = pl.BlockSpec((1, 2, d), lambda i: (i // tpb, 0, 0))
    next_g, next_shift, next_scale = next_norm
    return pl.pallas_call(
        functools.partial(_moe_combine_body, tm=tm, tiles_per_batch=tpb, ctx_len=ctx_len),
        grid=(t // tm,),
        in_specs=[row_spec, row_spec, row_spec, mod_spec, pl.BlockSpec((1, d), lambda i: (0, 0)),
                  mod_spec, mod_spec],
        out_specs=[row_spec, row_spec],
        out_shape=[jax.ShapeDtypeStruct((t, d), F32), jax.ShapeDtypeStruct((t, d), BF16)],
        compiler_params=_cp("parallel"),
        name="moe_combine",
    )(x, y0, y1, gate, next_g.reshape(1, d).astype(F32), next_shift, next_scale)


def kernel(x, c, ctx, c_ctx, ada_w, ada_b, norm_mix, norm_ffn, w_in, w_out, diff_lambda, diff_subln,
           s5_a_re, s5_a_im, s5_log_dt, s5_b_re, s5_b_im, s5_c_re, s5_c_im, s5_d, s5_glu_w, s5_glu_b,
           mla_q_norm, mla_kv_norm, mla_w_uq, mla_w_ukv, ret_decay, ret_norm,
           moe_wg, moe_bg, moe_we, moe_be, moe_w_gate, moe_w_up, moe_w_down, final_norm):
    batch, n_lat, d = x.shape
    ctx_len = ctx.shape[1]
    depth = ada_w.shape[0]
    seq = ctx_len + n_lat
    rows = batch * seq
    gw = d // 4
    heads = gw // LANES
    q_rank, kv_rank = 3 * d // 16, d // 16
    ret_qk = heads * RET_K
    n_route = MOE_GROUPS + MOE_GROUPS * MOE_PER_GROUP
    assert heads % 2 == 0 and ctx_len % RET_CHUNK == 0 and n_lat % RET_CHUNK == 0

    splits = (gw, gw, gw, gw, q_rank, kv_rank, ROPE_DIM, ret_qk, ret_qk, gw, gw)
    offs = [0]
    for s_ in splits:
        offs.append(offs[-1] + s_)
    names = ("dq", "dk", "dv", "su", "mcq", "mckv", "mkr", "rq", "rk", "rv", "rg")
    src_col = {n_: (offs[i], offs[i + 1]) for i, n_ in enumerate(names)}
    order = ("dq", "dk", "rq", "rk", "dv", "su", "rv", "rg", "mcq", "mckv")
    col = {}
    pos = 0
    for n_ in order:
        col[n_] = pos
        pos += src_col[n_][1] - src_col[n_][0]
    n_main = pos
    n_rope = col["dv"]
    uq_cols = jnp.arange(heads * (MLA_NOPE + ROPE_DIM)).reshape(heads, MLA_NOPE + ROPE_DIM)
    uq_perm = jnp.concatenate([uq_cols[:, :MLA_NOPE].reshape(-1), uq_cols[:, MLA_NOPE:].reshape(-1)])

    tables = _rope_tables(n_lat, ctx_len)
    log2e = math.log2(math.e)
    rope_scale = jnp.ones((n_rope,), F32).at[col["rk"]:col["rk"] + ret_qk].set(RET_K ** -0.5)
    rope_scale = rope_scale.at[col["dq"]:col["dq"] + gw].set(DIFF_HEAD_DIM ** -0.5 * log2e)
    mla_q_scale = (MLA_NOPE + ROPE_DIM) ** -0.5 * log2e
    main_scale = jnp.concatenate([rope_scale, jnp.ones((n_main - n_rope,), F32)])

    cond = jnp.concatenate([c_ctx[None, :], c], axis=0)
    cond = jnp.pad(cond * _sigmoid(cond), ((0, 8 - (batch + 1) % 8 if (batch + 1) % 8 else 0), (0, 0)))

    mod_all = _ada_mod(cond, ada_w, ada_b)

    xa = None
    tm_big = _pick_tile(seq, MM_ROW_TILE)
    tn = lambda n_: _pick_tile(n_, MM_COL_TILE, LANES)

    def layer_mods(l):
        mod = mod_all[l].reshape(cond.shape[0], 6, d)
        return [jnp.stack([jnp.broadcast_to(mod[0, i], (batch, d)), mod[1:batch + 1, i]], axis=1)
                for i in range(6)]

    h_mix = None
    for l in range(depth):
        lam_init = 0.8 - 0.6 * math.exp(-0.3 * l)
        mods = layer_mods(l)

        w_main = jnp.concatenate([w_in[l, :, src_col[n_][0]:src_col[n_][1]] for n_ in order],
                                 axis=1).astype(BF16)
        w_kr = jnp.concatenate([w_in[l, :, src_col["mkr"][0]:src_col["mkr"][1]]] * (LANES // ROPE_DIM),
                               axis=1).astype(BF16)

        if l == 0:
            h, xa = _norm_mod_first(ctx, x, norm_mix[l], mods[0], mods[1])
        else:
            h = h_mix
        proj = _mm([h], w_main, name="in_proj", out_dtype=BF16, tm=tm_big,
                   tn=tn(math.gcd(n_main, n_rope)), rope=(tables, main_scale, (0, n_rope)),
                   rows_per_batch=seq)
        krr = _mm([h], w_kr, name="in_proj_kr", out_dtype=BF16, tm=tm_big, tn=LANES,
                  rope=(tables, jnp.ones((LANES,), F32), (0, LANES)), rows_per_batch=seq)
        qk = proj

        lv = diff_lambda[l].astype(F32)
        lam = jnp.exp(jnp.sum(lv[0] * lv[1])) - jnp.exp(jnp.sum(lv[2] * lv[3])) + lam_init
        a_out = _diff_attn(qk, proj, lam, diff_subln[l], batch=batch, seq=seq, ctx_len=ctx_len,
                           heads=heads, q_blk=col["dq"] // LANES, k_blk=col["dk"] // LANES,
                           v_blk=col["dv"] // LANES, post=1.0 - lam_init)

        s5p = _s5n_params(s5_a_re[l], s5_a_im[l], s5_log_dt[l], s5_b_re[l], s5_b_im[l],
                          s5_c_re[l], s5_c_im[l], s5_d[l])
        s_act = _s5n_mix(proj, col["su"], s5p, batch=batch, seq=seq, ctx_len=ctx_len, gw=gw)
        s_out = _mm([s_act], s5_glu_w[l].astype(BF16), name="s5_glu", out_dtype=BF16, tm=tm_big, tn=tn(gw),
                    bias=s5_glu_b[l], glu_in=s_act)

        cq = proj[:, col["mcq"]:col["mcq"] + q_rank]
        ckv = proj[:, col["mckv"]:col["mckv"] + kv_rank]
        w_uq = (mla_w_uq[l][:, uq_perm] * mla_q_scale).astype(BF16)
        n_qn, n_q = heads * MLA_NOPE, heads * (MLA_NOPE + ROPE_DIM)
        q_up = _mm([cq], w_uq, name="mla_q_up", out_dtype=BF16, tm=tm_big, tn=tn(math.gcd(n_qn, n_q)),
                   norm_g=mla_q_norm[l], rope=(tables, jnp.ones((n_q,), F32), (n_qn, n_q)),
                   rows_per_batch=seq)
        kv_up = _mm([ckv], mla_w_ukv[l].astype(BF16), name="mla_kv_up", out_dtype=BF16, tm=tm_big,
                    tn=tn(mla_w_ukv.shape[2]), norm_g=mla_kv_norm[l])
        m_out = _mla_attn(q_up, kv_up, krr, batch=batch, seq=seq, ctx_len=ctx_len, heads=heads)

        log_g = jax.nn.log_sigmoid(ret_decay[l].astype(F32))
        r_out = _retention(qk, proj, log_g, ret_norm[l], batch=batch, seq=seq, ctx_len=ctx_len,
                           heads=heads, q_off=col["rq"], k_off=col["rk"], v_off=col["rv"], g_off=col["rg"])

        xa = _mm([a_out, s_out, m_out, r_out], w_out[l].astype(BF16), name="out_proj", out_dtype=F32, tm=tm_big, tn=tn(d),
                 res=xa, gate=mods[2], rows_per_batch=seq, ctx_len=ctx_len)

        w_r = jnp.concatenate([moe_wg[l], moe_we[l]], axis=1).astype(F32)
        w_r = jnp.pad(w_r, ((0, 0), (0, LANES - n_route)))
        w_r_hi = w_r.astype(BF16)
        w_r_lo = (w_r - w_r_hi.astype(F32)).astype(BF16)
        b_r = jnp.pad(jnp.concatenate([moe_bg[l], moe_be[l]]).astype(F32), (0, LANES - n_route))
        h, logits = _norm_mod(xa, norm_ffn[l], mods[3], mods[4], rows_per_batch=seq, ctx_len=ctx_len,
                              router=(w_r_hi, w_r_lo, b_r.reshape(1, LANES)))
        if l == depth - 1:
            return _moe(h, logits, xa, mods[5], moe_w_gate, moe_w_up, moe_w_down, l,
                        rows_per_batch=seq, ctx_len=ctx_len, final_g=final_norm)
        nxt = layer_mods(l + 1)
        xa, h_mix = _moe(h, logits, xa, mods[5], moe_w_gate, moe_w_up, moe_w_down, l,
                         rows_per_batch=seq, ctx_len=ctx_len, next_norm=(norm_mix[l + 1], nxt[0], nxt[1]))
```

```python
import functools
import math

import jax
import jax.numpy as jnp
import numpy as np
from jax import lax
from jax.experimental import pallas as pl
from jax.experimental.pallas import tpu as pltpu

BF16 = jnp.bfloat16
F32 = jnp.float32

V7X_VMEM_BYTES = 64 * 2**20
VMEM_LIMIT = V7X_VMEM_BYTES - 12 * 2**20
LANES = 128

GRID_W = 64
ROPE_DIM = 64
ROPE_BASE = 10000.0
NORM_EPS = 1e-6
DIFF_HEAD_DIM = 64
S5_CH = 16
S5_STATE = 64
MLA_NOPE = 128
MLA_V = 128
RET_K = 64
RET_V = 128
RET_CHUNK = 128
MOE_GROUPS = 4
MOE_PER_GROUP = 4
MOE_TOPK = 2
ROPE_QUARTER = ROPE_DIM // 4

MM_ROW_TILE = 1088
MM_COL_TILE = 512
ROW_TILE = 256
ATTN_Q_TILE = 512
S5_SCAN_LANES = 2048
MOE_TILE = 256
MOE_UP_COL_TILE = 512
MOE_DOWN_COL_TILE = 4096


def _cp(*sem):
    return pltpu.CompilerParams(dimension_semantics=sem, vmem_limit_bytes=VMEM_LIMIT)


def _pick_tile(n, target, mult=16):
    best = None
    for t in range(mult, min(n, target) + 1, mult):
        if n % t == 0:
            best = t
    assert best is not None, (n, target)
    return best


def _sigmoid(x):
    return 1.0 / (1.0 + jnp.exp(-x))


def _rotate(x, cos, sin, swap):
    reps = x.shape[1] // LANES
    tile = lambda t: jnp.tile(t, (1, reps))
    partner = jnp.dot(x.astype(BF16), swap, preferred_element_type=F32)
    return x * tile(cos) + partner * tile(sin)


def _rope_swap(width):
    lane = np.arange(width)
    first = (lane // ROPE_QUARTER) % 2 == 0
    src = np.where(first, lane + ROPE_QUARTER, lane - ROPE_QUARTER)
    swap = np.zeros((width, width), np.float32)
    swap[src, lane] = np.where(first, -1.0, 1.0)
    return jnp.asarray(swap, BF16)


def _mm_body(*refs, nx, ksizes, has_norm, has_bias, epilogue, tm, tiles_per_batch, ctx_len, rope_tiles):
    x_refs = refs[:nx]
    w_ref = refs[nx]
    idx = nx + 1
    g_ref = b_ref = e_ref = res_ref = gate_ref = None
    if has_norm:
        g_ref = refs[idx]; idx += 1
    if has_bias:
        b_ref = refs[idx]; idx += 1
    if epilogue == "glu":
        e_ref = refs[idx]; idx += 1
    if epilogue == "resgate":
        res_ref, gate_ref = refs[idx], refs[idx + 1]; idx += 2
    if epilogue == "rope":
        cos_ref, sin_ref, swap_ref, cs_ref = refs[idx:idx + 4]; idx += 4
    o_ref = refs[idx]

    acc = None
    off = 0
    for xr, ks in zip(x_refs, ksizes):
        x = xr[...]
        if has_norm:
            xf = x.astype(F32)
            xf = xf * lax.rsqrt(jnp.mean(xf * xf, axis=-1, keepdims=True) + NORM_EPS)
            x = xf * g_ref[...]
        x = x.astype(BF16)
        w = w_ref[off:off + ks, :].astype(BF16)
        part = jnp.dot(x, w, preferred_element_type=F32)
        acc = part if acc is None else acc + part
        off += ks
    if has_bias:
        acc = acc + b_ref[...]
    if epilogue == "glu":
        acc = e_ref[...].astype(F32) * _sigmoid(acc)
    elif epilogue == "resgate":
        i = pl.program_id(0)
        row = (i % tiles_per_batch) * tm + lax.broadcasted_iota(jnp.int32, (tm, 1), 0)
        gate = jnp.where(row < ctx_len, gate_ref[0, 0:1, :], gate_ref[0, 1:2, :])
        acc = res_ref[...] + gate * acc
    if epilogue == "rope":
        j = pl.program_id(1)
        roped = (j >= rope_tiles[0]) & (j < rope_tiles[1])

        @pl.when(roped)
        def _():
            y = _rotate(acc, cos_ref[...], sin_ref[...], swap_ref[...]) * cs_ref[...]
            o_ref[...] = y.astype(o_ref.dtype)

        @pl.when(jnp.logical_not(roped))
        def _():
            o_ref[...] = acc.astype(o_ref.dtype)
    else:
        o_ref[...] = acc.astype(o_ref.dtype)


def _mm(xs, w, *, name, out_dtype, tm, tn, norm_g=None, bias=None, glu_in=None, res=None, gate=None,
        rope=None, rows_per_batch=None, ctx_len=0):
    m = xs[0].shape[0]
    ksizes = tuple(x.shape[1] for x in xs)
    k, n = w.shape
    assert sum(ksizes) == k and m % tm == 0 and n % tn == 0
    epilogue = ("glu" if glu_in is not None else "resgate" if res is not None
                else "rope" if rope is not None else None)
    tiles_per_batch = (rows_per_batch // tm) if rows_per_batch else 1
    rope_tiles = None
    in_specs = [pl.BlockSpec((tm, ks), lambda i, j: (i, 0)) for ks in ksizes]
    in_specs.append(pl.BlockSpec((k, tn), lambda i, j: (0, j)))
    args = list(xs) + [w]
    if norm_g is not None:
        in_specs.append(pl.BlockSpec((1, k), lambda i, j: (0, 0)))
        args.append(norm_g.reshape(1, k).astype(F32))
    if bias is not None:
        in_specs.append(pl.BlockSpec((1, tn), lambda i, j: (0, j)))
        args.append(bias.reshape(1, n).astype(F32))
    if epilogue == "glu":
        in_specs.append(pl.BlockSpec((tm, tn), lambda i, j: (i, j)))
        args.append(glu_in)
    if epilogue == "resgate":
        tpb = tiles_per_batch
        in_specs.append(pl.BlockSpec((tm, tn), lambda i, j: (i, j)))
        in_specs.append(pl.BlockSpec((1, 2, tn), lambda i, j: (i // tpb, 0, j)))
        args += [res, gate]
    if epilogue == "rope":
        tables, col_scale, (lo, hi) = rope
        assert lo % tn == 0 and hi % tn == 0
        rope_tiles = (lo // tn, hi // tn)
        tpb = tiles_per_batch
        in_specs += [pl.BlockSpec((tm, LANES), lambda i, j: (i % tpb, 0))] * 2
        in_specs.append(pl.BlockSpec((tn, tn), lambda i, j: (0, 0)))
        in_specs.append(pl.BlockSpec((1, tn), lambda i, j: (0, j)))
        args += list(tables) + [_rope_swap(tn), col_scale.reshape(1, n).astype(F32)]
    body = functools.partial(_mm_body, nx=len(xs), ksizes=ksizes, has_norm=norm_g is not None,
                             has_bias=bias is not None, epilogue=epilogue, tm=tm,
                             tiles_per_batch=tiles_per_batch, ctx_len=ctx_len, rope_tiles=rope_tiles)
    return pl.pallas_call(
        body,
        grid=(m // tm, n // tn),
        in_specs=in_specs,
        out_specs=pl.BlockSpec((tm, tn), lambda i, j: (i, j)),
        out_shape=jax.ShapeDtypeStruct((m, n), out_dtype),
        compiler_params=_cp("parallel", "arbitrary"),
        name=name,
    )(*args)


def _ada_body(c_ref, w_ref, b_ref, o_ref):
    acc = jnp.dot(c_ref[...].astype(BF16), w_ref[0].astype(BF16), preferred_element_type=F32)
    o_ref[0] = acc + b_ref[0]


def _ada_mod(cond, ada_w, ada_b):
    depth, d, n6 = ada_w.shape
    rows = cond.shape[0]
    tn = _pick_tile(n6, MM_COL_TILE, LANES)
    return pl.pallas_call(
        _ada_body,
        grid=(depth, n6 // tn),
        in_specs=[pl.BlockSpec((rows, d), lambda l, j: (0, 0)),
                  pl.BlockSpec((1, d, tn), lambda l, j: (l, 0, j)),
                  pl.BlockSpec((1, 1, tn), lambda l, j: (l, 0, j))],
        out_specs=pl.BlockSpec((1, rows, tn), lambda l, j: (l, 0, j)),
        out_shape=jax.ShapeDtypeStruct((depth, rows, n6), F32),
        compiler_params=_cp("parallel", "arbitrary"),
        name="ada_mod",
    )(cond, ada_w, ada_b.reshape(depth, 1, n6).astype(F32))


def _norm_mod_body(*refs, tm, tiles_per_batch, ctx_len, router):
    if router:
        x_ref, g_ref, sh_ref, sc_ref, whi_ref, wlo_ref, br_ref, h_ref, lg_ref = refs
    else:
        x_ref, g_ref, sh_ref, sc_ref, h_ref = refs
    i = pl.program_id(0)
    x = x_ref[...]
    y = x * lax.rsqrt(jnp.mean(x * x, axis=-1, keepdims=True) + NORM_EPS) * g_ref[...]
    is_ctx = (i % tiles_per_batch) * tm < ctx_len
    sh = jnp.where(is_ctx, sh_ref[0, 0:1, :], sh_ref[0, 1:2, :])
    sc = jnp.where(is_ctx, sc_ref[0, 0:1, :], sc_ref[0, 1:2, :])
    h = y * (1.0 + sc) + sh
    h_ref[...] = h.astype(BF16)
    if router:
        hi = h.astype(BF16)
        lo = (h - hi.astype(F32)).astype(BF16)
        lg = jnp.dot(hi, whi_ref[...], preferred_element_type=F32)
        lg = lg + jnp.dot(hi, wlo_ref[...], preferred_element_type=F32)
        lg = lg + jnp.dot(lo, whi_ref[...], preferred_element_type=F32)
        lg_ref[...] = lg + br_ref[...]


def _norm_mod_first_body(c_ref, x_ref, g_ref, sh_ref, sc_ref, h_ref, xa_ref, *, ctx_tiles):
    is_ctx = pl.program_id(1) < ctx_tiles
    x = jnp.where(is_ctx, c_ref[...], x_ref[...])
    xa_ref[...] = x
    y = x * lax.rsqrt(jnp.mean(x * x, axis=-1, keepdims=True) + NORM_EPS) * g_ref[...]
    sh = jnp.where(is_ctx, sh_ref[0, 0:1, :], sh_ref[0, 1:2, :])
    sc = jnp.where(is_ctx, sc_ref[0, 0:1, :], sc_ref[0, 1:2, :])
    h_ref[...] = (y * (1.0 + sc) + sh).astype(BF16)


def _norm_mod_first(ctx, x, g, shift, scale):
    batch, ctx_len, d = ctx.shape
    n_lat = x.shape[1]
    tm = _pick_tile(math.gcd(n_lat, ctx_len), ROW_TILE, 8)
    ctx_tiles, lat_tiles = ctx_len // tm, n_lat // tm
    tpb = ctx_tiles + lat_tiles
    rows = batch * tpb * tm
    out_spec = pl.BlockSpec((tm, d), lambda b, i: (b * tpb + i, 0))
    mod_spec = pl.BlockSpec((1, 2, d), lambda b, i: (b, 0, 0))
    return pl.pallas_call(
        functools.partial(_norm_mod_first_body, ctx_tiles=ctx_tiles),
        grid=(batch, tpb),
        in_specs=[pl.BlockSpec((tm, d), lambda b, i: (b * ctx_tiles + jnp.minimum(i, ctx_tiles - 1), 0)),
                  pl.BlockSpec((tm, d), lambda b, i: (b * lat_tiles + jnp.maximum(i - ctx_tiles, 0), 0)),
                  pl.BlockSpec((1, d), lambda b, i: (0, 0)), mod_spec, mod_spec],
        out_specs=[out_spec, out_spec],
        out_shape=[jax.ShapeDtypeStruct((rows, d), BF16), jax.ShapeDtypeStruct((rows, d), F32)],
        compiler_params=_cp("parallel", "arbitrary"),
        name="norm_mod_first",
    )(ctx.reshape(batch * ctx_len, d), x.reshape(batch * n_lat, d), g.reshape(1, d), shift, scale)


def _norm_mod(x, g, shift, scale, *, rows_per_batch, ctx_len, router=None):
    m, d = x.shape
    tm = _pick_tile(math.gcd(rows_per_batch, ctx_len), ROW_TILE, 8)
    tpb = rows_per_batch // tm
    in_specs = [
        pl.BlockSpec((tm, d), lambda i: (i, 0)),
        pl.BlockSpec((1, d), lambda i: (0, 0)),
        pl.BlockSpec((1, 2, d), lambda i: (i // tpb, 0, 0)),
        pl.BlockSpec((1, 2, d), lambda i: (i // tpb, 0, 0)),
    ]
    args = [x, g.reshape(1, d), shift, scale]
    out_specs = [pl.BlockSpec((tm, d), lambda i: (i, 0))]
    out_shape = [jax.ShapeDtypeStruct((m, d), BF16)]
    if router is not None:
        whi, wlo, br = router
        in_specs += [pl.BlockSpec((d, LANES), lambda i: (0, 0)),
                     pl.BlockSpec((d, LANES), lambda i: (0, 0)),
                     pl.BlockSpec((1, LANES), lambda i: (0, 0))]
        args += [whi, wlo, br]
        out_specs.append(pl.BlockSpec((tm, LANES), lambda i: (i, 0)))
        out_shape.append(jax.ShapeDtypeStruct((m, LANES), F32))
    body = functools.partial(_norm_mod_body, tm=tm, tiles_per_batch=tpb, ctx_len=ctx_len,
                             router=router is not None)
    outs = pl.pallas_call(body, grid=(m // tm,), in_specs=in_specs, out_specs=out_specs,
                          out_shape=out_shape, compiler_params=_cp("parallel"),
                          name="norm_mod_router" if router is not None else "norm_mod")(*args)
    return outs if router is not None else outs[0]


def _rope_tables(n_lat, ctx_len):
    rows = n_lat // GRID_W
    row = jnp.repeat(jnp.arange(rows, dtype=F32), GRID_W)
    col = jnp.tile(jnp.arange(GRID_W, dtype=F32), rows)
    quarter = ROPE_QUARTER
    inv = ROPE_BASE ** (-jnp.arange(quarter, dtype=F32) / quarter)
    ar = row[:, None] * inv
    ac = col[:, None] * inv
    ang = jnp.concatenate([ar, ar, ac, ac], axis=-1)
    ang = jnp.concatenate([jnp.zeros((ctx_len, ROPE_DIM), F32), ang], axis=0)
    ang = jnp.tile(ang, (1, LANES // ROPE_DIM))
    return jnp.cos(ang), jnp.sin(ang)


ATTN_ALIGN = 256


def _softmax_pv(q, k_ref, va_ref, bounds):
    ms, ovs = [], []
    for lo, hi in bounds:
        s = lax.dot_general(q, k_ref[lo:hi, :], (((1,), (1,)), ((), ())), preferred_element_type=F32)
        m = jnp.max(s, axis=-1, keepdims=True)
        e = jnp.exp2(s - m).astype(BF16)
        ovs.append(jnp.dot(e, va_ref[lo:hi, :], preferred_element_type=F32))
        ms.append(m)
    m_all = functools.reduce(jnp.maximum, ms)
    acc = sum(ov * jnp.exp2(m - m_all) for m, ov in zip(ms, ovs))
    return acc[:, :LANES] / acc[:, LANES:]


def _key_chunks(n_keys):
    if n_keys < 2 * ATTN_ALIGN:
        return ((0, n_keys),)
    half = (n_keys // ATTN_ALIGN + 1) // 2 * ATTN_ALIGN
    return ((0, half), (half, n_keys))


def _lat_tile(seq, ctx_len):
    n_lat = seq - ctx_len
    assert ctx_len % ATTN_ALIGN == 0 and n_lat % ATTN_ALIGN == 0
    return _pick_tile(n_lat, ATTN_Q_TILE, ATTN_ALIGN)


def _lat_rows(seq, ctx_len, tq):
    return lambda b, i: pl.multiple_of(b * seq + ctx_len + i * tq, ATTN_ALIGN)


def _diff_attn_body(lam_ref, q_ref, k_ref, v_ref, sub_ref, *rest, post, bounds):
    o_ref, va_ref = rest[-2:]

    def fill():
        va_ref[:, :LANES] = v_ref[...]
        va_ref[:, LANES:] = jnp.ones(v_ref.shape, BF16)

    if len(rest) == 2:
        pl.when(pl.program_id(2) == 0)(fill)
    else:
        fill()

    q = q_ref[...]
    lane = lax.broadcasted_iota(jnp.int32, (1, LANES), 1)
    first = lane < DIFF_HEAD_DIM
    zero = jnp.zeros_like(q)
    q0 = jnp.where(first, q, zero)
    q1 = jnp.where(first, zero, q)
    o = _softmax_pv(q0, k_ref, va_ref, bounds) - lam_ref[0] * _softmax_pv(q1, k_ref, va_ref, bounds)
    o = o * lax.rsqrt(jnp.mean(o * o, axis=-1, keepdims=True) + NORM_EPS) * sub_ref[...] * post
    o_ref[...] = o.astype(o_ref.dtype)


def _diff_attn(qk, proj, lam, subln, *, batch, seq, ctx_len, heads, q_blk, k_blk, v_blk, post):
    tq = _lat_tile(seq, ctx_len)
    rows = _lat_rows(seq, ctx_len, tq)
    smem = pl.BlockSpec(memory_space=pltpu.SMEM)
    args = (lam.reshape(1).astype(F32), qk, qk, proj, subln.reshape(1, LANES).astype(F32))
    out_shape = jax.ShapeDtypeStruct((batch * seq, heads * LANES), BF16)
    elem = (pl.Element(tq), pl.Element(LANES))
    lat = pl.pallas_call(
        functools.partial(_diff_attn_body, post=post, bounds=_key_chunks(seq)),
        grid=(batch, heads, (seq - ctx_len) // tq),
        in_specs=[
            smem,
            pl.BlockSpec(elem, lambda b, h, i: (rows(b, i), pl.multiple_of((q_blk + h) * LANES, LANES))),
            pl.BlockSpec((seq, LANES), lambda b, h, i: (b, k_blk + h)),
            pl.BlockSpec((seq, LANES), lambda b, h, i: (b, v_blk + h)),
            pl.BlockSpec((1, LANES), lambda b, h, i: (0, 0)),
        ],
        out_specs=pl.BlockSpec(elem, lambda b, h, i: (rows(b, i), pl.multiple_of(h * LANES, LANES))),
        out_shape=out_shape,
        scratch_shapes=[pltpu.VMEM((seq, 2 * LANES), BF16)],
        compiler_params=_cp("parallel", "parallel", "arbitrary"),
        name="diff_attn",
    )(*args)
    cpb = seq // ctx_len
    ctx_spec = lambda blk: pl.BlockSpec((ctx_len, LANES), lambda b, h: (b * cpb, blk + h))
    return pl.pallas_call(
        functools.partial(_diff_attn_body, post=post, bounds=_key_chunks(ctx_len)),
        grid=(batch, heads),
        in_specs=[smem, ctx_spec(q_blk), ctx_spec(k_blk), ctx_spec(v_blk),
                  pl.BlockSpec((1, LANES), lambda b, h: (0, 0)), pl.BlockSpec(memory_space=pl.ANY)],
        out_specs=ctx_spec(0),
        out_shape=out_shape,
        scratch_shapes=[pltpu.VMEM((ctx_len, 2 * LANES), BF16)],
        input_output_aliases={5: 0},
        compiler_params=_cp("parallel", "parallel"),
        name="diff_attn_ctx",
    )(*args, lat)


def _mla_attn_body(qn_ref, qr_ref, kn_ref, kr_ref, v_ref, *rest, bounds):
    o_ref, ka_ref, va_ref = rest[-3:]
    h = pl.program_id(1)

    def fill():
        ka_ref[:, :LANES] = kn_ref[...]
        ka_ref[:, LANES:] = kr_ref[...]
        va_ref[:, :LANES] = v_ref[...]
        va_ref[:, LANES:] = jnp.ones(v_ref.shape, BF16)

    if len(rest) == 3:
        pl.when(pl.program_id(2) == 0)(fill)
    else:
        fill()

    qr = qr_ref[...]
    lane = lax.broadcasted_iota(jnp.int32, (1, LANES), 1)
    mine = (lane < ROPE_DIM) == (h % 2 == 0)
    qr = jnp.where(mine, qr, jnp.zeros_like(qr))
    q = jnp.concatenate([qn_ref[...], qr], axis=1)
    o_ref[...] = _softmax_pv(q, ka_ref, va_ref, bounds).astype(o_ref.dtype)


def _mla_attn(q_up, kv_up, k_rope, *, batch, seq, ctx_len, heads):
    tq = _lat_tile(seq, ctx_len)
    rows = _lat_rows(seq, ctx_len, tq)
    args = (q_up, q_up, kv_up, k_rope, kv_up)
    out_shape = jax.ShapeDtypeStruct((batch * seq, heads * LANES), BF16)
    elem = (pl.Element(tq), pl.Element(LANES))
    lat = pl.pallas_call(
        functools.partial(_mla_attn_body, bounds=_key_chunks(seq)),
        grid=(batch, heads, (seq - ctx_len) // tq),
        in_specs=[
            pl.BlockSpec(elem, lambda b, h, i: (rows(b, i), pl.multiple_of(h * LANES, LANES))),
            pl.BlockSpec(elem, lambda b, h, i: (rows(b, i), pl.multiple_of((heads + h // 2) * LANES, LANES))),
            pl.BlockSpec((seq, LANES), lambda b, h, i: (b, 2 * h)),
            pl.BlockSpec((seq, LANES), lambda b, h, i: (b, 0)),
            pl.BlockSpec((seq, LANES), lambda b, h, i: (b, 2 * h + 1)),
        ],
        out_specs=pl.BlockSpec(elem, lambda b, h, i: (rows(b, i), pl.multiple_of(h * LANES, LANES))),
        out_shape=out_shape,
        scratch_shapes=[pltpu.VMEM((seq, 2 * LANES), BF16), pltpu.VMEM((seq, 2 * LANES), BF16)],
        compiler_params=_cp("parallel", "parallel", "arbitrary"),
        name="mla_attn",
    )(*args)
    cpb = seq // ctx_len
    ctx_spec = lambda col: pl.BlockSpec((ctx_len, LANES), lambda b, h: (b * cpb, col(h)))
    return pl.pallas_call(
        functools.partial(_mla_attn_body, bounds=_key_chunks(ctx_len)),
        grid=(batch, heads),
        in_specs=[ctx_spec(lambda h: h), ctx_spec(lambda h: heads + h // 2), ctx_spec(lambda h: 2 * h),
                  ctx_spec(lambda h: 0), ctx_spec(lambda h: 2 * h + 1), pl.BlockSpec(memory_space=pl.ANY)],
        out_specs=ctx_spec(lambda h: h),
        out_shape=out_shape,
        scratch_shapes=[pltpu.VMEM((ctx_len, 2 * LANES), BF16), pltpu.VMEM((ctx_len, 2 * LANES), BF16)],
        input_output_aliases={5: 0},
        compiler_params=_cp("parallel", "parallel"),
        name="mla_attn_ctx",
    )(*args, lat)


def _retention_body(*refs, backward, cs, heads):
    if backward:
        lg_ref, q_ref, k_ref, v_ref, yf_ref, gate_ref, ng_ref, o_ref, s_ref, d_ref, qd_ref, kd_ref = refs
    else:
        lg_ref, q_ref, k_ref, v_ref, o_ref, s_ref, d_ref, qd_ref, kd_ref = refs
    t = pl.program_id(1)

    @pl.when(t == 0)
    def _():
        s_ref[...] = jnp.zeros_like(s_ref)
        pos_r = lax.broadcasted_iota(jnp.int32, (cs, cs), 0).astype(F32)
        pos_c = lax.broadcasted_iota(jnp.int32, (cs, cs), 1).astype(F32)
        pos = lax.broadcasted_iota(jnp.int32, (cs, 1), 0).astype(F32)
        for h in range(heads):
            lg = lg_ref[h]
            if backward:
                diff = pos_c - pos_r
                keep = diff > 0
                qd_ref[h] = jnp.exp(lg * (cs - pos))
                kd_ref[h] = jnp.exp(lg * pos)
            else:
                diff = pos_r - pos_c
                keep = diff >= 0
                qd_ref[h] = jnp.exp(lg * (pos + 1.0))
                kd_ref[h] = jnp.exp(lg * (cs - 1.0 - pos))
            d_ref[h] = jnp.where(keep, jnp.exp(lg * jnp.maximum(diff, 0.0)), 0.0)

    lane = lax.broadcasted_iota(jnp.int32, (1, LANES), 1)
    nt = (((1,), (1,)), ((), ()))
    tn = (((0,), (0,)), ((), ()))
    for h in range(heads):
        blk = slice((h // 2) * LANES, (h // 2 + 1) * LANES)
        col = slice(h * RET_V, (h + 1) * RET_V)
        mine = (lane < RET_K) == (h % 2 == 0)
        q = q_ref[:, blk]
        q = jnp.where(mine, q, jnp.zeros_like(q))
        k = k_ref[:, blk]
        v = v_ref[:, col]
        scores = lax.dot_general(q, k, nt, preferred_element_type=F32) * d_ref[h]
        intra = jnp.dot(scores.astype(BF16), v, preferred_element_type=F32)
        state = s_ref[h]
        q_w = (q.astype(F32) * qd_ref[h]).astype(BF16)
        cross = jnp.dot(q_w, state.astype(BF16), preferred_element_type=F32)
        k_w = (k.astype(F32) * kd_ref[h]).astype(BF16)
        upd = lax.dot_general(k_w, v, tn, preferred_element_type=F32)
        s_ref[h] = jnp.exp(lg_ref[h] * cs) * state + upd
        y = intra + cross
        if backward:
            y = y + yf_ref[:, col].astype(F32)
            y = y * lax.rsqrt(jnp.mean(y * y, axis=-1, keepdims=True) + NORM_EPS) * ng_ref[:, col]
            g = gate_ref[:, col].astype(F32)
            o_ref[:, col] = (g * _sigmoid(g) * y).astype(o_ref.dtype)
        else:
            o_ref[:, col] = y.astype(o_ref.dtype)


def _retention(qk, proj, log_g, norm_g, *, batch, seq, ctx_len, heads, q_off, k_off, v_off, g_off):
    cs = RET_CHUNK
    nc, nc_ctx = seq // cs, ctx_len // cs
    wqk, wv = heads * RET_K, heads * RET_V
    assert q_off % wqk == 0 and k_off % wqk == 0 and v_off % wv == 0 and g_off % wv == 0
    smem = pl.BlockSpec(memory_space=pltpu.SMEM)

    def fwd_chunk(t):
        return t

    def bwd_chunk(t):
        return jnp.where(t < nc_ctx, nc_ctx - 1 - t, nc - 1 - (t - nc_ctx))

    def specs(chunk):
        row = lambda b, t: b * nc + chunk(t)
        return (pl.BlockSpec((cs, wqk), lambda b, t: (row(b, t), q_off // wqk)),
                pl.BlockSpec((cs, wqk), lambda b, t: (row(b, t), k_off // wqk)),
                pl.BlockSpec((cs, wv), lambda b, t: (row(b, t), v_off // wv)),
                pl.BlockSpec((cs, wv), lambda b, t: (row(b, t), 0)),
                pl.BlockSpec((cs, wv), lambda b, t: (row(b, t), g_off // wv)))

    scratch = [pltpu.VMEM((heads, LANES, RET_V), F32), pltpu.VMEM((heads, cs, cs), F32),
               pltpu.VMEM((heads, cs, 1), F32), pltpu.VMEM((heads, cs, 1), F32)]
    qs, ks, vs, ys, gs = specs(fwd_chunk)
    y_f = pl.pallas_call(
        functools.partial(_retention_body, backward=False, cs=cs, heads=heads),
        grid=(batch, nc),
        in_specs=[smem, qs, ks, vs],
        out_specs=ys,
        out_shape=jax.ShapeDtypeStruct((batch * seq, wv), BF16),
        scratch_shapes=scratch,
        compiler_params=_cp("parallel", "arbitrary"),
        name="retention_fwd",
    )(log_g[0].astype(F32), qk, qk, proj)
    qs, ks, vs, ys, gs = specs(bwd_chunk)
    return pl.pallas_call(
        functools.partial(_retention_body, backward=True, cs=cs, heads=heads),
        grid=(batch, nc),
        in_specs=[smem, qs, ks, vs, ys, gs, pl.BlockSpec((1, wv), lambda b, t: (0, 0))],
        out_specs=ys,
        out_shape=jax.ShapeDtypeStruct((batch * seq, wv), BF16),
        scratch_shapes=scratch,
        compiler_params=_cp("parallel", "arbitrary"),
        name="retention_bwd",
    )(log_g[1].astype(F32), qk, qk, proj, y_f, proj, norm_g.reshape(1, wv).astype(F32))


S5N_CHUNK = 8
S5N_GROUPS = LANES // S5_CH
S5N_HALF = S5N_GROUPS * S5_STATE


def _s5n_params(a_re, a_im, log_dt, b_re, b_im, c_re, c_im, d_skip):
    L, ch, gl = S5N_CHUNK, S5_CH, S5N_GROUPS
    a_re, a_im = a_re.astype(F32), a_im.astype(F32)
    groups = a_re.shape[1]
    nb = groups // gl
    dt = jnp.exp(log_dt.astype(F32))[..., None]
    e = jnp.arange(L + 1, dtype=F32)[:, None, None, None]
    mag = jnp.exp(a_re * dt * e)
    pw_re, pw_im = mag * jnp.cos(a_im * dt * e), mag * jnp.sin(a_im * dt * e)
    ab_re, ab_im = pw_re[1], pw_im[1]
    den = a_re * a_re + a_im * a_im
    f_re = ((ab_re - 1.0) * a_re + ab_im * a_im) / den
    f_im = (ab_im * a_re - (ab_re - 1.0) * a_im) / den
    bb_re = f_re[..., None] * b_re - f_im[..., None] * b_im
    bb_im = f_re[..., None] * b_im + f_im[..., None] * b_re
    c_re, c_im = c_re.astype(F32), c_im.astype(F32)
    hp = lax.Precision.HIGHEST
    idx = jnp.arange(L)

    cp_re = c_re[None] * pw_re[:L, :, :, None, :] - c_im[None] * pw_im[:L, :, :, None, :]
    cp_im = c_re[None] * pw_im[:L, :, :, None, :] + c_im[None] * pw_re[:L, :, :, None, :]
    kmat = (jnp.einsum("ldgcp,dgpk->dglck", cp_re, bb_re, precision=hp)
            - jnp.einsum("ldgcp,dgpk->dglck", cp_im, bb_im, precision=hp))
    lag = idx[None, :] - idx[:, None]

    def toeplitz(k, lg):
        return jnp.where((lg >= 0)[None, :, :, None, None], k[:, jnp.clip(lg, 0, L - 1)], 0.0)

    t = toeplitz(kmat[0], lag) + toeplitz(kmat[1], -lag)
    skip = (idx[:, None] == idx[None, :])[None, :, :, None, None] * (
        jnp.eye(ch, dtype=F32)[None, None, None] * d_skip.astype(F32)[:, None, None, :, None])
    t = (t + skip).reshape(nb, gl, L, L, ch, ch)
    m_mat = t.transpose(0, 2, 1, 5, 3, 4).reshape(nb, L, LANES, L * ch)

    def state_in(d, exps):
        p_re, p_im = pw_re[exps, d], pw_im[exps, d]
        w_re = p_re[..., None] * bb_re[d][None] - p_im[..., None] * bb_im[d][None]
        w_im = p_re[..., None] * bb_im[d][None] + p_im[..., None] * bb_re[d][None]
        return jnp.stack([w_re, w_im], axis=0).transpose(2, 1, 4, 0, 3)

    w_full = jnp.stack([state_in(0, L - 1 - idx), state_in(1, idx)], axis=3)
    w_full = w_full.reshape(nb, gl, L, ch, 2, 2, S5_STATE)
    w_mat = w_full.transpose(0, 2, 1, 3, 4, 5, 6).reshape(nb, L, LANES, 4 * S5_STATE)

    def state_out(d, exps):
        p_re, p_im = pw_re[exps, d], pw_im[exps, d]
        v_re = c_re[d][None] * p_re[:, :, None, :] - c_im[d][None] * p_im[:, :, None, :]
        v_im = c_re[d][None] * p_im[:, :, None, :] + c_im[d][None] * p_re[:, :, None, :]
        return jnp.stack([v_re, -v_im], axis=0).transpose(2, 0, 4, 1, 3)

    v_full = jnp.stack([state_out(0, idx + 1), state_out(1, L - idx)], axis=1)
    v_full = v_full.reshape(nb, gl, 2, 2, S5_STATE, L, ch)
    v_mat = v_full.transpose(0, 2, 3, 1, 4, 5, 6).reshape(nb, 4 * S5N_HALF, L * ch)

    a_pow = jnp.stack([pw_re[L], pw_im[L]], axis=1).reshape(2, 2, nb * S5N_HALF)
    return m_mat.astype(BF16), w_mat.astype(BF16), v_mat.astype(BF16), a_pow


def _s5n_spreaders():
    L, ch, gl, st = S5N_CHUNK, S5_CH, S5N_GROUPS, S5_STATE
    e_out = jnp.einsum("ij,cd->icjd", jnp.eye(L), jnp.eye(ch))
    e_out = jnp.broadcast_to(e_out[:, :, :, None, :], (L, ch, L, gl, ch)).reshape(L * ch, L * LANES)
    e_st = jnp.broadcast_to(jnp.eye(4 * st).reshape(4 * st, 4, 1, st), (4 * st, 4, gl, st))
    return e_out.astype(BF16), e_st.reshape(4 * st, 4 * S5N_HALF).astype(BF16)


def _s5n_expand(compact, spread, row_group, col_group):
    full = jnp.dot(compact, spread, preferred_element_type=F32)
    rows = lax.broadcasted_iota(jnp.int32, full.shape, 0)
    cols = lax.broadcasted_iota(jnp.int32, full.shape, 1)
    return jnp.where(row_group(rows) == col_group(cols), full, 0.0).astype(BF16)


_S5N_IN_ROW_GROUP = lambda r: r // S5_CH
_S5N_STATE_GROUP = lambda c: (c % S5N_HALF) // S5_STATE
_S5N_OUT_COL_GROUP = lambda c: (c % LANES) // S5_CH


def _s5n_state_in_body(u_ref, w_ref, e_ref, o_ref, w_exp):
    @pl.when(pl.program_id(1) == 0)
    def _():
        for j in range(S5N_CHUNK):
            w_exp[j * LANES:(j + 1) * LANES, :] = _s5n_expand(w_ref[0, j], e_ref[...],
                                                              _S5N_IN_ROW_GROUP, _S5N_STATE_GROUP)

    u_cat = jnp.concatenate([u_ref[:, j, :] for j in range(S5N_CHUNK)], axis=1)
    acc = jnp.dot(u_cat, w_exp[...], preferred_element_type=F32)
    for d in range(2):
        for r in range(2):
            lo = (2 * d + r) * S5N_HALF
            o_ref[d, r] = acc[:, lo:lo + S5N_HALF]


def _s5n_scan_body(sr_ref, si_ref, a_ref, o_ref, *, nk, nk_ctx):
    d = pl.program_id(0)
    ar, ai = a_ref[0, 0], a_ref[0, 1]

    def step(k, carry):
        hr, hi = carry
        o_ref[0, 0, pl.ds(k, 1), :] = hr
        o_ref[0, 1, pl.ds(k, 1), :] = hi
        sr = sr_ref[0, 0, pl.ds(k, 1), :]
        si = si_ref[0, 0, pl.ds(k, 1), :]
        return ar * hr - ai * hi + sr, ar * hi + ai * hr + si

    zero = jnp.zeros_like(ar)
    unroll = 8 if (nk % 8 == 0 and nk_ctx % 8 == 0) else 1

    @pl.when(d == 0)
    def _():
        lax.fori_loop(0, nk, step, (zero, zero), unroll=unroll)

    @pl.when(d == 1)
    def _():
        hc = lax.fori_loop(0, nk_ctx, lambda t, c: step(nk_ctx - 1 - t, c), (zero, zero), unroll=unroll)
        lax.fori_loop(0, nk - nk_ctx, lambda t, c: step(nk - 1 - t, c), hc, unroll=unroll)


def _s5n_out_body(u_ref, m_ref, h_ref, v_ref, e_ref, o_ref, m_exp, v_exp):
    @pl.when(pl.program_id(1) == 0)
    def _():
        for j in range(S5N_CHUNK):
            m_exp[j * LANES:(j + 1) * LANES, :] = _s5n_expand(m_ref[0, j], e_ref[...],
                                                              _S5N_IN_ROW_GROUP, _S5N_OUT_COL_GROUP)
        v_exp[...] = _s5n_expand(v_ref[0], e_ref[...], _S5N_STATE_GROUP, _S5N_OUT_COL_GROUP)

    u_cat = jnp.concatenate([u_ref[:, j, :] for j in range(S5N_CHUNK)], axis=1)
    h_cat = jnp.concatenate([h_ref[d, r].astype(BF16) for d in range(2) for r in range(2)], axis=1)
    acc = (jnp.dot(u_cat, m_exp[...], preferred_element_type=F32)
           + jnp.dot(h_cat, v_exp[...], preferred_element_type=F32))
    y = jax.nn.gelu(acc).astype(o_ref.dtype)
    for i in range(S5N_CHUNK):
        o_ref[:, i, :] = y[:, i * LANES:(i + 1) * LANES]


def _s5n_mix(proj, su_off, params, *, batch, seq, ctx_len, gw):
    m_mat, w_mat, v_mat, a_pow = params
    L, half = S5N_CHUNK, S5N_HALF
    nb = gw // LANES
    nk, nk_ctx = seq // L, ctx_len // L
    assert su_off % LANES == 0 and seq % L == 0 and ctx_len % L == 0
    ub = su_off // LANES
    u3 = proj.reshape(batch * nk, L, proj.shape[1])
    lanes = batch * nb * half
    u_spec = pl.BlockSpec((nk, L, LANES), lambda n, b: (b, 0, ub + n))
    h_spec = pl.BlockSpec((2, 2, nk, half), lambda n, b: (0, 0, 0, b * nb + n))

    e_out, e_st = _s5n_spreaders()
    whole = lambda a: pl.BlockSpec(a.shape, lambda n, b: (0,) * a.ndim)
    s_in = pl.pallas_call(
        _s5n_state_in_body,
        grid=(nb, batch),
        in_specs=[u_spec, pl.BlockSpec((1,) + w_mat.shape[1:], lambda n, b: (n, 0, 0, 0)), whole(e_st)],
        out_specs=h_spec,
        out_shape=jax.ShapeDtypeStruct((2, 2, nk, lanes), F32),
        scratch_shapes=[pltpu.VMEM((L * LANES, 4 * half), BF16)],
        compiler_params=_cp("parallel", "arbitrary"),
        name="s5_state_in",
    )(u3, w_mat, e_st)

    wl = _pick_tile(lanes, S5_SCAN_LANES, LANES)
    part_spec = lambda r: pl.BlockSpec((1, 1, nk, wl), lambda d, w: (d, r, 0, w))
    a_lanes = jnp.tile(a_pow.reshape(2, 2, 1, nb * half), (1, 1, 1, batch))
    h_prev = pl.pallas_call(
        functools.partial(_s5n_scan_body, nk=nk, nk_ctx=nk_ctx),
        grid=(2, lanes // wl),
        in_specs=[part_spec(0), part_spec(1), pl.BlockSpec((1, 2, 1, wl), lambda d, w: (d, 0, 0, w))],
        out_specs=pl.BlockSpec((1, 2, nk, wl), lambda d, w: (d, 0, 0, w)),
        out_shape=jax.ShapeDtypeStruct((2, 2, nk, lanes), F32),
        compiler_params=_cp("parallel", "parallel"),
        name="s5_scan",
    )(s_in, s_in, a_lanes)

    y3 = pl.pallas_call(
        _s5n_out_body,
        grid=(nb, batch),
        in_specs=[u_spec,
                  pl.BlockSpec((1,) + m_mat.shape[1:], lambda n, b: (n, 0, 0, 0)),
                  h_spec,
                  pl.BlockSpec((1,) + v_mat.shape[1:], lambda n, b: (n, 0, 0)),
                  whole(e_out)],
        out_specs=pl.BlockSpec((nk, L, LANES), lambda n, b: (b, 0, n)),
        out_shape=jax.ShapeDtypeStruct((batch * nk, L, gw), BF16),
        scratch_shapes=[pltpu.VMEM((L * LANES, L * LANES), BF16), pltpu.VMEM((4 * half, L * LANES), BF16)],
        compiler_params=_cp("parallel", "arbitrary"),
        name="s5_out",
    )(u3, m_mat, h_prev, v_mat, e_out)
    return y3.reshape(batch * seq, gw)


def _expert_changed(te_ref):
    i = pl.program_id(1)
    return (i == 0) | (te_ref[i] != te_ref[jnp.maximum(i - 1, 0)])


def _moe_up_body(te_ref, x_ref, wg_ref, wu_ref, o_ref, wg_bf, wu_bf):
    @pl.when(_expert_changed(te_ref))
    def _():
        wg_bf[...] = wg_ref[0, 0].astype(BF16)
        wu_bf[...] = wu_ref[0, 0].astype(BF16)

    x = x_ref[...]
    a = jnp.dot(x, wg_bf[...], preferred_element_type=F32)
    b = jnp.dot(x, wu_bf[...], preferred_element_type=F32)
    o_ref[...] = (a * _sigmoid(a) * b).astype(o_ref.dtype)


def _moe_down_body(te_ref, h_ref, w_ref, rw_ref, o_ref, w_bf):
    @pl.when(_expert_changed(te_ref))
    def _():
        w_bf[...] = w_ref[0, 0].astype(BF16)

    y = jnp.dot(h_ref[...], w_bf[...], preferred_element_type=F32)
    o_ref[...] = (rw_ref[...] * y).astype(o_ref.dtype)


def _moe_down_into_body(te_ref, h_ref, w_ref, rw_ref, prev_ref, o_ref, w_bf):
    _moe_down_body(te_ref, h_ref, w_ref, rw_ref, o_ref, w_bf)


def _moe_combine_body(x_ref, y0_ref, y1_ref, gate_ref, g_ref, sh_ref, sc_ref, o_ref, h_ref, *,
                      tm, tiles_per_batch, ctx_len):
    i = pl.program_id(0)
    is_ctx = (i % tiles_per_batch) * tm < ctx_len
    pick = lambda r: jnp.where(is_ctx, r[0, 0:1, :], r[0, 1:2, :])
    x = x_ref[...] + pick(gate_ref) * (y0_ref[...].astype(F32) + y1_ref[...].astype(F32))
    o_ref[...] = x
    y = x * lax.rsqrt(jnp.mean(x * x, axis=-1, keepdims=True) + NORM_EPS) * g_ref[...]
    h_ref[...] = (y * (1.0 + pick(sc_ref)) + pick(sh_ref)).astype(BF16)


def _moe_final_body(x_ref, y0_ref, y1_ref, gate_ref, g_ref, o_ref):
    x = x_ref[...] + gate_ref[0, 1:2, :] * (y0_ref[...].astype(F32) + y1_ref[...].astype(F32))
    o_ref[...] = x * lax.rsqrt(jnp.mean(x * x, axis=-1, keepdims=True) + NORM_EPS) * g_ref[...]


def _route(logits):
    assert MOE_TOPK == 2
    g_logit = logits[:, :MOE_GROUPS]
    g_prob = jax.nn.softmax(g_logit, axis=-1)
    g_idx = jnp.argmax(g_prob, axis=-1)
    g_p = jnp.max(g_prob, axis=-1)
    e_logit = logits[:, MOE_GROUPS:MOE_GROUPS + MOE_GROUPS * MOE_PER_GROUP]
    e_logit = e_logit.reshape(-1, MOE_GROUPS, MOE_PER_GROUP)
    sel = (jnp.arange(MOE_GROUPS)[None, :] == g_idx[:, None])[:, :, None]
    e_logit = jnp.sum(jnp.where(sel, e_logit, 0.0), axis=1)
    e_prob = jax.nn.softmax(e_logit, axis=-1)
    i0 = jnp.argmax(e_prob, axis=-1)
    p0 = jnp.max(e_prob, axis=-1)
    rest = jnp.where(jnp.arange(MOE_PER_GROUP)[None, :] == i0[:, None], -1.0, e_prob)
    i1 = jnp.argmax(rest, axis=-1)
    p1 = jnp.max(rest, axis=-1)
    e_p = jnp.stack([p0, p1], axis=-1)
    w = g_p[:, None] * e_p / jnp.sum(e_p, axis=-1, keepdims=True)
    ids = g_idx[:, None] * MOE_PER_GROUP + jnp.stack([i0, i1], axis=-1)
    return ids.astype(jnp.int32), w


def _moe(h, logits, x, gate, w_gate, w_up, w_down, layer, *, rows_per_batch, ctx_len,
         final_g=None, next_norm=None):
    t, d = h.shape
    _, n_exp, _, dff = w_gate.shape
    tile = MOE_TILE
    ids, wts = _route(logits)
    flat_e = ids.reshape(-1)
    onehot = (flat_e[:, None] == jnp.arange(n_exp)[None, :]).astype(jnp.int32)
    counts = onehot.sum(0)
    rank = jnp.take_along_axis(jnp.cumsum(onehot, axis=0) - onehot, flat_e[:, None], axis=1)[:, 0]
    padded = (counts + tile - 1) // tile * tile
    starts = jnp.cumsum(padded) - padded
    pos = starts[flat_e] + rank
    n_rows = (t * MOE_TOPK // tile + n_exp) * tile
    n_tiles = n_rows // tile
    token = (jnp.arange(t * MOE_TOPK, dtype=jnp.int32) // MOE_TOPK).astype(F32)
    table = jnp.zeros((n_rows, 2), F32).at[pos].set(jnp.stack([token, wts.reshape(-1)], axis=1),
                                                    unique_indices=True, mode="promise_in_bounds")
    src = table[:, 0].astype(jnp.int32)
    row_w = table[:, 1]
    tile_start = jnp.arange(n_tiles, dtype=jnp.int32) * tile
    ends = starts + padded
    tile_e = jnp.minimum(jnp.sum(tile_start[:, None] >= ends[None, :], axis=1), n_exp - 1).astype(jnp.int32)

    n_chunks = max(c for c in (8, 4, 2, 1) if n_tiles % c == 0)
    tpc = n_tiles // n_chunks
    tn_up = _pick_tile(dff, MOE_UP_COL_TILE, LANES)
    tn_dn = _pick_tile(d, MOE_DOWN_COL_TILE, LANES)
    w_up_spec = pl.BlockSpec((1, 1, d, tn_up), lambda j, i, te: (layer, te[i], 0, j))
    ys = None
    for c in range(n_chunks):
        rows_c = slice(c * tpc * tile, (c + 1) * tpc * tile)
        te_c = tile_e[c * tpc:(c + 1) * tpc]
        xs = h.at[src[rows_c]].get(mode="promise_in_bounds")
        hid = pl.pallas_call(
            _moe_up_body,
            grid_spec=pltpu.PrefetchScalarGridSpec(
                num_scalar_prefetch=1,
                grid=(dff // tn_up, tpc),
                in_specs=[pl.BlockSpec((tile, d), lambda j, i, te: (i, 0)), w_up_spec, w_up_spec],
                out_specs=pl.BlockSpec((tile, tn_up), lambda j, i, te: (i, j)),
                scratch_shapes=[pltpu.VMEM((d, tn_up), BF16), pltpu.VMEM((d, tn_up), BF16)]),
            out_shape=jax.ShapeDtypeStruct((tpc * tile, dff), BF16),
            compiler_params=_cp("arbitrary", "arbitrary"),
            name="moe_up",
        )(te_c, xs, w_gate, w_up)
        first = ys is None
        ys = pl.pallas_call(
            _moe_down_body if first else _moe_down_into_body,
            grid_spec=pltpu.PrefetchScalarGridSpec(
                num_scalar_prefetch=1,
                grid=(d // tn_dn, tpc),
                in_specs=[pl.BlockSpec((tile, dff), lambda j, i, te: (i, 0)),
                          pl.BlockSpec((1, 1, dff, tn_dn), lambda j, i, te: (layer, te[i], 0, j)),
                          pl.BlockSpec((tile, 1), lambda j, i, te: (i, 0))]
                         + ([] if first else [pl.BlockSpec(memory_space=pl.ANY)]),
                out_specs=pl.BlockSpec((tile, tn_dn), lambda j, i, te, c=c: (c * tpc + i, j)),
                scratch_shapes=[pltpu.VMEM((dff, tn_dn), BF16)]),
            out_shape=jax.ShapeDtypeStruct((n_rows, d), BF16),
            input_output_aliases={} if first else {4: 0},
            compiler_params=_cp("arbitrary", "arbitrary"),
            name="moe_down",
        )(te_c, hid, w_down, row_w[rows_c].reshape(tpc * tile, 1), *(() if first else (ys,)))

    pos2 = pos.reshape(t, MOE_TOPK)
    y0 = ys.at[pos2[:, 0]].get(mode="promise_in_bounds")
    y1 = ys.at[pos2[:, 1]].get(mode="promise_in_bounds")
    tm = _pick_tile(math.gcd(rows_per_batch, ctx_len), ROW_TILE, 8)
    tpb = rows_per_batch // tm
    if final_g is not None:
        batch = t // rows_per_batch
        ctx_tiles, lat_tiles = ctx_len // tm, (rows_per_batch - ctx_len) // tm
        lat_spec = pl.BlockSpec((tm, d), lambda b, i: (b * tpb + ctx_tiles + i, 0))
        out = pl.pallas_call(
            _moe_final_body,
            grid=(batch, lat_tiles),
            in_specs=[lat_spec, lat_spec, lat_spec, pl.BlockSpec((1, 2, d), lambda b, i: (b, 0, 0)),
                      pl.BlockSpec((1, d), lambda b, i: (0, 0))],
            out_specs=pl.BlockSpec((tm, d), lambda b, i: (b * lat_tiles + i, 0)),
            out_shape=jax.ShapeDtypeStruct((batch * lat_tiles * tm, d), F32),
            compiler_params=_cp("parallel", "parallel"),
            name="moe_combine_final",
        )(x, y0, y1, gate, final_g.reshape(1, d).astype(F32))
        return out.reshape(batch, lat_tiles * tm, d)
    row_spec = pl.BlockSpec((tm, d), lambda i: (i, 0))
    mod_spec = pl.BlockSpec((1, 2, d), lambda i: (i // tpb, 0, 0))
    next_g, next_shift, next_scale = next_norm
    return pl.pallas_call(
        functools.partial(_moe_combine_body, tm=tm, tiles_per_batch=tpb, ctx_len=ctx_len),
        grid=(t // tm,),
        in_specs=[row_spec, row_spec, row_spec, mod_spec, pl.BlockSpec((1, d), lambda i: (0, 0)),
                  mod_spec, mod_spec],
        out_specs=[row_spec, row_spec],
        out_shape=[jax.ShapeDtypeStruct((t, d), F32), jax.ShapeDtypeStruct((t, d), BF16)],
        compiler_params=_cp("parallel"),
        name="moe_combine",
    )(x, y0, y1, gate, next_g.reshape(1, d).astype(F32), next_shift, next_scale)


def kernel(x, c, ctx, c_ctx, ada_w, ada_b, norm_mix, norm_ffn, w_in, w_out, diff_lambda, diff_subln,
           s5_a_re, s5_a_im, s5_log_dt, s5_b_re, s5_b_im, s5_c_re, s5_c_im, s5_d, s5_glu_w, s5_glu_b,
           mla_q_norm, mla_kv_norm, mla_w_uq, mla_w_ukv, ret_decay, ret_norm,
           moe_wg, moe_bg, moe_we, moe_be, moe_w_gate, moe_w_up, moe_w_down, final_norm):
    batch, n_lat, d = x.shape
    ctx_len = ctx.shape[1]
    depth = ada_w.shape[0]
    seq = ctx_len + n_lat
    rows = batch * seq
    gw = d // 4
    heads = gw // LANES
    q_rank, kv_rank = 3 * d // 16, d // 16
    ret_qk = heads * RET_K
    n_route = MOE_GROUPS + MOE_GROUPS * MOE_PER_GROUP
    assert heads % 2 == 0 and ctx_len % RET_CHUNK == 0 and n_lat % RET_CHUNK == 0

    splits = (gw, gw, gw, gw, q_rank, kv_rank, ROPE_DIM, ret_qk, ret_qk, gw, gw)
    offs = [0]
    for s_ in splits:
        offs.append(offs[-1] + s_)
    names = ("dq", "dk", "dv", "su", "mcq", "mckv", "mkr", "rq", "rk", "rv", "rg")
    src_col = {n_: (offs[i], offs[i + 1]) for i, n_ in enumerate(names)}
    order = ("dq", "dk", "rq", "rk", "dv", "su", "rv", "rg", "mcq", "mckv")
    col = {}
    pos = 0
    for n_ in order:
        col[n_] = pos
        pos += src_col[n_][1] - src_col[n_][0]
    n_main = pos
    n_rope = col["dv"]
    uq_cols = jnp.arange(heads * (MLA_NOPE + ROPE_DIM)).reshape(heads, MLA_NOPE + ROPE_DIM)
    uq_perm = jnp.concatenate([uq_cols[:, :MLA_NOPE].reshape(-1), uq_cols[:, MLA_NOPE:].reshape(-1)])

    tables = _rope_tables(n_lat, ctx_len)
    log2e = math.log2(math.e)
    rope_scale = jnp.ones((n_rope,), F32).at[col["rk"]:col["rk"] + ret_qk].set(RET_K ** -0.5)
    rope_scale = rope_scale.at[col["dq"]:col["dq"] + gw].set(DIFF_HEAD_DIM ** -0.5 * log2e)
    mla_q_scale = (MLA_NOPE + ROPE_DIM) ** -0.5 * log2e
    main_scale = jnp.concatenate([rope_scale, jnp.ones((n_main - n_rope,), F32)])

    cond = jnp.concatenate([c_ctx[None, :], c], axis=0)
    cond = jnp.pad(cond * _sigmoid(cond), ((0, 8 - (batch + 1) % 8 if (batch + 1) % 8 else 0), (0, 0)))

    mod_all = _ada_mod(cond, ada_w, ada_b)

    xa = None
    tm_big = _pick_tile(seq, MM_ROW_TILE)
    tn = lambda n_: _pick_tile(n_, MM_COL_TILE, LANES)

    def layer_mods(l):
        mod = mod_all[l].reshape(cond.shape[0], 6, d)
        return [jnp.stack([jnp.broadcast_to(mod[0, i], (batch, d)), mod[1:batch + 1, i]], axis=1)
                for i in range(6)]

    h_mix = None
    for l in range(depth):
        lam_init = 0.8 - 0.6 * math.exp(-0.3 * l)
        mods = layer_mods(l)

        w_main = jnp.concatenate([w_in[l, :, src_col[n_][0]:src_col[n_][1]] for n_ in order],
                                 axis=1).astype(BF16)
        w_kr = jnp.concatenate([w_in[l, :, src_col["mkr"][0]:src_col["mkr"][1]]] * (LANES // ROPE_DIM),
                               axis=1).astype(BF16)

        if l == 0:
            h, xa = _norm_mod_first(ctx, x, norm_mix[l], mods[0], mods[1])
        else:
            h = h_mix
        proj = _mm([h], w_main, name="in_proj", out_dtype=BF16, tm=tm_big,
                   tn=tn(math.gcd(n_main, n_rope)), rope=(tables, main_scale, (0, n_rope)),
                   rows_per_batch=seq)
        krr = _mm([h], w_kr, name="in_proj_kr", out_dtype=BF16, tm=tm_big, tn=LANES,
                  rope=(tables, jnp.ones((LANES,), F32), (0, LANES)), rows_per_batch=seq)
        qk = proj

        lv = diff_lambda[l].astype(F32)
        lam = jnp.exp(jnp.sum(lv[0] * lv[1])) - jnp.exp(jnp.sum(lv[2] * lv[3])) + lam_init
        a_out = _diff_attn(qk, proj, lam, diff_subln[l], batch=batch, seq=seq, ctx_len=ctx_len,
                           heads=heads, q_blk=col["dq"] // LANES, k_blk=col["dk"] // LANES,
                           v_blk=col["dv"] // LANES, post=1.0 - lam_init)

        s5p = _s5n_params(s5_a_re[l], s5_a_im[l], s5_log_dt[l], s5_b_re[l], s5_b_im[l],
                          s5_c_re[l], s5_c_im[l], s5_d[l])
        s_act = _s5n_mix(proj, col["su"], s5p, batch=batch, seq=seq, ctx_len=ctx_len, gw=gw)
        s_out = _mm([s_act], s5_glu_w[l].astype(BF16), name="s5_glu", out_dtype=BF16, tm=tm_big, tn=tn(gw),
                    bias=s5_glu_b[l], glu_in=s_act)

        cq = proj[:, col["mcq"]:col["mcq"] + q_rank]
        ckv = proj[:, col["mckv"]:col["mckv"] + kv_rank]
        w_uq = (mla_w_uq[l][:, uq_perm] * mla_q_scale).astype(BF16)
        n_qn, n_q = heads * MLA_NOPE, heads * (MLA_NOPE + ROPE_DIM)
        q_up = _mm([cq], w_uq, name="mla_q_up", out_dtype=BF16, tm=tm_big, tn=tn(math.gcd(n_qn, n_q)),
                   norm_g=mla_q_norm[l], rope=(tables, jnp.ones((n_q,), F32), (n_qn, n_q)),
                   rows_per_batch=seq)
        kv_up = _mm([ckv], mla_w_ukv[l].astype(BF16), name="mla_kv_up", out_dtype=BF16, tm=tm_big,
                    tn=tn(mla_w_ukv.shape[2]), norm_g=mla_kv_norm[l])
        m_out = _mla_attn(q_up, kv_up, krr, batch=batch, seq=seq, ctx_len=ctx_len, heads=heads)

        log_g = jax.nn.log_sigmoid(ret_decay[l].astype(F32))
        r_out = _retention(qk, proj, log_g, ret_norm[l], batch=batch, seq=seq, ctx_len=ctx_len,
                           heads=heads, q_off=col["rq"], k_off=col["rk"], v_off=col["rv"], g_off=col["rg"])

        xa = _mm([a_out, s_out, m_out, r_out], w_out[l].astype(BF16), name="out_proj", out_dtype=F32, tm=tm_big, tn=tn(d),
                 res=xa, gate=mods[2], rows_per_batch=seq, ctx_len=ctx_len)

        w_r = jnp.concatenate([moe_wg[l], moe_we[l]], axis=1).astype(F32)
        w_r = jnp.pad(w_r, ((0, 0), (0, LANES - n_route)))
        w_r_hi = w_r.astype(BF16)
        w_r_lo = (w_r - w_r_hi.astype(F32)).astype(BF16)
        b_r = jnp.pad(jnp.concatenate([moe_bg[l], moe_be[l]]).astype(F32), (0, LANES - n_route))
        h, logits = _norm_mod(xa, norm_ffn[l], mods[3], mods[4], rows_per_batch=seq, ctx_len=ctx_len,
                              router=(w_r_hi, w_r_lo, b_r.reshape(1, LANES)))
        if l == depth - 1:
            return _moe(h, logits, xa, mods[5], moe_w_gate, moe_w_up, moe_w_down, l,
                        rows_per_batch=seq, ctx_len=ctx_len, final_g=final_norm)
        nxt = layer_mods(l + 1)
        xa, h_mix = _moe(h, logits, xa, mods[5], moe_w_gate, moe_w_up, moe_w_down, l,
                         rows_per_batch=seq, ctx_len=ctx_len, next_norm=(norm_mix[l + 1], nxt[0], nxt[1]))
```

```python
import functools
import math

import jax
import jax.numpy as jnp
import numpy as np
from jax import lax
from jax.experimental import pallas as pl
from jax.experimental.pallas import tpu as pltpu

BF16 = jnp.bfloat16
F32 = jnp.float32

V7X_VMEM_BYTES = 64 * 2**20
VMEM_LIMIT = V7X_VMEM_BYTES - 12 * 2**20
LANES = 128

GRID_W = 64
ROPE_DIM = 64
ROPE_BASE = 10000.0
NORM_EPS = 1e-6
DIFF_HEAD_DIM = 64
S5_CH = 16
S5_STATE = 64
MLA_NOPE = 128
MLA_V = 128
RET_K = 64
RET_V = 128
RET_CHUNK = 128
MOE_GROUPS = 4
MOE_PER_GROUP = 4
MOE_TOPK = 2
ROPE_QUARTER = ROPE_DIM // 4

MM_ROW_TILE = 1088
MM_COL_TILE = 512
ROW_TILE = 256
ATTN_Q_TILE = 512
S5_SCAN_LANES = 2048
MOE_TILE = 256
MOE_UP_COL_TILE = 512
MOE_DOWN_COL_TILE = 4096


def _cp(*sem):
    return pltpu.CompilerParams(dimension_semantics=sem, vmem_limit_bytes=VMEM_LIMIT)


def _pick_tile(n, target, mult=16):
    best = None
    for t in range(mult, min(n, target) + 1, mult):
        if n % t == 0:
            best = t
    assert best is not None, (n, target)
    return best


def _sigmoid(x):
    return 1.0 / (1.0 + jnp.exp(-x))


def _rotate(x, cos, sin, swap):
    reps = x.shape[1] // LANES
    tile = lambda t: jnp.tile(t, (1, reps))
    partner = jnp.dot(x.astype(BF16), swap, preferred_element_type=F32)
    return x * tile(cos) + partner * tile(sin)


def _rope_swap(width):
    lane = np.arange(width)
    first = (lane // ROPE_QUARTER) % 2 == 0
    src = np.where(first, lane + ROPE_QUARTER, lane - ROPE_QUARTER)
    swap = np.zeros((width, width), np.float32)
    swap[src, lane] = np.where(first, -1.0, 1.0)
    return jnp.asarray(swap, BF16)


def _mm_body(*refs, nx, ksizes, has_norm, has_bias, epilogue, tm, tiles_per_batch, ctx_len, rope_tiles):
    x_refs = refs[:nx]
    w_ref = refs[nx]
    idx = nx + 1
    g_ref = b_ref = e_ref = res_ref = gate_ref = None
    if has_norm:
        g_ref = refs[idx]; idx += 1
    if has_bias:
        b_ref = refs[idx]; idx += 1
    if epilogue == "glu":
        e_ref = refs[idx]; idx += 1
    if epilogue == "resgate":
        res_ref, gate_ref = refs[idx], refs[idx + 1]; idx += 2
    if epilogue == "rope":
        cos_ref, sin_ref, swap_ref, cs_ref = refs[idx:idx + 4]; idx += 4
    o_ref = refs[idx]

    acc = None
    off = 0
    for xr, ks in zip(x_refs, ksizes):
        x = xr[...]
        if has_norm:
            xf = x.astype(F32)
            xf = xf * lax.rsqrt(jnp.mean(xf * xf, axis=-1, keepdims=True) + NORM_EPS)
            x = xf * g_ref[...]
        x = x.astype(BF16)
        w = w_ref[off:off + ks, :].astype(BF16)
        part = jnp.dot(x, w, preferred_element_type=F32)
        acc = part if acc is None else acc + part
        off += ks
    if has_bias:
        acc = acc + b_ref[...]
    if epilogue == "glu":
        acc = e_ref[...].astype(F32) * _sigmoid(acc)
    elif epilogue == "resgate":
        i = pl.program_id(0)
        row = (i % tiles_per_batch) * tm + lax.broadcasted_iota(jnp.int32, (tm, 1), 0)
        gate = jnp.where(row < ctx_len, gate_ref[0, 0:1, :], gate_ref[0, 1:2, :])
        acc = res_ref[...] + gate * acc
    if epilogue == "rope":
        j = pl.program_id(1)
        roped = (j >= rope_tiles[0]) & (j < rope_tiles[1])

        @pl.when(roped)
        def _():
            y = _rotate(acc, cos_ref[...], sin_ref[...], swap_ref[...]) * cs_ref[...]
            o_ref[...] = y.astype(o_ref.dtype)

        @pl.when(jnp.logical_not(roped))
        def _():
            o_ref[...] = acc.astype(o_ref.dtype)
    else:
        o_ref[...] = acc.astype(o_ref.dtype)


def _mm(xs, w, *, name, out_dtype, tm, tn, norm_g=None, bias=None, glu_in=None, res=None, gate=None,
        rope=None, rows_per_batch=None, ctx_len=0):
    m = xs[0].shape[0]
    ksizes = tuple(x.shape[1] for x in xs)
    k, n = w.shape
    assert sum(ksizes) == k and m % tm == 0 and n % tn == 0
    epilogue = ("glu" if glu_in is not None else "resgate" if res is not None
                else "rope" if rope is not None else None)
    tiles_per_batch = (rows_per_batch // tm) if rows_per_batch else 1
    rope_tiles = None
    in_specs = [pl.BlockSpec((tm, ks), lambda i, j: (i, 0)) for ks in ksizes]
    in_specs.append(pl.BlockSpec((k, tn), lambda i, j: (0, j)))
    args = list(xs) + [w]
    if norm_g is not None:
        in_specs.append(pl.BlockSpec((1, k), lambda i, j: (0, 0)))
        args.append(norm_g.reshape(1, k).astype(F32))
    if bias is not None:
        in_specs.append(pl.BlockSpec((1, tn), lambda i, j: (0, j)))
        args.append(bias.reshape(1, n).astype(F32))
    if epilogue == "glu":
        in_specs.append(pl.BlockSpec((tm, tn), lambda i, j: (i, j)))
        args.append(glu_in)
    if epilogue == "resgate":
        tpb = tiles_per_batch
        in_specs.append(pl.BlockSpec((tm, tn), lambda i, j: (i, j)))
        in_specs.append(pl.BlockSpec((1, 2, tn), lambda i, j: (i // tpb, 0, j)))
        args += [res, gate]
    if epilogue == "rope":
        tables, col_scale, (lo, hi) = rope
        assert lo % tn == 0 and hi % tn == 0
        rope_tiles = (lo // tn, hi // tn)
        tpb = tiles_per_batch
        in_specs += [pl.BlockSpec((tm, LANES), lambda i, j: (i % tpb, 0))] * 2
        in_specs.append(pl.BlockSpec((tn, tn), lambda i, j: (0, 0)))
        in_specs.append(pl.BlockSpec((1, tn), lambda i, j: (0, j)))
        args += list(tables) + [_rope_swap(tn), col_scale.reshape(1, n).astype(F32)]
    body = functools.partial(_mm_body, nx=len(xs), ksizes=ksizes, has_norm=norm_g is not None,
                             has_bias=bias is not None, epilogue=epilogue, tm=tm,
                             tiles_per_batch=tiles_per_batch, ctx_len=ctx_len, rope_tiles=rope_tiles)
    return pl.pallas_call(
        body,
        grid=(m // tm, n // tn),
        in_specs=in_specs,
        out_specs=pl.BlockSpec((tm, tn), lambda i, j: (i, j)),
        out_shape=jax.ShapeDtypeStruct((m, n), out_dtype),
        compiler_params=_cp("parallel", "arbitrary"),
        name=name,
    )(*args)


def _ada_body(c_ref, w_ref, b_ref, o_ref):
    acc = jnp.dot(c_ref[...].astype(BF16), w_ref[0].astype(BF16), preferred_element_type=F32)
    o_ref[0] = acc + b_ref[0]


def _ada_mod(cond, ada_w, ada_b):
    depth, d, n6 = ada_w.shape
    rows = cond.shape[0]
    tn = _pick_tile(n6, MM_COL_TILE, LANES)
    return pl.pallas_call(
        _ada_body,
        grid=(depth, n6 // tn),
        in_specs=[pl.BlockSpec((rows, d), lambda l, j: (0, 0)),
                  pl.BlockSpec((1, d, tn), lambda l, j: (l, 0, j)),
                  pl.BlockSpec((1, 1, tn), lambda l, j: (l, 0, j))],
        out_specs=pl.BlockSpec((1, rows, tn), lambda l, j: (l, 0, j)),
        out_shape=jax.ShapeDtypeStruct((depth, rows, n6), F32),
        compiler_params=_cp("parallel", "arbitrary"),
        name="ada_mod",
    )(cond, ada_w, ada_b.reshape(depth, 1, n6).astype(F32))


def _norm_mod_body(*refs, tm, tiles_per_batch, ctx_len, router):
    if router:
        x_ref, g_ref, sh_ref, sc_ref, whi_ref, wlo_ref, br_ref, h_ref, lg_ref = refs
    else:
        x_ref, g_ref, sh_ref, sc_ref, h_ref = refs
    i = pl.program_id(0)
    x = x_ref[...]
    y = x * lax.rsqrt(jnp.mean(x * x, axis=-1, keepdims=True) + NORM_EPS) * g_ref[...]
    is_ctx = (i % tiles_per_batch) * tm < ctx_len
    sh = jnp.where(is_ctx, sh_ref[0, 0:1, :], sh_ref[0, 1:2, :])
    sc = jnp.where(is_ctx, sc_ref[0, 0:1, :], sc_ref[0, 1:2, :])
    h = y * (1.0 + sc) + sh
    h_ref[...] = h.astype(BF16)
    if router:
        hi = h.astype(BF16)
        lo = (h - hi.astype(F32)).astype(BF16)
        lg = jnp.dot(hi, whi_ref[...], preferred_element_type=F32)
        lg = lg + jnp.dot(hi, wlo_ref[...], preferred_element_type=F32)
        lg = lg + jnp.dot(lo, whi_ref[...], preferred_element_type=F32)
        lg_ref[...] = lg + br_ref[...]


def _norm_mod_first_body(c_ref, x_ref, g_ref, sh_ref, sc_ref, h_ref, xa_ref, *, ctx_tiles):
    is_ctx = pl.program_id(1) < ctx_tiles
    x = jnp.where(is_ctx, c_ref[...], x_ref[...])
    xa_ref[...] = x
    y = x * lax.rsqrt(jnp.mean(x * x, axis=-1, keepdims=True) + NORM_EPS) * g_ref[...]
    sh = jnp.where(is_ctx, sh_ref[0, 0:1, :], sh_ref[0, 1:2, :])
    sc = jnp.where(is_ctx, sc_ref[0, 0:1, :], sc_ref[0, 1:2, :])
    h_ref[...] = (y * (1.0 + sc) + sh).astype(BF16)


def _norm_mod_first(ctx, x, g, shift, scale):
    batch, ctx_len, d = ctx.shape
    n_lat = x.shape[1]
    tm = _pick_tile(math.gcd(n_lat, ctx_len), ROW_TILE, 8)
    ctx_tiles, lat_tiles = ctx_len // tm, n_lat // tm
    tpb = ctx_tiles + lat_tiles
    rows = batch * tpb * tm
    out_spec = pl.BlockSpec((tm, d), lambda b, i: (b * tpb + i, 0))
    mod_spec = pl.BlockSpec((1, 2, d), lambda b, i: (b, 0, 0))
    return pl.pallas_call(
        functools.partial(_norm_mod_first_body, ctx_tiles=ctx_tiles),
        grid=(batch, tpb),
        in_specs=[pl.BlockSpec((tm, d), lambda b, i: (b * ctx_tiles + jnp.minimum(i, ctx_tiles - 1), 0)),
                  pl.BlockSpec((tm, d), lambda b, i: (b * lat_tiles + jnp.maximum(i - ctx_tiles, 0), 0)),
                  pl.BlockSpec((1, d), lambda b, i: (0, 0)), mod_spec, mod_spec],
        out_specs=[out_spec, out_spec],
        out_shape=[jax.ShapeDtypeStruct((rows, d), BF16), jax.ShapeDtypeStruct((rows, d), F32)],
        compiler_params=_cp("parallel", "arbitrary"),
        name="norm_mod_first",
    )(ctx.reshape(batch * ctx_len, d), x.reshape(batch * n_lat, d), g.reshape(1, d), shift, scale)


def _norm_mod(x, g, shift, scale, *, rows_per_batch, ctx_len, router=None):
    m, d = x.shape
    tm = _pick_tile(math.gcd(rows_per_batch, ctx_len), ROW_TILE, 8)
    tpb = rows_per_batch // tm
    in_specs = [
        pl.BlockSpec((tm, d), lambda i: (i, 0)),
        pl.BlockSpec((1, d), lambda i: (0, 0)),
        pl.BlockSpec((1, 2, d), lambda i: (i // tpb, 0, 0)),
        pl.BlockSpec((1, 2, d), lambda i: (i // tpb, 0, 0)),
    ]
    args = [x, g.reshape(1, d), shift, scale]
    out_specs = [pl.BlockSpec((tm, d), lambda i: (i, 0))]
    out_shape = [jax.ShapeDtypeStruct((m, d), BF16)]
    if router is not None:
        whi, wlo, br = router
        in_specs += [pl.BlockSpec((d, LANES), lambda i: (0, 0)),
                     pl.BlockSpec((d, LANES), lambda i: (0, 0)),
                     pl.BlockSpec((1, LANES), lambda i: (0, 0))]
        args += [whi, wlo, br]
        out_specs.append(pl.BlockSpec((tm, LANES), lambda i: (i, 0)))
        out_shape.append(jax.ShapeDtypeStruct((m, LANES), F32))
    body = functools.partial(_norm_mod_body, tm=tm, tiles_per_batch=tpb, ctx_len=ctx_len,
                             router=router is not None)
    outs = pl.pallas_call(body, grid=(m // tm,), in_specs=in_specs, out_specs=out_specs,
                          out_shape=out_shape, compiler_params=_cp("parallel"),
                          name="norm_mod_router" if router is not None else "norm_mod")(*args)
    return outs if router is not None else outs[0]


def _rope_tables(n_lat, ctx_len):
    rows = n_lat // GRID_W
    row = jnp.repeat(jnp.arange(rows, dtype=F32), GRID_W)
    col = jnp.tile(jnp.arange(GRID_W, dtype=F32), rows)
    quarter = ROPE_QUARTER
    inv = ROPE_BASE ** (-jnp.arange(quarter, dtype=F32) / quarter)
    ar = row[:, None] * inv
    ac = col[:, None] * inv
    ang = jnp.concatenate([ar, ar, ac, ac], axis=-1)
    ang = jnp.concatenate([jnp.zeros((ctx_len, ROPE_DIM), F32), ang], axis=0)
    ang = jnp.tile(ang, (1, LANES // ROPE_DIM))
    return jnp.cos(ang), jnp.sin(ang)


ATTN_ALIGN = 256


def _softmax_pv(q, k_ref, va_ref, bounds):
    ms, ovs = [], []
    for lo, hi in bounds:
        s = lax.dot_general(q, k_ref[lo:hi, :], (((1,), (1,)), ((), ())), preferred_element_type=F32)
        m = jnp.max(s, axis=-1, keepdims=True)
        e = jnp.exp2(s - m).astype(BF16)
        ovs.append(jnp.dot(e, va_ref[lo:hi, :], preferred_element_type=F32))
        ms.append(m)
    m_all = functools.reduce(jnp.maximum, ms)
    acc = sum(ov * jnp.exp2(m - m_all) for m, ov in zip(ms, ovs))
    return acc[:, :LANES] / acc[:, LANES:]


def _key_chunks(n_keys):
    if n_keys < 2 * ATTN_ALIGN:
        return ((0, n_keys),)
    half = (n_keys // ATTN_ALIGN + 1) // 2 * ATTN_ALIGN
    return ((0, half), (half, n_keys))


def _lat_tile(seq, ctx_len):
    n_lat = seq - ctx_len
    assert ctx_len % ATTN_ALIGN == 0 and n_lat % ATTN_ALIGN == 0
    return _pick_tile(n_lat, ATTN_Q_TILE, ATTN_ALIGN)


def _lat_rows(seq, ctx_len, tq):
    return lambda b, i: pl.multiple_of(b * seq + ctx_len + i * tq, ATTN_ALIGN)


def _diff_attn_body(lam_ref, q_ref, k_ref, v_ref, sub_ref, *rest, post, bounds):
    o_ref, va_ref = rest[-2:]

    def fill():
        va_ref[:, :LANES] = v_ref[...]
        va_ref[:, LANES:] = jnp.ones(v_ref.shape, BF16)

    if len(rest) == 2:
        pl.when(pl.program_id(2) == 0)(fill)
    else:
        fill()

    q = q_ref[...]
    lane = lax.broadcasted_iota(jnp.int32, (1, LANES), 1)
    first = lane < DIFF_HEAD_DIM
    zero = jnp.zeros_like(q)
    q0 = jnp.where(first, q, zero)
    q1 = jnp.where(first, zero, q)
    o = _softmax_pv(q0, k_ref, va_ref, bounds) - lam_ref[0] * _softmax_pv(q1, k_ref, va_ref, bounds)
    o = o * lax.rsqrt(jnp.mean(o * o, axis=-1, keepdims=True) + NORM_EPS) * sub_ref[...] * post
    o_ref[...] = o.astype(o_ref.dtype)


def _diff_attn(qk, proj, lam, subln, *, batch, seq, ctx_len, heads, q_blk, k_blk, v_blk, post):
    tq = _lat_tile(seq, ctx_len)
    rows = _lat_rows(seq, ctx_len, tq)
    smem = pl.BlockSpec(memory_space=pltpu.SMEM)
    args = (lam.reshape(1).astype(F32), qk, qk, proj, subln.reshape(1, LANES).astype(F32))
    out_shape = jax.ShapeDtypeStruct((batch * seq, heads * LANES), BF16)
    elem = (pl.Element(tq), pl.Element(LANES))
    lat = pl.pallas_call(
        functools.partial(_diff_attn_body, post=post, bounds=_key_chunks(seq)),
        grid=(batch, heads, (seq - ctx_len) // tq),
        in_specs=[
            smem,
            pl.BlockSpec(elem, lambda b, h, i: (rows(b, i), pl.multiple_of((q_blk + h) * LANES, LANES))),
            pl.BlockSpec((seq, LANES), lambda b, h, i: (b, k_blk + h)),
            pl.BlockSpec((seq, LANES), lambda b, h, i: (b, v_blk + h)),
            pl.BlockSpec((1, LANES), lambda b, h, i: (0, 0)),
        ],
        out_specs=pl.BlockSpec(elem, lambda b, h, i: (rows(b, i), pl.multiple_of(h * LANES, LANES))),
        out_shape=out_shape,
        scratch_shapes=[pltpu.VMEM((seq, 2 * LANES), BF16)],
        compiler_params=_cp("parallel", "parallel", "arbitrary"),
        name="diff_attn",
    )(*args)
    cpb = seq // ctx_len
    ctx_spec = lambda blk: pl.BlockSpec((ctx_len, LANES), lambda b, h: (b * cpb, blk + h))
    return pl.pallas_call(
        functools.partial(_diff_attn_body, post=post, bounds=_key_chunks(ctx_len)),
        grid=(batch, heads),
        in_specs=[smem, ctx_spec(q_blk), ctx_spec(k_blk), ctx_spec(v_blk),
                  pl.BlockSpec((1, LANES), lambda b, h: (0, 0)), pl.BlockSpec(memory_space=pl.ANY)],
        out_specs=ctx_spec(0),
        out_shape=out_shape,
        scratch_shapes=[pltpu.VMEM((ctx_len, 2 * LANES), BF16)],
        input_output_aliases={5: 0},
        compiler_params=_cp("parallel", "parallel"),
        name="diff_attn_ctx",
    )(*args, lat)


def _mla_attn_body(qn_ref, qr_ref, kn_ref, kr_ref, v_ref, *rest, bounds):
    o_ref, ka_ref, va_ref = rest[-3:]
    h = pl.program_id(1)

    def fill():
        ka_ref[:, :LANES] = kn_ref[...]
        ka_ref[:, LANES:] = kr_ref[...]
        va_ref[:, :LANES] = v_ref[...]
        va_ref[:, LANES:] = jnp.ones(v_ref.shape, BF16)

    if len(rest) == 3:
        pl.when(pl.program_id(2) == 0)(fill)
    else:
        fill()

    qr = qr_ref[...]
    lane = lax.broadcasted_iota(jnp.int32, (1, LANES), 1)
    mine = (lane < ROPE_DIM) == (h % 2 == 0)
    qr = jnp.where(mine, qr, jnp.zeros_like(qr))
    q = jnp.concatenate([qn_ref[...], qr], axis=1)
    o_ref[...] = _softmax_pv(q, ka_ref, va_ref, bounds).astype(o_ref.dtype)


def _mla_attn(q_up, kv_up, k_rope, *, batch, seq, ctx_len, heads):
    tq = _lat_tile(seq, ctx_len)
    rows = _lat_rows(seq, ctx_len, tq)
    args = (q_up, q_up, kv_up, k_rope, kv_up)
    out_shape = jax.ShapeDtypeStruct((batch * seq, heads * LANES), BF16)
    elem = (pl.Element(tq), pl.Element(LANES))
    lat = pl.pallas_call(
        functools.partial(_mla_attn_body, bounds=_key_chunks(seq)),
        grid=(batch, heads, (seq - ctx_len) // tq),
        in_specs=[
            pl.BlockSpec(elem, lambda b, h, i: (rows(b, i), pl.multiple_of(h * LANES, LANES))),
            pl.BlockSpec(elem, lambda b, h, i: (rows(b, i), pl.multiple_of((heads + h // 2) * LANES, LANES))),
            pl.BlockSpec((seq, LANES), lambda b, h, i: (b, 2 * h)),
            pl.BlockSpec((seq, LANES), lambda b, h, i: (b, 0)),
            pl.BlockSpec((seq, LANES), lambda b, h, i: (b, 2 * h + 1)),
        ],
        out_specs=pl.BlockSpec(elem, lambda b, h, i: (rows(b, i), pl.multiple_of(h * LANES, LANES))),
        out_shape=out_shape,
        scratch_shapes=[pltpu.VMEM((seq, 2 * LANES), BF16), pltpu.VMEM((seq, 2 * LANES), BF16)],
        compiler_params=_cp("parallel", "parallel", "arbitrary"),
        name="mla_attn",
    )(*args)
    cpb = seq // ctx_len
    ctx_spec = lambda col: pl.BlockSpec((ctx_len, LANES), lambda b, h: (b * cpb, col(h)))
    return pl.pallas_call(
        functools.partial(_mla_attn_body, bounds=_key_chunks(ctx_len)),
        grid=(batch, heads),
        in_specs=[ctx_spec(lambda h: h), ctx_spec(lambda h: heads + h // 2), ctx_spec(lambda h: 2 * h),
                  ctx_spec(lambda h: 0), ctx_spec(lambda h: 2 * h + 1), pl.BlockSpec(memory_space=pl.ANY)],
        out_specs=ctx_spec(lambda h: h),
        out_shape=out_shape,
        scratch_shapes=[pltpu.VMEM((ctx_len, 2 * LANES), BF16), pltpu.VMEM((ctx_len, 2 * LANES), BF16)],
        input_output_aliases={5: 0},
        compiler_params=_cp("parallel", "parallel"),
        name="mla_attn_ctx",
    )(*args, lat)


def _retention_body(*refs, backward, cs, heads):
    if backward:
        lg_ref, q_ref, k_ref, v_ref, yf_ref, gate_ref, ng_ref, o_ref, s_ref, d_ref, qd_ref, kd_ref = refs
    else:
        lg_ref, q_ref, k_ref, v_ref, o_ref, s_ref, d_ref, qd_ref, kd_ref = refs
    t = pl.program_id(1)

    @pl.when(t == 0)
    def _():
        s_ref[...] = jnp.zeros_like(s_ref)
        pos_r = lax.broadcasted_iota(jnp.int32, (cs, cs), 0).astype(F32)
        pos_c = lax.broadcasted_iota(jnp.int32, (cs, cs), 1).astype(F32)
        pos = lax.broadcasted_iota(jnp.int32, (cs, 1), 0).astype(F32)
        for h in range(heads):
            lg = lg_ref[h]
            if backward:
                diff = pos_c - pos_r
                keep = diff > 0
                qd_ref[h] = jnp.exp(lg * (cs - pos))
                kd_ref[h] = jnp.exp(lg * pos)
            else:
                diff = pos_r - pos_c
                keep = diff >= 0
                qd_ref[h] = jnp.exp(lg * (pos + 1.0))
                kd_ref[h] = jnp.exp(lg * (cs - 1.0 - pos))
            d_ref[h] = jnp.where(keep, jnp.exp(lg * jnp.maximum(diff, 0.0)), 0.0)

    lane = lax.broadcasted_iota(jnp.int32, (1, LANES), 1)
    nt = (((1,), (1,)), ((), ()))
    tn = (((0,), (0,)), ((), ()))
    for h in range(heads):
        blk = slice((h // 2) * LANES, (h // 2 + 1) * LANES)
        col = slice(h * RET_V, (h + 1) * RET_V)
        mine = (lane < RET_K) == (h % 2 == 0)
        q = q_ref[:, blk]
        q = jnp.where(mine, q, jnp.zeros_like(q))
        k = k_ref[:, blk]
        v = v_ref[:, col]
        scores = lax.dot_general(q, k, nt, preferred_element_type=F32) * d_ref[h]
        intra = jnp.dot(scores.astype(BF16), v, preferred_element_type=F32)
        state = s_ref[h]
        q_w = (q.astype(F32) * qd_ref[h]).astype(BF16)
        cross = jnp.dot(q_w, state.astype(BF16), preferred_element_type=F32)
        k_w = (k.astype(F32) * kd_ref[h]).astype(BF16)
        upd = lax.dot_general(k_w, v, tn, preferred_element_type=F32)
        s_ref[h] = jnp.exp(lg_ref[h] * cs) * state + upd
        y = intra + cross
        if backward:
            y = y + yf_ref[:, col].astype(F32)
            y = y * lax.rsqrt(jnp.mean(y * y, axis=-1, keepdims=True) + NORM_EPS) * ng_ref[:, col]
            g = gate_ref[:, col].astype(F32)
            o_ref[:, col] = (g * _sigmoid(g) * y).astype(o_ref.dtype)
        else:
            o_ref[:, col] = y.astype(o_ref.dtype)


def _retention(qk, proj, log_g, norm_g, *, batch, seq, ctx_len, heads, q_off, k_off, v_off, g_off):
    cs = RET_CHUNK
    nc, nc_ctx = seq // cs, ctx_len // cs
    wqk, wv = heads * RET_K, heads * RET_V
    assert q_off % wqk == 0 and k_off % wqk == 0 and v_off % wv == 0 and g_off % wv == 0
    smem = pl.BlockSpec(memory_space=pltpu.SMEM)

    def fwd_chunk(t):
        return t

    def bwd_chunk(t):
        return jnp.where(t < nc_ctx, nc_ctx - 1 - t, nc - 1 - (t - nc_ctx))

    def specs(chunk):
        row = lambda b, t: b * nc + chunk(t)
        return (pl.BlockSpec((cs, wqk), lambda b, t: (row(b, t), q_off // wqk)),
                pl.BlockSpec((cs, wqk), lambda b, t: (row(b, t), k_off // wqk)),
                pl.BlockSpec((cs, wv), lambda b, t: (row(b, t), v_off // wv)),
                pl.BlockSpec((cs, wv), lambda b, t: (row(b, t), 0)),
                pl.BlockSpec((cs, wv), lambda b, t: (row(b, t), g_off // wv)))

    scratch = [pltpu.VMEM((heads, LANES, RET_V), F32), pltpu.VMEM((heads, cs, cs), F32),
               pltpu.VMEM((heads, cs, 1), F32), pltpu.VMEM((heads, cs, 1), F32)]
    qs, ks, vs, ys, gs = specs(fwd_chunk)
    y_f = pl.pallas_call(
        functools.partial(_retention_body, backward=False, cs=cs, heads=heads),
        grid=(batch, nc),
        in_specs=[smem, qs, ks, vs],
        out_specs=ys,
        out_shape=jax.ShapeDtypeStruct((batch * seq, wv), BF16),
        scratch_shapes=scratch,
        compiler_params=_cp("parallel", "arbitrary"),
        name="retention_fwd",
    )(log_g[0].astype(F32), qk, qk, proj)
    qs, ks, vs, ys, gs = specs(bwd_chunk)
    return pl.pallas_call(
        functools.partial(_retention_body, backward=True, cs=cs, heads=heads),
        grid=(batch, nc),
        in_specs=[smem, qs, ks, vs, ys, gs, pl.BlockSpec((1, wv), lambda b, t: (0, 0))],
        out_specs=ys,
        out_shape=jax.ShapeDtypeStruct((batch * seq, wv), BF16),
        scratch_shapes=scratch,
        compiler_params=_cp("parallel", "arbitrary"),
        name="retention_bwd",
    )(log_g[1].astype(F32), qk, qk, proj, y_f, proj, norm_g.reshape(1, wv).astype(F32))


S5N_CHUNK = 8
S5N_GROUPS = LANES // S5_CH
S5N_HALF = S5N_GROUPS * S5_STATE


def _s5n_params(a_re, a_im, log_dt, b_re, b_im, c_re, c_im, d_skip):
    L, ch, gl = S5N_CHUNK, S5_CH, S5N_GROUPS
    a_re, a_im = a_re.astype(F32), a_im.astype(F32)
    groups = a_re.shape[1]
    nb = groups // gl
    dt = jnp.exp(log_dt.astype(F32))[..., None]
    e = jnp.arange(L + 1, dtype=F32)[:, None, None, None]
    mag = jnp.exp(a_re * dt * e)
    pw_re, pw_im = mag * jnp.cos(a_im * dt * e), mag * jnp.sin(a_im * dt * e)
    ab_re, ab_im = pw_re[1], pw_im[1]
    den = a_re * a_re + a_im * a_im
    f_re = ((ab_re - 1.0) * a_re + ab_im * a_im) / den
    f_im = (ab_im * a_re - (ab_re - 1.0) * a_im) / den
    bb_re = f_re[..., None] * b_re - f_im[..., None] * b_im
    bb_im = f_re[..., None] * b_im + f_im[..., None] * b_re
    c_re, c_im = c_re.astype(F32), c_im.astype(F32)
    hp = lax.Precision.HIGHEST
    idx = jnp.arange(L)

    cp_re = c_re[None] * pw_re[:L, :, :, None, :] - c_im[None] * pw_im[:L, :, :, None, :]
    cp_im = c_re[None] * pw_im[:L, :, :, None, :] + c_im[None] * pw_re[:L, :, :, None, :]
    kmat = (jnp.einsum("ldgcp,dgpk->dglck", cp_re, bb_re, precision=hp)
            - jnp.einsum("ldgcp,dgpk->dglck", cp_im, bb_im, precision=hp))
    lag = idx[None, :] - idx[:, None]

    def toeplitz(k, lg):
        return jnp.where((lg >= 0)[None, :, :, None, None], k[:, jnp.clip(lg, 0, L - 1)], 0.0)

    t = toeplitz(kmat[0], lag) + toeplitz(kmat[1], -lag)
    skip = (idx[:, None] == idx[None, :])[None, :, :, None, None] * (
        jnp.eye(ch, dtype=F32)[None, None, None] * d_skip.astype(F32)[:, None, None, :, None])
    t = (t + skip).reshape(nb, gl, L, L, ch, ch)
    m_mat = t.transpose(0, 2, 1, 5, 3, 4).reshape(nb, L, LANES, L * ch)

    def state_in(d, exps):
        p_re, p_im = pw_re[exps, d], pw_im[exps, d]
        w_re = p_re[..., None] * bb_re[d][None] - p_im[..., None] * bb_im[d][None]
        w_im = p_re[..., None] * bb_im[d][None] + p_im[..., None] * bb_re[d][None]
        return jnp.stack([w_re, w_im], axis=0).transpose(2, 1, 4, 0, 3)

    w_full = jnp.stack([state_in(0, L - 1 - idx), state_in(1, idx)], axis=3)
    w_full = w_full.reshape(nb, gl, L, ch, 2, 2, S5_STATE)
    w_mat = w_full.transpose(0, 2, 1, 3, 4, 5, 6).reshape(nb, L, LANES, 4 * S5_STATE)

    def state_out(d, exps):
        p_re, p_im = pw_re[exps, d], pw_im[exps, d]
        v_re = c_re[d][None] * p_re[:, :, None, :] - c_im[d][None] * p_im[:, :, None, :]
        v_im = c_re[d][None] * p_im[:, :, None, :] + c_im[d][None] * p_re[:, :, None, :]
        return jnp.stack([v_re, -v_im], axis=0).transpose(2, 0, 4, 1, 3)

    v_full = jnp.stack([state_out(0, idx + 1), state_out(1, L - idx)], axis=1)
    v_full = v_full.reshape(nb, gl, 2, 2, S5_STATE, L, ch)
    v_mat = v_full.transpose(0, 2, 3, 1, 4, 5, 6).reshape(nb, 4 * S5N_HALF, L * ch)

    a_pow = jnp.stack([pw_re[L], pw_im[L]], axis=1).reshape(2, 2, nb * S5N_HALF)
    return m_mat.astype(BF16), w_mat.astype(BF16), v_mat.astype(BF16), a_pow


def _s5n_spreaders():
    L, ch, gl, st = S5N_CHUNK, S5_CH, S5N_GROUPS, S5_STATE
    e_out = jnp.einsum("ij,cd->icjd", jnp.eye(L), jnp.eye(ch))
    e_out = jnp.broadcast_to(e_out[:, :, :, None, :], (L, ch, L, gl, ch)).reshape(L * ch, L * LANES)
    e_st = jnp.broadcast_to(jnp.eye(4 * st).reshape(4 * st, 4, 1, st), (4 * st, 4, gl, st))
    return e_out.astype(BF16), e_st.reshape(4 * st, 4 * S5N_HALF).astype(BF16)


def _s5n_expand(compact, spread, row_group, col_group):
    full = jnp.dot(compact, spread, preferred_element_type=F32)
    rows = lax.broadcasted_iota(jnp.int32, full.shape, 0)
    cols = lax.broadcasted_iota(jnp.int32, full.shape, 1)
    return jnp.where(row_group(rows) == col_group(cols), full, 0.0).astype(BF16)


_S5N_IN_ROW_GROUP = lambda r: r // S5_CH
_S5N_STATE_GROUP = lambda c: (c % S5N_HALF) // S5_STATE
_S5N_OUT_COL_GROUP = lambda c: (c % LANES) // S5_CH


def _s5n_state_in_body(u_ref, w_ref, e_ref, o_ref, w_exp):
    @pl.when(pl.program_id(1) == 0)
    def _():
        for j in range(S5N_CHUNK):
            w_exp[j * LANES:(j + 1) * LANES, :] = _s5n_expand(w_ref[0, j], e_ref[...],
                                                              _S5N_IN_ROW_GROUP, _S5N_STATE_GROUP)

    u_cat = jnp.concatenate([u_ref[:, j, :] for j in range(S5N_CHUNK)], axis=1)
    acc = jnp.dot(u_cat, w_exp[...], preferred_element_type=F32)
    for d in range(2):
        for r in range(2):
            lo = (2 * d + r) * S5N_HALF
            o_ref[d, r] = acc[:, lo:lo + S5N_HALF]


def _s5n_scan_body(sr_ref, si_ref, a_ref, o_ref, *, nk, nk_ctx):
    d = pl.program_id(0)
    ar, ai = a_ref[0, 0], a_ref[0, 1]

    def step(k, carry):
        hr, hi = carry
        o_ref[0, 0, pl.ds(k, 1), :] = hr
        o_ref[0, 1, pl.ds(k, 1), :] = hi
        sr = sr_ref[0, 0, pl.ds(k, 1), :]
        si = si_ref[0, 0, pl.ds(k, 1), :]
        return ar * hr - ai * hi + sr, ar * hi + ai * hr + si

    zero = jnp.zeros_like(ar)
    unroll = 8 if (nk % 8 == 0 and nk_ctx % 8 == 0) else 1

    @pl.when(d == 0)
    def _():
        lax.fori_loop(0, nk, step, (zero, zero), unroll=unroll)

    @pl.when(d == 1)
    def _():
        hc = lax.fori_loop(0, nk_ctx, lambda t, c: step(nk_ctx - 1 - t, c), (zero, zero), unroll=unroll)
        lax.fori_loop(0, nk - nk_ctx, lambda t, c: step(nk - 1 - t, c), hc, unroll=unroll)


def _s5n_out_body(u_ref, m_ref, h_ref, v_ref, e_ref, o_ref, m_exp, v_exp):
    @pl.when(pl.program_id(1) == 0)
    def _():
        for j in range(S5N_CHUNK):
            m_exp[j * LANES:(j + 1) * LANES, :] = _s5n_expand(m_ref[0, j], e_ref[...],
                                                              _S5N_IN_ROW_GROUP, _S5N_OUT_COL_GROUP)
        v_exp[...] = _s5n_expand(v_ref[0], e_ref[...], _S5N_STATE_GROUP, _S5N_OUT_COL_GROUP)

    u_cat = jnp.concatenate([u_ref[:, j, :] for j in range(S5N_CHUNK)], axis=1)
    h_cat = jnp.concatenate([h_ref[d, r].astype(BF16) for d in range(2) for r in range(2)], axis=1)
    acc = (jnp.dot(u_cat, m_exp[...], preferred_element_type=F32)
           + jnp.dot(h_cat, v_exp[...], preferred_element_type=F32))
    y = jax.nn.gelu(acc).astype(o_ref.dtype)
    for i in range(S5N_CHUNK):
        o_ref[:, i, :] = y[:, i * LANES:(i + 1) * LANES]


def _s5n_mix(proj, su_off, params, *, batch, seq, ctx_len, gw):
    m_mat, w_mat, v_mat, a_pow = params
    L, half = S5N_CHUNK, S5N_HALF
    nb = gw // LANES
    nk, nk_ctx = seq // L, ctx_len // L
    assert su_off % LANES == 0 and seq % L == 0 and ctx_len % L == 0
    ub = su_off // LANES
    u3 = proj.reshape(batch * nk, L, proj.shape[1])
    lanes = batch * nb * half
    u_spec = pl.BlockSpec((nk, L, LANES), lambda n, b: (b, 0, ub + n))
    h_spec = pl.BlockSpec((2, 2, nk, half), lambda n, b: (0, 0, 0, b * nb + n))

    e_out, e_st = _s5n_spreaders()
    whole = lambda a: pl.BlockSpec(a.shape, lambda n, b: (0,) * a.ndim)
    s_in = pl.pallas_call(
        _s5n_state_in_body,
        grid=(nb, batch),
        in_specs=[u_spec, pl.BlockSpec((1,) + w_mat.shape[1:], lambda n, b: (n, 0, 0, 0)), whole(e_st)],
        out_specs=h_spec,
        out_shape=jax.ShapeDtypeStruct((2, 2, nk, lanes), F32),
        scratch_shapes=[pltpu.VMEM((L * LANES, 4 * half), BF16)],
        compiler_params=_cp("parallel", "arbitrary"),
        name="s5_state_in",
    )(u3, w_mat, e_st)

    wl = _pick_tile(lanes, S5_SCAN_LANES, LANES)
    part_spec = lambda r: pl.BlockSpec((1, 1, nk, wl), lambda d, w: (d, r, 0, w))
    a_lanes = jnp.tile(a_pow.reshape(2, 2, 1, nb * half), (1, 1, 1, batch))
    h_prev = pl.pallas_call(
        functools.partial(_s5n_scan_body, nk=nk, nk_ctx=nk_ctx),
        grid=(2, lanes // wl),
        in_specs=[part_spec(0), part_spec(1), pl.BlockSpec((1, 2, 1, wl), lambda d, w: (d, 0, 0, w))],
        out_specs=pl.BlockSpec((1, 2, nk, wl), lambda d, w: (d, 0, 0, w)),
        out_shape=jax.ShapeDtypeStruct((2, 2, nk, lanes), F32),
        compiler_params=_cp("parallel", "parallel"),
        name="s5_scan",
    )(s_in, s_in, a_lanes)

    y3 = pl.pallas_call(
        _s5n_out_body,
        grid=(nb, batch),
        in_specs=[u_spec,
                  pl.BlockSpec((1,) + m_mat.shape[1:], lambda n, b: (n, 0, 0, 0)),
                  h_spec,
                  pl.BlockSpec((1,) + v_mat.shape[1:], lambda n, b: (n, 0, 0)),
                  whole(e_out)],
        out_specs=pl.BlockSpec((nk, L, LANES), lambda n, b: (b, 0, n)),
        out_shape=jax.ShapeDtypeStruct((batch * nk, L, gw), BF16),
        scratch_shapes=[pltpu.VMEM((L * LANES, L * LANES), BF16), pltpu.VMEM((4 * half, L * LANES), BF16)],
        compiler_params=_cp("parallel", "arbitrary"),
        name="s5_out",
    )(u3, m_mat, h_prev, v_mat, e_out)
    return y3.reshape(batch * seq, gw)


def _expert_changed(te_ref):
    i = pl.program_id(1)
    return (i == 0) | (te_ref[i] != te_ref[jnp.maximum(i - 1, 0)])


def _moe_up_body(te_ref, x_ref, wg_ref, wu_ref, o_ref, wg_bf, wu_bf):
    @pl.when(_expert_changed(te_ref))
    def _():
        wg_bf[...] = wg_ref[0, 0].astype(BF16)
        wu_bf[...] = wu_ref[0, 0].astype(BF16)

    x = x_ref[...]
    a = jnp.dot(x, wg_bf[...], preferred_element_type=F32)
    b = jnp.dot(x, wu_bf[...], preferred_element_type=F32)
    o_ref[...] = (a * _sigmoid(a) * b).astype(o_ref.dtype)


def _moe_down_body(te_ref, h_ref, w_ref, rw_ref, o_ref, w_bf):
    @pl.when(_expert_changed(te_ref))
    def _():
        w_bf[...] = w_ref[0, 0].astype(BF16)

    y = jnp.dot(h_ref[...], w_bf[...], preferred_element_type=F32)
    o_ref[...] = (rw_ref[...] * y).astype(o_ref.dtype)


def _moe_down_into_body(te_ref, h_ref, w_ref, rw_ref, prev_ref, o_ref, w_bf):
    _moe_down_body(te_ref, h_ref, w_ref, rw_ref, o_ref, w_bf)


def _moe_combine_body(x_ref, y0_ref, y1_ref, gate_ref, g_ref, sh_ref, sc_ref, o_ref, h_ref, *,
                      tm, tiles_per_batch, ctx_len):
    i = pl.program_id(0)
    is_ctx = (i % tiles_per_batch) * tm < ctx_len
    pick = lambda r: jnp.where(is_ctx, r[0, 0:1, :], r[0, 1:2, :])
    x = x_ref[...] + pick(gate_ref) * (y0_ref[...].astype(F32) + y1_ref[...].astype(F32))
    o_ref[...] = x
    y = x * lax.rsqrt(jnp.mean(x * x, axis=-1, keepdims=True) + NORM_EPS) * g_ref[...]
    h_ref[...] = (y * (1.0 + pick(sc_ref)) + pick(sh_ref)).astype(BF16)


def _moe_final_body(x_ref, y0_ref, y1_ref, gate_ref, g_ref, o_ref):
    x = x_ref[...] + gate_ref[0, 1:2, :] * (y0_ref[...].astype(F32) + y1_ref[...].astype(F32))
    o_ref[...] = x * lax.rsqrt(jnp.mean(x * x, axis=-1, keepdims=True) + NORM_EPS) * g_ref[...]


def _route(logits):
    assert MOE_TOPK == 2
    g_logit = logits[:, :MOE_GROUPS]
    g_prob = jax.nn.softmax(g_logit, axis=-1)
    g_idx = jnp.argmax(g_prob, axis=-1)
    g_p = jnp.max(g_prob, axis=-1)
    e_logit = logits[:, MOE_GROUPS:MOE_GROUPS + MOE_GROUPS * MOE_PER_GROUP]
    e_logit = e_logit.reshape(-1, MOE_GROUPS, MOE_PER_GROUP)
    sel = (jnp.arange(MOE_GROUPS)[None, :] == g_idx[:, None])[:, :, None]
    e_logit = jnp.sum(jnp.where(sel, e_logit, 0.0), axis=1)
    e_prob = jax.nn.softmax(e_logit, axis=-1)
    i0 = jnp.argmax(e_prob, axis=-1)
    p0 = jnp.max(e_prob, axis=-1)
    rest = jnp.where(jnp.arange(MOE_PER_GROUP)[None, :] == i0[:, None], -1.0, e_prob)
    i1 = jnp.argmax(rest, axis=-1)
    p1 = jnp.max(rest, axis=-1)
    e_p = jnp.stack([p0, p1], axis=-1)
    w = g_p[:, None] * e_p / jnp.sum(e_p, axis=-1, keepdims=True)
    ids = g_idx[:, None] * MOE_PER_GROUP + jnp.stack([i0, i1], axis=-1)
    return ids.astype(jnp.int32), w


def _moe(h, logits, x, gate, w_gate, w_up, w_down, layer, *, rows_per_batch, ctx_len,
         final_g=None, next_norm=None):
    t, d = h.shape
    _, n_exp, _, dff = w_gate.shape
    tile = MOE_TILE
    ids, wts = _route(logits)
    flat_e = ids.reshape(-1)
    onehot = (flat_e[:, None] == jnp.arange(n_exp)[None, :]).astype(jnp.int32)
    counts = onehot.sum(0)
    rank = jnp.take_along_axis(jnp.cumsum(onehot, axis=0) - onehot, flat_e[:, None], axis=1)[:, 0]
    padded = (counts + tile - 1) // tile * tile
    starts = jnp.cumsum(padded) - padded
    pos = starts[flat_e] + rank
    n_rows = (t * MOE_TOPK // tile + n_exp) * tile
    n_tiles = n_rows // tile
    token = (jnp.arange(t * MOE_TOPK, dtype=jnp.int32) // MOE_TOPK).astype(F32)
    table = jnp.zeros((n_rows, 2), F32).at[pos].set(jnp.stack([token, wts.reshape(-1)], axis=1),
                                                    unique_indices=True, mode="promise_in_bounds")
    src = table[:, 0].astype(jnp.int32)
    row_w = table[:, 1]
    tile_start = jnp.arange(n_tiles, dtype=jnp.int32) * tile
    ends = starts + padded
    tile_e = jnp.minimum(jnp.sum(tile_start[:, None] >= ends[None, :], axis=1), n_exp - 1).astype(jnp.int32)

    n_chunks = max(c for c in (2, 1) if n_tiles % c == 0)
    tpc = n_tiles // n_chunks
    tn_up = _pick_tile(dff, MOE_UP_COL_TILE, LANES)
    tn_dn = _pick_tile(d, MOE_DOWN_COL_TILE, LANES)
    w_up_spec = pl.BlockSpec((1, 1, d, tn_up), lambda j, i, te: (layer, te[i], 0, j))
    ys = None
    for c in range(n_chunks):
        rows_c = slice(c * tpc * tile, (c + 1) * tpc * tile)
        te_c = tile_e[c * tpc:(c + 1) * tpc]
        xs = h.at[src[rows_c]].get(mode="promise_in_bounds")
        hid = pl.pallas_call(
            _moe_up_body,
            grid_spec=pltpu.PrefetchScalarGridSpec(
                num_scalar_prefetch=1,
                grid=(dff // tn_up, tpc),
                in_specs=[pl.BlockSpec((tile, d), lambda j, i, te: (i, 0)), w_up_spec, w_up_spec],
                out_specs=pl.BlockSpec((tile, tn_up), lambda j, i, te: (i, j)),
                scratch_shapes=[pltpu.VMEM((d, tn_up), BF16), pltpu.VMEM((d, tn_up), BF16)]),
            out_shape=jax.ShapeDtypeStruct((tpc * tile, dff), BF16),
            compiler_params=_cp("arbitrary", "arbitrary"),
            name="moe_up",
        )(te_c, xs, w_gate, w_up)
        first = ys is None
        ys = pl.pallas_call(
            _moe_down_body if first else _moe_down_into_body,
            grid_spec=pltpu.PrefetchScalarGridSpec(
                num_scalar_prefetch=1,
                grid=(d // tn_dn, tpc),
                in_specs=[pl.BlockSpec((tile, dff), lambda j, i, te: (i, 0)),
                          pl.BlockSpec((1, 1, dff, tn_dn), lambda j, i, te: (layer, te[i], 0, j)),
                          pl.BlockSpec((tile, 1), lambda j, i, te: (i, 0))]
                         + ([] if first else [pl.BlockSpec(memory_space=pl.ANY)]),
                out_specs=pl.BlockSpec((tile, tn_dn), lambda j, i, te, c=c: (c * tpc + i, j)),
                scratch_shapes=[pltpu.VMEM((dff, tn_dn), BF16)]),
            out_shape=jax.ShapeDtypeStruct((n_rows, d), BF16),
            input_output_aliases={} if first else {4: 0},
            compiler_params=_cp("arbitrary", "arbitrary"),
            name="moe_down",
        )(te_c, hid, w_down, row_w[rows_c].reshape(tpc * tile, 1), *(() if first else (ys,)))

    pos2 = pos.reshape(t, MOE_TOPK)
    y0 = ys.at[pos2[:, 0]].get(mode="promise_in_bounds")
    y1 = ys.at[pos2[:, 1]].get(mode="promise_in_bounds")
    tm = _pick_tile(math.gcd(rows_per_batch, ctx_len), ROW_TILE, 8)
    tpb = rows_per_batch // tm
    if final_g is not None:
        batch = t // rows_per_batch
        ctx_tiles, lat_tiles = ctx_len // tm, (rows_per_batch - ctx_len) // tm
        lat_spec = pl.BlockSpec((tm, d), lambda b, i: (b * tpb + ctx_tiles + i, 0))
        out = pl.pallas_call(
            _moe_final_body,
            grid=(batch, lat_tiles),
            in_specs=[lat_spec, lat_spec, lat_spec, pl.BlockSpec((1, 2, d), lambda b, i: (b, 0, 0)),
                      pl.BlockSpec((1, d), lambda b, i: (0, 0))],
            out_specs=pl.BlockSpec((tm, d), lambda b, i: (b * lat_tiles + i, 0)),
            out_shape=jax.ShapeDtypeStruct((batch * lat_tiles * tm, d), F32),
            compiler_params=_cp("parallel", "parallel"),
            name="moe_combine_final",
        )(x, y0, y1, gate, final_g.reshape(1, d).astype(F32))
        return out.reshape(batch, lat_tiles * tm, d)
    row_spec = pl.BlockSpec((tm, d), lambda i: (i, 0))
    mod_spec = pl.BlockSpec((1, 2, d), lambda i: (i // tpb, 0, 0))
    next_g, next_shift, next_scale = next_norm
    return pl.pallas_call(
        functools.partial(_moe_combine_body, tm=tm, tiles_per_batch=tpb, ctx_len=ctx_len),
        grid=(t // tm,),
        in_specs=[row_spec, row_spec, row_spec, mod_spec, pl.BlockSpec((1, d), lambda i: (0, 0)),
                  mod_spec, mod_spec],
        out_specs=[row_spec, row_spec],
        out_shape=[jax.ShapeDtypeStruct((t, d), F32), jax.ShapeDtypeStruct((t, d), BF16)],
        compiler_params=_cp("parallel"),
        name="moe_combine",
    )(x, y0, y1, gate, next_g.reshape(1, d).astype(F32), next_shift, next_scale)


def kernel(x, c, ctx, c_ctx, ada_w, ada_b, norm_mix, norm_ffn, w_in, w_out, diff_lambda, diff_subln,
           s5_a_re, s5_a_im, s5_log_dt, s5_b_re, s5_b_im, s5_c_re, s5_c_im, s5_d, s5_glu_w, s5_glu_b,
           mla_q_norm, mla_kv_norm, mla_w_uq, mla_w_ukv, ret_decay, ret_norm,
           moe_wg, moe_bg, moe_we, moe_be, moe_w_gate, moe_w_up, moe_w_down, final_norm):
    batch, n_lat, d = x.shape
    ctx_len = ctx.shape[1]
    depth = ada_w.shape[0]
    seq = ctx_len + n_lat
    rows = batch * seq
    gw = d // 4
    heads = gw // LANES
    q_rank, kv_rank = 3 * d // 16, d // 16
    ret_qk = heads * RET_K
    n_route = MOE_GROUPS + MOE_GROUPS * MOE_PER_GROUP
    assert heads % 2 == 0 and ctx_len % RET_CHUNK == 0 and n_lat % RET_CHUNK == 0

    splits = (gw, gw, gw, gw, q_rank, kv_rank, ROPE_DIM, ret_qk, ret_qk, gw, gw)
    offs = [0]
    for s_ in splits:
        offs.append(offs[-1] + s_)
    names = ("dq", "dk", "dv", "su", "mcq", "mckv", "mkr", "rq", "rk", "rv", "rg")
    src_col = {n_: (offs[i], offs[i + 1]) for i, n_ in enumerate(names)}
    order = ("dq", "dk", "rq", "rk", "dv", "su", "rv", "rg", "mcq", "mckv")
    col = {}
    pos = 0
    for n_ in order:
        col[n_] = pos
        pos += src_col[n_][1] - src_col[n_][0]
    n_main = pos
    n_rope = col["dv"]
    uq_cols = jnp.arange(heads * (MLA_NOPE + ROPE_DIM)).reshape(heads, MLA_NOPE + ROPE_DIM)
    uq_perm = jnp.concatenate([uq_cols[:, :MLA_NOPE].reshape(-1), uq_cols[:, MLA_NOPE:].reshape(-1)])

    tables = _rope_tables(n_lat, ctx_len)
    log2e = math.log2(math.e)
    rope_scale = jnp.ones((n_rope,), F32).at[col["rk"]:col["rk"] + ret_qk].set(RET_K ** -0.5)
    rope_scale = rope_scale.at[col["dq"]:col["dq"] + gw].set(DIFF_HEAD_DIM ** -0.5 * log2e)
    mla_q_scale = (MLA_NOPE + ROPE_DIM) ** -0.5 * log2e
    main_scale = jnp.concatenate([rope_scale, jnp.ones((n_main - n_rope,), F32)])

    cond = jnp.concatenate([c_ctx[None, :], c], axis=0)
    cond = jnp.pad(cond * _sigmoid(cond), ((0, 8 - (batch + 1) % 8 if (batch + 1) % 8 else 0), (0, 0)))

    mod_all = _ada_mod(cond, ada_w, ada_b)

    xa = None
    tm_big = _pick_tile(seq, MM_ROW_TILE)
    tn = lambda n_: _pick_tile(n_, MM_COL_TILE, LANES)

    def layer_mods(l):
        mod = mod_all[l].reshape(cond.shape[0], 6, d)
        return [jnp.stack([jnp.broadcast_to(mod[0, i], (batch, d)), mod[1:batch + 1, i]], axis=1)
                for i in range(6)]

    h_mix = None
    for l in range(depth):
        lam_init = 0.8 - 0.6 * math.exp(-0.3 * l)
        mods = layer_mods(l)

        w_main = jnp.concatenate([w_in[l, :, src_col[n_][0]:src_col[n_][1]] for n_ in order],
                                 axis=1).astype(BF16)
        w_kr = jnp.concatenate([w_in[l, :, src_col["mkr"][0]:src_col["mkr"][1]]] * (LANES // ROPE_DIM),
                               axis=1).astype(BF16)

        if l == 0:
            h, xa = _norm_mod_first(ctx, x, norm_mix[l], mods[0], mods[1])
        else:
            h = h_mix
        proj = _mm([h], w_main, name="in_proj", out_dtype=BF16, tm=tm_big,
                   tn=tn(math.gcd(n_main, n_rope)), rope=(tables, main_scale, (0, n_rope)),
                   rows_per_batch=seq)
        krr = _mm([h], w_kr, name="in_proj_kr", out_dtype=BF16, tm=tm_big, tn=LANES,
                  rope=(tables, jnp.ones((LANES,), F32), (0, LANES)), rows_per_batch=seq)
        qk = proj

        lv = diff_lambda[l].astype(F32)
        lam = jnp.exp(jnp.sum(lv[0] * lv[1])) - jnp.exp(jnp.sum(lv[2] * lv[3])) + lam_init
        a_out = _diff_attn(qk, proj, lam, diff_subln[l], batch=batch, seq=seq, ctx_len=ctx_len,
                           heads=heads, q_blk=col["dq"] // LANES, k_blk=col["dk"] // LANES,
                           v_blk=col["dv"] // LANES, post=1.0 - lam_init)

        s5p = _s5n_params(s5_a_re[l], s5_a_im[l], s5_log_dt[l], s5_b_re[l], s5_b_im[l],
                          s5_c_re[l], s5_c_im[l], s5_d[l])
        s_act = _s5n_mix(proj, col["su"], s5p, batch=batch, seq=seq, ctx_len=ctx_len, gw=gw)
        s_out = _mm([s_act], s5_glu_w[l].astype(BF16), name="s5_glu", out_dtype=BF16, tm=tm_big, tn=tn(gw),
                    bias=s5_glu_b[l], glu_in=s_act)

        cq = proj[:, col["mcq"]:col["mcq"] + q_rank]
        ckv = proj[:, col["mckv"]:col["mckv"] + kv_rank]
        w_uq = (mla_w_uq[l][:, uq_perm] * mla_q_scale).astype(BF16)
        n_qn, n_q = heads * MLA_NOPE, heads * (MLA_NOPE + ROPE_DIM)
        q_up = _mm([cq], w_uq, name="mla_q_up", out_dtype=BF16, tm=tm_big, tn=tn(math.gcd(n_qn, n_q)),
                   norm_g=mla_q_norm[l], rope=(tables, jnp.ones((n_q,), F32), (n_qn, n_q)),
                   rows_per_batch=seq)
        kv_up = _mm([ckv], mla_w_ukv[l].astype(BF16), name="mla_kv_up", out_dtype=BF16, tm=tm_big,
                    tn=tn(mla_w_ukv.shape[2]), norm_g=mla_kv_norm[l])
        m_out = _mla_attn(q_up, kv_up, krr, batch=batch, seq=seq, ctx_len=ctx_len, heads=heads)

        log_g = jax.nn.log_sigmoid(ret_decay[l].astype(F32))
        r_out = _retention(qk, proj, log_g, ret_norm[l], batch=batch, seq=seq, ctx_len=ctx_len,
                           heads=heads, q_off=col["rq"], k_off=col["rk"], v_off=col["rv"], g_off=col["rg"])

        xa = _mm([a_out, s_out, m_out, r_out], w_out[l].astype(BF16), name="out_proj", out_dtype=F32, tm=tm_big, tn=tn(d),
                 res=xa, gate=mods[2], rows_per_batch=seq, ctx_len=ctx_len)

        w_r = jnp.concatenate([moe_wg[l], moe_we[l]], axis=1).astype(F32)
        w_r = jnp.pad(w_r, ((0, 0), (0, LANES - n_route)))
        w_r_hi = w_r.astype(BF16)
        w_r_lo = (w_r - w_r_hi.astype(F32)).astype(BF16)
        b_r = jnp.pad(jnp.concatenate([moe_bg[l], moe_be[l]]).astype(F32), (0, LANES - n_route))
        h, logits = _norm_mod(xa, norm_ffn[l], mods[3], mods[4], rows_per_batch=seq, ctx_len=ctx_len,
                              router=(w_r_hi, w_r_lo, b_r.reshape(1, LANES)))
        if l == depth - 1:
            return _moe(h, logits, xa, mods[5], moe_w_gate, moe_w_up, moe_w_down, l,
                        rows_per_batch=seq, ctx_len=ctx_len, final_g=final_norm)
        nxt = layer_mods(l + 1)
        xa, h_mix = _moe(h, logits, xa, mods[5], moe_w_gate, moe_w_up, moe_w_down, l,
                         rows_per_batch=seq, ctx_len=ctx_len, next_norm=(norm_mix[l + 1], nxt[0], nxt[1]))
```

```python
import functools
import math

import jax
import jax.numpy as jnp
import numpy as np
from jax import lax
from jax.experimental import pallas as pl
from jax.experimental.pallas import tpu as pltpu

BF16 = jnp.bfloat16
F32 = jnp.float32

V7X_VMEM_BYTES = 64 * 2**20
VMEM_LIMIT = V7X_VMEM_BYTES - 12 * 2**20
LANES = 128

GRID_W = 64
ROPE_DIM = 64
ROPE_BASE = 10000.0
NORM_EPS = 1e-6
DIFF_HEAD_DIM = 64
S5_CH = 16
S5_STATE = 64
MLA_NOPE = 128
MLA_V = 128
RET_K = 64
RET_V = 128
RET_CHUNK = 128
MOE_GROUPS = 4
MOE_PER_GROUP = 4
MOE_TOPK = 2
ROPE_QUARTER = ROPE_DIM // 4

MM_ROW_TILE = 1088
MM_COL_TILE = 512
ROW_TILE = 256
ATTN_Q_TILE = 512
S5_SCAN_LANES = 2048
MOE_TILE = 256
MOE_UP_COL_TILE = 512
MOE_DOWN_COL_TILE = 4096


def _cp(*sem):
    return pltpu.CompilerParams(dimension_semantics=sem, vmem_limit_bytes=VMEM_LIMIT)


def _pick_tile(n, target, mult=16):
    best = None
    for t in range(mult, min(n, target) + 1, mult):
        if n % t == 0:
            best = t
    assert best is not None, (n, target)
    return best


def _sigmoid(x):
    return 1.0 / (1.0 + jnp.exp(-x))


def _rotate(x, cos, sin, swap):
    reps = x.shape[1] // LANES
    tile = lambda t: jnp.tile(t, (1, reps))
    partner = jnp.dot(x.astype(BF16), swap, preferred_element_type=F32)
    return x * tile(cos) + partner * tile(sin)


def _rope_swap(width):
    lane = np.arange(width)
    first = (lane // ROPE_QUARTER) % 2 == 0
    src = np.where(first, lane + ROPE_QUARTER, lane - ROPE_QUARTER)
    swap = np.zeros((width, width), np.float32)
    swap[src, lane] = np.where(first, -1.0, 1.0)
    return jnp.asarray(swap, BF16)


def _mm_body(*refs, nx, ksizes, has_norm, has_bias, epilogue, tm, tiles_per_batch, ctx_len, rope_tiles):
    x_refs = refs[:nx]
    w_ref = refs[nx]
    idx = nx + 1
    g_ref = b_ref = e_ref = res_ref = gate_ref = None
    if has_norm:
        g_ref = refs[idx]; idx += 1
    if has_bias:
        b_ref = refs[idx]; idx += 1
    if epilogue == "glu":
        e_ref = refs[idx]; idx += 1
    if epilogue == "resgate":
        res_ref, gate_ref = refs[idx], refs[idx + 1]; idx += 2
    if epilogue == "rope":
        cos_ref, sin_ref, swap_ref, cs_ref = refs[idx:idx + 4]; idx += 4
    o_ref = refs[idx]

    acc = None
    off = 0
    if nx > 1 and not has_norm:
        x_cat = jnp.concatenate([xr[...].astype(BF16) for xr in x_refs], axis=1)
        acc = jnp.dot(x_cat, w_ref[...].astype(BF16), preferred_element_type=F32)
        x_refs = ()
    for xr, ks in zip(x_refs, ksizes):
        x = xr[...]
        if has_norm:
            xf = x.astype(F32)
            xf = xf * lax.rsqrt(jnp.mean(xf * xf, axis=-1, keepdims=True) + NORM_EPS)
            x = xf * g_ref[...]
        x = x.astype(BF16)
        w = w_ref[off:off + ks, :].astype(BF16)
        part = jnp.dot(x, w, preferred_element_type=F32)
        acc = part if acc is None else acc + part
        off += ks
    if has_bias:
        acc = acc + b_ref[...]
    if epilogue == "glu":
        acc = e_ref[...].astype(F32) * _sigmoid(acc)
    elif epilogue == "resgate":
        i = pl.program_id(0)
        row = (i % tiles_per_batch) * tm + lax.broadcasted_iota(jnp.int32, (tm, 1), 0)
        gate = jnp.where(row < ctx_len, gate_ref[0, 0:1, :], gate_ref[0, 1:2, :])
        acc = res_ref[...] + gate * acc
    if epilogue == "rope":
        j = pl.program_id(1)
        roped = (j >= rope_tiles[0]) & (j < rope_tiles[1])

        @pl.when(roped)
        def _():
            y = _rotate(acc, cos_ref[...], sin_ref[...], swap_ref[...]) * cs_ref[...]
            o_ref[...] = y.astype(o_ref.dtype)

        @pl.when(jnp.logical_not(roped))
        def _():
            o_ref[...] = acc.astype(o_ref.dtype)
    else:
        o_ref[...] = acc.astype(o_ref.dtype)


def _mm(xs, w, *, name, out_dtype, tm, tn, norm_g=None, bias=None, glu_in=None, res=None, gate=None,
        rope=None, rows_per_batch=None, ctx_len=0):
    m = xs[0].shape[0]
    ksizes = tuple(x.shape[1] for x in xs)
    k, n = w.shape
    assert sum(ksizes) == k and m % tm == 0 and n % tn == 0
    epilogue = ("glu" if glu_in is not None else "resgate" if res is not None
                else "rope" if rope is not None else None)
    tiles_per_batch = (rows_per_batch // tm) if rows_per_batch else 1
    rope_tiles = None
    in_specs = [pl.BlockSpec((tm, ks), lambda i, j: (i, 0)) for ks in ksizes]
    in_specs.append(pl.BlockSpec((k, tn), lambda i, j: (0, j)))
    args = list(xs) + [w]
    if norm_g is not None:
        in_specs.append(pl.BlockSpec((1, k), lambda i, j: (0, 0)))
        args.append(norm_g.reshape(1, k).astype(F32))
    if bias is not None:
        in_specs.append(pl.BlockSpec((1, tn), lambda i, j: (0, j)))
        args.append(bias.reshape(1, n).astype(F32))
    if epilogue == "glu":
        in_specs.append(pl.BlockSpec((tm, tn), lambda i, j: (i, j)))
        args.append(glu_in)
    if epilogue == "resgate":
        tpb = tiles_per_batch
        in_specs.append(pl.BlockSpec((tm, tn), lambda i, j: (i, j)))
        in_specs.append(pl.BlockSpec((1, 2, tn), lambda i, j: (i // tpb, 0, j)))
        args += [res, gate]
    if epilogue == "rope":
        tables, col_scale, (lo, hi) = rope
        assert lo % tn == 0 and hi % tn == 0
        rope_tiles = (lo // tn, hi // tn)
        tpb = tiles_per_batch
        in_specs += [pl.BlockSpec((tm, LANES), lambda i, j: (i % tpb, 0))] * 2
        in_specs.append(pl.BlockSpec((tn, tn), lambda i, j: (0, 0)))
        in_specs.append(pl.BlockSpec((1, tn), lambda i, j: (0, j)))
        args += list(tables) + [_rope_swap(tn), col_scale.reshape(1, n).astype(F32)]
    body = functools.partial(_mm_body, nx=len(xs), ksizes=ksizes, has_norm=norm_g is not None,
                             has_bias=bias is not None, epilogue=epilogue, tm=tm,
                             tiles_per_batch=tiles_per_batch, ctx_len=ctx_len, rope_tiles=rope_tiles)
    return pl.pallas_call(
        body,
        grid=(m // tm, n // tn),
        in_specs=in_specs,
        out_specs=pl.BlockSpec((tm, tn), lambda i, j: (i, j)),
        out_shape=jax.ShapeDtypeStruct((m, n), out_dtype),
        compiler_params=_cp("parallel", "arbitrary"),
        name=name,
    )(*args)


def _ada_body(c_ref, w_ref, b_ref, o_ref):
    acc = jnp.dot(c_ref[...].astype(BF16), w_ref[0].astype(BF16), preferred_element_type=F32)
    o_ref[0] = acc + b_ref[0]


def _ada_mod(cond, ada_w, ada_b):
    depth, d, n6 = ada_w.shape
    rows = cond.shape[0]
    tn = _pick_tile(n6, MM_COL_TILE, LANES)
    return pl.pallas_call(
        _ada_body,
        grid=(depth, n6 // tn),
        in_specs=[pl.BlockSpec((rows, d), lambda l, j: (0, 0)),
                  pl.BlockSpec((1, d, tn), lambda l, j: (l, 0, j)),
                  pl.BlockSpec((1, 1, tn), lambda l, j: (l, 0, j))],
        out_specs=pl.BlockSpec((1, rows, tn), lambda l, j: (l, 0, j)),
        out_shape=jax.ShapeDtypeStruct((depth, rows, n6), F32),
        compiler_params=_cp("parallel", "arbitrary"),
        name="ada_mod",
    )(cond, ada_w, ada_b.reshape(depth, 1, n6).astype(F32))


def _norm_mod_body(*refs, tm, tiles_per_batch, ctx_len, router):
    if router:
        x_ref, g_ref, sh_ref, sc_ref, whi_ref, wlo_ref, br_ref, h_ref, lg_ref = refs
    else:
        x_ref, g_ref, sh_ref, sc_ref, h_ref = refs
    i = pl.program_id(0)
    x = x_ref[...]
    y = x * lax.rsqrt(jnp.mean(x * x, axis=-1, keepdims=True) + NORM_EPS) * g_ref[...]
    is_ctx = (i % tiles_per_batch) * tm < ctx_len
    sh = jnp.where(is_ctx, sh_ref[0, 0:1, :], sh_ref[0, 1:2, :])
    sc = jnp.where(is_ctx, sc_ref[0, 0:1, :], sc_ref[0, 1:2, :])
    h = y * (1.0 + sc) + sh
    h_ref[...] = h.astype(BF16)
    if router:
        hi = h.astype(BF16)
        lo = (h - hi.astype(F32)).astype(BF16)
        lg = jnp.dot(hi, whi_ref[...], preferred_element_type=F32)
        lg = lg + jnp.dot(hi, wlo_ref[...], preferred_element_type=F32)
        lg = lg + jnp.dot(lo, whi_ref[...], preferred_element_type=F32)
        lg_ref[...] = lg + br_ref[...]


def _norm_mod_first_body(c_ref, x_ref, g_ref, sh_ref, sc_ref, h_ref, xa_ref, *, ctx_tiles):
    is_ctx = pl.program_id(1) < ctx_tiles
    x = jnp.where(is_ctx, c_ref[...], x_ref[...])
    xa_ref[...] = x
    y = x * lax.rsqrt(jnp.mean(x * x, axis=-1, keepdims=True) + NORM_EPS) * g_ref[...]
    sh = jnp.where(is_ctx, sh_ref[0, 0:1, :], sh_ref[0, 1:2, :])
    sc = jnp.where(is_ctx, sc_ref[0, 0:1, :], sc_ref[0, 1:2, :])
    h_ref[...] = (y * (1.0 + sc) + sh).astype(BF16)


def _norm_mod_first(ctx, x, g, shift, scale):
    batch, ctx_len, d = ctx.shape
    n_lat = x.shape[1]
    tm = _pick_tile(math.gcd(n_lat, ctx_len), ROW_TILE, 8)
    ctx_tiles, lat_tiles = ctx_len // tm, n_lat // tm
    tpb = ctx_tiles + lat_tiles
    rows = batch * tpb * tm
    out_spec = pl.BlockSpec((tm, d), lambda b, i: (b * tpb + i, 0))
    mod_spec = pl.BlockSpec((1, 2, d), lambda b, i: (b, 0, 0))
    return pl.pallas_call(
        functools.partial(_norm_mod_first_body, ctx_tiles=ctx_tiles),
        grid=(batch, tpb),
        in_specs=[pl.BlockSpec((tm, d), lambda b, i: (b * ctx_tiles + jnp.minimum(i, ctx_tiles - 1), 0)),
                  pl.BlockSpec((tm, d), lambda b, i: (b * lat_tiles + jnp.maximum(i - ctx_tiles, 0), 0)),
                  pl.BlockSpec((1, d), lambda b, i: (0, 0)), mod_spec, mod_spec],
        out_specs=[out_spec, out_spec],
        out_shape=[jax.ShapeDtypeStruct((rows, d), BF16), jax.ShapeDtypeStruct((rows, d), F32)],
        compiler_params=_cp("parallel", "arbitrary"),
        name="norm_mod_first",
    )(ctx.reshape(batch * ctx_len, d), x.reshape(batch * n_lat, d), g.reshape(1, d), shift, scale)


def _norm_mod(x, g, shift, scale, *, rows_per_batch, ctx_len, router=None):
    m, d = x.shape
    tm = _pick_tile(math.gcd(rows_per_batch, ctx_len), ROW_TILE, 8)
    tpb = rows_per_batch // tm
    in_specs = [
        pl.BlockSpec((tm, d), lambda i: (i, 0)),
        pl.BlockSpec((1, d), lambda i: (0, 0)),
        pl.BlockSpec((1, 2, d), lambda i: (i // tpb, 0, 0)),
        pl.BlockSpec((1, 2, d), lambda i: (i // tpb, 0, 0)),
    ]
    args = [x, g.reshape(1, d), shift, scale]
    out_specs = [pl.BlockSpec((tm, d), lambda i: (i, 0))]
    out_shape = [jax.ShapeDtypeStruct((m, d), BF16)]
    if router is not None:
        whi, wlo, br = router
        in_specs += [pl.BlockSpec((d, LANES), lambda i: (0, 0)),
                     pl.BlockSpec((d, LANES), lambda i: (0, 0)),
                     pl.BlockSpec((1, LANES), lambda i: (0, 0))]
        args += [whi, wlo, br]
        out_specs.append(pl.BlockSpec((tm, LANES), lambda i: (i, 0)))
        out_shape.append(jax.ShapeDtypeStruct((m, LANES), F32))
    body = functools.partial(_norm_mod_body, tm=tm, tiles_per_batch=tpb, ctx_len=ctx_len,
                             router=router is not None)
    outs = pl.pallas_call(body, grid=(m // tm,), in_specs=in_specs, out_specs=out_specs,
                          out_shape=out_shape, compiler_params=_cp("parallel"),
                          name="norm_mod_router" if router is not None else "norm_mod")(*args)
    return outs if router is not None else outs[0]


def _rope_tables(n_lat, ctx_len):
    rows = n_lat // GRID_W
    row = jnp.repeat(jnp.arange(rows, dtype=F32), GRID_W)
    col = jnp.tile(jnp.arange(GRID_W, dtype=F32), rows)
    quarter = ROPE_QUARTER
    inv = ROPE_BASE ** (-jnp.arange(quarter, dtype=F32) / quarter)
    ar = row[:, None] * inv
    ac = col[:, None] * inv
    ang = jnp.concatenate([ar, ar, ac, ac], axis=-1)
    ang = jnp.concatenate([jnp.zeros((ctx_len, ROPE_DIM), F32), ang], axis=0)
    ang = jnp.tile(ang, (1, LANES // ROPE_DIM))
    return jnp.cos(ang), jnp.sin(ang)


ATTN_ALIGN = 256


def _softmax_pv(q, k_ref, va_ref, bounds):
    ms, ovs = [], []
    for lo, hi in bounds:
        s = lax.dot_general(q, k_ref[lo:hi, :], (((1,), (1,)), ((), ())), preferred_element_type=F32)
        m = jnp.max(s, axis=-1, keepdims=True)
        e = jnp.exp2(s - m).astype(BF16)
        ovs.append(jnp.dot(e, va_ref[lo:hi, :], preferred_element_type=F32))
        ms.append(m)
    m_all = functools.reduce(jnp.maximum, ms)
    acc = sum(ov * jnp.exp2(m - m_all) for m, ov in zip(ms, ovs))
    return acc[:, :LANES] / acc[:, LANES:]


def _key_chunks(n_keys):
    if n_keys < 2 * ATTN_ALIGN:
        return ((0, n_keys),)
    half = (n_keys // ATTN_ALIGN + 1) // 2 * ATTN_ALIGN
    return ((0, half), (half, n_keys))


def _lat_tile(seq, ctx_len):
    n_lat = seq - ctx_len
    assert ctx_len % ATTN_ALIGN == 0 and n_lat % ATTN_ALIGN == 0
    return _pick_tile(n_lat, ATTN_Q_TILE, ATTN_ALIGN)


def _lat_rows(seq, ctx_len, tq):
    return lambda b, i: pl.multiple_of(b * seq + ctx_len + i * tq, ATTN_ALIGN)


def _diff_attn_body(lam_ref, q_ref, k_ref, v_ref, sub_ref, *rest, post, bounds):
    o_ref, va_ref = rest[-2:]

    def fill():
        va_ref[:, :LANES] = v_ref[...]
        va_ref[:, LANES:] = jnp.ones(v_ref.shape, BF16)

    if len(rest) == 2:
        pl.when(pl.program_id(2) == 0)(fill)
    else:
        fill()

    q = q_ref[...]
    lane = lax.broadcasted_iota(jnp.int32, (1, LANES), 1)
    first = lane < DIFF_HEAD_DIM
    zero = jnp.zeros_like(q)
    q0 = jnp.where(first, q, zero)
    q1 = jnp.where(first, zero, q)
    o = _softmax_pv(q0, k_ref, va_ref, bounds) - lam_ref[0] * _softmax_pv(q1, k_ref, va_ref, bounds)
    o = o * lax.rsqrt(jnp.mean(o * o, axis=-1, keepdims=True) + NORM_EPS) * sub_ref[...] * post
    o_ref[...] = o.astype(o_ref.dtype)


def _diff_attn(qk, proj, lam, subln, *, batch, seq, ctx_len, heads, q_blk, k_blk, v_blk, post):
    tq = _lat_tile(seq, ctx_len)
    rows = _lat_rows(seq, ctx_len, tq)
    smem = pl.BlockSpec(memory_space=pltpu.SMEM)
    args = (lam.reshape(1).astype(F32), qk, qk, proj, subln.reshape(1, LANES).astype(F32))
    out_shape = jax.ShapeDtypeStruct((batch * seq, heads * LANES), BF16)
    elem = (pl.Element(tq), pl.Element(LANES))
    lat = pl.pallas_call(
        functools.partial(_diff_attn_body, post=post, bounds=_key_chunks(seq)),
        grid=(batch, heads, (seq - ctx_len) // tq),
        in_specs=[
            smem,
            pl.BlockSpec(elem, lambda b, h, i: (rows(b, i), pl.multiple_of((q_blk + h) * LANES, LANES))),
            pl.BlockSpec((seq, LANES), lambda b, h, i: (b, k_blk + h)),
            pl.BlockSpec((seq, LANES), lambda b, h, i: (b, v_blk + h)),
            pl.BlockSpec((1, LANES), lambda b, h, i: (0, 0)),
        ],
        out_specs=pl.BlockSpec(elem, lambda b, h, i: (rows(b, i), pl.multiple_of(h * LANES, LANES))),
        out_shape=out_shape,
        scratch_shapes=[pltpu.VMEM((seq, 2 * LANES), BF16)],
        compiler_params=_cp("parallel", "parallel", "arbitrary"),
        name="diff_attn",
    )(*args)
    cpb = seq // ctx_len
    ctx_spec = lambda blk: pl.BlockSpec((ctx_len, LANES), lambda b, h: (b * cpb, blk + h))
    return pl.pallas_call(
        functools.partial(_diff_attn_body, post=post, bounds=_key_chunks(ctx_len)),
        grid=(batch, heads),
        in_specs=[smem, ctx_spec(q_blk), ctx_spec(k_blk), ctx_spec(v_blk),
                  pl.BlockSpec((1, LANES), lambda b, h: (0, 0)), pl.BlockSpec(memory_space=pl.ANY)],
        out_specs=ctx_spec(0),
        out_shape=out_shape,
        scratch_shapes=[pltpu.VMEM((ctx_len, 2 * LANES), BF16)],
        input_output_aliases={5: 0},
        compiler_params=_cp("parallel", "parallel"),
        name="diff_attn_ctx",
    )(*args, lat)


def _mla_attn_body(qn_ref, qr_ref, kn_ref, kr_ref, v_ref, *rest, bounds):
    o_ref, ka_ref, va_ref = rest[-3:]
    h = pl.program_id(1)

    def fill():
        ka_ref[:, :LANES] = kn_ref[...]
        ka_ref[:, LANES:] = kr_ref[...]
        va_ref[:, :LANES] = v_ref[...]
        va_ref[:, LANES:] = jnp.ones(v_ref.shape, BF16)

    if len(rest) == 3:
        pl.when(pl.program_id(2) == 0)(fill)
    else:
        fill()

    qr = qr_ref[...]
    lane = lax.broadcasted_iota(jnp.int32, (1, LANES), 1)
    mine = (lane < ROPE_DIM) == (h % 2 == 0)
    qr = jnp.where(mine, qr, jnp.zeros_like(qr))
    q = jnp.concatenate([qn_ref[...], qr], axis=1)
    o_ref[...] = _softmax_pv(q, ka_ref, va_ref, bounds).astype(o_ref.dtype)


def _mla_attn(q_up, kv_up, k_rope, *, batch, seq, ctx_len, heads):
    tq = _lat_tile(seq, ctx_len)
    rows = _lat_rows(seq, ctx_len, tq)
    args = (q_up, q_up, kv_up, k_rope, kv_up)
    out_shape = jax.ShapeDtypeStruct((batch * seq, heads * LANES), BF16)
    elem = (pl.Element(tq), pl.Element(LANES))
    lat = pl.pallas_call(
        functools.partial(_mla_attn_body, bounds=_key_chunks(seq)),
        grid=(batch, heads, (seq - ctx_len) // tq),
        in_specs=[
            pl.BlockSpec(elem, lambda b, h, i: (rows(b, i), pl.multiple_of(h * LANES, LANES))),
            pl.BlockSpec(elem, lambda b, h, i: (rows(b, i), pl.multiple_of((heads + h // 2) * LANES, LANES))),
            pl.BlockSpec((seq, LANES), lambda b, h, i: (b, 2 * h)),
            pl.BlockSpec((seq, LANES), lambda b, h, i: (b, 0)),
            pl.BlockSpec((seq, LANES), lambda b, h, i: (b, 2 * h + 1)),
        ],
        out_specs=pl.BlockSpec(elem, lambda b, h, i: (rows(b, i), pl.multiple_of(h * LANES, LANES))),
        out_shape=out_shape,
        scratch_shapes=[pltpu.VMEM((seq, 2 * LANES), BF16), pltpu.VMEM((seq, 2 * LANES), BF16)],
        compiler_params=_cp("parallel", "parallel", "arbitrary"),
        name="mla_attn",
    )(*args)
    cpb = seq // ctx_len
    ctx_spec = lambda col: pl.BlockSpec((ctx_len, LANES), lambda b, h: (b * cpb, col(h)))
    return pl.pallas_call(
        functools.partial(_mla_attn_body, bounds=_key_chunks(ctx_len)),
        grid=(batch, heads),
        in_specs=[ctx_spec(lambda h: h), ctx_spec(lambda h: heads + h // 2), ctx_spec(lambda h: 2 * h),
                  ctx_spec(lambda h: 0), ctx_spec(lambda h: 2 * h + 1), pl.BlockSpec(memory_space=pl.ANY)],
        out_specs=ctx_spec(lambda h: h),
        out_shape=out_shape,
        scratch_shapes=[pltpu.VMEM((ctx_len, 2 * LANES), BF16), pltpu.VMEM((ctx_len, 2 * LANES), BF16)],
        input_output_aliases={5: 0},
        compiler_params=_cp("parallel", "parallel"),
        name="mla_attn_ctx",
    )(*args, lat)


def _retention_body(*refs, backward, cs, heads):
    if backward:
        lg_ref, q_ref, k_ref, v_ref, yf_ref, gate_ref, ng_ref, o_ref, s_ref, d_ref, qd_ref, kd_ref = refs
    else:
        lg_ref, q_ref, k_ref, v_ref, o_ref, s_ref, d_ref, qd_ref, kd_ref = refs
    t = pl.program_id(1)

    @pl.when(t == 0)
    def _():
        s_ref[...] = jnp.zeros_like(s_ref)
        pos_r = lax.broadcasted_iota(jnp.int32, (cs, cs), 0).astype(F32)
        pos_c = lax.broadcasted_iota(jnp.int32, (cs, cs), 1).astype(F32)
        pos = lax.broadcasted_iota(jnp.int32, (cs, 1), 0).astype(F32)
        for h in range(heads):
            lg = lg_ref[h]
            if backward:
                diff = pos_c - pos_r
                keep = diff > 0
                qd_ref[h] = jnp.exp(lg * (cs - pos))
                kd_ref[h] = jnp.exp(lg * pos)
            else:
                diff = pos_r - pos_c
                keep = diff >= 0
                qd_ref[h] = jnp.exp(lg * (pos + 1.0))
                kd_ref[h] = jnp.exp(lg * (cs - 1.0 - pos))
            d_ref[h] = jnp.where(keep, jnp.exp(lg * jnp.maximum(diff, 0.0)), 0.0)

    lane = lax.broadcasted_iota(jnp.int32, (1, LANES), 1)
    nt = (((1,), (1,)), ((), ()))
    tn = (((0,), (0,)), ((), ()))
    for h in range(heads):
        blk = slice((h // 2) * LANES, (h // 2 + 1) * LANES)
        col = slice(h * RET_V, (h + 1) * RET_V)
        mine = (lane < RET_K) == (h % 2 == 0)
        q = q_ref[:, blk]
        q = jnp.where(mine, q, jnp.zeros_like(q))
        k = k_ref[:, blk]
        v = v_ref[:, col]
        scores = lax.dot_general(q, k, nt, preferred_element_type=F32) * d_ref[h]
        intra = jnp.dot(scores.astype(BF16), v, preferred_element_type=F32)
        state = s_ref[h]
        q_w = (q.astype(F32) * qd_ref[h]).astype(BF16)
        cross = jnp.dot(q_w, state.astype(BF16), preferred_element_type=F32)
        k_w = (k.astype(F32) * kd_ref[h]).astype(BF16)
        upd = lax.dot_general(k_w, v, tn, preferred_element_type=F32)
        s_ref[h] = jnp.exp(lg_ref[h] * cs) * state + upd
        y = intra + cross
        if backward:
            y = y + yf_ref[:, col].astype(F32)
            y = y * lax.rsqrt(jnp.mean(y * y, axis=-1, keepdims=True) + NORM_EPS) * ng_ref[:, col]
            g = gate_ref[:, col].astype(F32)
            o_ref[:, col] = (g * _sigmoid(g) * y).astype(o_ref.dtype)
        else:
            o_ref[:, col] = y.astype(o_ref.dtype)


def _retention(qk, proj, log_g, norm_g, *, batch, seq, ctx_len, heads, q_off, k_off, v_off, g_off):
    cs = RET_CHUNK
    nc, nc_ctx = seq // cs, ctx_len // cs
    wqk, wv = heads * RET_K, heads * RET_V
    assert q_off % wqk == 0 and k_off % wqk == 0 and v_off % wv == 0 and g_off % wv == 0
    smem = pl.BlockSpec(memory_space=pltpu.SMEM)

    def fwd_chunk(t):
        return t

    def bwd_chunk(t):
        return jnp.where(t < nc_ctx, nc_ctx - 1 - t, nc - 1 - (t - nc_ctx))

    def specs(chunk):
        row = lambda b, t: b * nc + chunk(t)
        return (pl.BlockSpec((cs, wqk), lambda b, t: (row(b, t), q_off // wqk)),
                pl.BlockSpec((cs, wqk), lambda b, t: (row(b, t), k_off // wqk)),
                pl.BlockSpec((cs, wv), lambda b, t: (row(b, t), v_off // wv)),
                pl.BlockSpec((cs, wv), lambda b, t: (row(b, t), 0)),
                pl.BlockSpec((cs, wv), lambda b, t: (row(b, t), g_off // wv)))

    scratch = [pltpu.VMEM((heads, LANES, RET_V), F32), pltpu.VMEM((heads, cs, cs), F32),
               pltpu.VMEM((heads, cs, 1), F32), pltpu.VMEM((heads, cs, 1), F32)]
    qs, ks, vs, ys, gs = specs(fwd_chunk)
    y_f = pl.pallas_call(
        functools.partial(_retention_body, backward=False, cs=cs, heads=heads),
        grid=(batch, nc),
        in_specs=[smem, qs, ks, vs],
        out_specs=ys,
        out_shape=jax.ShapeDtypeStruct((batch * seq, wv), BF16),
        scratch_shapes=scratch,
        compiler_params=_cp("parallel", "arbitrary"),
        name="retention_fwd",
    )(log_g[0].astype(F32), qk, qk, proj)
    qs, ks, vs, ys, gs = specs(bwd_chunk)
    return pl.pallas_call(
        functools.partial(_retention_body, backward=True, cs=cs, heads=heads),
        grid=(batch, nc),
        in_specs=[smem, qs, ks, vs, ys, gs, pl.BlockSpec((1, wv), lambda b, t: (0, 0))],
        out_specs=ys,
        out_shape=jax.ShapeDtypeStruct((batch * seq, wv), BF16),
        scratch_shapes=scratch,
        compiler_params=_cp("parallel", "arbitrary"),
        name="retention_bwd",
    )(log_g[1].astype(F32), qk, qk, proj, y_f, proj, norm_g.reshape(1, wv).astype(F32))


S5N_CHUNK = 8
S5N_GROUPS = LANES // S5_CH
S5N_HALF = S5N_GROUPS * S5_STATE


def _s5n_params(a_re, a_im, log_dt, b_re, b_im, c_re, c_im, d_skip):
    L, ch, gl = S5N_CHUNK, S5_CH, S5N_GROUPS
    a_re, a_im = a_re.astype(F32), a_im.astype(F32)
    groups = a_re.shape[1]
    nb = groups // gl
    dt = jnp.exp(log_dt.astype(F32))[..., None]
    e = jnp.arange(L + 1, dtype=F32)[:, None, None, None]
    mag = jnp.exp(a_re * dt * e)
    pw_re, pw_im = mag * jnp.cos(a_im * dt * e), mag * jnp.sin(a_im * dt * e)
    ab_re, ab_im = pw_re[1], pw_im[1]
    den = a_re * a_re + a_im * a_im
    f_re = ((ab_re - 1.0) * a_re + ab_im * a_im) / den
    f_im = (ab_im * a_re - (ab_re - 1.0) * a_im) / den
    bb_re = f_re[..., None] * b_re - f_im[..., None] * b_im
    bb_im = f_re[..., None] * b_im + f_im[..., None] * b_re
    c_re, c_im = c_re.astype(F32), c_im.astype(F32)
    hp = lax.Precision.HIGHEST
    idx = jnp.arange(L)

    cp_re = c_re[None] * pw_re[:L, :, :, None, :] - c_im[None] * pw_im[:L, :, :, None, :]
    cp_im = c_re[None] * pw_im[:L, :, :, None, :] + c_im[None] * pw_re[:L, :, :, None, :]
    kmat = (jnp.einsum("ldgcp,dgpk->dglck", cp_re, bb_re, precision=hp)
            - jnp.einsum("ldgcp,dgpk->dglck", cp_im, bb_im, precision=hp))
    lag = idx[None, :] - idx[:, None]

    def toeplitz(k, lg):
        return jnp.where((lg >= 0)[None, :, :, None, None], k[:, jnp.clip(lg, 0, L - 1)], 0.0)

    t = toeplitz(kmat[0], lag) + toeplitz(kmat[1], -lag)
    skip = (idx[:, None] == idx[None, :])[None, :, :, None, None] * (
        jnp.eye(ch, dtype=F32)[None, None, None] * d_skip.astype(F32)[:, None, None, :, None])
    t = (t + skip).reshape(nb, gl, L, L, ch, ch)
    m_mat = t.transpose(0, 2, 1, 5, 3, 4).reshape(nb, L, LANES, L * ch)

    def state_in(d, exps):
        p_re, p_im = pw_re[exps, d], pw_im[exps, d]
        w_re = p_re[..., None] * bb_re[d][None] - p_im[..., None] * bb_im[d][None]
        w_im = p_re[..., None] * bb_im[d][None] + p_im[..., None] * bb_re[d][None]
        return jnp.stack([w_re, w_im], axis=0).transpose(2, 1, 4, 0, 3)

    w_full = jnp.stack([state_in(0, L - 1 - idx), state_in(1, idx)], axis=3)
    w_full = w_full.reshape(nb, gl, L, ch, 2, 2, S5_STATE)
    w_mat = w_full.transpose(0, 2, 1, 3, 4, 5, 6).reshape(nb, L, LANES, 4 * S5_STATE)

    def state_out(d, exps):
        p_re, p_im = pw_re[exps, d], pw_im[exps, d]
        v_re = c_re[d][None] * p_re[:, :, None, :] - c_im[d][None] * p_im[:, :, None, :]
        v_im = c_re[d][None] * p_im[:, :, None, :] + c_im[d][None] * p_re[:, :, None, :]
        return jnp.stack([v_re, -v_im], axis=0).transpose(2, 0, 4, 1, 3)

    v_full = jnp.stack([state_out(0, idx + 1), state_out(1, L - idx)], axis=1)
    v_full = v_full.reshape(nb, gl, 2, 2, S5_STATE, L, ch)
    v_mat = v_full.transpose(0, 2, 3, 1, 4, 5, 6).reshape(nb, 4 * S5N_HALF, L * ch)

    a_pow = jnp.stack([pw_re[L], pw_im[L]], axis=1).reshape(2, 2, nb * S5N_HALF)
    return m_mat.astype(BF16), w_mat.astype(BF16), v_mat.astype(BF16), a_pow


def _s5n_spreaders():
    L, ch, gl, st = S5N_CHUNK, S5_CH, S5N_GROUPS, S5_STATE
    e_out = jnp.einsum("ij,cd->icjd", jnp.eye(L), jnp.eye(ch))
    e_out = jnp.broadcast_to(e_out[:, :, :, None, :], (L, ch, L, gl, ch)).reshape(L * ch, L * LANES)
    e_st = jnp.broadcast_to(jnp.eye(4 * st).reshape(4 * st, 4, 1, st), (4 * st, 4, gl, st))
    return e_out.astype(BF16), e_st.reshape(4 * st, 4 * S5N_HALF).astype(BF16)


def _s5n_expand(compact, spread, row_group, col_group):
    full = jnp.dot(compact, spread, preferred_element_type=F32)
    rows = lax.broadcasted_iota(jnp.int32, full.shape, 0)
    cols = lax.broadcasted_iota(jnp.int32, full.shape, 1)
    return jnp.where(row_group(rows) == col_group(cols), full, 0.0).astype(BF16)


_S5N_IN_ROW_GROUP = lambda r: r // S5_CH
_S5N_STATE_GROUP = lambda c: (c % S5N_HALF) // S5_STATE
_S5N_OUT_COL_GROUP = lambda c: (c % LANES) // S5_CH


def _s5n_state_in_body(u_ref, w_ref, e_ref, o_ref, w_exp):
    @pl.when(pl.program_id(1) == 0)
    def _():
        for j in range(S5N_CHUNK):
            w_exp[j * LANES:(j + 1) * LANES, :] = _s5n_expand(w_ref[0, j], e_ref[...],
                                                              _S5N_IN_ROW_GROUP, _S5N_STATE_GROUP)

    u_cat = jnp.concatenate([u_ref[:, j, :] for j in range(S5N_CHUNK)], axis=1)
    acc = jnp.dot(u_cat, w_exp[...], preferred_element_type=F32)
    for d in range(2):
        for r in range(2):
            lo = (2 * d + r) * S5N_HALF
            o_ref[d, r] = acc[:, lo:lo + S5N_HALF]


def _s5n_scan_body(sr_ref, si_ref, a_ref, o_ref, *, nk, nk_ctx):
    d = pl.program_id(0)
    ar, ai = a_ref[0, 0], a_ref[0, 1]

    def step(k, carry):
        hr, hi = carry
        o_ref[0, 0, pl.ds(k, 1), :] = hr
        o_ref[0, 1, pl.ds(k, 1), :] = hi
        sr = sr_ref[0, 0, pl.ds(k, 1), :]
        si = si_ref[0, 0, pl.ds(k, 1), :]
        return ar * hr - ai * hi + sr, ar * hi + ai * hr + si

    zero = jnp.zeros_like(ar)
    unroll = 8 if (nk % 8 == 0 and nk_ctx % 8 == 0) else 1

    @pl.when(d == 0)
    def _():
        lax.fori_loop(0, nk, step, (zero, zero), unroll=unroll)

    @pl.when(d == 1)
    def _():
        hc = lax.fori_loop(0, nk_ctx, lambda t, c: step(nk_ctx - 1 - t, c), (zero, zero), unroll=unroll)
        lax.fori_loop(0, nk - nk_ctx, lambda t, c: step(nk - 1 - t, c), hc, unroll=unroll)


def _s5n_out_body(u_ref, m_ref, h_ref, v_ref, e_ref, o_ref, m_exp, v_exp):
    @pl.when(pl.program_id(1) == 0)
    def _():
        for j in range(S5N_CHUNK):
            m_exp[j * LANES:(j + 1) * LANES, :] = _s5n_expand(m_ref[0, j], e_ref[...],
                                                              _S5N_IN_ROW_GROUP, _S5N_OUT_COL_GROUP)
        v_exp[...] = _s5n_expand(v_ref[0], e_ref[...], _S5N_STATE_GROUP, _S5N_OUT_COL_GROUP)

    u_cat = jnp.concatenate([u_ref[:, j, :] for j in range(S5N_CHUNK)], axis=1)
    h_cat = jnp.concatenate([h_ref[d, r].astype(BF16) for d in range(2) for r in range(2)], axis=1)
    acc = (jnp.dot(u_cat, m_exp[...], preferred_element_type=F32)
           + jnp.dot(h_cat, v_exp[...], preferred_element_type=F32))
    y = jax.nn.gelu(acc).astype(o_ref.dtype)
    for i in range(S5N_CHUNK):
        o_ref[:, i, :] = y[:, i * LANES:(i + 1) * LANES]


def _s5n_mix(proj, su_off, params, *, batch, seq, ctx_len, gw):
    m_mat, w_mat, v_mat, a_pow = params
    L, half = S5N_CHUNK, S5N_HALF
    nb = gw // LANES
    nk, nk_ctx = seq // L, ctx_len // L
    assert su_off % LANES == 0 and seq % L == 0 and ctx_len % L == 0
    ub = su_off // LANES
    u3 = proj.reshape(batch * nk, L, proj.shape[1])
    lanes = batch * nb * half
    u_spec = pl.BlockSpec((nk, L, LANES), lambda n, b: (b, 0, ub + n))
    h_spec = pl.BlockSpec((2, 2, nk, half), lambda n, b: (0, 0, 0, b * nb + n))

    e_out, e_st = _s5n_spreaders()
    whole = lambda a: pl.BlockSpec(a.shape, lambda n, b: (0,) * a.ndim)
    s_in = pl.pallas_call(
        _s5n_state_in_body,
        grid=(nb, batch),
        in_specs=[u_spec, pl.BlockSpec((1,) + w_mat.shape[1:], lambda n, b: (n, 0, 0, 0)), whole(e_st)],
        out_specs=h_spec,
        out_shape=jax.ShapeDtypeStruct((2, 2, nk, lanes), F32),
        scratch_shapes=[pltpu.VMEM((L * LANES, 4 * half), BF16)],
        compiler_params=_cp("parallel", "arbitrary"),
        name="s5_state_in",
    )(u3, w_mat, e_st)

    wl = _pick_tile(lanes, S5_SCAN_LANES, LANES)
    part_spec = lambda r: pl.BlockSpec((1, 1, nk, wl), lambda d, w: (d, r, 0, w))
    a_lanes = jnp.tile(a_pow.reshape(2, 2, 1, nb * half), (1, 1, 1, batch))
    h_prev = pl.pallas_call(
        functools.partial(_s5n_scan_body, nk=nk, nk_ctx=nk_ctx),
        grid=(2, lanes // wl),
        in_specs=[part_spec(0), part_spec(1), pl.BlockSpec((1, 2, 1, wl), lambda d, w: (d, 0, 0, w))],
        out_specs=pl.BlockSpec((1, 2, nk, wl), lambda d, w: (d, 0, 0, w)),
        out_shape=jax.ShapeDtypeStruct((2, 2, nk, lanes), F32),
        compiler_params=_cp("parallel", "parallel"),
        name="s5_scan",
    )(s_in, s_in, a_lanes)

    y3 = pl.pallas_call(
        _s5n_out_body,
        grid=(nb, batch),
        in_specs=[u_spec,
                  pl.BlockSpec((1,) + m_mat.shape[1:], lambda n, b: (n, 0, 0, 0)),
                  h_spec,
                  pl.BlockSpec((1,) + v_mat.shape[1:], lambda n, b: (n, 0, 0)),
                  whole(e_out)],
        out_specs=pl.BlockSpec((nk, L, LANES), lambda n, b: (b, 0, n)),
        out_shape=jax.ShapeDtypeStruct((batch * nk, L, gw), BF16),
        scratch_shapes=[pltpu.VMEM((L * LANES, L * LANES), BF16), pltpu.VMEM((4 * half, L * LANES), BF16)],
        compiler_params=_cp("parallel", "arbitrary"),
        name="s5_out",
    )(u3, m_mat, h_prev, v_mat, e_out)
    return y3.reshape(batch * seq, gw)


def _expert_changed(te_ref):
    i = pl.program_id(1)
    return (i == 0) | (te_ref[i] != te_ref[jnp.maximum(i - 1, 0)])


def _moe_up_body(te_ref, x_ref, wg_ref, wu_ref, o_ref, wg_bf, wu_bf):
    @pl.when(_expert_changed(te_ref))
    def _():
        wg_bf[...] = wg_ref[0, 0].astype(BF16)
        wu_bf[...] = wu_ref[0, 0].astype(BF16)

    x = x_ref[...]
    a = jnp.dot(x, wg_bf[...], preferred_element_type=F32)
    b = jnp.dot(x, wu_bf[...], preferred_element_type=F32)
    o_ref[...] = (a * _sigmoid(a) * b).astype(o_ref.dtype)


def _moe_down_body(te_ref, h_ref, w_ref, rw_ref, o_ref, w_bf):
    @pl.when(_expert_changed(te_ref))
    def _():
        w_bf[...] = w_ref[0, 0].astype(BF16)

    y = jnp.dot(h_ref[...], w_bf[...], preferred_element_type=F32)
    o_ref[...] = (rw_ref[...] * y).astype(o_ref.dtype)


def _moe_down_into_body(te_ref, h_ref, w_ref, rw_ref, prev_ref, o_ref, w_bf):
    _moe_down_body(te_ref, h_ref, w_ref, rw_ref, o_ref, w_bf)


def _moe_combine_body(x_ref, y0_ref, y1_ref, gate_ref, g_ref, sh_ref, sc_ref, o_ref, h_ref, *,
                      tm, tiles_per_batch, ctx_len):
    i = pl.program_id(0)
    is_ctx = (i % tiles_per_batch) * tm < ctx_len
    pick = lambda r: jnp.where(is_ctx, r[0, 0:1, :], r[0, 1:2, :])
    x = x_ref[...] + pick(gate_ref) * (y0_ref[...].astype(F32) + y1_ref[...].astype(F32))
    o_ref[...] = x
    y = x * lax.rsqrt(jnp.mean(x * x, axis=-1, keepdims=True) + NORM_EPS) * g_ref[...]
    h_ref[...] = (y * (1.0 + pick(sc_ref)) + pick(sh_ref)).astype(BF16)


def _moe_final_body(x_ref, y0_ref, y1_ref, gate_ref, g_ref, o_ref):
    x = x_ref[...] + gate_ref[0, 1:2, :] * (y0_ref[...].astype(F32) + y1_ref[...].astype(F32))
    o_ref[...] = x * lax.rsqrt(jnp.mean(x * x, axis=-1, keepdims=True) + NORM_EPS) * g_ref[...]


def _route(logits):
    assert MOE_TOPK == 2
    g_logit = logits[:, :MOE_GROUPS]
    g_prob = jax.nn.softmax(g_logit, axis=-1)
    g_idx = jnp.argmax(g_prob, axis=-1)
    g_p = jnp.max(g_prob, axis=-1)
    e_logit = logits[:, MOE_GROUPS:MOE_GROUPS + MOE_GROUPS * MOE_PER_GROUP]
    e_logit = e_logit.reshape(-1, MOE_GROUPS, MOE_PER_GROUP)
    sel = (jnp.arange(MOE_GROUPS)[None, :] == g_idx[:, None])[:, :, None]
    e_logit = jnp.sum(jnp.where(sel, e_logit, 0.0), axis=1)
    e_prob = jax.nn.softmax(e_logit, axis=-1)
    i0 = jnp.argmax(e_prob, axis=-1)
    p0 = jnp.max(e_prob, axis=-1)
    rest = jnp.where(jnp.arange(MOE_PER_GROUP)[None, :] == i0[:, None], -1.0, e_prob)
    i1 = jnp.argmax(rest, axis=-1)
    p1 = jnp.max(rest, axis=-1)
    e_p = jnp.stack([p0, p1], axis=-1)
    w = g_p[:, None] * e_p / jnp.sum(e_p, axis=-1, keepdims=True)
    ids = g_idx[:, None] * MOE_PER_GROUP + jnp.stack([i0, i1], axis=-1)
    return ids.astype(jnp.int32), w


def _moe(h, logits, x, gate, w_gate, w_up, w_down, layer, *, rows_per_batch, ctx_len,
         final_g=None, next_norm=None):
    t, d = h.shape
    _, n_exp, _, dff = w_gate.shape
    tile = MOE_TILE
    ids, wts = _route(logits)
    flat_e = ids.reshape(-1)
    onehot = (flat_e[:, None] == jnp.arange(n_exp)[None, :]).astype(jnp.int32)
    counts = onehot.sum(0)
    rank = jnp.take_along_axis(jnp.cumsum(onehot, axis=0) - onehot, flat_e[:, None], axis=1)[:, 0]
    padded = (counts + tile - 1) // tile * tile
    starts = jnp.cumsum(padded) - padded
    pos = starts[flat_e] + rank
    n_rows = (t * MOE_TOPK // tile + n_exp) * tile
    n_tiles = n_rows // tile
    token = (jnp.arange(t * MOE_TOPK, dtype=jnp.int32) // MOE_TOPK).astype(F32)
    table = jnp.zeros((n_rows, 2), F32).at[pos].set(jnp.stack([token, wts.reshape(-1)], axis=1),
                                                    unique_indices=True, mode="promise_in_bounds")
    src = table[:, 0].astype(jnp.int32)
    row_w = table[:, 1]
    tile_start = jnp.arange(n_tiles, dtype=jnp.int32) * tile
    ends = starts + padded
    tile_e = jnp.minimum(jnp.sum(tile_start[:, None] >= ends[None, :], axis=1), n_exp - 1).astype(jnp.int32)

    n_chunks = max(c for c in (4, 2, 1) if n_tiles % c == 0)
    tpc = n_tiles // n_chunks
    tn_up = _pick_tile(dff, MOE_UP_COL_TILE, LANES)
    tn_dn = _pick_tile(d, MOE_DOWN_COL_TILE, LANES)
    w_up_spec = pl.BlockSpec((1, 1, d, tn_up), lambda j, i, te: (layer, te[i], 0, j))
    ys = None
    for c in range(n_chunks):
        rows_c = slice(c * tpc * tile, (c + 1) * tpc * tile)
        te_c = tile_e[c * tpc:(c + 1) * tpc]
        xs = h.at[src[rows_c]].get(mode="promise_in_bounds")
        hid = pl.pallas_call(
            _moe_up_body,
            grid_spec=pltpu.PrefetchScalarGridSpec(
                num_scalar_prefetch=1,
                grid=(dff // tn_up, tpc),
                in_specs=[pl.BlockSpec((tile, d), lambda j, i, te: (i, 0)), w_up_spec, w_up_spec],
                out_specs=pl.BlockSpec((tile, tn_up), lambda j, i, te: (i, j)),
                scratch_shapes=[pltpu.VMEM((d, tn_up), BF16), pltpu.VMEM((d, tn_up), BF16)]),
            out_shape=jax.ShapeDtypeStruct((tpc * tile, dff), BF16),
            compiler_params=_cp("arbitrary", "arbitrary"),
            name="moe_up",
        )(te_c, xs, w_gate, w_up)
        first = ys is None
        ys = pl.pallas_call(
            _moe_down_body if first else _moe_down_into_body,
            grid_spec=pltpu.PrefetchScalarGridSpec(
                num_scalar_prefetch=1,
                grid=(d // tn_dn, tpc),
                in_specs=[pl.BlockSpec((tile, dff), lambda j, i, te: (i, 0)),
                          pl.BlockSpec((1, 1, dff, tn_dn), lambda j, i, te: (layer, te[i], 0, j)),
                          pl.BlockSpec((tile, 1), lambda j, i, te: (i, 0))]
                         + ([] if first else [pl.BlockSpec(memory_space=pl.ANY)]),
                out_specs=pl.BlockSpec((tile, tn_dn), lambda j, i, te, c=c: (c * tpc + i, j)),
                scratch_shapes=[pltpu.VMEM((dff, tn_dn), BF16)]),
            out_shape=jax.ShapeDtypeStruct((n_rows, d), BF16),
            input_output_aliases={} if first else {4: 0},
            compiler_params=_cp("arbitrary", "arbitrary"),
            name="moe_down",
        )(te_c, hid, w_down, row_w[rows_c].reshape(tpc * tile, 1), *(() if first else (ys,)))

    pos2 = pos.reshape(t, MOE_TOPK)
    y0 = ys.at[pos2[:, 0]].get(mode="promise_in_bounds")
    y1 = ys.at[pos2[:, 1]].get(mode="promise_in_bounds")
    tm = _pick_tile(math.gcd(rows_per_batch, ctx_len), ROW_TILE, 8)
    tpb = rows_per_batch // tm
    if final_g is not None:
        batch = t // rows_per_batch
        ctx_tiles, lat_tiles = ctx_len // tm, (rows_per_batch - ctx_len) // tm
        lat_spec = pl.BlockSpec((tm, d), lambda b, i: (b * tpb + ctx_tiles + i, 0))
        out = pl.pallas_call(
            _moe_final_body,
            grid=(batch, lat_tiles),
            in_specs=[lat_spec, lat_spec, lat_spec, pl.BlockSpec((1, 2, d), lambda b, i: (b, 0, 0)),
                      pl.BlockSpec((1, d), lambda b, i: (0, 0))],
            out_specs=pl.BlockSpec((tm, d), lambda b, i: (b * lat_tiles + i, 0)),
            out_shape=jax.ShapeDtypeStruct((batch * lat_tiles * tm, d), F32),
            compiler_params=_cp("parallel", "parallel"),
            name="moe_combine_final",
        )(x, y0, y1, gate, final_g.reshape(1, d).astype(F32))
        return out.reshape(batch, lat_tiles * tm, d)
    row_spec = pl.BlockSpec((tm, d), lambda i: (i, 0))
    mod_spec = pl.BlockSpec((1, 2, d), lambda i: (i // tpb, 0, 0))
    next_g, next_shift, next_scale = next_norm
    return pl.pallas_call(
        functools.partial(_moe_combine_body, tm=tm, tiles_per_batch=tpb, ctx_len=ctx_len),
        grid=(t // tm,),
        in_specs=[row_spec, row_spec, row_spec, mod_spec, pl.BlockSpec((1, d), lambda i: (0, 0)),
                  mod_spec, mod_spec],
        out_specs=[row_spec, row_spec],
        out_shape=[jax.ShapeDtypeStruct((t, d), F32), jax.ShapeDtypeStruct((t, d), BF16)],
        compiler_params=_cp("parallel"),
        name="moe_combine",
    )(x, y0, y1, gate, next_g.reshape(1, d).astype(F32), next_shift, next_scale)


def kernel(x, c, ctx, c_ctx, ada_w, ada_b, norm_mix, norm_ffn, w_in, w_out, diff_lambda, diff_subln,
           s5_a_re, s5_a_im, s5_log_dt, s5_b_re, s5_b_im, s5_c_re, s5_c_im, s5_d, s5_glu_w, s5_glu_b,
           mla_q_norm, mla_kv_norm, mla_w_uq, mla_w_ukv, ret_decay, ret_norm,
           moe_wg, moe_bg, moe_we, moe_be, moe_w_gate, moe_w_up, moe_w_down, final_norm):
    batch, n_lat, d = x.shape
    ctx_len = ctx.shape[1]
    depth = ada_w.shape[0]
    seq = ctx_len + n_lat
    rows = batch * seq
    gw = d // 4
    heads = gw // LANES
    q_rank, kv_rank = 3 * d // 16, d // 16
    ret_qk = heads * RET_K
    n_route = MOE_GROUPS + MOE_GROUPS * MOE_PER_GROUP
    assert heads % 2 == 0 and ctx_len % RET_CHUNK == 0 and n_lat % RET_CHUNK == 0

    splits = (gw, gw, gw, gw, q_rank, kv_rank, ROPE_DIM, ret_qk, ret_qk, gw, gw)
    offs = [0]
    for s_ in splits:
        offs.append(offs[-1] + s_)
    names = ("dq", "dk", "dv", "su", "mcq", "mckv", "mkr", "rq", "rk", "rv", "rg")
    src_col = {n_: (offs[i], offs[i + 1]) for i, n_ in enumerate(names)}
    order = ("dq", "dk", "rq", "rk", "dv", "su", "rv", "rg", "mcq", "mckv")
    col = {}
    pos = 0
    for n_ in order:
        col[n_] = pos
        pos += src_col[n_][1] - src_col[n_][0]
    n_main = pos
    n_rope = col["dv"]
    uq_cols = jnp.arange(heads * (MLA_NOPE + ROPE_DIM)).reshape(heads, MLA_NOPE + ROPE_DIM)
    uq_perm = jnp.concatenate([uq_cols[:, :MLA_NOPE].reshape(-1), uq_cols[:, MLA_NOPE:].reshape(-1)])

    tables = _rope_tables(n_lat, ctx_len)
    log2e = math.log2(math.e)
    rope_scale = jnp.ones((n_rope,), F32).at[col["rk"]:col["rk"] + ret_qk].set(RET_K ** -0.5)
    rope_scale = rope_scale.at[col["dq"]:col["dq"] + gw].set(DIFF_HEAD_DIM ** -0.5 * log2e)
    mla_q_scale = (MLA_NOPE + ROPE_DIM) ** -0.5 * log2e
    main_scale = jnp.concatenate([rope_scale, jnp.ones((n_main - n_rope,), F32)])

    cond = jnp.concatenate([c_ctx[None, :], c], axis=0)
    cond = jnp.pad(cond * _sigmoid(cond), ((0, 8 - (batch + 1) % 8 if (batch + 1) % 8 else 0), (0, 0)))

    mod_all = _ada_mod(cond, ada_w, ada_b)

    xa = None
    tm_big = _pick_tile(seq, MM_ROW_TILE)
    tn = lambda n_: _pick_tile(n_, MM_COL_TILE, LANES)

    def layer_mods(l):
        mod = mod_all[l].reshape(cond.shape[0], 6, d)
        return [jnp.stack([jnp.broadcast_to(mod[0, i], (batch, d)), mod[1:batch + 1, i]], axis=1)
                for i in range(6)]

    h_mix = None
    for l in range(depth):
        lam_init = 0.8 - 0.6 * math.exp(-0.3 * l)
        mods = layer_mods(l)

        w_main = jnp.concatenate([w_in[l, :, src_col[n_][0]:src_col[n_][1]] for n_ in order],
                                 axis=1).astype(BF16)
        w_kr = jnp.concatenate([w_in[l, :, src_col["mkr"][0]:src_col["mkr"][1]]] * (LANES // ROPE_DIM),
                               axis=1).astype(BF16)

        if l == 0:
            h, xa = _norm_mod_first(ctx, x, norm_mix[l], mods[0], mods[1])
        else:
            h = h_mix
        proj = _mm([h], w_main, name="in_proj", out_dtype=BF16, tm=tm_big,
                   tn=tn(math.gcd(n_main, n_rope)), rope=(tables, main_scale, (0, n_rope)),
                   rows_per_batch=seq)
        krr = _mm([h], w_kr, name="in_proj_kr", out_dtype=BF16, tm=tm_big, tn=LANES,
                  rope=(tables, jnp.ones((LANES,), F32), (0, LANES)), rows_per_batch=seq)
        qk = proj

        lv = diff_lambda[l].astype(F32)
        lam = jnp.exp(jnp.sum(lv[0] * lv[1])) - jnp.exp(jnp.sum(lv[2] * lv[3])) + lam_init
        a_out = _diff_attn(qk, proj, lam, diff_subln[l], batch=batch, seq=seq, ctx_len=ctx_len,
                           heads=heads, q_blk=col["dq"] // LANES, k_blk=col["dk"] // LANES,
                           v_blk=col["dv"] // LANES, post=1.0 - lam_init)

        s5p = _s5n_params(s5_a_re[l], s5_a_im[l], s5_log_dt[l], s5_b_re[l], s5_b_im[l],
                          s5_c_re[l], s5_c_im[l], s5_d[l])
        s_act = _s5n_mix(proj, col["su"], s5p, batch=batch, seq=seq, ctx_len=ctx_len, gw=gw)
        s_out = _mm([s_act], s5_glu_w[l].astype(BF16), name="s5_glu", out_dtype=BF16, tm=tm_big, tn=tn(gw),
                    bias=s5_glu_b[l], glu_in=s_act)

        cq = proj[:, col["mcq"]:col["mcq"] + q_rank]
        ckv = proj[:, col["mckv"]:col["mckv"] + kv_rank]
        w_uq = (mla_w_uq[l][:, uq_perm] * mla_q_scale).astype(BF16)
        n_qn, n_q = heads * MLA_NOPE, heads * (MLA_NOPE + ROPE_DIM)
        q_up = _mm([cq], w_uq, name="mla_q_up", out_dtype=BF16, tm=tm_big, tn=tn(math.gcd(n_qn, n_q)),
                   norm_g=mla_q_norm[l], rope=(tables, jnp.ones((n_q,), F32), (n_qn, n_q)),
                   rows_per_batch=seq)
        kv_up = _mm([ckv], mla_w_ukv[l].astype(BF16), name="mla_kv_up", out_dtype=BF16, tm=tm_big,
                    tn=tn(mla_w_ukv.shape[2]), norm_g=mla_kv_norm[l])
        m_out = _mla_attn(q_up, kv_up, krr, batch=batch, seq=seq, ctx_len=ctx_len, heads=heads)

        log_g = jax.nn.log_sigmoid(ret_decay[l].astype(F32))
        r_out = _retention(qk, proj, log_g, ret_norm[l], batch=batch, seq=seq, ctx_len=ctx_len,
                           heads=heads, q_off=col["rq"], k_off=col["rk"], v_off=col["rv"], g_off=col["rg"])

        xa = _mm([a_out, s_out, m_out, r_out], w_out[l].astype(BF16), name="out_proj", out_dtype=F32, tm=tm_big, tn=tn(d),
                 res=xa, gate=mods[2], rows_per_batch=seq, ctx_len=ctx_len)

        w_r = jnp.concatenate([moe_wg[l], moe_we[l]], axis=1).astype(F32)
        w_r = jnp.pad(w_r, ((0, 0), (0, LANES - n_route)))
        w_r_hi = w_r.astype(BF16)
        w_r_lo = (w_r - w_r_hi.astype(F32)).astype(BF16)
        b_r = jnp.pad(jnp.concatenate([moe_bg[l], moe_be[l]]).astype(F32), (0, LANES - n_route))
        h, logits = _norm_mod(xa, norm_ffn[l], mods[3], mods[4], rows_per_batch=seq, ctx_len=ctx_len,
                              router=(w_r_hi, w_r_lo, b_r.reshape(1, LANES)))
        if l == depth - 1:
            return _moe(h, logits, xa, mods[5], moe_w_gate, moe_w_up, moe_w_down, l,
                        rows_per_batch=seq, ctx_len=ctx_len, final_g=final_norm)
        nxt = layer_mods(l + 1)
        xa, h_mix = _moe(h, logits, xa, mods[5], moe_w_gate, moe_w_up, moe_w_down, l,
                         rows_per_batch=seq, ctx_len=ctx_len, next_norm=(norm_mix[l + 1], nxt[0], nxt[1]))
```
